```python
import math
import jax
import jax.numpy as jnp
from jax import lax
import numpy as np

D_MODEL = 2048
BATCH = 4
SEQ = 2048
DEPTH = 1

HEAD_DIM = D_MODEL // 16
N_HEADS_MOBA = 6
N_HEADS_FOX = 6
N_HEADS_MEM = 4
W_MOBA = N_HEADS_MOBA * HEAD_DIM
W_FOX = N_HEADS_FOX * HEAD_DIM
W_MEM = N_HEADS_MEM * HEAD_DIM
MIX_WIDTH = W_MOBA + W_FOX + W_MEM
IN_WIDTH = 4 * W_MOBA + 4 * W_FOX + N_HEADS_FOX + 2 * W_MEM
N_MEM = 256
MOBA_BLOCK = 256
MOBA_TOPK = 3
MOBA_Q_CHUNK = 16
FOX_Q_BLOCK = 128
T5_BUCKETS = 32
T5_MAX_DIST = 128
EPS = 1e-6
NEG = -1e30

kernel_name = "hybrid_moba_fox_memory_parallel_heads"


def rms_norm(x, g):
    xf = x.astype(jnp.float32)
    y = xf * lax.rsqrt(jnp.mean(xf * xf, axis=-1, keepdims=True) + EPS)
    return (y * g.astype(jnp.float32)).astype(x.dtype)


def split_heads(t, n_heads):
    b, s, _ = t.shape
    return t.reshape(b, s, n_heads, HEAD_DIM).transpose(0, 2, 1, 3)


def merge_heads(t):
    b, h, s, d = t.shape
    return t.transpose(0, 2, 1, 3).reshape(b, s, h * d)


def t5_bucket(dist):
    n = jnp.maximum(dist, 0)
    max_exact = T5_BUCKETS // 2
    nf = jnp.maximum(n, 1).astype(jnp.float32)
    large = max_exact + (jnp.log(nf / max_exact) / math.log(T5_MAX_DIST / max_exact)
                         * (T5_BUCKETS - max_exact)).astype(jnp.int32)
    large = jnp.minimum(large, T5_BUCKETS - 1)
    return jnp.where(n < max_exact, n, large)


def in_splits():
    sizes = [W_MOBA] * 4 + [W_FOX] * 4 + [N_HEADS_FOX] + [W_MEM] * 2
    return [int(v) for v in np.cumsum(sizes)[:-1]]


def moba_attention(q, k, v, rel_bias):
    b, h, s, dh = q.shape
    nb = -(-s // MOBA_BLOCK)
    pad = nb * MOBA_BLOCK - s
    k_blk = jnp.pad(k, ((0, 0), (0, 0), (0, pad), (0, 0))).reshape(b, h, nb, MOBA_BLOCK, dh)
    v_blk = jnp.pad(v, ((0, 0), (0, 0), (0, pad), (0, 0))).reshape(b, h, nb, MOBA_BLOCK, dh)
    k_mean = jnp.mean(k_blk.astype(jnp.float32), axis=3)
    scale = dh ** -0.5
    kk = max(min(MOBA_TOPK, nb - 1), 1)
    q_blk_id = jnp.arange(s) // MOBA_BLOCK
    gate = jnp.einsum('bhsd,bhnd->bhsn', q, k_mean.astype(q.dtype),
                      preferred_element_type=jnp.float32)
    past = jnp.arange(nb)[None, :] < q_blk_id[:, None]
    gate = jnp.where(past, gate, NEG)
    _, sel = lax.top_k(gate, kk)
    sel_valid = sel < q_blk_id[:, None]

    n_chunks = s // MOBA_Q_CHUNK
    q_c = q.reshape(b, h, n_chunks, MOBA_Q_CHUNK, dh).transpose(2, 0, 1, 3, 4)
    sel_c = sel.reshape(b, h, n_chunks, MOBA_Q_CHUNK, kk).transpose(2, 0, 1, 3, 4)
    val_c = sel_valid.reshape(b, h, n_chunks, MOBA_Q_CHUNK, kk).transpose(2, 0, 1, 3, 4)
    starts = jnp.arange(n_chunks, dtype=jnp.int32) * MOBA_Q_CHUNK
    b_idx = jnp.arange(b)[:, None, None, None]
    h_idx = jnp.arange(h)[None, :, None, None]
    h_idx5 = h_idx[..., None]
    blk_off = jnp.arange(MOBA_BLOCK)

    def chunk_fn(args):
        qc, selc, validc, start = args
        tq = start + jnp.arange(MOBA_Q_CHUNK)
        own = start // MOBA_BLOCK
        k_own = lax.dynamic_index_in_dim(k_blk, own, axis=2, keepdims=False)
        v_own = lax.dynamic_index_in_dim(v_blk, own, axis=2, keepdims=False)
        dist_own = tq[:, None] - (own * MOBA_BLOCK + blk_off)[None, :]
        l_own = jnp.einsum('bhqd,bhkd->bhqk', qc, k_own,
                           preferred_element_type=jnp.float32) * scale
        l_own = l_own + rel_bias[:, t5_bucket(dist_own)]
        l_own = jnp.where(dist_own >= 0, l_own, NEG)
        k_sel = k_blk[b_idx, h_idx, selc]
        v_sel = v_blk[b_idx, h_idx, selc]
        l_sel = jnp.einsum('bhqd,bhqnkd->bhqnk', qc, k_sel,
                           preferred_element_type=jnp.float32) * scale
        dist_sel = tq[None, None, :, None, None] - (selc[..., None] * MOBA_BLOCK + blk_off)
        l_sel = l_sel + rel_bias[h_idx5, t5_bucket(dist_sel)]
        l_sel = jnp.where(validc[..., None], l_sel, NEG)
        l_sel = l_sel.reshape(b, h, MOBA_Q_CHUNK, kk * MOBA_BLOCK)
        p = jax.nn.softmax(jnp.concatenate([l_sel, l_own], axis=-1), axis=-1).astype(v.dtype)
        out = jnp.einsum('bhqm,bhqmd->bhqd', p[..., :kk * MOBA_BLOCK],
                         v_sel.reshape(b, h, MOBA_Q_CHUNK, kk * MOBA_BLOCK, dh))
        out = out + jnp.einsum('bhqk,bhkd->bhqd', p[..., kk * MOBA_BLOCK:], v_own)
        return out.astype(q.dtype)

    out = lax.map(chunk_fn, (q_c, sel_c, val_c, starts))
    return out.transpose(1, 2, 0, 3, 4).reshape(b, h, s, dh)


def forgetting_attention(q, k, v, log_f):
    b, h, s, dh = q.shape
    c = jnp.cumsum(log_f, axis=-1)
    n_blk = s // FOX_Q_BLOCK
    q_c = q.reshape(b, h, n_blk, FOX_Q_BLOCK, dh).transpose(2, 0, 1, 3, 4)
    c_c = c.reshape(b, h, n_blk, FOX_Q_BLOCK).transpose(2, 0, 1, 3)
    starts = jnp.arange(n_blk, dtype=jnp.int32) * FOX_Q_BLOCK
    kpos = jnp.arange(s)
    scale = dh ** -0.5

    def blk_fn(args):
        qb, cb, start = args
        tq = start + jnp.arange(FOX_Q_BLOCK)
        l = jnp.einsum('bhqd,bhkd->bhqk', qb, k, preferred_element_type=jnp.float32) * scale
        l = l + cb[..., None] - c[:, :, None, :]
        l = jnp.where(kpos[None, :] <= tq[:, None], l, NEG)
        p = jax.nn.softmax(l, axis=-1).astype(v.dtype)
        return jnp.einsum('bhqk,bhkd->bhqd', p, v)

    out = lax.map(blk_fn, (q_c, c_c, starts))
    return out.transpose(1, 2, 0, 3, 4).reshape(b, h, s, dh)


def memory_attention(q, k, v):
    l = jnp.einsum('bhqd,bhmd->bhqm', q, k, preferred_element_type=jnp.float32) * (HEAD_DIM ** -0.5)
    p = jax.nn.softmax(l, axis=-1).astype(v.dtype)
    return jnp.einsum('bhqm,bhmd->bhqd', p, v)


def setup_inputs(seed: int = 0) -> dict:
    key = jax.random.key(seed)
    ks = jax.random.split(key, 11)
    f32 = jnp.float32
    x = jax.random.normal(ks[0], (BATCH, SEQ, D_MODEL), f32)
    mem = jax.random.normal(ks[1], (BATCH, N_MEM, D_MODEL), f32)
    w_in = jax.random.normal(ks[2], (DEPTH, D_MODEL, IN_WIDTH), f32) * D_MODEL ** -0.5
    b_forget = 2.0 + 0.5 * jax.random.normal(ks[3], (DEPTH, N_HEADS_FOX), f32)
    w_mem_kv = jax.random.normal(ks[4], (DEPTH, D_MODEL, 2 * W_MEM), f32) * D_MODEL ** -0.5
    w_out = jax.random.normal(ks[5], (DEPTH, MIX_WIDTH, D_MODEL), f32) * MIX_WIDTH ** -0.5
    g_pre = 1.0 + 0.1 * jax.random.normal(ks[6], (DEPTH, D_MODEL), f32)
    g_post = 1.0 + 0.1 * jax.random.normal(ks[7], (DEPTH, D_MODEL), f32)
    g_mem = 1.0 + 0.1 * jax.random.normal(ks[8], (DEPTH, D_MODEL), f32)
    rel_bias = 0.5 * jax.random.normal(ks[9], (N_HEADS_MOBA, T5_BUCKETS), f32)
    return {"x": x, "mem": mem, "w_in": w_in, "b_forget": b_forget, "w_mem_kv": w_mem_kv,
            "w_out": w_out, "g_pre": g_pre, "g_post": g_post, "g_mem": g_mem,
            "rel_bias": rel_bias}


def reference(x, mem, w_in, b_forget, w_mem_kv, w_out, g_pre, g_post, g_mem, rel_bias):
    for layer in range(DEPTH):
        h = rms_norm(x, g_pre[layer])
        proj = jnp.einsum('bsd,de->bse', h, w_in[layer])
        qa, ka, va, ga, qb, kb, vb, gb, fb, qm, gm = jnp.split(proj, in_splits(), axis=-1)
        out_a = moba_attention(split_heads(qa, N_HEADS_MOBA), split_heads(ka, N_HEADS_MOBA),
                               split_heads(va, N_HEADS_MOBA), rel_bias)
        log_f = jax.nn.log_sigmoid((fb + b_forget[layer]).astype(jnp.float32)).transpose(0, 2, 1)
        out_b = forgetting_attention(split_heads(qb, N_HEADS_FOX), split_heads(kb, N_HEADS_FOX),
                                     split_heads(vb, N_HEADS_FOX), log_f)
        mkv = jnp.einsum('bmd,de->bme', rms_norm(mem, g_mem[layer]), w_mem_kv[layer])
        km, vm = jnp.split(mkv, 2, axis=-1)
        out_m = memory_attention(split_heads(qm, N_HEADS_MEM), split_heads(km, N_HEADS_MEM),
                                 split_heads(vm, N_HEADS_MEM))
        y = jnp.concatenate([merge_heads(out_a) * jax.nn.silu(ga),
                             merge_heads(out_b) * jax.nn.silu(gb),
                             merge_heads(out_m) * jax.nn.silu(gm)], axis=-1)
        o = jnp.einsum('bse,ed->bsd', y, w_out[layer])
        x = x + rms_norm(o, g_post[layer])
    return x
```

```python
import functools
import math

import jax
import jax.numpy as jnp
from jax import lax
from jax.experimental import pallas as pl
from jax.experimental.pallas import tpu as pltpu

HEAD_DIM = 128
N_HEADS_MOBA = 6
N_HEADS_FOX = 6
N_HEADS_MEM = 4
MOBA_BLOCK = 256
MOBA_TOPK = 3
T5_BUCKETS = 32
T5_MAX_DIST = 128
EPS = 1e-6
NEG = -1e30

LANES = 128
KV_BLOCK = MOBA_BLOCK
VMEM_LIMIT_BYTES = 56 * 1024 * 1024

F32 = jnp.float32
BF16 = jnp.bfloat16
_NT = (((1,), (1,)), ((), ()))


def _cparams(n_axes):
    return pltpu.CompilerParams(dimension_semantics=("arbitrary",) * n_axes,
                                vmem_limit_bytes=VMEM_LIMIT_BYTES)


def _norm_proj_kernel(*refs, n_col_blocks, with_gate_logits, norm_chunk):
    if with_gate_logits:
        x_ref, g_ref, w_ref, wf_ref, p_ref, f_ref, h_ref = refs
    else:
        x_ref, g_ref, w_ref, p_ref, h_ref = refs
    tm = x_ref.shape[0]

    @pl.when(pl.program_id(1) == 0)
    def _normalize():
        def body(c, carry):
            rows = pl.ds(pl.multiple_of(c * norm_chunk, norm_chunk), norm_chunk)
            xf = x_ref[rows, :]
            ms = jnp.mean(xf * xf, axis=-1, keepdims=True)
            h_ref[rows, :] = (xf * lax.rsqrt(ms + EPS) * g_ref[...]).astype(BF16)
            return carry
        lax.fori_loop(0, tm // norm_chunk, body, 0)
        if with_gate_logits:
            f_ref[...] = jnp.dot(h_ref[...], wf_ref[...], preferred_element_type=F32)

    acc = jnp.dot(h_ref[...], w_ref[...], preferred_element_type=F32)
    for c in range(n_col_blocks):
        p_ref[c] = acc[:, c * LANES:(c + 1) * LANES].astype(BF16)


def _norm_proj(x2, g, w, wf, *, tm, tn):
    m, d = x2.shape
    n = w.shape[1]
    ncb = tn // LANES
    with_f = wf is not None
    in_specs = [pl.BlockSpec((tm, d), lambda i, j: (i, 0)),
                pl.BlockSpec((1, d), lambda i, j: (0, 0)),
                pl.BlockSpec((d, tn), lambda i, j: (0, j))]
    out_shape = [jax.ShapeDtypeStruct((n // LANES, m, LANES), BF16)]
    out_specs = [pl.BlockSpec((ncb, tm, LANES), lambda i, j: (j, i, 0))]
    args = [x2, g, w]
    if with_f:
        in_specs.append(pl.BlockSpec((d, LANES), lambda i, j: (0, 0)))
        out_shape.append(jax.ShapeDtypeStruct((m, LANES), F32))
        out_specs.append(pl.BlockSpec((tm, LANES), lambda i, j: (i, 0)))
        args.append(wf)
    res = pl.pallas_call(
        functools.partial(_norm_proj_kernel, n_col_blocks=ncb, with_gate_logits=with_f,
                          norm_chunk=min(tm, 256)),
        grid=(m // tm, n // tn),
        in_specs=in_specs, out_specs=out_specs, out_shape=out_shape,
        scratch_shapes=[pltpu.VMEM((tm, d), BF16)],
        compiler_params=_cparams(2),
        name="norm_proj_gate" if with_f else "norm_proj",
    )(*args)
    return res if with_f else (res[0], None)


def _forget_scan_kernel(f_ref, b_ref, c_ref):
    z = f_ref[...] + b_ref[...]
    logf = jnp.minimum(z, 0.0) - jnp.log1p(jnp.exp(-jnp.abs(z)))
    x = logf.T[0:8, :]
    s_len = x.shape[1]
    lane = lax.broadcasted_iota(jnp.int32, x.shape, 1)
    shift = 1
    while shift < s_len:
        x = x + jnp.where(lane >= shift, pltpu.roll(x, shift, 1), 0.0)
        shift *= 2
    c_ref[...] = x


def _forget_scan(fb, bvec, batch, seq):
    return pl.pallas_call(
        _forget_scan_kernel,
        grid=(batch,),
        in_specs=[pl.BlockSpec((seq, LANES), lambda b: (b, 0)),
                  pl.BlockSpec((1, LANES), lambda b: (0, 0))],
        out_specs=pl.BlockSpec((None, 8, seq), lambda b: (b, 0, 0)),
        out_shape=jax.ShapeDtypeStruct((batch, 8, seq), F32),
        compiler_params=_cparams(1),
        name="forget_scan",
    )(fb, bvec)


def _softmax_init(s, vb, m_ref, l_ref, acc_ref):
    m = jnp.max(s, axis=1, keepdims=True)
    p = jnp.exp(s - m)
    m_ref[...] = m
    l_ref[...] = jnp.sum(p, axis=1, keepdims=True)
    acc_ref[...] = jnp.dot(p.astype(BF16), vb, preferred_element_type=F32)


def _softmax_update(s, vb, m_ref, l_ref, acc_ref):
    m_old = m_ref[...]
    m_new = jnp.maximum(m_old, jnp.max(s, axis=1, keepdims=True))
    alpha = jnp.exp(m_old - m_new)
    p = jnp.exp(s - m_new)
    m_ref[...] = m_new
    l_ref[...] = alpha * l_ref[...] + jnp.sum(p, axis=1, keepdims=True)
    acc_ref[...] = alpha * acc_ref[...] + jnp.dot(p.astype(BF16), vb, preferred_element_type=F32)


def _gated_output(g_ref, l_ref, acc_ref):
    g = g_ref[...].astype(F32)
    silu = g * (1.0 / (1.0 + jnp.exp(-g)))
    return (acc_ref[...] / l_ref[...] * silu).astype(BF16)


def _kv_rows(j):
    return pl.ds(pl.multiple_of(j * KV_BLOCK, KV_BLOCK), KV_BLOCK)


def _t5_bias(rb_ref, h, dist):
    max_exact = T5_BUCKETS // 2
    n = jnp.maximum(dist, 0)
    nf = jnp.maximum(n, 1).astype(F32)
    large = max_exact + (jnp.log(nf / max_exact) / math.log(T5_MAX_DIST / max_exact)
                         * (T5_BUCKETS - max_exact)).astype(jnp.int32)
    large = jnp.minimum(large, T5_BUCKETS - 1)
    bucket = jnp.where(n < max_exact, n, large)
    bias = jnp.zeros(dist.shape, F32)
    for kk in range(T5_BUCKETS):
        bias = jnp.where(bucket == kk, rb_ref[h, kk], bias)
    return bias


def _moba_kernel(rb_ref, q_ref, k_ref, v_ref, g_ref, o_ref,
                 kaug_ref, kmean_ref, bown_ref, bprev_ref, m_ref, l_ref, acc_ref, *, n_blocks):
    h = pl.program_id(0)
    b = pl.program_id(1)
    i = pl.program_id(2)
    scale = HEAD_DIM ** -0.5
    blk = MOBA_BLOCK
    seq = n_blocks * blk

    @pl.when((h == 0) & (b == 0) & (i == 0))
    def _block_indicator():
        srow = lax.broadcasted_iota(jnp.int32, (seq, LANES), 0)
        col = lax.broadcasted_iota(jnp.int32, (seq, LANES), 1)
        kaug_ref[:, LANES:] = jnp.where(col * blk == srow - (srow & (blk - 1)), 1.0, 0.0).astype(BF16)

    @pl.when((b == 0) & (i == 0))
    def _bias_tiles():
        r = lax.broadcasted_iota(jnp.int32, (blk, blk), 0)
        c = lax.broadcasted_iota(jnp.int32, (blk, blk), 1)
        dist = r - c
        bown_ref[...] = jnp.where(dist >= 0, _t5_bias(rb_ref, h, dist), NEG)
        bprev_ref[...] = _t5_bias(rb_ref, h, dist + blk)

    @pl.when(i == 0)
    def _keys():
        kaug_ref[:, :LANES] = k_ref[...]
        row8 = lax.broadcasted_iota(jnp.int32, (8, LANES), 0)
        km = jnp.zeros((8, LANES), F32)
        for j in range(n_blocks):
            mean_j = jnp.mean(k_ref[j * blk:(j + 1) * blk, :].astype(F32), axis=0, keepdims=True)
            km = jnp.where(row8 == j, mean_j, km)
        kmean_ref[...] = jnp.concatenate([km, jnp.zeros((LANES - 8, LANES), F32)], axis=0).astype(BF16)

    q = q_ref[...]
    gate_t = lax.dot_general(kmean_ref[...], q, _NT, preferred_element_type=F32)[0:8, :]
    jrow = lax.broadcasted_iota(jnp.int32, gate_t.shape, 0)
    gm = jnp.where(jrow < i, gate_t, NEG)
    sel_bias = jnp.zeros(gate_t.shape, F32)
    for j in range(n_blocks):
        gj = gm[j:j + 1, :]
        beats = (gm > gj) | ((gm == gj) & (jrow < j))
        rank = jnp.sum(beats.astype(F32), axis=0, keepdims=True)
        k_eff = jnp.where(j < i, float(MOBA_TOPK), 0.0)
        sel_bias = jnp.where(jrow == j, jnp.where(rank < k_eff, 0.0, NEG), sel_bias)
    sel_bias = jnp.where(jrow == i, 0.0, sel_bias)
    sel_cols = jnp.concatenate([sel_bias, jnp.zeros((LANES - 8, blk), F32)], axis=0).T
    q_aug = jnp.concatenate([q, sel_cols.astype(BF16)], axis=1)

    def scores(j):
        return lax.dot_general(q_aug, kaug_ref[_kv_rows(j), :], _NT, preferred_element_type=F32) * scale

    _softmax_init(scores(i) + bown_ref[...], v_ref[_kv_rows(i), :], m_ref, l_ref, acc_ref)

    @pl.when(i >= 1)
    def _prev_block():
        _softmax_update(scores(i - 1) + bprev_ref[...], v_ref[_kv_rows(i - 1), :], m_ref, l_ref, acc_ref)

    far_bias = rb_ref[h, T5_BUCKETS - 1]

    def far_block(j, carry):
        _softmax_update(scores(j) + far_bias, v_ref[_kv_rows(j), :], m_ref, l_ref, acc_ref)
        return carry
    lax.fori_loop(0, i - 1, far_block, 0)

    o_ref[...] = _gated_output(g_ref, l_ref, acc_ref)


def _moba_attention(p, rel_bias, batch, seq, q0, k0, v0, g0):
    nb = seq // MOBA_BLOCK
    assert seq % MOBA_BLOCK == 0 and MOBA_TOPK < 8 and nb <= 8 and MOBA_BLOCK > T5_MAX_DIST
    blk = MOBA_BLOCK
    tile = lambda off: pl.BlockSpec((None, blk, LANES), lambda h, b, i: (off + h, b * nb + i, 0))
    full = lambda off: pl.BlockSpec((None, seq, LANES), lambda h, b, i: (off + h, b, 0))
    return pl.pallas_call(
        functools.partial(_moba_kernel, n_blocks=nb),
        grid=(N_HEADS_MOBA, batch, nb),
        in_specs=[pl.BlockSpec(memory_space=pltpu.SMEM), tile(q0), full(k0), full(v0), tile(g0)],
        out_specs=pl.BlockSpec((None, blk, LANES), lambda h, b, i: (h, b * nb + i, 0)),
        out_shape=jax.ShapeDtypeStruct((N_HEADS_MOBA, batch * seq, LANES), BF16),
        scratch_shapes=[pltpu.VMEM((seq, 2 * LANES), BF16),
                        pltpu.VMEM((LANES, LANES), BF16),
                        pltpu.VMEM((blk, blk), F32),
                        pltpu.VMEM((blk, blk), F32),
                        pltpu.VMEM((blk, 1), F32), pltpu.VMEM((blk, 1), F32),
                        pltpu.VMEM((blk, LANES), F32)],
        compiler_params=_cparams(3),
        name="moba_attention",
    )(rel_bias, p, p, p, p)


def _fox_kernel(q_ref, k_ref, v_ref, g_ref, c_ref, o_ref, m_ref, l_ref, acc_ref):
    i = pl.program_id(2)
    scale = HEAD_DIM ** -0.5
    q = q_ref[...]

    def scores(j):
        a = lax.dot_general(q, k_ref[_kv_rows(j), :], _NT, preferred_element_type=F32)
        return a * scale - c_ref[j]

    r = lax.broadcasted_iota(jnp.int32, (KV_BLOCK, KV_BLOCK), 0)
    c = lax.broadcasted_iota(jnp.int32, (KV_BLOCK, KV_BLOCK), 1)
    _softmax_init(jnp.where(c <= r, scores(i), NEG), v_ref[_kv_rows(i), :], m_ref, l_ref, acc_ref)

    def past_block(j, carry):
        _softmax_update(scores(j), v_ref[_kv_rows(j), :], m_ref, l_ref, acc_ref)
        return carry
    lax.fori_loop(0, i, past_block, 0)

    o_ref[...] = _gated_output(g_ref, l_ref, acc_ref)


def _fox_attention(p, c, batch, seq, q0, k0, v0, g0):
    nt = seq // KV_BLOCK
    tile = lambda off: pl.BlockSpec((None, KV_BLOCK, LANES), lambda b, h, i: (off + h, b * nt + i, 0))
    full = lambda off: pl.BlockSpec((None, seq, LANES), lambda b, h, i: (off + h, b, 0))
    return pl.pallas_call(
        _fox_kernel,
        grid=(batch, N_HEADS_FOX, nt),
        in_specs=[tile(q0), full(k0), full(v0), tile(g0),
                  pl.BlockSpec((None, nt, 1, KV_BLOCK), lambda b, h, i: (b * N_HEADS_FOX + h, 0, 0, 0))],
        out_specs=pl.BlockSpec((None, KV_BLOCK, LANES), lambda b, h, i: (h, b * nt + i, 0)),
        out_shape=jax.ShapeDtypeStruct((N_HEADS_FOX, batch * seq, LANES), BF16),
        scratch_shapes=[pltpu.VMEM((KV_BLOCK, 1), F32), pltpu.VMEM((KV_BLOCK, 1), F32),
                        pltpu.VMEM((KV_BLOCK, LANES), F32)],
        compiler_params=_cparams(3),
        name="fox_attention",
    )(p, p, p, p, c)


def _mem_kernel(q_ref, k_ref, v_ref, g_ref, o_ref, m_ref, l_ref, acc_ref):
    scale = HEAD_DIM ** -0.5
    s = lax.dot_general(q_ref[...], k_ref[...], _NT, preferred_element_type=F32) * scale
    _softmax_init(s, v_ref[...], m_ref, l_ref, acc_ref)
    o_ref[...] = _gated_output(g_ref, l_ref, acc_ref)


def _mem_attention(p, mkv, batch, seq, n_mem, q0, g0, tq):
    nt = seq // tq
    tile = lambda off: pl.BlockSpec((None, tq, LANES), lambda b, h, i: (off + h, b * nt + i, 0))
    kv = lambda off: pl.BlockSpec((None, n_mem, LANES), lambda b, h, i: (off + h, b, 0))
    return pl.pallas_call(
        _mem_kernel,
        grid=(batch, N_HEADS_MEM, nt),
        in_specs=[tile(q0), kv(0), kv(N_HEADS_MEM), tile(g0)],
        out_specs=pl.BlockSpec((None, tq, LANES), lambda b, h, i: (h, b * nt + i, 0)),
        out_shape=jax.ShapeDtypeStruct((N_HEADS_MEM, batch * seq, LANES), BF16),
        scratch_shapes=[pltpu.VMEM((tq, 1), F32), pltpu.VMEM((tq, 1), F32),
                        pltpu.VMEM((tq, LANES), F32)],
        compiler_params=_cparams(3),
        name="mem_attention",
    )(p, mkv, mkv, p)


def _out_proj_kernel(ya_ref, yb_ref, ym_ref, w_ref, x_ref, g_ref, o_ref, y_ref):
    col = 0
    for src in (ya_ref, yb_ref, ym_ref):
        for c in range(src.shape[0]):
            y_ref[:, col * LANES:(col + 1) * LANES] = src[c]
            col += 1
    o = jnp.dot(y_ref[...], w_ref[...], preferred_element_type=F32)
    ms = jnp.mean(o * o, axis=-1, keepdims=True)
    o_ref[...] = x_ref[...] + o * lax.rsqrt(ms + EPS) * g_ref[...]


def _out_proj(ya, yb, ym, w, x2, g, *, tm):
    m, d = x2.shape
    heads = lambda y: pl.BlockSpec((y.shape[0], tm, LANES), lambda i: (0, i, 0))
    return pl.pallas_call(
        _out_proj_kernel,
        grid=(m // tm,),
        in_specs=[heads(ya), heads(yb), heads(ym),
                  pl.BlockSpec(w.shape, lambda i: (0, 0)),
                  pl.BlockSpec((tm, d), lambda i: (i, 0)),
                  pl.BlockSpec((1, d), lambda i: (0, 0))],
        out_specs=pl.BlockSpec((tm, d), lambda i: (i, 0)),
        out_shape=jax.ShapeDtypeStruct((m, d), F32),
        scratch_shapes=[pltpu.VMEM((tm, w.shape[0]), BF16)],
        compiler_params=_cparams(1),
        name="out_proj",
    )(ya, yb, ym, w, x2, g)


def kernel(x, mem, w_in, b_forget, w_mem_kv, w_out, g_pre, g_post, g_mem, rel_bias):
    batch, seq, d = x.shape
    n_mem = mem.shape[1]
    depth = w_in.shape[0]
    wa, wb, wm = N_HEADS_MOBA * HEAD_DIM, N_HEADS_FOX * HEAD_DIM, N_HEADS_MEM * HEAD_DIM
    f0 = 4 * wa + 4 * wb
    f1 = f0 + N_HEADS_FOX
    ha, hb, hm = N_HEADS_MOBA, N_HEADS_FOX, N_HEADS_MEM
    qa0, ka0, va0, ga0 = 0, ha, 2 * ha, 3 * ha
    qb0, kb0, vb0, gb0 = 4 * ha, 4 * ha + hb, 4 * ha + 2 * hb, 4 * ha + 3 * hb
    qm0, gm0 = 4 * ha + 4 * hb, 4 * ha + 4 * hb + hm

    x2 = x.reshape(batch * seq, d)
    mem2 = mem.reshape(batch * n_mem, d)
    for layer in range(depth):
        w = w_in[layer]
        w_main = jnp.concatenate([w[:, :f0], w[:, f1:]], axis=1).astype(BF16)
        w_f = jnp.pad(w[:, f0:f1], ((0, 0), (0, LANES - N_HEADS_FOX))).astype(BF16)
        b_vec = jnp.pad(b_forget[layer], (0, LANES - N_HEADS_FOX)).reshape(1, LANES)

        p, fb = _norm_proj(x2, g_pre[layer].reshape(1, d), w_main, w_f, tm=1024, tn=1024)
        mkv, _ = _norm_proj(mem2, g_mem[layer].reshape(1, d), w_mem_kv[layer].astype(BF16), None,
                            tm=batch * n_mem, tn=2 * wm)
        c = _forget_scan(fb, b_vec, batch, seq)
        c = c[:, :N_HEADS_FOX].reshape(batch * N_HEADS_FOX, seq // KV_BLOCK, 1, KV_BLOCK)

        ya = _moba_attention(p, rel_bias, batch, seq, qa0, ka0, va0, ga0)
        yb = _fox_attention(p, c, batch, seq, qb0, kb0, vb0, gb0)
        ym = _mem_attention(p, mkv, batch, seq, n_mem, qm0, gm0, tq=1024)
        x2 = _out_proj(ya, yb, ym, w_out[layer].astype(BF16), x2, g_post[layer].reshape(1, d), tm=512)
    return x2.reshape(batch, seq, d)
```

```python
import functools
import math

import jax
import jax.numpy as jnp
from jax import lax
from jax.experimental import pallas as pl
from jax.experimental.pallas import tpu as pltpu

HEAD_DIM = 128
N_HEADS_MOBA = 6
N_HEADS_FOX = 6
N_HEADS_MEM = 4
MOBA_BLOCK = 256
MOBA_TOPK = 3
T5_BUCKETS = 32
T5_MAX_DIST = 128
EPS = 1e-6
NEG = -1e30

LANES = 128
Q_TILE = MOBA_BLOCK
MEM_Q_TILE = 512
VMEM_LIMIT_BYTES = 56 * 1024 * 1024

F32 = jnp.float32
BF16 = jnp.bfloat16
_NT = (((1,), (1,)), ((), ()))


def _cparams(n_axes):
    return pltpu.CompilerParams(dimension_semantics=("arbitrary",) * n_axes,
                                vmem_limit_bytes=VMEM_LIMIT_BYTES)


def _norm_proj_kernel(*refs, n_col_blocks, with_gate_logits, norm_chunk):
    if with_gate_logits:
        x_ref, g_ref, w_ref, wf_ref, p_ref, f_ref, h_ref = refs
    else:
        x_ref, g_ref, w_ref, p_ref, h_ref = refs
    tm = x_ref.shape[0]

    @pl.when(pl.program_id(1) == 0)
    def _normalize():
        def body(c, carry):
            rows = pl.ds(pl.multiple_of(c * norm_chunk, norm_chunk), norm_chunk)
            xf = x_ref[rows, :]
            ms = jnp.mean(xf * xf, axis=-1, keepdims=True)
            h_ref[rows, :] = (xf * lax.rsqrt(ms + EPS) * g_ref[...]).astype(BF16)
            return carry
        lax.fori_loop(0, tm // norm_chunk, body, 0)
        if with_gate_logits:
            f_ref[...] = jnp.dot(h_ref[...], wf_ref[...], preferred_element_type=F32)

    acc = jnp.dot(h_ref[...], w_ref[...], preferred_element_type=F32)
    for c in range(n_col_blocks):
        p_ref[c] = acc[:, c * LANES:(c + 1) * LANES].astype(BF16)


def _norm_proj(x2, g, w, wf, *, tm, tn):
    m, d = x2.shape
    n = w.shape[1]
    ncb = tn // LANES
    with_f = wf is not None
    in_specs = [pl.BlockSpec((tm, d), lambda i, j: (i, 0)),
                pl.BlockSpec((1, d), lambda i, j: (0, 0)),
                pl.BlockSpec((d, tn), lambda i, j: (0, j))]
    out_shape = [jax.ShapeDtypeStruct((n // LANES, m, LANES), BF16)]
    out_specs = [pl.BlockSpec((ncb, tm, LANES), lambda i, j: (j, i, 0))]
    args = [x2, g, w]
    if with_f:
        in_specs.append(pl.BlockSpec((d, LANES), lambda i, j: (0, 0)))
        out_shape.append(jax.ShapeDtypeStruct((m, LANES), F32))
        out_specs.append(pl.BlockSpec((tm, LANES), lambda i, j: (i, 0)))
        args.append(wf)
    res = pl.pallas_call(
        functools.partial(_norm_proj_kernel, n_col_blocks=ncb, with_gate_logits=with_f,
                          norm_chunk=min(tm, 256)),
        grid=(m // tm, n // tn),
        in_specs=in_specs, out_specs=out_specs, out_shape=out_shape,
        scratch_shapes=[pltpu.VMEM((tm, d), BF16)],
        compiler_params=_cparams(2),
        name="norm_proj_gate" if with_f else "norm_proj",
    )(*args)
    return res if with_f else (res[0], None)


def _forget_scan_kernel(f_ref, b_ref, c_ref):
    z = f_ref[...] + b_ref[...]
    logf = jnp.minimum(z, 0.0) - jnp.log1p(jnp.exp(-jnp.abs(z)))
    x = logf.T[0:8, :]
    s_len = x.shape[1]
    lane = lax.broadcasted_iota(jnp.int32, x.shape, 1)
    shift = 1
    while shift < s_len:
        x = x + jnp.where(lane >= shift, pltpu.roll(x, shift, 1), 0.0)
        shift *= 2
    c_ref[...] = x


def _forget_scan(fb, bvec, batch, seq):
    return pl.pallas_call(
        _forget_scan_kernel,
        grid=(batch,),
        in_specs=[pl.BlockSpec((seq, LANES), lambda b: (b, 0)),
                  pl.BlockSpec((1, LANES), lambda b: (0, 0))],
        out_specs=pl.BlockSpec((None, 8, seq), lambda b: (b, 0, 0)),
        out_shape=jax.ShapeDtypeStruct((batch, 8, seq), F32),
        compiler_params=_cparams(1),
        name="forget_scan",
    )(fb, bvec)


def _softmax_pv(pieces, v, g):
    m = functools.reduce(jnp.maximum, [jnp.max(s, axis=1, keepdims=True) for s in pieces])
    ps = [jnp.exp(s - m) for s in pieces]
    l = functools.reduce(jnp.add, [jnp.sum(p, axis=1, keepdims=True) for p in ps])
    p = ps[0] if len(ps) == 1 else jnp.concatenate(ps, axis=1)
    acc = jnp.dot(p.astype(BF16), v, preferred_element_type=F32)
    gf = g.astype(F32)
    silu = gf * (1.0 / (1.0 + jnp.exp(-gf)))
    return (acc / l * silu).astype(BF16)


def _t5_bias(rb_ref, h, dist):
    max_exact = T5_BUCKETS // 2
    n = jnp.maximum(dist, 0)
    nf = jnp.maximum(n, 1).astype(F32)
    large = max_exact + (jnp.log(nf / max_exact) / math.log(T5_MAX_DIST / max_exact)
                         * (T5_BUCKETS - max_exact)).astype(jnp.int32)
    large = jnp.minimum(large, T5_BUCKETS - 1)
    bucket = jnp.where(n < max_exact, n, large)
    bias = jnp.zeros(dist.shape, F32)
    for kk in range(T5_BUCKETS):
        bias = jnp.where(bucket == kk, rb_ref[h, kk], bias)
    return bias


def _moba_kernel(rb_ref, q_ref, k_ref, v_ref, g_ref, o_ref,
                 kaug_ref, qaug_ref, kmean_ref, bown_ref, bprev_ref, *, n_blocks):
    h = pl.program_id(0)
    b = pl.program_id(1)
    scale = HEAD_DIM ** -0.5
    blk = MOBA_BLOCK
    seq = n_blocks * blk

    @pl.when((h == 0) & (b == 0))
    def _block_indicator():
        srow = lax.broadcasted_iota(jnp.int32, (seq, LANES), 0)
        col = lax.broadcasted_iota(jnp.int32, (seq, LANES), 1)
        kaug_ref[:, LANES:] = jnp.where(col * blk == srow - (srow & (blk - 1)), 1.0, 0.0).astype(BF16)

    @pl.when(b == 0)
    def _bias_tiles():
        r = lax.broadcasted_iota(jnp.int32, (blk, blk), 0)
        c = lax.broadcasted_iota(jnp.int32, (blk, blk), 1)
        dist = r - c
        bown_ref[...] = jnp.where(dist >= 0, _t5_bias(rb_ref, h, dist), NEG)
        bprev_ref[...] = _t5_bias(rb_ref, h, dist + blk)

    kaug_ref[:, :LANES] = k_ref[...]
    row8 = lax.broadcasted_iota(jnp.int32, (8, LANES), 0)
    km = jnp.zeros((8, LANES), F32)
    for j in range(n_blocks):
        mean_j = jnp.mean(k_ref[j * blk:(j + 1) * blk, :].astype(F32), axis=0, keepdims=True)
        km = jnp.where(row8 == j, mean_j, km)
    kmean_ref[...] = jnp.concatenate([km, jnp.zeros((LANES - 8, LANES), F32)], axis=0).astype(BF16)

    gate_t = lax.dot_general(kmean_ref[...], q_ref[...], _NT, preferred_element_type=F32)[0:8, :]
    jrow = lax.broadcasted_iota(jnp.int32, gate_t.shape, 0)
    qpos = lax.broadcasted_iota(jnp.int32, gate_t.shape, 1)
    qblk = jnp.right_shift(qpos, int(math.log2(blk)))
    gm = jnp.where(jrow < qblk, gate_t, NEG)
    sel_bias = jnp.zeros(gate_t.shape, F32)
    for j in range(n_blocks):
        gj = gm[j:j + 1, :]
        beats = (gm > gj) | ((gm == gj) & (jrow < j))
        rank = jnp.sum(beats.astype(F32), axis=0, keepdims=True)
        allowed = ((rank < MOBA_TOPK) & (qblk[0:1, :] > j)) | (qblk[0:1, :] == j)
        sel_bias = jnp.where(jrow == j, jnp.where(allowed, 0.0, NEG), sel_bias)
    sel_cols = jnp.concatenate([sel_bias, jnp.zeros((LANES - 8, seq), F32)], axis=0).T
    qaug_ref[:, :LANES] = q_ref[...]
    qaug_ref[:, LANES:] = sel_cols.astype(BF16)

    far_bias = rb_ref[h, T5_BUCKETS - 1]
    for i in range(n_blocks):
        rows = slice(i * blk, (i + 1) * blk)
        n = (i + 1) * blk
        s = lax.dot_general(qaug_ref[rows, :], kaug_ref[0:n, :], _NT, preferred_element_type=F32) * scale
        pieces = [s[:, n - blk:] + bown_ref[...]]
        if i >= 1:
            pieces.insert(0, s[:, n - 2 * blk:n - blk] + bprev_ref[...])
        if i >= 2:
            pieces.insert(0, s[:, :n - 2 * blk] + far_bias)
        o_ref[rows, :] = _softmax_pv(pieces, v_ref[0:n, :], g_ref[rows, :])


def _moba_attention(p, rel_bias, batch, seq, q0, k0, v0, g0):
    nb = seq // MOBA_BLOCK
    assert seq % MOBA_BLOCK == 0 and MOBA_TOPK < 8 and nb <= 8 and MOBA_BLOCK > T5_MAX_DIST
    blk = MOBA_BLOCK
    full = lambda off: pl.BlockSpec((None, seq, LANES), lambda h, b: (off + h, b, 0))
    return pl.pallas_call(
        functools.partial(_moba_kernel, n_blocks=nb),
        grid=(N_HEADS_MOBA, batch),
        in_specs=[pl.BlockSpec(memory_space=pltpu.SMEM), full(q0), full(k0), full(v0), full(g0)],
        out_specs=pl.BlockSpec((None, seq, LANES), lambda h, b: (h, b, 0)),
        out_shape=jax.ShapeDtypeStruct((N_HEADS_MOBA, batch * seq, LANES), BF16),
        scratch_shapes=[pltpu.VMEM((seq, 2 * LANES), BF16),
                        pltpu.VMEM((seq, 2 * LANES), BF16),
                        pltpu.VMEM((LANES, LANES), BF16),
                        pltpu.VMEM((blk, blk), F32),
                        pltpu.VMEM((blk, blk), F32)],
        compiler_params=_cparams(2),
        name="moba_attention",
    )(rel_bias, p, p, p, p)


def _fox_kernel(q_ref, k_ref, v_ref, g_ref, c_ref, o_ref):
    scale = HEAD_DIM ** -0.5
    seq = q_ref.shape[0]
    r = lax.broadcasted_iota(jnp.int32, (Q_TILE, Q_TILE), 0)
    c = lax.broadcasted_iota(jnp.int32, (Q_TILE, Q_TILE), 1)
    for i in range(seq // Q_TILE):
        rows = slice(i * Q_TILE, (i + 1) * Q_TILE)
        n = (i + 1) * Q_TILE
        a = lax.dot_general(q_ref[rows, :], k_ref[0:n, :], _NT, preferred_element_type=F32)
        s = a * scale - c_ref[:, 0:n]
        pieces = [jnp.where(c <= r, s[:, n - Q_TILE:], NEG)]
        if i >= 1:
            pieces.insert(0, s[:, :n - Q_TILE])
        o_ref[rows, :] = _softmax_pv(pieces, v_ref[0:n, :], g_ref[rows, :])


def _fox_attention(p, c, batch, seq, q0, k0, v0, g0):
    full = lambda off: pl.BlockSpec((None, seq, LANES), lambda b, h: (off + h, b, 0))
    return pl.pallas_call(
        _fox_kernel,
        grid=(batch, N_HEADS_FOX),
        in_specs=[full(q0), full(k0), full(v0), full(g0),
                  pl.BlockSpec((None, 1, seq), lambda b, h: (b * N_HEADS_FOX + h, 0, 0))],
        out_specs=pl.BlockSpec((None, seq, LANES), lambda b, h: (h, b, 0)),
        out_shape=jax.ShapeDtypeStruct((N_HEADS_FOX, batch * seq, LANES), BF16),
        compiler_params=_cparams(2),
        name="fox_attention",
    )(p, p, p, p, c)


def _mem_kernel(q_ref, k_ref, v_ref, g_ref, o_ref):
    scale = HEAD_DIM ** -0.5
    n_heads, seq, _ = q_ref.shape
    for h in range(n_heads):
        for i in range(seq // MEM_Q_TILE):
            rows = slice(i * MEM_Q_TILE, (i + 1) * MEM_Q_TILE)
            s = lax.dot_general(q_ref[h, rows, :], k_ref[h], _NT, preferred_element_type=F32) * scale
            o_ref[h, rows, :] = _softmax_pv([s], v_ref[h], g_ref[h, rows, :])


def _mem_attention(p, mkv, batch, seq, n_mem, q0, g0):
    nh = N_HEADS_MEM
    assert q0 % nh == 0 and g0 % nh == 0
    heads = lambda off: pl.BlockSpec((nh, seq, LANES), lambda b: (off // nh, b, 0))
    kv = lambda blk: pl.BlockSpec((nh, n_mem, LANES), lambda b: (blk, b, 0))
    return pl.pallas_call(
        _mem_kernel,
        grid=(batch,),
        in_specs=[heads(q0), kv(0), kv(1), heads(g0)],
        out_specs=pl.BlockSpec((nh, seq, LANES), lambda b: (0, b, 0)),
        out_shape=jax.ShapeDtypeStruct((nh, batch * seq, LANES), BF16),
        compiler_params=_cparams(1),
        name="mem_attention",
    )(p, mkv, mkv, p)


def _out_proj_kernel(ya_ref, yb_ref, ym_ref, w_ref, x_ref, g_ref, o_ref, y_ref):
    col = 0
    for src in (ya_ref, yb_ref, ym_ref):
        for c in range(src.shape[0]):
            y_ref[:, col * LANES:(col + 1) * LANES] = src[c]
            col += 1
    o = jnp.dot(y_ref[...], w_ref[...], preferred_element_type=F32)
    ms = jnp.mean(o * o, axis=-1, keepdims=True)
    o_ref[...] = x_ref[...] + o * lax.rsqrt(ms + EPS) * g_ref[...]


def _out_proj(ya, yb, ym, w, x2, g, *, tm):
    m, d = x2.shape
    heads = lambda y: pl.BlockSpec((y.shape[0], tm, LANES), lambda i: (0, i, 0))
    return pl.pallas_call(
        _out_proj_kernel,
        grid=(m // tm,),
        in_specs=[heads(ya), heads(yb), heads(ym),
                  pl.BlockSpec(w.shape, lambda i: (0, 0)),
                  pl.BlockSpec((tm, d), lambda i: (i, 0)),
                  pl.BlockSpec((1, d), lambda i: (0, 0))],
        out_specs=pl.BlockSpec((tm, d), lambda i: (i, 0)),
        out_shape=jax.ShapeDtypeStruct((m, d), F32),
        scratch_shapes=[pltpu.VMEM((tm, w.shape[0]), BF16)],
        compiler_params=_cparams(1),
        name="out_proj",
    )(ya, yb, ym, w, x2, g)


def kernel(x, mem, w_in, b_forget, w_mem_kv, w_out, g_pre, g_post, g_mem, rel_bias):
    batch, seq, d = x.shape
    n_mem = mem.shape[1]
    depth = w_in.shape[0]
    wa, wb, wm = N_HEADS_MOBA * HEAD_DIM, N_HEADS_FOX * HEAD_DIM, N_HEADS_MEM * HEAD_DIM
    f0 = 4 * wa + 4 * wb
    f1 = f0 + N_HEADS_FOX
    ha, hb, hm = N_HEADS_MOBA, N_HEADS_FOX, N_HEADS_MEM
    qa0, ka0, va0, ga0 = 0, ha, 2 * ha, 3 * ha
    qb0, kb0, vb0, gb0 = 4 * ha, 4 * ha + hb, 4 * ha + 2 * hb, 4 * ha + 3 * hb
    qm0, gm0 = 4 * ha + 4 * hb, 4 * ha + 4 * hb + hm

    x2 = x.reshape(batch * seq, d)
    mem2 = mem.reshape(batch * n_mem, d)
    for layer in range(depth):
        w = w_in[layer]
        w_main = jnp.concatenate([w[:, :f0], w[:, f1:]], axis=1).astype(BF16)
        w_f = jnp.pad(w[:, f0:f1], ((0, 0), (0, LANES - N_HEADS_FOX))).astype(BF16)
        b_vec = jnp.pad(b_forget[layer], (0, LANES - N_HEADS_FOX)).reshape(1, LANES)

        p, fb = _norm_proj(x2, g_pre[layer].reshape(1, d), w_main, w_f, tm=1024, tn=1024)
        mkv, _ = _norm_proj(mem2, g_mem[layer].reshape(1, d), w_mem_kv[layer].astype(BF16), None,
                            tm=batch * n_mem, tn=2 * wm)
        c = _forget_scan(fb, b_vec, batch, seq)
        c = c[:, :N_HEADS_FOX].reshape(batch * N_HEADS_FOX, 1, seq)

        ya = _moba_attention(p, rel_bias, batch, seq, qa0, ka0, va0, ga0)
        yb = _fox_attention(p, c, batch, seq, qb0, kb0, vb0, gb0)
        ym = _mem_attention(p, mkv, batch, seq, n_mem, qm0, g0=gm0)
        x2 = _out_proj(ya, yb, ym, w_out[layer].astype(BF16), x2, g_post[layer].reshape(1, d), tm=512)
    return x2.reshape(batch, seq, d)
```

```python
import functools
import math

import jax
import jax.numpy as jnp
from jax import lax
from jax.experimental import pallas as pl
from jax.experimental.pallas import tpu as pltpu

HEAD_DIM = 128
N_HEADS_MOBA = 6
N_HEADS_FOX = 6
N_HEADS_MEM = 4
MOBA_BLOCK = 256
MOBA_TOPK = 3
T5_BUCKETS = 32
T5_MAX_DIST = 128
EPS = 1e-6
NEG = -1e30

LANES = 128
Q_TILE = MOBA_BLOCK
MEM_Q_TILE = 512
VMEM_LIMIT_BYTES = 56 * 1024 * 1024

F32 = jnp.float32
BF16 = jnp.bfloat16
_NT = (((1,), (1,)), ((), ()))


def _cparams(n_axes):
    return pltpu.CompilerParams(dimension_semantics=("arbitrary",) * n_axes,
                                vmem_limit_bytes=VMEM_LIMIT_BYTES)


def _norm_proj_kernel(*refs, n_col_blocks, tail_shift, row_chunk):
    if tail_shift:
        x_ref, g_ref, w_ref, wl_ref, p_ref, f_ref, h_ref = refs
    else:
        x_ref, g_ref, w_ref, p_ref, h_ref = refs
    tm, d = x_ref.shape
    j = pl.program_id(1)
    last = pl.num_programs(1) - 1

    def chunk(c):
        return pl.ds(pl.multiple_of(c * row_chunk, row_chunk), row_chunk)

    @pl.when(j == 0)
    def _normalize():
        def body(c, carry):
            xf = x_ref[chunk(c), :]
            ms = jnp.mean(xf * xf, axis=-1, keepdims=True)
            h_ref[chunk(c), :] = (xf * lax.rsqrt(ms + EPS) * g_ref[...]).astype(BF16)
            return carry
        lax.fori_loop(0, tm // row_chunk, body, 0)

    acc = jnp.dot(h_ref[...], w_ref[...].astype(BF16), preferred_element_type=F32)

    def write_aligned():
        for c in range(n_col_blocks):
            p_ref[c] = acc[:, c * LANES:(c + 1) * LANES].astype(BF16)

    if not tail_shift:
        write_aligned()
        return

    pl.when(j < last)(write_aligned)

    @pl.when(j == last)
    def _shifted_tail():
        f_ref[...] = acc[:, :LANES]
        extra = jnp.dot(h_ref[...], wl_ref[...].astype(BF16), preferred_element_type=F32)
        keep = LANES - tail_shift
        lane = lax.broadcasted_iota(jnp.int32, (tm, LANES), 1)
        blocks = [acc[:, c * LANES:(c + 1) * LANES] for c in range(n_col_blocks)] + [extra]
        rolled = [pltpu.roll(blk, keep, 1) for blk in blocks]
        for c in range(n_col_blocks):
            p_ref[c] = jnp.where(lane < keep, rolled[c], rolled[c + 1]).astype(BF16)


def _norm_proj(x2, g, w3, layer, *, tm, tn, tail_shift=0):
    m, d = x2.shape
    n = w3.shape[2]
    ncb = tn // LANES
    n_tiles = (n - tail_shift) // tn
    assert (n - tail_shift) % tn == 0 and m % tm == 0
    in_specs = [pl.BlockSpec((tm, d), lambda i, j: (i, 0)),
                pl.BlockSpec((1, d), lambda i, j: (0, 0)),
                pl.BlockSpec((None, d, tn), lambda i, j: (layer, 0, j))]
    out_shape = [jax.ShapeDtypeStruct((n_tiles * ncb, m, LANES), BF16)]
    out_specs = [pl.BlockSpec((ncb, tm, LANES), lambda i, j: (j, i, 0))]
    args = [x2, g, w3]
    if tail_shift:
        in_specs.append(pl.BlockSpec((None, d, LANES), lambda i, j: (layer, 0, n_tiles * ncb)))
        out_shape.append(jax.ShapeDtypeStruct((m, LANES), F32))
        out_specs.append(pl.BlockSpec((tm, LANES), lambda i, j: (i, 0)))
        args.append(w3)
    res = pl.pallas_call(
        functools.partial(_norm_proj_kernel, n_col_blocks=ncb, tail_shift=tail_shift, row_chunk=256),
        grid=(m // tm, n_tiles),
        in_specs=in_specs, out_specs=out_specs, out_shape=out_shape,
        scratch_shapes=[pltpu.VMEM((tm, d), BF16)],
        compiler_params=_cparams(2),
        name="norm_proj_gate" if tail_shift else "norm_proj",
    )(*args)
    return res if tail_shift else (res[0], None)


def _forget_scan_kernel(f_ref, b_ref, c_ref):
    z = f_ref[...] + b_ref[...]
    logf = jnp.minimum(z, 0.0) - jnp.log1p(jnp.exp(-jnp.abs(z)))
    x = logf.T[0:8, :]
    s_len = x.shape[1]
    lane = lax.broadcasted_iota(jnp.int32, x.shape, 1)
    shift = 1
    while shift < s_len:
        x = x + jnp.where(lane >= shift, pltpu.roll(x, shift, 1), 0.0)
        shift *= 2
    c_ref[...] = x


def _forget_scan(fb, bvec, batch, seq):
    return pl.pallas_call(
        _forget_scan_kernel,
        grid=(batch,),
        in_specs=[pl.BlockSpec((seq, LANES), lambda b: (b, 0)),
                  pl.BlockSpec((1, LANES), lambda b: (0, 0))],
        out_specs=pl.BlockSpec((None, 8, seq), lambda b: (b, 0, 0)),
        out_shape=jax.ShapeDtypeStruct((batch, 8, seq), F32),
        compiler_params=_cparams(1),
        name="forget_scan",
    )(fb, bvec)


def _softmax_pv(pieces, v, g):
    m = functools.reduce(jnp.maximum, [jnp.max(s, axis=1, keepdims=True) for s in pieces])
    ps = [jnp.exp(s - m) for s in pieces]
    l = functools.reduce(jnp.add, [jnp.sum(p, axis=1, keepdims=True) for p in ps])
    p = ps[0] if len(ps) == 1 else jnp.concatenate(ps, axis=1)
    acc = jnp.dot(p.astype(BF16), v, preferred_element_type=F32)
    gf = g.astype(F32)
    silu = gf * (1.0 / (1.0 + jnp.exp(-gf)))
    return (acc / l * silu).astype(BF16)


def _t5_bias(rb_ref, h, dist):
    max_exact = T5_BUCKETS // 2
    n = jnp.maximum(dist, 0)
    nf = jnp.maximum(n, 1).astype(F32)
    large = max_exact + (jnp.log(nf / max_exact) / math.log(T5_MAX_DIST / max_exact)
                         * (T5_BUCKETS - max_exact)).astype(jnp.int32)
    large = jnp.minimum(large, T5_BUCKETS - 1)
    bucket = jnp.where(n < max_exact, n, large)
    bias = jnp.zeros(dist.shape, F32)
    for kk in range(T5_BUCKETS):
        bias = jnp.where(bucket == kk, rb_ref[h, kk], bias)
    return bias


def _moba_kernel(rb_ref, q_ref, k_ref, v_ref, g_ref, o_ref,
                 kaug_ref, qaug_ref, kmean_ref, bown_ref, bprev_ref, *, n_blocks):
    h = pl.program_id(0)
    b = pl.program_id(1)
    scale = HEAD_DIM ** -0.5
    blk = MOBA_BLOCK
    seq = n_blocks * blk

    @pl.when((h == 0) & (b == 0))
    def _block_indicator():
        srow = lax.broadcasted_iota(jnp.int32, (seq, LANES), 0)
        col = lax.broadcasted_iota(jnp.int32, (seq, LANES), 1)
        kaug_ref[:, LANES:] = jnp.where(col * blk == srow - (srow & (blk - 1)), 1.0, 0.0).astype(BF16)

    @pl.when(b == 0)
    def _bias_tiles():
        r = lax.broadcasted_iota(jnp.int32, (blk, blk), 0)
        c = lax.broadcasted_iota(jnp.int32, (blk, blk), 1)
        dist = r - c
        bown_ref[...] = jnp.where(dist >= 0, _t5_bias(rb_ref, h, dist), NEG)
        bprev_ref[...] = _t5_bias(rb_ref, h, dist + blk)

    kaug_ref[:, :LANES] = k_ref[...]
    row8 = lax.broadcasted_iota(jnp.int32, (8, LANES), 0)
    km = jnp.zeros((8, LANES), F32)
    for j in range(n_blocks):
        mean_j = jnp.mean(k_ref[j * blk:(j + 1) * blk, :].astype(F32), axis=0, keepdims=True)
        km = jnp.where(row8 == j, mean_j, km)
    kmean_ref[...] = jnp.concatenate([km, jnp.zeros((LANES - 8, LANES), F32)], axis=0).astype(BF16)

    gate_t = lax.dot_general(kmean_ref[...], q_ref[...], _NT, preferred_element_type=F32)[0:8, :]
    jrow = lax.broadcasted_iota(jnp.int32, gate_t.shape, 0)
    qpos = lax.broadcasted_iota(jnp.int32, gate_t.shape, 1)
    qblk = jnp.right_shift(qpos, int(math.log2(blk)))
    gm = jnp.where(jrow < qblk, gate_t, NEG)
    sel_bias = jnp.zeros(gate_t.shape, F32)
    for j in range(n_blocks):
        gj = gm[j:j + 1, :]
        beats = (gm > gj) | ((gm == gj) & (jrow < j))
        rank = jnp.sum(beats.astype(F32), axis=0, keepdims=True)
        allowed = ((rank < MOBA_TOPK) & (qblk[0:1, :] > j)) | (qblk[0:1, :] == j)
        sel_bias = jnp.where(jrow == j, jnp.where(allowed, 0.0, NEG), sel_bias)
    sel_cols = jnp.concatenate([sel_bias, jnp.zeros((LANES - 8, seq), F32)], axis=0).T
    qaug_ref[:, :LANES] = q_ref[...]
    qaug_ref[:, LANES:] = sel_cols.astype(BF16)

    far_bias = rb_ref[h, T5_BUCKETS - 1]
    for i in range(n_blocks):
        rows = slice(i * blk, (i + 1) * blk)
        n = (i + 1) * blk
        s = lax.dot_general(qaug_ref[rows, :], kaug_ref[0:n, :], _NT, preferred_element_type=F32) * scale
        pieces = [s[:, n - blk:] + bown_ref[...]]
        if i >= 1:
            pieces.insert(0, s[:, n - 2 * blk:n - blk] + bprev_ref[...])
        if i >= 2:
            pieces.insert(0, s[:, :n - 2 * blk] + far_bias)
        o_ref[rows, :] = _softmax_pv(pieces, v_ref[0:n, :], g_ref[rows, :])


def _moba_attention(p, rel_bias, batch, seq, q0, k0, v0, g0):
    nb = seq // MOBA_BLOCK
    assert seq % MOBA_BLOCK == 0 and MOBA_TOPK < 8 and nb <= 8 and MOBA_BLOCK > T5_MAX_DIST
    blk = MOBA_BLOCK
    full = lambda off: pl.BlockSpec((None, seq, LANES), lambda h, b: (off + h, b, 0))
    return pl.pallas_call(
        functools.partial(_moba_kernel, n_blocks=nb),
        grid=(N_HEADS_MOBA, batch),
        in_specs=[pl.BlockSpec(memory_space=pltpu.SMEM), full(q0), full(k0), full(v0), full(g0)],
        out_specs=pl.BlockSpec((None, seq, LANES), lambda h, b: (h, b, 0)),
        out_shape=jax.ShapeDtypeStruct((N_HEADS_MOBA, batch * seq, LANES), BF16),
        scratch_shapes=[pltpu.VMEM((seq, 2 * LANES), BF16),
                        pltpu.VMEM((seq, 2 * LANES), BF16),
                        pltpu.VMEM((LANES, LANES), BF16),
                        pltpu.VMEM((blk, blk), F32),
                        pltpu.VMEM((blk, blk), F32)],
        compiler_params=_cparams(2),
        name="moba_attention",
    )(rel_bias, p, p, p, p)


def _fox_kernel(q_ref, k_ref, v_ref, g_ref, c_ref, o_ref):
    scale = HEAD_DIM ** -0.5
    seq = q_ref.shape[0]
    r = lax.broadcasted_iota(jnp.int32, (Q_TILE, Q_TILE), 0)
    c = lax.broadcasted_iota(jnp.int32, (Q_TILE, Q_TILE), 1)
    for i in range(seq // Q_TILE):
        rows = slice(i * Q_TILE, (i + 1) * Q_TILE)
        n = (i + 1) * Q_TILE
        a = lax.dot_general(q_ref[rows, :], k_ref[0:n, :], _NT, preferred_element_type=F32)
        s = a * scale - c_ref[:, 0:n]
        pieces = [jnp.where(c <= r, s[:, n - Q_TILE:], NEG)]
        if i >= 1:
            pieces.insert(0, s[:, :n - Q_TILE])
        o_ref[rows, :] = _softmax_pv(pieces, v_ref[0:n, :], g_ref[rows, :])


def _fox_attention(p, c, batch, seq, q0, k0, v0, g0):
    full = lambda off: pl.BlockSpec((None, seq, LANES), lambda b, h: (off + h, b, 0))
    return pl.pallas_call(
        _fox_kernel,
        grid=(batch, N_HEADS_FOX),
        in_specs=[full(q0), full(k0), full(v0), full(g0),
                  pl.BlockSpec((None, 1, seq), lambda b, h: (b * N_HEADS_FOX + h, 0, 0))],
        out_specs=pl.BlockSpec((None, seq, LANES), lambda b, h: (h, b, 0)),
        out_shape=jax.ShapeDtypeStruct((N_HEADS_FOX, batch * seq, LANES), BF16),
        compiler_params=_cparams(2),
        name="fox_attention",
    )(p, p, p, p, c)


def _mem_kernel(q_ref, k_ref, v_ref, g_ref, o_ref):
    scale = HEAD_DIM ** -0.5
    n_heads, seq, _ = q_ref.shape
    for h in range(n_heads):
        for i in range(seq // MEM_Q_TILE):
            rows = slice(i * MEM_Q_TILE, (i + 1) * MEM_Q_TILE)
            s = lax.dot_general(q_ref[h, rows, :], k_ref[h], _NT, preferred_element_type=F32) * scale
            o_ref[h, rows, :] = _softmax_pv([s], v_ref[h], g_ref[h, rows, :])


def _mem_attention(p, mkv, batch, seq, n_mem, q0, g0):
    nh = N_HEADS_MEM
    assert q0 % nh == 0 and g0 % nh == 0
    heads = lambda off: pl.BlockSpec((nh, seq, LANES), lambda b: (off // nh, b, 0))
    kv = lambda blk: pl.BlockSpec((nh, n_mem, LANES), lambda b: (blk, b, 0))
    return pl.pallas_call(
        _mem_kernel,
        grid=(batch,),
        in_specs=[heads(q0), kv(0), kv(1), heads(g0)],
        out_specs=pl.BlockSpec((nh, seq, LANES), lambda b: (0, b, 0)),
        out_shape=jax.ShapeDtypeStruct((nh, batch * seq, LANES), BF16),
        compiler_params=_cparams(1),
        name="mem_attention",
    )(p, mkv, mkv, p)


def _out_proj_kernel(ya_ref, yb_ref, ym_ref, w_ref, x_ref, g_ref, o_ref, y_ref, wb_ref, *, row_chunk):
    @pl.when(pl.program_id(0) == 0)
    def _cast_weights():
        def body(c, carry):
            rows = pl.ds(pl.multiple_of(c * row_chunk, row_chunk), row_chunk)
            wb_ref[rows, :] = w_ref[rows, :].astype(BF16)
            return carry
        lax.fori_loop(0, w_ref.shape[0] // row_chunk, body, 0)

    col = 0
    for src in (ya_ref, yb_ref, ym_ref):
        for c in range(src.shape[0]):
            y_ref[:, col * LANES:(col + 1) * LANES] = src[c]
            col += 1
    o = jnp.dot(y_ref[...], wb_ref[...], preferred_element_type=F32)
    ms = jnp.mean(o * o, axis=-1, keepdims=True)
    o_ref[...] = x_ref[...] + o * lax.rsqrt(ms + EPS) * g_ref[...]


def _out_proj(ya, yb, ym, w3, layer, x2, g, *, tm):
    m, d = x2.shape
    dw = w3.shape[1]
    heads = lambda y: pl.BlockSpec((y.shape[0], tm, LANES), lambda i: (0, i, 0))
    return pl.pallas_call(
        functools.partial(_out_proj_kernel, row_chunk=256),
        grid=(m // tm,),
        in_specs=[heads(ya), heads(yb), heads(ym),
                  pl.BlockSpec((None, dw, d), lambda i: (layer, 0, 0), pipeline_mode=pl.Buffered(1)),
                  pl.BlockSpec((tm, d), lambda i: (i, 0)),
                  pl.BlockSpec((1, d), lambda i: (0, 0))],
        out_specs=pl.BlockSpec((tm, d), lambda i: (i, 0)),
        out_shape=jax.ShapeDtypeStruct((m, d), F32),
        scratch_shapes=[pltpu.VMEM((tm, dw), BF16), pltpu.VMEM((dw, d), BF16)],
        compiler_params=_cparams(1),
        name="out_proj",
    )(ya, yb, ym, w3, x2, g)


def kernel(x, mem, w_in, b_forget, w_mem_kv, w_out, g_pre, g_post, g_mem, rel_bias):
    batch, seq, d = x.shape
    n_mem = mem.shape[1]
    depth = w_in.shape[0]
    wa, wb, wm = N_HEADS_MOBA * HEAD_DIM, N_HEADS_FOX * HEAD_DIM, N_HEADS_MEM * HEAD_DIM
    f0 = 4 * wa + 4 * wb
    f1 = f0 + N_HEADS_FOX
    ha, hb, hm = N_HEADS_MOBA, N_HEADS_FOX, N_HEADS_MEM
    qa0, ka0, va0, ga0 = 0, ha, 2 * ha, 3 * ha
    qb0, kb0, vb0, gb0 = 4 * ha, 4 * ha + hb, 4 * ha + 2 * hb, 4 * ha + 3 * hb
    qm0, gm0 = 4 * ha + 4 * hb, 4 * ha + 4 * hb + hm

    x2 = x.reshape(batch * seq, d)
    mem2 = mem.reshape(batch * n_mem, d)
    for layer in range(depth):
        b_vec = jnp.pad(b_forget[layer], (0, LANES - N_HEADS_FOX)).reshape(1, LANES)

        assert w_in.shape[2] == f1 + 2 * wm
        p, fb = _norm_proj(x2, g_pre[layer].reshape(1, d), w_in, layer, tm=1024, tn=2 * wm,
                           tail_shift=N_HEADS_FOX)
        mkv, _ = _norm_proj(mem2, g_mem[layer].reshape(1, d), w_mem_kv, layer,
                            tm=batch * n_mem, tn=2 * wm)
        c = _forget_scan(fb, b_vec, batch, seq)
        c = c[:, :N_HEADS_FOX].reshape(batch * N_HEADS_FOX, 1, seq)

        ya = _moba_attention(p, rel_bias, batch, seq, qa0, ka0, va0, ga0)
        yb = _fox_attention(p, c, batch, seq, qb0, kb0, vb0, gb0)
        ym = _mem_attention(p, mkv, batch, seq, n_mem, qm0, g0=gm0)
        x2 = _out_proj(ya, yb, ym, w_out, layer, x2, g_post[layer].reshape(1, d), tm=512)
    return x2.reshape(batch, seq, d)
```

```python
import functools
import math

import jax
import jax.numpy as jnp
from jax import lax
from jax.experimental import pallas as pl
from jax.experimental.pallas import tpu as pltpu

HEAD_DIM = 128
N_HEADS_MOBA = 6
N_HEADS_FOX = 6
N_HEADS_MEM = 4
MOBA_BLOCK = 256
MOBA_TOPK = 3
T5_BUCKETS = 32
T5_MAX_DIST = 128
EPS = 1e-6
NEG = -1e30

LANES = 128
Q_TILE = MOBA_BLOCK
MEM_Q_TILE = 512
VMEM_LIMIT_BYTES = 56 * 1024 * 1024

F32 = jnp.float32
BF16 = jnp.bfloat16
_NT = (((1,), (1,)), ((), ()))


def _cparams(n_axes):
    return pltpu.CompilerParams(dimension_semantics=("arbitrary",) * n_axes,
                                vmem_limit_bytes=VMEM_LIMIT_BYTES)


def _norm_proj_kernel(*refs, n_col_blocks, w_transposed, with_gate_logits, row_chunk):
    if with_gate_logits:
        x_ref, g_ref, w_ref, wf_ref, p_ref, f_ref, h_ref = refs
    else:
        x_ref, g_ref, w_ref, p_ref, h_ref = refs
    tm = x_ref.shape[0]

    @pl.when(pl.program_id(1) == 0)
    def _normalize():
        def body(c, carry):
            rows = pl.ds(pl.multiple_of(c * row_chunk, row_chunk), row_chunk)
            xf = x_ref[rows, :]
            ms = jnp.mean(xf * xf, axis=-1, keepdims=True)
            h_ref[rows, :] = (xf * lax.rsqrt(ms + EPS) * g_ref[...]).astype(BF16)
            return carry
        lax.fori_loop(0, tm // row_chunk, body, 0)
        if with_gate_logits:
            f_ref[...] = lax.dot_general(h_ref[...], wf_ref[...], _NT, preferred_element_type=F32)

    w = w_ref[...].astype(BF16)
    if w_transposed:
        acc = lax.dot_general(h_ref[...], w, _NT, preferred_element_type=F32)
    else:
        acc = jnp.dot(h_ref[...], w, preferred_element_type=F32)
    for c in range(n_col_blocks):
        p_ref[c] = acc[:, c * LANES:(c + 1) * LANES].astype(BF16)


def _norm_proj(x2, g, w, wf_t=None, *, w_transposed, tm, tn):
    m, d = x2.shape
    n = w.shape[0] if w_transposed else w.shape[1]
    ncb = tn // LANES
    with_f = wf_t is not None
    assert n % tn == 0 and m % tm == 0
    w_spec = (pl.BlockSpec((tn, d), lambda i, j: (j, 0)) if w_transposed
              else pl.BlockSpec((d, tn), lambda i, j: (0, j)))
    in_specs = [pl.BlockSpec((tm, d), lambda i, j: (i, 0)),
                pl.BlockSpec((1, d), lambda i, j: (0, 0)), w_spec]
    out_shape = [jax.ShapeDtypeStruct((n // LANES, m, LANES), BF16)]
    out_specs = [pl.BlockSpec((ncb, tm, LANES), lambda i, j: (j, i, 0))]
    args = [x2, g, w]
    if with_f:
        in_specs.append(pl.BlockSpec((LANES, d), lambda i, j: (0, 0)))
        out_shape.append(jax.ShapeDtypeStruct((m, LANES), F32))
        out_specs.append(pl.BlockSpec((tm, LANES), lambda i, j: (i, 0)))
        args.append(wf_t)
    res = pl.pallas_call(
        functools.partial(_norm_proj_kernel, n_col_blocks=ncb, w_transposed=w_transposed,
                          with_gate_logits=with_f, row_chunk=256),
        grid=(m // tm, n // tn),
        in_specs=in_specs, out_specs=out_specs, out_shape=out_shape,
        scratch_shapes=[pltpu.VMEM((tm, d), BF16)],
        compiler_params=_cparams(2),
        name="norm_proj_gate" if with_f else "norm_proj",
    )(*args)
    return res if with_f else (res[0], None)


def _w_repack_kernel(src_ref, main_ref, gate_ref, buf_ref, gbuf_ref, sem_ref, gsem_ref, *,
                     rows, base_row, n_aligned_blocks, n_gate):
    t = pl.program_id(0)
    n_steps = pl.num_programs(0)
    n_slices = src_ref.shape[1]

    def block_copies(step, slot):
        row0 = base_row + step * rows + jnp.where(step >= n_aligned_blocks, n_gate, 0)
        return [pltpu.make_async_copy(src_ref.at[pl.ds(row0, rows), s, :],
                                      buf_ref.at[slot, :, pl.ds(s * LANES, LANES)],
                                      sem_ref.at[slot]) for s in range(n_slices)]

    def gate_copies():
        gate_row0 = base_row + n_aligned_blocks * rows
        return [pltpu.make_async_copy(src_ref.at[pl.ds(gate_row0, 8), s, :],
                                      gbuf_ref.at[:, pl.ds(s * LANES, LANES)],
                                      gsem_ref.at[0]) for s in range(n_slices)]

    @pl.when(t == 0)
    def _first():
        for cp in gate_copies() + block_copies(0, 0):
            cp.start()

    @pl.when(t + 1 < n_steps)
    def _prefetch():
        for cp in block_copies(t + 1, (t + 1) % 2):
            cp.start()

    @pl.when(t == 0)
    def _gate_rows():
        for cp in gate_copies():
            cp.wait()
        row = lax.broadcasted_iota(jnp.int32, gbuf_ref.shape, 0)
        g8 = jnp.where(row < n_gate, gbuf_ref[...], 0.0)
        gate_ref[...] = jnp.concatenate(
            [g8, jnp.zeros((gate_ref.shape[0] - 8, g8.shape[1]), F32)], axis=0).astype(BF16)

    for cp in block_copies(t, t % 2):
        cp.wait()
    main_ref[...] = buf_ref[t % 2].astype(BF16)


def _w_repack(w_t3, layer, n_total, n_aligned, n_gate, *, rows):
    n_slices = w_t3.shape[1]
    d = n_slices * LANES
    n_main = n_total - n_gate
    assert n_aligned % rows == 0 and n_main % rows == 0 and n_gate <= 8
    return pl.pallas_call(
        functools.partial(_w_repack_kernel, rows=rows, base_row=layer * n_total,
                          n_aligned_blocks=n_aligned // rows, n_gate=n_gate),
        grid=(n_main // rows,),
        in_specs=[pl.BlockSpec(memory_space=pl.ANY)],
        out_specs=[pl.BlockSpec((rows, d), lambda t: (t, 0)),
                   pl.BlockSpec((LANES, d), lambda t: (0, 0))],
        out_shape=[jax.ShapeDtypeStruct((n_main, d), BF16),
                   jax.ShapeDtypeStruct((LANES, d), BF16)],
        scratch_shapes=[pltpu.VMEM((2, rows, d), F32), pltpu.VMEM((8, d), F32),
                        pltpu.SemaphoreType.DMA((2,)), pltpu.SemaphoreType.DMA((1,))],
        compiler_params=_cparams(1),
        name="w_repack",
    )(w_t3)


def _forget_scan_kernel(f_ref, b_ref, c_ref):
    z = f_ref[...] + b_ref[...]
    logf = jnp.minimum(z, 0.0) - jnp.log1p(jnp.exp(-jnp.abs(z)))
    x = logf.T[0:8, :]
    s_len = x.shape[1]
    lane = lax.broadcasted_iota(jnp.int32, x.shape, 1)
    shift = 1
    while shift < s_len:
        x = x + jnp.where(lane >= shift, pltpu.roll(x, shift, 1), 0.0)
        shift *= 2
    c_ref[...] = x


def _forget_scan(fb, bvec, batch, seq):
    return pl.pallas_call(
        _forget_scan_kernel,
        grid=(batch,),
        in_specs=[pl.BlockSpec((seq, LANES), lambda b: (b, 0)),
                  pl.BlockSpec((1, LANES), lambda b: (0, 0))],
        out_specs=pl.BlockSpec((None, 8, seq), lambda b: (b, 0, 0)),
        out_shape=jax.ShapeDtypeStruct((batch, 8, seq), F32),
        compiler_params=_cparams(1),
        name="forget_scan",
    )(fb, bvec)


def _softmax_pv(pieces, v, g):
    m = functools.reduce(jnp.maximum, [jnp.max(s, axis=1, keepdims=True) for s in pieces])
    ps = [jnp.exp(s - m) for s in pieces]
    l = functools.reduce(jnp.add, [jnp.sum(p, axis=1, keepdims=True) for p in ps])
    p = ps[0] if len(ps) == 1 else jnp.concatenate(ps, axis=1)
    acc = jnp.dot(p.astype(BF16), v, preferred_element_type=F32)
    gf = g.astype(F32)
    silu = gf * (1.0 / (1.0 + jnp.exp(-gf)))
    return (acc / l * silu).astype(BF16)


def _t5_bias(rb_ref, h, dist):
    max_exact = T5_BUCKETS // 2
    n = jnp.maximum(dist, 0)
    nf = jnp.maximum(n, 1).astype(F32)
    large = max_exact + (jnp.log(nf / max_exact) / math.log(T5_MAX_DIST / max_exact)
                         * (T5_BUCKETS - max_exact)).astype(jnp.int32)
    large = jnp.minimum(large, T5_BUCKETS - 1)
    bucket = jnp.where(n < max_exact, n, large)
    bias = jnp.zeros(dist.shape, F32)
    for kk in range(T5_BUCKETS):
        bias = jnp.where(bucket == kk, rb_ref[h, kk], bias)
    return bias


def _moba_kernel(rb_ref, q_ref, k_ref, v_ref, g_ref, o_ref,
                 kaug_ref, qaug_ref, kmean_ref, bown_ref, bprev_ref, *, n_blocks):
    h = pl.program_id(0)
    b = pl.program_id(1)
    scale = HEAD_DIM ** -0.5
    blk = MOBA_BLOCK
    seq = n_blocks * blk

    @pl.when((h == 0) & (b == 0))
    def _block_indicator():
        srow = lax.broadcasted_iota(jnp.int32, (seq, LANES), 0)
        col = lax.broadcasted_iota(jnp.int32, (seq, LANES), 1)
        kaug_ref[:, LANES:] = jnp.where(col * blk == srow - (srow & (blk - 1)), 1.0, 0.0).astype(BF16)

    @pl.when(b == 0)
    def _bias_tiles():
        r = lax.broadcasted_iota(jnp.int32, (blk, blk), 0)
        c = lax.broadcasted_iota(jnp.int32, (blk, blk), 1)
        dist = r - c
        bown_ref[...] = jnp.where(dist >= 0, _t5_bias(rb_ref, h, dist), NEG)
        bprev_ref[...] = _t5_bias(rb_ref, h, dist + blk)

    kaug_ref[:, :LANES] = k_ref[...]
    row8 = lax.broadcasted_iota(jnp.int32, (8, LANES), 0)
    km = jnp.zeros((8, LANES), F32)
    for j in range(n_blocks):
        mean_j = jnp.mean(k_ref[j * blk:(j + 1) * blk, :].astype(F32), axis=0, keepdims=True)
        km = jnp.where(row8 == j, mean_j, km)
    kmean_ref[...] = jnp.concatenate([km, jnp.zeros((LANES - 8, LANES), F32)], axis=0).astype(BF16)

    gate_t = lax.dot_general(kmean_ref[...], q_ref[...], _NT, preferred_element_type=F32)[0:8, :]
    jrow = lax.broadcasted_iota(jnp.int32, gate_t.shape, 0)
    qpos = lax.broadcasted_iota(jnp.int32, gate_t.shape, 1)
    qblk = jnp.right_shift(qpos, int(math.log2(blk)))
    gm = jnp.where(jrow < qblk, gate_t, NEG)
    sel_bias = jnp.zeros(gate_t.shape, F32)
    for j in range(n_blocks):
        gj = gm[j:j + 1, :]
        beats = (gm > gj) | ((gm == gj) & (jrow < j))
        rank = jnp.sum(beats.astype(F32), axis=0, keepdims=True)
        allowed = ((rank < MOBA_TOPK) & (qblk[0:1, :] > j)) | (qblk[0:1, :] == j)
        sel_bias = jnp.where(jrow == j, jnp.where(allowed, 0.0, NEG), sel_bias)
    sel_cols = jnp.concatenate([sel_bias, jnp.zeros((LANES - 8, seq), F32)], axis=0).T
    qaug_ref[:, :LANES] = q_ref[...]
    qaug_ref[:, LANES:] = sel_cols.astype(BF16)

    far_bias = rb_ref[h, T5_BUCKETS - 1]
    for i in range(n_blocks):
        rows = slice(i * blk, (i + 1) * blk)
        n = (i + 1) * blk
        s = lax.dot_general(qaug_ref[rows, :], kaug_ref[0:n, :], _NT, preferred_element_type=F32) * scale
        pieces = [s[:, n - blk:] + bown_ref[...]]
        if i >= 1:
            pieces.insert(0, s[:, n - 2 * blk:n - blk] + bprev_ref[...])
        if i >= 2:
            pieces.insert(0, s[:, :n - 2 * blk] + far_bias)
        o_ref[rows, :] = _softmax_pv(pieces, v_ref[0:n, :], g_ref[rows, :])


def _moba_attention(p, rel_bias, batch, seq, q0, k0, v0, g0):
    nb = seq // MOBA_BLOCK
    assert seq % MOBA_BLOCK == 0 and MOBA_TOPK < 8 and nb <= 8 and MOBA_BLOCK > T5_MAX_DIST
    blk = MOBA_BLOCK
    full = lambda off: pl.BlockSpec((None, seq, LANES), lambda h, b: (off + h, b, 0))
    return pl.pallas_call(
        functools.partial(_moba_kernel, n_blocks=nb),
        grid=(N_HEADS_MOBA, batch),
        in_specs=[pl.BlockSpec(memory_space=pltpu.SMEM), full(q0), full(k0), full(v0), full(g0)],
        out_specs=pl.BlockSpec((None, seq, LANES), lambda h, b: (h, b, 0)),
        out_shape=jax.ShapeDtypeStruct((N_HEADS_MOBA, batch * seq, LANES), BF16),
        scratch_shapes=[pltpu.VMEM((seq, 2 * LANES), BF16),
                        pltpu.VMEM((seq, 2 * LANES), BF16),
                        pltpu.VMEM((LANES, LANES), BF16),
                        pltpu.VMEM((blk, blk), F32),
                        pltpu.VMEM((blk, blk), F32)],
        compiler_params=_cparams(2),
        name="moba_attention",
    )(rel_bias, p, p, p, p)


def _fox_kernel(q_ref, k_ref, v_ref, g_ref, c_ref, o_ref):
    scale = HEAD_DIM ** -0.5
    seq = q_ref.shape[0]
    r = lax.broadcasted_iota(jnp.int32, (Q_TILE, Q_TILE), 0)
    c = lax.broadcasted_iota(jnp.int32, (Q_TILE, Q_TILE), 1)
    for i in range(seq // Q_TILE):
        rows = slice(i * Q_TILE, (i + 1) * Q_TILE)
        n = (i + 1) * Q_TILE
        a = lax.dot_general(q_ref[rows, :], k_ref[0:n, :], _NT, preferred_element_type=F32)
        s = a * scale - c_ref[:, 0:n]
        pieces = [jnp.where(c <= r, s[:, n - Q_TILE:], NEG)]
        if i >= 1:
            pieces.insert(0, s[:, :n - Q_TILE])
        o_ref[rows, :] = _softmax_pv(pieces, v_ref[0:n, :], g_ref[rows, :])


def _fox_attention(p, c, batch, seq, q0, k0, v0, g0):
    full = lambda off: pl.BlockSpec((None, seq, LANES), lambda b, h: (off + h, b, 0))
    return pl.pallas_call(
        _fox_kernel,
        grid=(batch, N_HEADS_FOX),
        in_specs=[full(q0), full(k0), full(v0), full(g0),
                  pl.BlockSpec((None, 1, seq), lambda b, h: (b * N_HEADS_FOX + h, 0, 0))],
        out_specs=pl.BlockSpec((None, seq, LANES), lambda b, h: (h, b, 0)),
        out_shape=jax.ShapeDtypeStruct((N_HEADS_FOX, batch * seq, LANES), BF16),
        compiler_params=_cparams(2),
        name="fox_attention",
    )(p, p, p, p, c)


def _mem_kernel(q_ref, k_ref, v_ref, g_ref, o_ref):
    scale = HEAD_DIM ** -0.5
    n_heads, seq, _ = q_ref.shape
    for h in range(n_heads):
        for i in range(seq // MEM_Q_TILE):
            rows = slice(i * MEM_Q_TILE, (i + 1) * MEM_Q_TILE)
            s = lax.dot_general(q_ref[h, rows, :], k_ref[h], _NT, preferred_element_type=F32) * scale
            o_ref[h, rows, :] = _softmax_pv([s], v_ref[h], g_ref[h, rows, :])


def _mem_attention(p, mkv, batch, seq, n_mem, q0, g0):
    nh = N_HEADS_MEM
    assert q0 % nh == 0 and g0 % nh == 0
    heads = lambda off: pl.BlockSpec((nh, seq, LANES), lambda b: (off // nh, b, 0))
    kv = lambda blk: pl.BlockSpec((nh, n_mem, LANES), lambda b: (blk, b, 0))
    return pl.pallas_call(
        _mem_kernel,
        grid=(batch,),
        in_specs=[heads(q0), kv(0), kv(1), heads(g0)],
        out_specs=pl.BlockSpec((nh, seq, LANES), lambda b: (0, b, 0)),
        out_shape=jax.ShapeDtypeStruct((nh, batch * seq, LANES), BF16),
        compiler_params=_cparams(1),
        name="mem_attention",
    )(p, mkv, mkv, p)


def _out_proj_kernel(ya_ref, yb_ref, ym_ref, w_ref, x_ref, g_ref, o_ref, y_ref, wb_ref, *, row_chunk):
    @pl.when(pl.program_id(0) == 0)
    def _cast_weights():
        def body(c, carry):
            rows = pl.ds(pl.multiple_of(c * row_chunk, row_chunk), row_chunk)
            wb_ref[rows, :] = w_ref[rows, :].astype(BF16)
            return carry
        lax.fori_loop(0, w_ref.shape[0] // row_chunk, body, 0)

    col = 0
    for src in (ya_ref, yb_ref, ym_ref):
        for c in range(src.shape[0]):
            y_ref[:, col * LANES:(col + 1) * LANES] = src[c]
            col += 1
    o = jnp.dot(y_ref[...], wb_ref[...], preferred_element_type=F32)
    ms = jnp.mean(o * o, axis=-1, keepdims=True)
    o_ref[...] = x_ref[...] + o * lax.rsqrt(ms + EPS) * g_ref[...]


def _out_proj(ya, yb, ym, w3, layer, x2, g, *, tm):
    m, d = x2.shape
    dw = w3.shape[1]
    heads = lambda y: pl.BlockSpec((y.shape[0], tm, LANES), lambda i: (0, i, 0))
    return pl.pallas_call(
        functools.partial(_out_proj_kernel, row_chunk=256),
        grid=(m // tm,),
        in_specs=[heads(ya), heads(yb), heads(ym),
                  pl.BlockSpec((None, dw, d), lambda i: (layer, 0, 0), pipeline_mode=pl.Buffered(1)),
                  pl.BlockSpec((tm, d), lambda i: (i, 0)),
                  pl.BlockSpec((1, d), lambda i: (0, 0))],
        out_specs=pl.BlockSpec((tm, d), lambda i: (i, 0)),
        out_shape=jax.ShapeDtypeStruct((m, d), F32),
        scratch_shapes=[pltpu.VMEM((tm, dw), BF16), pltpu.VMEM((dw, d), BF16)],
        compiler_params=_cparams(1),
        name="out_proj",
    )(ya, yb, ym, w3, x2, g)


def kernel(x, mem, w_in, b_forget, w_mem_kv, w_out, g_pre, g_post, g_mem, rel_bias):
    batch, seq, d = x.shape
    n_mem = mem.shape[1]
    depth = w_in.shape[0]
    wa, wb, wm = N_HEADS_MOBA * HEAD_DIM, N_HEADS_FOX * HEAD_DIM, N_HEADS_MEM * HEAD_DIM
    f0 = 4 * wa + 4 * wb
    f1 = f0 + N_HEADS_FOX
    ha, hb, hm = N_HEADS_MOBA, N_HEADS_FOX, N_HEADS_MEM
    qa0, ka0, va0, ga0 = 0, ha, 2 * ha, 3 * ha
    qb0, kb0, vb0, gb0 = 4 * ha, 4 * ha + hb, 4 * ha + 2 * hb, 4 * ha + 3 * hb
    qm0, gm0 = 4 * ha + 4 * hb, 4 * ha + 4 * hb + hm

    x2 = x.reshape(batch * seq, d)
    mem2 = mem.reshape(batch * n_mem, d)
    for layer in range(depth):
        b_vec = jnp.pad(b_forget[layer], (0, LANES - N_HEADS_FOX)).reshape(1, LANES)

        n_in = w_in.shape[2]
        assert n_in == f1 + 2 * wm
        w_t3 = jnp.transpose(w_in, (0, 2, 1)).reshape(depth * n_in, d // LANES, LANES)
        w_main_t, w_gate_t = _w_repack(w_t3, layer, n_in, f0, N_HEADS_FOX, rows=512)
        p, fb = _norm_proj(x2, g_pre[layer].reshape(1, d), w_main_t, w_gate_t, w_transposed=True,
                           tm=1024, tn=1024)
        mkv, _ = _norm_proj(mem2, g_mem[layer].reshape(1, d), w_mem_kv[layer], w_transposed=False,
                            tm=batch * n_mem, tn=2 * wm)
        c = _forget_scan(fb, b_vec, batch, seq)
        c = c[:, :N_HEADS_FOX].reshape(batch * N_HEADS_FOX, 1, seq)

        ya = _moba_attention(p, rel_bias, batch, seq, qa0, ka0, va0, ga0)
        yb = _fox_attention(p, c, batch, seq, qb0, kb0, vb0, gb0)
        ym = _mem_attention(p, mkv, batch, seq, n_mem, qm0, g0=gm0)
        x2 = _out_proj(ya, yb, ym, w_out, layer, x2, g_post[layer].reshape(1, d), tm=512)
    return x2.reshape(batch, seq, d)
```

```python
import functools
import math

import jax
import jax.numpy as jnp
from jax import lax
from jax.experimental import pallas as pl
from jax.experimental.pallas import tpu as pltpu

HEAD_DIM = 128
N_HEADS_MOBA = 6
N_HEADS_FOX = 6
N_HEADS_MEM = 4
MOBA_BLOCK = 256
MOBA_TOPK = 3
T5_BUCKETS = 32
T5_MAX_DIST = 128
EPS = 1e-6
NEG = -1e30

LANES = 128
Q_TILE = MOBA_BLOCK
MEM_Q_TILE = 512
VMEM_LIMIT_BYTES = 56 * 1024 * 1024

F32 = jnp.float32
BF16 = jnp.bfloat16
_NT = (((1,), (1,)), ((), ()))


def _cparams(n_axes):
    return pltpu.CompilerParams(dimension_semantics=("arbitrary",) * n_axes,
                                vmem_limit_bytes=VMEM_LIMIT_BYTES)


def _norm_proj_kernel(*refs, n_col_blocks, w_transposed, with_gate_logits, row_chunk):
    if with_gate_logits:
        x_ref, g_ref, w_ref, wf_ref, p_ref, f_ref, h_ref = refs
    else:
        x_ref, g_ref, w_ref, p_ref, h_ref = refs
    tm = x_ref.shape[0]

    @pl.when(pl.program_id(1) == 0)
    def _normalize():
        def body(c, carry):
            rows = pl.ds(pl.multiple_of(c * row_chunk, row_chunk), row_chunk)
            xf = x_ref[rows, :]
            ms = jnp.mean(xf * xf, axis=-1, keepdims=True)
            h_ref[rows, :] = (xf * lax.rsqrt(ms + EPS) * g_ref[...]).astype(BF16)
            return carry
        lax.fori_loop(0, tm // row_chunk, body, 0)
        if with_gate_logits:
            f_ref[...] = lax.dot_general(h_ref[...], wf_ref[...], _NT, preferred_element_type=F32)

    w = w_ref[...].astype(BF16)
    if w_transposed:
        acc = lax.dot_general(h_ref[...], w, _NT, preferred_element_type=F32)
    else:
        acc = jnp.dot(h_ref[...], w, preferred_element_type=F32)
    for c in range(n_col_blocks):
        p_ref[c] = acc[:, c * LANES:(c + 1) * LANES].astype(BF16)


def _norm_proj(x2, g, w, wf_t=None, *, w_transposed, tm, tn):
    m, d = x2.shape
    n = w.shape[0] if w_transposed else w.shape[1]
    ncb = tn // LANES
    with_f = wf_t is not None
    assert n % tn == 0 and m % tm == 0
    w_spec = (pl.BlockSpec((tn, d), lambda i, j: (j, 0)) if w_transposed
              else pl.BlockSpec((d, tn), lambda i, j: (0, j)))
    in_specs = [pl.BlockSpec((tm, d), lambda i, j: (i, 0)),
                pl.BlockSpec((1, d), lambda i, j: (0, 0)), w_spec]
    out_shape = [jax.ShapeDtypeStruct((n // LANES, m, LANES), BF16)]
    out_specs = [pl.BlockSpec((ncb, tm, LANES), lambda i, j: (j, i, 0))]
    args = [x2, g, w]
    if with_f:
        in_specs.append(pl.BlockSpec((LANES, d), lambda i, j: (0, 0)))
        out_shape.append(jax.ShapeDtypeStruct((m, LANES), F32))
        out_specs.append(pl.BlockSpec((tm, LANES), lambda i, j: (i, 0)))
        args.append(wf_t)
    res = pl.pallas_call(
        functools.partial(_norm_proj_kernel, n_col_blocks=ncb, w_transposed=w_transposed,
                          with_gate_logits=with_f, row_chunk=256),
        grid=(m // tm, n // tn),
        in_specs=in_specs, out_specs=out_specs, out_shape=out_shape,
        scratch_shapes=[pltpu.VMEM((tm, d), BF16)],
        compiler_params=_cparams(2),
        name="norm_proj_gate" if with_f else "norm_proj",
    )(*args)
    return res if with_f else (res[0], None)


def _w_repack_kernel(src_ref, main_ref, gate_ref, buf_ref, gbuf_ref, sem_ref, gsem_ref, *,
                     rows, base_row, n_aligned_blocks, n_gate):
    t = pl.program_id(0)
    n_steps = pl.num_programs(0)
    n_slices = src_ref.shape[1]

    def block_copies(step, slot):
        row0 = base_row + step * rows + jnp.where(step >= n_aligned_blocks, n_gate, 0)
        return [pltpu.make_async_copy(src_ref.at[pl.ds(row0, rows), s, :],
                                      buf_ref.at[slot, :, pl.ds(s * LANES, LANES)],
                                      sem_ref.at[slot]) for s in range(n_slices)]

    def gate_copies():
        gate_row0 = base_row + n_aligned_blocks * rows
        return [pltpu.make_async_copy(src_ref.at[pl.ds(gate_row0, 8), s, :],
                                      gbuf_ref.at[:, pl.ds(s * LANES, LANES)],
                                      gsem_ref.at[0]) for s in range(n_slices)]

    @pl.when(t == 0)
    def _first():
        for cp in gate_copies() + block_copies(0, 0):
            cp.start()

    @pl.when(t + 1 < n_steps)
    def _prefetch():
        for cp in block_copies(t + 1, (t + 1) % 2):
            cp.start()

    @pl.when(t == 0)
    def _gate_rows():
        for cp in gate_copies():
            cp.wait()
        row = lax.broadcasted_iota(jnp.int32, gbuf_ref.shape, 0)
        g8 = jnp.where(row < n_gate, gbuf_ref[...], 0.0)
        gate_ref[...] = jnp.concatenate(
            [g8, jnp.zeros((gate_ref.shape[0] - 8, g8.shape[1]), F32)], axis=0).astype(BF16)

    for cp in block_copies(t, t % 2):
        cp.wait()
    main_ref[...] = buf_ref[t % 2].astype(BF16)


def _w_repack(w_t3, layer, n_total, n_aligned, n_gate, *, rows):
    n_slices = w_t3.shape[1]
    d = n_slices * LANES
    n_main = n_total - n_gate
    assert n_aligned % rows == 0 and n_main % rows == 0 and n_gate <= 8
    return pl.pallas_call(
        functools.partial(_w_repack_kernel, rows=rows, base_row=layer * n_total,
                          n_aligned_blocks=n_aligned // rows, n_gate=n_gate),
        grid=(n_main // rows,),
        in_specs=[pl.BlockSpec(memory_space=pl.ANY)],
        out_specs=[pl.BlockSpec((rows, d), lambda t: (t, 0)),
                   pl.BlockSpec((LANES, d), lambda t: (0, 0))],
        out_shape=[jax.ShapeDtypeStruct((n_main, d), BF16),
                   jax.ShapeDtypeStruct((LANES, d), BF16)],
        scratch_shapes=[pltpu.VMEM((2, rows, d), F32), pltpu.VMEM((8, d), F32),
                        pltpu.SemaphoreType.DMA((2,)), pltpu.SemaphoreType.DMA((1,))],
        compiler_params=_cparams(1),
        name="w_repack",
    )(w_t3)


def _forget_scan_kernel(f_ref, b_ref, c_ref):
    z = f_ref[...] + b_ref[...]
    logf = jnp.minimum(z, 0.0) - jnp.log1p(jnp.exp(-jnp.abs(z)))
    x = logf.T[0:8, :]
    s_len = x.shape[1]
    lane = lax.broadcasted_iota(jnp.int32, x.shape, 1)
    shift = 1
    while shift < s_len:
        x = x + jnp.where(lane >= shift, pltpu.roll(x, shift, 1), 0.0)
        shift *= 2
    c_ref[...] = x


def _forget_scan(fb, bvec, batch, seq):
    return pl.pallas_call(
        _forget_scan_kernel,
        grid=(batch,),
        in_specs=[pl.BlockSpec((seq, LANES), lambda b: (b, 0)),
                  pl.BlockSpec((1, LANES), lambda b: (0, 0))],
        out_specs=pl.BlockSpec((None, 8, seq), lambda b: (b, 0, 0)),
        out_shape=jax.ShapeDtypeStruct((batch, 8, seq), F32),
        compiler_params=_cparams(1),
        name="forget_scan",
    )(fb, bvec)


LOG2E = math.log2(math.e)


def _softmax_pv(pieces, values, g):
    m = functools.reduce(jnp.maximum, [jnp.max(s, axis=1, keepdims=True) for s in pieces])
    ps = [jnp.exp2(s - m) for s in pieces]
    l = functools.reduce(jnp.add, [jnp.sum(p, axis=1, keepdims=True) for p in ps])
    acc = functools.reduce(jnp.add, [jnp.dot(p.astype(BF16), v, preferred_element_type=F32)
                                     for p, v in zip(ps, values)])
    gf = g.astype(F32)
    silu = gf * (1.0 / (1.0 + jnp.exp(-gf)))
    return (acc / l * silu).astype(BF16)


def _skewed_tiles(n_tiles, score_matmul, finish):
    nxt = score_matmul(0)
    for i in range(n_tiles):
        cur = nxt
        if i + 1 < n_tiles:
            nxt = score_matmul(i + 1)
        finish(i, cur)


def _t5_bias(rb_ref, h, dist):
    max_exact = T5_BUCKETS // 2
    n = jnp.maximum(dist, 0)
    nf = jnp.maximum(n, 1).astype(F32)
    large = max_exact + (jnp.log(nf / max_exact) / math.log(T5_MAX_DIST / max_exact)
                         * (T5_BUCKETS - max_exact)).astype(jnp.int32)
    large = jnp.minimum(large, T5_BUCKETS - 1)
    bucket = jnp.where(n < max_exact, n, large)
    bias = jnp.zeros(dist.shape, F32)
    for kk in range(T5_BUCKETS):
        bias = jnp.where(bucket == kk, rb_ref[h, kk], bias)
    return bias


def _moba_kernel(rb_ref, q_ref, k_ref, v_ref, g_ref, o_ref,
                 kaug_ref, qaug_ref, kmean_ref, bown_ref, bprev_ref, *, n_blocks):
    h = pl.program_id(0)
    b = pl.program_id(1)
    scale = HEAD_DIM ** -0.5
    blk = MOBA_BLOCK
    seq = n_blocks * blk

    @pl.when((h == 0) & (b == 0))
    def _block_indicator():
        srow = lax.broadcasted_iota(jnp.int32, (seq, LANES), 0)
        col = lax.broadcasted_iota(jnp.int32, (seq, LANES), 1)
        kaug_ref[:, LANES:] = jnp.where(col * blk == srow - (srow & (blk - 1)), 1.0, 0.0).astype(BF16)

    @pl.when(b == 0)
    def _bias_tiles():
        r = lax.broadcasted_iota(jnp.int32, (blk, blk), 0)
        c = lax.broadcasted_iota(jnp.int32, (blk, blk), 1)
        dist = r - c
        bown_ref[...] = jnp.where(dist >= 0, _t5_bias(rb_ref, h, dist) * LOG2E, NEG)
        bprev_ref[...] = _t5_bias(rb_ref, h, dist + blk) * LOG2E

    kaug_ref[:, :LANES] = k_ref[...]
    row8 = lax.broadcasted_iota(jnp.int32, (8, LANES), 0)
    km = jnp.zeros((8, LANES), F32)
    for j in range(n_blocks):
        mean_j = jnp.mean(k_ref[j * blk:(j + 1) * blk, :].astype(F32), axis=0, keepdims=True)
        km = jnp.where(row8 == j, mean_j, km)
    kmean_ref[...] = jnp.concatenate([km, jnp.zeros((LANES - 8, LANES), F32)], axis=0).astype(BF16)

    gate_t = lax.dot_general(kmean_ref[...], q_ref[...], _NT, preferred_element_type=F32)[0:8, :]
    jrow = lax.broadcasted_iota(jnp.int32, gate_t.shape, 0)
    qpos = lax.broadcasted_iota(jnp.int32, gate_t.shape, 1)
    qblk = jnp.right_shift(qpos, int(math.log2(blk)))
    gm = jnp.where(jrow < qblk, gate_t, NEG)
    sel_bias = jnp.zeros(gate_t.shape, F32)
    for j in range(n_blocks):
        gj = gm[j:j + 1, :]
        beats = (gm > gj) | ((gm == gj) & (jrow < j))
        rank = jnp.sum(beats.astype(F32), axis=0, keepdims=True)
        allowed = ((rank < MOBA_TOPK) & (qblk[0:1, :] > j)) | (qblk[0:1, :] == j)
        sel_bias = jnp.where(jrow == j, jnp.where(allowed, 0.0, NEG), sel_bias)
    sel_cols = jnp.concatenate([sel_bias, jnp.zeros((LANES - 8, seq), F32)], axis=0).T
    qaug_ref[:, :LANES] = q_ref[...]
    qaug_ref[:, LANES:] = sel_cols.astype(BF16)

    far_bias = rb_ref[h, T5_BUCKETS - 1] * LOG2E
    def score_matmul(i):
        return lax.dot_general(qaug_ref[i * blk:(i + 1) * blk, :], kaug_ref[0:(i + 1) * blk, :], _NT,
                               preferred_element_type=F32)

    def finish(i, a):
        rows = slice(i * blk, (i + 1) * blk)
        n = (i + 1) * blk
        s = a * (scale * LOG2E)
        pieces = [s[:, n - blk:] + bown_ref[...]]
        values = [v_ref[n - blk:n, :]]
        if i >= 1:
            pieces.append(s[:, n - 2 * blk:n - blk] + bprev_ref[...])
            values.append(v_ref[n - 2 * blk:n - blk, :])
        if i >= 2:
            pieces.append(s[:, :n - 2 * blk] + far_bias)
            values.append(v_ref[0:n - 2 * blk, :])
        o_ref[rows, :] = _softmax_pv(pieces, values, g_ref[rows, :])

    _skewed_tiles(n_blocks, score_matmul, finish)


def _moba_attention(p, rel_bias, batch, seq, q0, k0, v0, g0):
    nb = seq // MOBA_BLOCK
    assert seq % MOBA_BLOCK == 0 and MOBA_TOPK < 8 and nb <= 8 and MOBA_BLOCK > T5_MAX_DIST
    blk = MOBA_BLOCK
    full = lambda off: pl.BlockSpec((None, seq, LANES), lambda h, b: (off + h, b, 0))
    return pl.pallas_call(
        functools.partial(_moba_kernel, n_blocks=nb),
        grid=(N_HEADS_MOBA, batch),
        in_specs=[pl.BlockSpec(memory_space=pltpu.SMEM), full(q0), full(k0), full(v0), full(g0)],
        out_specs=pl.BlockSpec((None, seq, LANES), lambda h, b: (h, b, 0)),
        out_shape=jax.ShapeDtypeStruct((N_HEADS_MOBA, batch * seq, LANES), BF16),
        scratch_shapes=[pltpu.VMEM((seq, 2 * LANES), BF16),
                        pltpu.VMEM((seq, 2 * LANES), BF16),
                        pltpu.VMEM((LANES, LANES), BF16),
                        pltpu.VMEM((blk, blk), F32),
                        pltpu.VMEM((blk, blk), F32)],
        compiler_params=_cparams(2),
        name="moba_attention",
    )(rel_bias, p, p, p, p)


def _fox_kernel(q_ref, k_ref, v_ref, g_ref, c_ref, o_ref):
    scale = HEAD_DIM ** -0.5
    seq = q_ref.shape[0]
    r = lax.broadcasted_iota(jnp.int32, (Q_TILE, Q_TILE), 0)
    c = lax.broadcasted_iota(jnp.int32, (Q_TILE, Q_TILE), 1)
    c2 = c_ref[...] * LOG2E
    def score_matmul(i):
        return lax.dot_general(q_ref[i * Q_TILE:(i + 1) * Q_TILE, :], k_ref[0:(i + 1) * Q_TILE, :], _NT,
                               preferred_element_type=F32)

    def finish(i, a):
        rows = slice(i * Q_TILE, (i + 1) * Q_TILE)
        n = (i + 1) * Q_TILE
        s = a * (scale * LOG2E) - c2[:, 0:n]
        pieces = [jnp.where(c <= r, s[:, n - Q_TILE:], NEG)]
        values = [v_ref[n - Q_TILE:n, :]]
        if i >= 1:
            pieces.append(s[:, :n - Q_TILE])
            values.append(v_ref[0:n - Q_TILE, :])
        o_ref[rows, :] = _softmax_pv(pieces, values, g_ref[rows, :])

    _skewed_tiles(seq // Q_TILE, score_matmul, finish)


def _fox_attention(p, c, batch, seq, q0, k0, v0, g0):
    full = lambda off: pl.BlockSpec((None, seq, LANES), lambda b, h: (off + h, b, 0))
    return pl.pallas_call(
        _fox_kernel,
        grid=(batch, N_HEADS_FOX),
        in_specs=[full(q0), full(k0), full(v0), full(g0),
                  pl.BlockSpec((None, 1, seq), lambda b, h: (b * N_HEADS_FOX + h, 0, 0))],
        out_specs=pl.BlockSpec((None, seq, LANES), lambda b, h: (h, b, 0)),
        out_shape=jax.ShapeDtypeStruct((N_HEADS_FOX, batch * seq, LANES), BF16),
        compiler_params=_cparams(2),
        name="fox_attention",
    )(p, p, p, p, c)


def _mem_kernel(q_ref, k_ref, v_ref, g_ref, o_ref):
    scale = HEAD_DIM ** -0.5
    n_heads, seq, _ = q_ref.shape
    tiles_per_head = seq // MEM_Q_TILE

    def tile(t):
        h, i = divmod(t, tiles_per_head)
        return h, slice(i * MEM_Q_TILE, (i + 1) * MEM_Q_TILE)

    def score_matmul(t):
        h, rows = tile(t)
        return lax.dot_general(q_ref[h, rows, :], k_ref[h], _NT, preferred_element_type=F32)

    def finish(t, a):
        h, rows = tile(t)
        o_ref[h, rows, :] = _softmax_pv([a * (scale * LOG2E)], [v_ref[h]], g_ref[h, rows, :])

    _skewed_tiles(n_heads * tiles_per_head, score_matmul, finish)


def _mem_attention(p, mkv, batch, seq, n_mem, q0, g0):
    nh = N_HEADS_MEM
    assert q0 % nh == 0 and g0 % nh == 0
    heads = lambda off: pl.BlockSpec((nh, seq, LANES), lambda b: (off // nh, b, 0))
    kv = lambda blk: pl.BlockSpec((nh, n_mem, LANES), lambda b: (blk, b, 0))
    return pl.pallas_call(
        _mem_kernel,
        grid=(batch,),
        in_specs=[heads(q0), kv(0), kv(1), heads(g0)],
        out_specs=pl.BlockSpec((nh, seq, LANES), lambda b: (0, b, 0)),
        out_shape=jax.ShapeDtypeStruct((nh, batch * seq, LANES), BF16),
        compiler_params=_cparams(1),
        name="mem_attention",
    )(p, mkv, mkv, p)


def _out_proj_kernel(ya_ref, yb_ref, ym_ref, w_ref, x_ref, g_ref, o_ref, y_ref, wb_ref, *, row_chunk):
    @pl.when(pl.program_id(0) == 0)
    def _cast_weights():
        def body(c, carry):
            rows = pl.ds(pl.multiple_of(c * row_chunk, row_chunk), row_chunk)
            wb_ref[rows, :] = w_ref[rows, :].astype(BF16)
            return carry
        lax.fori_loop(0, w_ref.shape[0] // row_chunk, body, 0)

    col = 0
    for src in (ya_ref, yb_ref, ym_ref):
        for c in range(src.shape[0]):
            y_ref[:, col * LANES:(col + 1) * LANES] = src[c]
            col += 1
    o = jnp.dot(y_ref[...], wb_ref[...], preferred_element_type=F32)
    ms = jnp.mean(o * o, axis=-1, keepdims=True)
    o_ref[...] = x_ref[...] + o * lax.rsqrt(ms + EPS) * g_ref[...]


def _out_proj(ya, yb, ym, w3, layer, x2, g, *, tm):
    m, d = x2.shape
    dw = w3.shape[1]
    heads = lambda y: pl.BlockSpec((y.shape[0], tm, LANES), lambda i: (0, i, 0))
    return pl.pallas_call(
        functools.partial(_out_proj_kernel, row_chunk=256),
        grid=(m // tm,),
        in_specs=[heads(ya), heads(yb), heads(ym),
                  pl.BlockSpec((None, dw, d), lambda i: (layer, 0, 0), pipeline_mode=pl.Buffered(1)),
                  pl.BlockSpec((tm, d), lambda i: (i, 0)),
                  pl.BlockSpec((1, d), lambda i: (0, 0))],
        out_specs=pl.BlockSpec((tm, d), lambda i: (i, 0)),
        out_shape=jax.ShapeDtypeStruct((m, d), F32),
        scratch_shapes=[pltpu.VMEM((tm, dw), BF16), pltpu.VMEM((dw, d), BF16)],
        compiler_params=_cparams(1),
        name="out_proj",
    )(ya, yb, ym, w3, x2, g)


def kernel(x, mem, w_in, b_forget, w_mem_kv, w_out, g_pre, g_post, g_mem, rel_bias):
    batch, seq, d = x.shape
    n_mem = mem.shape[1]
    depth = w_in.shape[0]
    wa, wb, wm = N_HEADS_MOBA * HEAD_DIM, N_HEADS_FOX * HEAD_DIM, N_HEADS_MEM * HEAD_DIM
    f0 = 4 * wa + 4 * wb
    f1 = f0 + N_HEADS_FOX
    ha, hb, hm = N_HEADS_MOBA, N_HEADS_FOX, N_HEADS_MEM
    qa0, ka0, va0, ga0 = 0, ha, 2 * ha, 3 * ha
    qb0, kb0, vb0, gb0 = 4 * ha, 4 * ha + hb, 4 * ha + 2 * hb, 4 * ha + 3 * hb
    qm0, gm0 = 4 * ha + 4 * hb, 4 * ha + 4 * hb + hm

    x2 = x.reshape(batch * seq, d)
    mem2 = mem.reshape(batch * n_mem, d)
    for layer in range(depth):
        b_vec = jnp.pad(b_forget[layer], (0, LANES - N_HEADS_FOX)).reshape(1, LANES)

        n_in = w_in.shape[2]
        assert n_in == f1 + 2 * wm
        w_t3 = jnp.transpose(w_in, (0, 2, 1)).reshape(depth * n_in, d // LANES, LANES)
        w_main_t, w_gate_t = _w_repack(w_t3, layer, n_in, f0, N_HEADS_FOX, rows=512)
        p, fb = _norm_proj(x2, g_pre[layer].reshape(1, d), w_main_t, w_gate_t, w_transposed=True,
                           tm=1024, tn=1024)
        mkv, _ = _norm_proj(mem2, g_mem[layer].reshape(1, d), w_mem_kv[layer], w_transposed=False,
                            tm=batch * n_mem, tn=2 * wm)
        c = _forget_scan(fb, b_vec, batch, seq)
        c = c[:, :N_HEADS_FOX].reshape(batch * N_HEADS_FOX, 1, seq)

        ya = _moba_attention(p, rel_bias, batch, seq, qa0, ka0, va0, ga0)
        yb = _fox_attention(p, c, batch, seq, qb0, kb0, vb0, gb0)
        ym = _mem_attention(p, mkv, batch, seq, n_mem, qm0, g0=gm0)
        x2 = _out_proj(ya, yb, ym, w_out, layer, x2, g_post[layer].reshape(1, d), tm=512)
    return x2.reshape(batch, seq, d)
```

```python
import functools
import math

import jax
import jax.numpy as jnp
from jax import lax
from jax.experimental import pallas as pl
from jax.experimental.pallas import tpu as pltpu

HEAD_DIM = 128
N_HEADS_MOBA = 6
N_HEADS_FOX = 6
N_HEADS_MEM = 4
MOBA_BLOCK = 256
MOBA_TOPK = 3
T5_BUCKETS = 32
T5_MAX_DIST = 128
EPS = 1e-6
NEG = -1e30

LANES = 128
Q_TILE = MOBA_BLOCK
MEM_Q_TILE = 512
VMEM_LIMIT_BYTES = 56 * 1024 * 1024

F32 = jnp.float32
BF16 = jnp.bfloat16
_NT = (((1,), (1,)), ((), ()))


def _cparams(n_axes):
    return pltpu.CompilerParams(dimension_semantics=("arbitrary",) * n_axes,
                                vmem_limit_bytes=VMEM_LIMIT_BYTES)


def _norm_proj_kernel(*refs, n_col_blocks, w_transposed, with_gate_logits, row_chunk):
    if with_gate_logits:
        x_ref, g_ref, w_ref, wf_ref, p_ref, f_ref, h_ref = refs
    else:
        x_ref, g_ref, w_ref, p_ref, h_ref = refs
    tm = x_ref.shape[0]

    @pl.when(pl.program_id(1) == 0)
    def _normalize():
        def body(c, carry):
            rows = pl.ds(pl.multiple_of(c * row_chunk, row_chunk), row_chunk)
            xf = x_ref[rows, :]
            ms = jnp.mean(xf * xf, axis=-1, keepdims=True)
            h_ref[rows, :] = (xf * lax.rsqrt(ms + EPS) * g_ref[...]).astype(BF16)
            return carry
        lax.fori_loop(0, tm // row_chunk, body, 0)
        if with_gate_logits:
            f_ref[...] = lax.dot_general(h_ref[...], wf_ref[...], _NT, preferred_element_type=F32)

    w = w_ref[...].astype(BF16)
    if w_transposed:
        acc = lax.dot_general(h_ref[...], w, _NT, preferred_element_type=F32)
    else:
        acc = jnp.dot(h_ref[...], w, preferred_element_type=F32)
    for c in range(n_col_blocks):
        p_ref[c] = acc[:, c * LANES:(c + 1) * LANES].astype(BF16)


def _norm_proj(x2, g, w, wf_t=None, *, w_transposed, tm, tn):
    m, d = x2.shape
    n = w.shape[0] if w_transposed else w.shape[1]
    ncb = tn // LANES
    with_f = wf_t is not None
    assert n % tn == 0 and m % tm == 0
    w_spec = (pl.BlockSpec((tn, d), lambda i, j: (j, 0)) if w_transposed
              else pl.BlockSpec((d, tn), lambda i, j: (0, j)))
    in_specs = [pl.BlockSpec((tm, d), lambda i, j: (i, 0)),
                pl.BlockSpec((1, d), lambda i, j: (0, 0)), w_spec]
    out_shape = [jax.ShapeDtypeStruct((n // LANES, m, LANES), BF16)]
    out_specs = [pl.BlockSpec((ncb, tm, LANES), lambda i, j: (j, i, 0))]
    args = [x2, g, w]
    if with_f:
        in_specs.append(pl.BlockSpec((LANES, d), lambda i, j: (0, 0)))
        out_shape.append(jax.ShapeDtypeStruct((m, LANES), F32))
        out_specs.append(pl.BlockSpec((tm, LANES), lambda i, j: (i, 0)))
        args.append(wf_t)
    res = pl.pallas_call(
        functools.partial(_norm_proj_kernel, n_col_blocks=ncb, w_transposed=w_transposed,
                          with_gate_logits=with_f, row_chunk=256),
        grid=(m // tm, n // tn),
        in_specs=in_specs, out_specs=out_specs, out_shape=out_shape,
        scratch_shapes=[pltpu.VMEM((tm, d), BF16)],
        compiler_params=_cparams(2),
        name="norm_proj_gate" if with_f else "norm_proj",
    )(*args)
    return res if with_f else (res[0], None)


def _w_repack_kernel(src_ref, main_ref, gate_ref, buf_ref, gbuf_ref, sem_ref, gsem_ref, *,
                     rows, base_row, n_aligned_blocks, n_gate):
    t = pl.program_id(0)
    n_steps = pl.num_programs(0)
    n_slices = src_ref.shape[1]

    def block_copies(step, slot):
        row0 = base_row + step * rows + jnp.where(step >= n_aligned_blocks, n_gate, 0)
        return [pltpu.make_async_copy(src_ref.at[pl.ds(row0, rows), s, :],
                                      buf_ref.at[slot, :, pl.ds(s * LANES, LANES)],
                                      sem_ref.at[slot]) for s in range(n_slices)]

    def gate_copies():
        gate_row0 = base_row + n_aligned_blocks * rows
        return [pltpu.make_async_copy(src_ref.at[pl.ds(gate_row0, 8), s, :],
                                      gbuf_ref.at[:, pl.ds(s * LANES, LANES)],
                                      gsem_ref.at[0]) for s in range(n_slices)]

    @pl.when(t == 0)
    def _first():
        for cp in gate_copies() + block_copies(0, 0):
            cp.start()

    @pl.when(t + 1 < n_steps)
    def _prefetch():
        for cp in block_copies(t + 1, (t + 1) % 2):
            cp.start()

    @pl.when(t == 0)
    def _gate_rows():
        for cp in gate_copies():
            cp.wait()
        row = lax.broadcasted_iota(jnp.int32, gbuf_ref.shape, 0)
        g8 = jnp.where(row < n_gate, gbuf_ref[...], 0.0)
        gate_ref[...] = jnp.concatenate(
            [g8, jnp.zeros((gate_ref.shape[0] - 8, g8.shape[1]), F32)], axis=0).astype(BF16)

    for cp in block_copies(t, t % 2):
        cp.wait()
    main_ref[...] = buf_ref[t % 2].astype(BF16)


def _w_repack(w_t3, layer, n_total, n_aligned, n_gate, *, rows):
    n_slices = w_t3.shape[1]
    d = n_slices * LANES
    n_main = n_total - n_gate
    assert n_aligned % rows == 0 and n_main % rows == 0 and n_gate <= 8
    return pl.pallas_call(
        functools.partial(_w_repack_kernel, rows=rows, base_row=layer * n_total,
                          n_aligned_blocks=n_aligned // rows, n_gate=n_gate),
        grid=(n_main // rows,),
        in_specs=[pl.BlockSpec(memory_space=pl.ANY)],
        out_specs=[pl.BlockSpec((rows, d), lambda t: (t, 0)),
                   pl.BlockSpec((LANES, d), lambda t: (0, 0))],
        out_shape=[jax.ShapeDtypeStruct((n_main, d), BF16),
                   jax.ShapeDtypeStruct((LANES, d), BF16)],
        scratch_shapes=[pltpu.VMEM((2, rows, d), F32), pltpu.VMEM((8, d), F32),
                        pltpu.SemaphoreType.DMA((2,)), pltpu.SemaphoreType.DMA((1,))],
        compiler_params=_cparams(1),
        name="w_repack",
    )(w_t3)


def _split3(v):
    hi = v.astype(BF16).astype(F32)
    mid = (v - hi).astype(BF16).astype(F32)
    lo = (v - hi - mid).astype(BF16).astype(F32)
    return hi, mid, lo


def _forget_scan_kernel(f_ref, b_ref, kc_ref):
    z = f_ref[...] + b_ref[...]
    logf = jnp.minimum(z, 0.0) - jnp.log1p(jnp.exp(-jnp.abs(z)))
    x = logf.T[0:8, :]
    s_len = x.shape[1]
    lane = lax.broadcasted_iota(jnp.int32, x.shape, 1)
    shift = 1
    while shift < s_len:
        x = x + jnp.where(lane >= shift, pltpu.roll(x, shift, 1), 0.0)
        shift *= 2
    hi, mid, lo = _split3(x * (-1.0 / HEAD_DIM ** -0.5))
    row = lax.broadcasted_iota(jnp.int32, x.shape, 0)
    pad = jnp.zeros((LANES - 8, s_len), F32)
    for h in range(kc_ref.shape[0]):
        rows3 = jnp.where(row == 0, hi[h:h + 1, :],
                          jnp.where(row == 1, mid[h:h + 1, :],
                                    jnp.where(row == 2, lo[h:h + 1, :], 0.0)))
        kc_ref[h] = jnp.concatenate([rows3, pad], axis=0).T.astype(BF16)


def _forget_scan(fb, bvec, batch, seq, n_heads):
    assert n_heads <= 8
    return pl.pallas_call(
        _forget_scan_kernel,
        grid=(batch,),
        in_specs=[pl.BlockSpec((seq, LANES), lambda b: (b, 0)),
                  pl.BlockSpec((1, LANES), lambda b: (0, 0))],
        out_specs=pl.BlockSpec((None, n_heads, seq, LANES), lambda b: (b, 0, 0, 0)),
        out_shape=jax.ShapeDtypeStruct((batch, n_heads, seq, LANES), BF16),
        compiler_params=_cparams(1),
        name="forget_scan",
    )(fb, bvec)


LOG2E = math.log2(math.e)


def _softmax_pv_folded(a, v_aug, g):
    m = jnp.max(a, axis=1, keepdims=True)
    p = jnp.exp2((a - m) * (HEAD_DIM ** -0.5 * LOG2E)).astype(BF16)
    acc = jnp.dot(p, v_aug, preferred_element_type=F32)
    gf = g.astype(F32)
    silu = gf * (1.0 / (1.0 + jnp.exp(-gf)))
    return (acc[:, :LANES] / acc[:, LANES:] * silu).astype(BF16)


def _skewed_tiles(n_tiles, score_matmul, finish):
    nxt = score_matmul(0)
    for i in range(n_tiles):
        cur = nxt
        if i + 1 < n_tiles:
            nxt = score_matmul(i + 1)
        finish(i, cur)


def _t5_bias(rb_ref, h, dist):
    max_exact = T5_BUCKETS // 2
    n = jnp.maximum(dist, 0)
    nf = jnp.maximum(n, 1).astype(F32)
    large = max_exact + (jnp.log(nf / max_exact) / math.log(T5_MAX_DIST / max_exact)
                         * (T5_BUCKETS - max_exact)).astype(jnp.int32)
    large = jnp.minimum(large, T5_BUCKETS - 1)
    bucket = jnp.where(n < max_exact, n, large)
    bias = jnp.zeros(dist.shape, F32)
    for kk in range(T5_BUCKETS):
        bias = jnp.where(bucket == kk, rb_ref[h, kk], bias)
    return bias


def _moba_kernel(rb_ref, q_ref, k_ref, v_ref, g_ref, o_ref,
                 kaug_ref, qaug_ref, vaug_ref, kmean_ref, bown_ref, bprev_ref, *, n_blocks):
    h = pl.program_id(0)
    b = pl.program_id(1)
    inv_scale = 1.0 / HEAD_DIM ** -0.5
    blk = MOBA_BLOCK
    seq = n_blocks * blk
    n_far = 3
    far_bias = rb_ref[h, T5_BUCKETS - 1]

    @pl.when((h == 0) & (b == 0))
    def _ones():
        vaug_ref[:, LANES:] = jnp.ones((seq, LANES), BF16)

    @pl.when(b == 0)
    def _per_head_constants():
        srow = lax.broadcasted_iota(jnp.int32, (seq, LANES), 0)
        col = lax.broadcasted_iota(jnp.int32, (seq, LANES), 1)
        onehot = jnp.where(col * blk == srow - (srow & (blk - 1)), 1.0, 0.0)
        hi, mid, lo = _split3(jnp.full((seq, LANES), far_bias * inv_scale, F32))
        kaug_ref[:, LANES:] = jnp.where(col == 8, hi, jnp.where(col == 9, mid, jnp.where(
            col == 10, lo, onehot))).astype(BF16)
        r = lax.broadcasted_iota(jnp.int32, (blk, blk), 0)
        c = lax.broadcasted_iota(jnp.int32, (blk, blk), 1)
        dist = r - c
        bown_ref[...] = jnp.where(dist >= 0, (_t5_bias(rb_ref, h, dist) - far_bias) * inv_scale, NEG)
        bprev_ref[...] = (_t5_bias(rb_ref, h, dist + blk) - far_bias) * inv_scale

    kaug_ref[:, :LANES] = k_ref[...]
    vaug_ref[:, :LANES] = v_ref[...]
    row8 = lax.broadcasted_iota(jnp.int32, (8, LANES), 0)
    km = jnp.zeros((8, LANES), F32)
    for j in range(n_blocks):
        mean_j = jnp.mean(k_ref[j * blk:(j + 1) * blk, :].astype(F32), axis=0, keepdims=True)
        km = jnp.where(row8 == j, mean_j, km)
    kmean_ref[...] = jnp.concatenate([km, jnp.zeros((LANES - 8, LANES), F32)], axis=0).astype(BF16)

    gate_t = lax.dot_general(kmean_ref[...], q_ref[...], _NT, preferred_element_type=F32)[0:8, :]
    jrow = lax.broadcasted_iota(jnp.int32, gate_t.shape, 0)
    qpos = lax.broadcasted_iota(jnp.int32, gate_t.shape, 1)
    qblk = jnp.right_shift(qpos, int(math.log2(blk)))
    gm = jnp.where(jrow < qblk, gate_t, NEG)
    sel_bias = jnp.zeros(gate_t.shape, F32)
    for j in range(n_blocks):
        gj = gm[j:j + 1, :]
        beats = (gm > gj) | ((gm == gj) & (jrow < j))
        rank = jnp.sum(beats.astype(F32), axis=0, keepdims=True)
        allowed = ((rank < MOBA_TOPK) & (qblk[0:1, :] > j)) | (qblk[0:1, :] == j)
        sel_bias = jnp.where(jrow == j, jnp.where(allowed, 0.0, NEG), sel_bias)
    sel_cols = jnp.concatenate([sel_bias, jnp.zeros((LANES - 8, seq), F32)], axis=0).T
    col = lax.broadcasted_iota(jnp.int32, sel_cols.shape, 1)
    qaug_ref[:, :LANES] = q_ref[...]
    qaug_ref[:, LANES:] = jnp.where((col >= 8) & (col < 8 + n_far), 1.0, sel_cols).astype(BF16)

    def score_matmul(i):
        return lax.dot_general(qaug_ref[i * blk:(i + 1) * blk, :], kaug_ref[0:(i + 1) * blk, :], _NT,
                               preferred_element_type=F32)

    def finish(i, a):
        rows = slice(i * blk, (i + 1) * blk)
        n = (i + 1) * blk
        parts = [a[:, n - blk:] + bown_ref[...]]
        if i >= 1:
            parts.insert(0, a[:, n - 2 * blk:n - blk] + bprev_ref[...])
        if i >= 2:
            parts.insert(0, a[:, :n - 2 * blk])
        a = parts[0] if len(parts) == 1 else jnp.concatenate(parts, axis=1)
        o_ref[rows, :] = _softmax_pv_folded(a, vaug_ref[0:n, :], g_ref[rows, :])

    _skewed_tiles(n_blocks, score_matmul, finish)


def _moba_attention(p, rel_bias, batch, seq, q0, k0, v0, g0):
    nb = seq // MOBA_BLOCK
    assert seq % MOBA_BLOCK == 0 and MOBA_TOPK < 8 and nb <= 8 and MOBA_BLOCK > T5_MAX_DIST
    blk = MOBA_BLOCK
    full = lambda off: pl.BlockSpec((None, seq, LANES), lambda h, b: (off + h, b, 0))
    return pl.pallas_call(
        functools.partial(_moba_kernel, n_blocks=nb),
        grid=(N_HEADS_MOBA, batch),
        in_specs=[pl.BlockSpec(memory_space=pltpu.SMEM), full(q0), full(k0), full(v0), full(g0)],
        out_specs=pl.BlockSpec((None, seq, LANES), lambda h, b: (h, b, 0)),
        out_shape=jax.ShapeDtypeStruct((N_HEADS_MOBA, batch * seq, LANES), BF16),
        scratch_shapes=[pltpu.VMEM((seq, 2 * LANES), BF16),
                        pltpu.VMEM((seq, 2 * LANES), BF16),
                        pltpu.VMEM((seq, 2 * LANES), BF16),
                        pltpu.VMEM((LANES, LANES), BF16),
                        pltpu.VMEM((blk, blk), F32),
                        pltpu.VMEM((blk, blk), F32)],
        compiler_params=_cparams(2),
        name="moba_attention",
    )(rel_bias, p, p, p, p)


def _fox_kernel(q_ref, k_ref, v_ref, g_ref, kc_ref, o_ref, qaug_ref, kaug_ref, vaug_ref, causal_ref):
    seq = q_ref.shape[0]

    @pl.when((pl.program_id(0) == 0) & (pl.program_id(1) == 0))
    def _constants():
        col = lax.broadcasted_iota(jnp.int32, (seq, LANES), 1)
        qaug_ref[:, LANES:] = jnp.where(col < 3, 1.0, 0.0).astype(BF16)
        vaug_ref[:, LANES:] = jnp.ones((seq, LANES), BF16)
        r = lax.broadcasted_iota(jnp.int32, (Q_TILE, Q_TILE), 0)
        c = lax.broadcasted_iota(jnp.int32, (Q_TILE, Q_TILE), 1)
        causal_ref[...] = jnp.where(c <= r, 0.0, NEG)

    qaug_ref[:, :LANES] = q_ref[...]
    kaug_ref[:, :LANES] = k_ref[...]
    kaug_ref[:, LANES:] = kc_ref[...]
    vaug_ref[:, :LANES] = v_ref[...]

    def score_matmul(i):
        return lax.dot_general(qaug_ref[i * Q_TILE:(i + 1) * Q_TILE, :], kaug_ref[0:(i + 1) * Q_TILE, :],
                               _NT, preferred_element_type=F32)

    def finish(i, a):
        rows = slice(i * Q_TILE, (i + 1) * Q_TILE)
        n = (i + 1) * Q_TILE
        own = a[:, n - Q_TILE:] + causal_ref[...]
        a = own if i == 0 else jnp.concatenate([a[:, :n - Q_TILE], own], axis=1)
        o_ref[rows, :] = _softmax_pv_folded(a, vaug_ref[0:n, :], g_ref[rows, :])

    _skewed_tiles(seq // Q_TILE, score_matmul, finish)


def _fox_attention(p, kc, batch, seq, q0, k0, v0, g0):
    full = lambda off: pl.BlockSpec((None, seq, LANES), lambda b, h: (off + h, b, 0))
    return pl.pallas_call(
        _fox_kernel,
        grid=(batch, N_HEADS_FOX),
        in_specs=[full(q0), full(k0), full(v0), full(g0),
                  pl.BlockSpec((None, None, seq, LANES), lambda b, h: (b, h, 0, 0))],
        out_specs=pl.BlockSpec((None, seq, LANES), lambda b, h: (h, b, 0)),
        out_shape=jax.ShapeDtypeStruct((N_HEADS_FOX, batch * seq, LANES), BF16),
        scratch_shapes=[pltpu.VMEM((seq, 2 * LANES), BF16), pltpu.VMEM((seq, 2 * LANES), BF16),
                        pltpu.VMEM((seq, 2 * LANES), BF16), pltpu.VMEM((Q_TILE, Q_TILE), F32)],
        compiler_params=_cparams(2),
        name="fox_attention",
    )(p, p, p, p, kc)


def _mem_kernel(q_ref, k_ref, v_ref, g_ref, o_ref, vaug_ref):
    n_heads, seq, _ = q_ref.shape
    tiles_per_head = seq // MEM_Q_TILE

    @pl.when(pl.program_id(0) == 0)
    def _ones():
        vaug_ref[:, :, LANES:] = jnp.ones(v_ref.shape, BF16)

    vaug_ref[:, :, :LANES] = v_ref[...]

    def tile(t):
        h, i = divmod(t, tiles_per_head)
        return h, slice(i * MEM_Q_TILE, (i + 1) * MEM_Q_TILE)

    def score_matmul(t):
        h, rows = tile(t)
        return lax.dot_general(q_ref[h, rows, :], k_ref[h], _NT, preferred_element_type=F32)

    def finish(t, a):
        h, rows = tile(t)
        o_ref[h, rows, :] = _softmax_pv_folded(a, vaug_ref[h], g_ref[h, rows, :])

    _skewed_tiles(n_heads * tiles_per_head, score_matmul, finish)


def _mem_attention(p, mkv, batch, seq, n_mem, q0, g0):
    nh = N_HEADS_MEM
    assert q0 % nh == 0 and g0 % nh == 0
    heads = lambda off: pl.BlockSpec((nh, seq, LANES), lambda b: (off // nh, b, 0))
    kv = lambda blk: pl.BlockSpec((nh, n_mem, LANES), lambda b: (blk, b, 0))
    return pl.pallas_call(
        _mem_kernel,
        grid=(batch,),
        in_specs=[heads(q0), kv(0), kv(1), heads(g0)],
        out_specs=pl.BlockSpec((nh, seq, LANES), lambda b: (0, b, 0)),
        out_shape=jax.ShapeDtypeStruct((nh, batch * seq, LANES), BF16),
        scratch_shapes=[pltpu.VMEM((nh, n_mem, 2 * LANES), BF16)],
        compiler_params=_cparams(1),
        name="mem_attention",
    )(p, mkv, mkv, p)


def _out_proj_kernel(ya_ref, yb_ref, ym_ref, w_ref, x_ref, g_ref, o_ref, y_ref, wb_ref, *, row_chunk):
    @pl.when(pl.program_id(0) == 0)
    def _cast_weights():
        def body(c, carry):
            rows = pl.ds(pl.multiple_of(c * row_chunk, row_chunk), row_chunk)
            wb_ref[rows, :] = w_ref[rows, :].astype(BF16)
            return carry
        lax.fori_loop(0, w_ref.shape[0] // row_chunk, body, 0)

    col = 0
    for src in (ya_ref, yb_ref, ym_ref):
        for c in range(src.shape[0]):
            y_ref[:, col * LANES:(col + 1) * LANES] = src[c]
            col += 1
    o = jnp.dot(y_ref[...], wb_ref[...], preferred_element_type=F32)
    ms = jnp.mean(o * o, axis=-1, keepdims=True)
    o_ref[...] = x_ref[...] + o * lax.rsqrt(ms + EPS) * g_ref[...]


def _out_proj(ya, yb, ym, w3, layer, x2, g, *, tm):
    m, d = x2.shape
    dw = w3.shape[1]
    heads = lambda y: pl.BlockSpec((y.shape[0], tm, LANES), lambda i: (0, i, 0))
    return pl.pallas_call(
        functools.partial(_out_proj_kernel, row_chunk=256),
        grid=(m // tm,),
        in_specs=[heads(ya), heads(yb), heads(ym),
                  pl.BlockSpec((None, dw, d), lambda i: (layer, 0, 0), pipeline_mode=pl.Buffered(1)),
                  pl.BlockSpec((tm, d), lambda i: (i, 0)),
                  pl.BlockSpec((1, d), lambda i: (0, 0))],
        out_specs=pl.BlockSpec((tm, d), lambda i: (i, 0)),
        out_shape=jax.ShapeDtypeStruct((m, d), F32),
        scratch_shapes=[pltpu.VMEM((tm, dw), BF16), pltpu.VMEM((dw, d), BF16)],
        compiler_params=_cparams(1),
        name="out_proj",
    )(ya, yb, ym, w3, x2, g)


def kernel(x, mem, w_in, b_forget, w_mem_kv, w_out, g_pre, g_post, g_mem, rel_bias):
    batch, seq, d = x.shape
    n_mem = mem.shape[1]
    depth = w_in.shape[0]
    wa, wb, wm = N_HEADS_MOBA * HEAD_DIM, N_HEADS_FOX * HEAD_DIM, N_HEADS_MEM * HEAD_DIM
    f0 = 4 * wa + 4 * wb
    f1 = f0 + N_HEADS_FOX
    ha, hb, hm = N_HEADS_MOBA, N_HEADS_FOX, N_HEADS_MEM
    qa0, ka0, va0, ga0 = 0, ha, 2 * ha, 3 * ha
    qb0, kb0, vb0, gb0 = 4 * ha, 4 * ha + hb, 4 * ha + 2 * hb, 4 * ha + 3 * hb
    qm0, gm0 = 4 * ha + 4 * hb, 4 * ha + 4 * hb + hm

    x2 = x.reshape(batch * seq, d)
    mem2 = mem.reshape(batch * n_mem, d)
    for layer in range(depth):
        b_vec = jnp.pad(b_forget[layer], (0, LANES - N_HEADS_FOX)).reshape(1, LANES)

        n_in = w_in.shape[2]
        assert n_in == f1 + 2 * wm
        w_t3 = jnp.transpose(w_in, (0, 2, 1)).reshape(depth * n_in, d // LANES, LANES)
        w_main_t, w_gate_t = _w_repack(w_t3, layer, n_in, f0, N_HEADS_FOX, rows=512)
        p, fb = _norm_proj(x2, g_pre[layer].reshape(1, d), w_main_t, w_gate_t, w_transposed=True,
                           tm=1024, tn=1024)
        mkv, _ = _norm_proj(mem2, g_mem[layer].reshape(1, d), w_mem_kv[layer], w_transposed=False,
                            tm=batch * n_mem, tn=2 * wm)
        kc = _forget_scan(fb, b_vec, batch, seq, N_HEADS_FOX)

        ya = _moba_attention(p, rel_bias, batch, seq, qa0, ka0, va0, ga0)
        yb = _fox_attention(p, kc, batch, seq, qb0, kb0, vb0, gb0)
        ym = _mem_attention(p, mkv, batch, seq, n_mem, qm0, g0=gm0)
        x2 = _out_proj(ya, yb, ym, w_out, layer, x2, g_post[layer].reshape(1, d), tm=512)
    return x2.reshape(batch, seq, d)
```

```python
import functools
import math

import jax
import jax.numpy as jnp
from jax import lax
from jax.experimental import pallas as pl
from jax.experimental.pallas import tpu as pltpu

HEAD_DIM = 128
N_HEADS_MOBA = 6
N_HEADS_FOX = 6
N_HEADS_MEM = 4
MOBA_BLOCK = 256
MOBA_TOPK = 3
T5_BUCKETS = 32
T5_MAX_DIST = 128
EPS = 1e-6
NEG = -1e30

LANES = 128
Q_TILE = MOBA_BLOCK
MEM_Q_TILE = 512
VMEM_LIMIT_BYTES = 56 * 1024 * 1024

F32 = jnp.float32
BF16 = jnp.bfloat16
_NT = (((1,), (1,)), ((), ()))


def _cparams(n_axes):
    return pltpu.CompilerParams(dimension_semantics=("arbitrary",) * n_axes,
                                vmem_limit_bytes=VMEM_LIMIT_BYTES)


def _norm_proj_kernel(*refs, n_col_blocks, w_transposed, with_gate_logits, row_chunk):
    if with_gate_logits:
        x_ref, g_ref, w_ref, wf_ref, p_ref, f_ref, h_ref = refs
    else:
        x_ref, g_ref, w_ref, p_ref, h_ref = refs
    tm = x_ref.shape[0]

    @pl.when(pl.program_id(1) == 0)
    def _normalize():
        def body(c, carry):
            rows = pl.ds(pl.multiple_of(c * row_chunk, row_chunk), row_chunk)
            xf = x_ref[rows, :]
            ms = jnp.mean(xf * xf, axis=-1, keepdims=True)
            h_ref[rows, :] = (xf * lax.rsqrt(ms + EPS) * g_ref[...]).astype(BF16)
            return carry
        lax.fori_loop(0, tm // row_chunk, body, 0)
        if with_gate_logits:
            f_ref[...] = lax.dot_general(h_ref[...], wf_ref[...], _NT, preferred_element_type=F32)

    w = w_ref[...].astype(BF16)
    if w_transposed:
        acc = lax.dot_general(h_ref[...], w, _NT, preferred_element_type=F32)
    else:
        acc = jnp.dot(h_ref[...], w, preferred_element_type=F32)
    for c in range(n_col_blocks):
        p_ref[c] = acc[:, c * LANES:(c + 1) * LANES].astype(BF16)


def _norm_proj(x2, g, w, wf_t=None, *, w_transposed, tm, tn):
    m, d = x2.shape
    n = w.shape[0] if w_transposed else w.shape[1]
    ncb = tn // LANES
    with_f = wf_t is not None
    assert n % tn == 0 and m % tm == 0
    w_spec = (pl.BlockSpec((tn, d), lambda i, j: (j, 0)) if w_transposed
              else pl.BlockSpec((d, tn), lambda i, j: (0, j)))
    in_specs = [pl.BlockSpec((tm, d), lambda i, j: (i, 0)),
                pl.BlockSpec((1, d), lambda i, j: (0, 0)), w_spec]
    out_shape = [jax.ShapeDtypeStruct((n // LANES, m, LANES), BF16)]
    out_specs = [pl.BlockSpec((ncb, tm, LANES), lambda i, j: (j, i, 0))]
    args = [x2, g, w]
    if with_f:
        in_specs.append(pl.BlockSpec((LANES, d), lambda i, j: (0, 0)))
        out_shape.append(jax.ShapeDtypeStruct((m, LANES), F32))
        out_specs.append(pl.BlockSpec((tm, LANES), lambda i, j: (i, 0)))
        args.append(wf_t)
    res = pl.pallas_call(
        functools.partial(_norm_proj_kernel, n_col_blocks=ncb, w_transposed=w_transposed,
                          with_gate_logits=with_f, row_chunk=256),
        grid=(m // tm, n // tn),
        in_specs=in_specs, out_specs=out_specs, out_shape=out_shape,
        scratch_shapes=[pltpu.VMEM((tm, d), BF16)],
        compiler_params=_cparams(2),
        name="norm_proj_gate" if with_f else "norm_proj",
    )(*args)
    return res if with_f else (res[0], None)


def _w_repack_kernel(src_ref, main_ref, gate_ref, buf_ref, gbuf_ref, sem_ref, gsem_ref, *,
                     rows, base_row, n_aligned_blocks, n_gate):
    t = pl.program_id(0)
    n_steps = pl.num_programs(0)
    n_slices = src_ref.shape[1]

    def block_copies(step, slot):
        row0 = base_row + step * rows + jnp.where(step >= n_aligned_blocks, n_gate, 0)
        return [pltpu.make_async_copy(src_ref.at[pl.ds(row0, rows), s, :],
                                      buf_ref.at[slot, :, pl.ds(s * LANES, LANES)],
                                      sem_ref.at[slot]) for s in range(n_slices)]

    def gate_copies():
        gate_row0 = base_row + n_aligned_blocks * rows
        return [pltpu.make_async_copy(src_ref.at[pl.ds(gate_row0, 8), s, :],
                                      gbuf_ref.at[:, pl.ds(s * LANES, LANES)],
                                      gsem_ref.at[0]) for s in range(n_slices)]

    @pl.when(t == 0)
    def _first():
        for cp in gate_copies() + block_copies(0, 0):
            cp.start()

    @pl.when(t + 1 < n_steps)
    def _prefetch():
        for cp in block_copies(t + 1, (t + 1) % 2):
            cp.start()

    @pl.when(t == 0)
    def _gate_rows():
        for cp in gate_copies():
            cp.wait()
        row = lax.broadcasted_iota(jnp.int32, gbuf_ref.shape, 0)
        g8 = jnp.where(row < n_gate, gbuf_ref[...], 0.0)
        gate_ref[...] = jnp.concatenate(
            [g8, jnp.zeros((gate_ref.shape[0] - 8, g8.shape[1]), F32)], axis=0).astype(BF16)

    for cp in block_copies(t, t % 2):
        cp.wait()
    main_ref[...] = buf_ref[t % 2].astype(BF16)


def _w_repack(w_t3, layer, n_total, n_aligned, n_gate, *, rows):
    n_slices = w_t3.shape[1]
    d = n_slices * LANES
    n_main = n_total - n_gate
    assert n_aligned % rows == 0 and n_main % rows == 0 and n_gate <= 8
    return pl.pallas_call(
        functools.partial(_w_repack_kernel, rows=rows, base_row=layer * n_total,
                          n_aligned_blocks=n_aligned // rows, n_gate=n_gate),
        grid=(n_main // rows,),
        in_specs=[pl.BlockSpec(memory_space=pl.ANY)],
        out_specs=[pl.BlockSpec((rows, d), lambda t: (t, 0)),
                   pl.BlockSpec((LANES, d), lambda t: (0, 0))],
        out_shape=[jax.ShapeDtypeStruct((n_main, d), BF16),
                   jax.ShapeDtypeStruct((LANES, d), BF16)],
        scratch_shapes=[pltpu.VMEM((2, rows, d), F32), pltpu.VMEM((8, d), F32),
                        pltpu.SemaphoreType.DMA((2,)), pltpu.SemaphoreType.DMA((1,))],
        compiler_params=_cparams(1),
        name="w_repack",
    )(w_t3)


def _split3(v):
    hi = v.astype(BF16).astype(F32)
    mid = (v - hi).astype(BF16).astype(F32)
    lo = (v - hi - mid).astype(BF16).astype(F32)
    return hi, mid, lo


def _forget_scan_kernel(f_ref, b_ref, kc_ref):
    z = f_ref[...] + b_ref[...]
    logf = jnp.minimum(z, 0.0) - jnp.log1p(jnp.exp(-jnp.abs(z)))
    x = logf.T[0:8, :]
    s_len = x.shape[1]
    lane = lax.broadcasted_iota(jnp.int32, x.shape, 1)
    shift = 1
    while shift < s_len:
        x = x + jnp.where(lane >= shift, pltpu.roll(x, shift, 1), 0.0)
        shift *= 2
    hi, mid, lo = _split3(x * (-1.0 / HEAD_DIM ** -0.5))
    row = lax.broadcasted_iota(jnp.int32, x.shape, 0)
    pad = jnp.zeros((LANES - 8, s_len), F32)
    for h in range(kc_ref.shape[0]):
        rows3 = jnp.where(row == 0, hi[h:h + 1, :],
                          jnp.where(row == 1, mid[h:h + 1, :],
                                    jnp.where(row == 2, lo[h:h + 1, :], 0.0)))
        kc_ref[h] = jnp.concatenate([rows3, pad], axis=0).T.astype(BF16)


def _forget_scan(fb, bvec, batch, seq, n_heads):
    assert n_heads <= 8
    return pl.pallas_call(
        _forget_scan_kernel,
        grid=(batch,),
        in_specs=[pl.BlockSpec((seq, LANES), lambda b: (b, 0)),
                  pl.BlockSpec((1, LANES), lambda b: (0, 0))],
        out_specs=pl.BlockSpec((None, n_heads, seq, LANES), lambda b: (b, 0, 0, 0)),
        out_shape=jax.ShapeDtypeStruct((batch, n_heads, seq, LANES), BF16),
        compiler_params=_cparams(1),
        name="forget_scan",
    )(fb, bvec)


LOG2E = math.log2(math.e)


def _softmax_pv_folded(a, v_aug, g):
    m = jnp.max(a, axis=1, keepdims=True)
    p = jnp.exp2((a - m) * (HEAD_DIM ** -0.5 * LOG2E)).astype(BF16)
    acc = jnp.dot(p, v_aug, preferred_element_type=F32)
    gf = g.astype(F32)
    silu = gf * (1.0 / (1.0 + jnp.exp(-gf)))
    return (acc[:, :LANES] / acc[:, LANES:] * silu).astype(BF16)


def _stream_events(setup, n_tiles, score_matmul, finish):
    scores = {}

    def score(i):
        return lambda: scores.__setitem__(i, score_matmul(i))

    def fin(i):
        return lambda: finish(i, scores.pop(i))

    events = [setup, score(0)]
    for i in range(n_tiles):
        if i + 1 < n_tiles:
            events.append(score(i + 1))
        events.append(fin(i))
    return events


def _run_interleaved(streams, lead=0):
    queues = [list(ev) for ev in streams]
    for _ in range(min(lead, len(queues[0]))):
        queues[0].pop(0)()
    while any(queues):
        for q in queues:
            if q:
                q.pop(0)()


def _t5_bias(rb_ref, h, dist):
    max_exact = T5_BUCKETS // 2
    n = jnp.maximum(dist, 0)
    nf = jnp.maximum(n, 1).astype(F32)
    large = max_exact + (jnp.log(nf / max_exact) / math.log(T5_MAX_DIST / max_exact)
                         * (T5_BUCKETS - max_exact)).astype(jnp.int32)
    large = jnp.minimum(large, T5_BUCKETS - 1)
    bucket = jnp.where(n < max_exact, n, large)
    bias = jnp.zeros(dist.shape, F32)
    for kk in range(T5_BUCKETS):
        bias = jnp.where(bucket == kk, rb_ref[h, kk], bias)
    return bias


def _moba_head_constants(rb_ref, h, kaug_ref, bown_ref, bprev_ref):
    inv_scale = 1.0 / HEAD_DIM ** -0.5
    blk = MOBA_BLOCK
    seq = kaug_ref.shape[0]
    far_bias = rb_ref[h, T5_BUCKETS - 1]
    srow = lax.broadcasted_iota(jnp.int32, (seq, LANES), 0)
    col = lax.broadcasted_iota(jnp.int32, (seq, LANES), 1)
    kaug_ref[:, LANES:] = jnp.where((col & 7) * blk == srow - (srow & (blk - 1)), 1.0, 0.0).astype(BF16)
    r = lax.broadcasted_iota(jnp.int32, (blk, blk), 0)
    c = lax.broadcasted_iota(jnp.int32, (blk, blk), 1)
    dist = r - c
    bown_ref[...] = jnp.where(dist >= 0, (_t5_bias(rb_ref, h, dist) - far_bias) * inv_scale, NEG)
    bprev_ref[...] = (_t5_bias(rb_ref, h, dist + blk) - far_bias) * inv_scale


def _moba_stream(q_ref, k_ref, v_ref, g_ref, o_ref,
                 kaug_ref, qaug_ref, vaug_ref, kmean_ref, bown_ref, bprev_ref):
    blk = MOBA_BLOCK
    seq = q_ref.shape[0]
    n_blocks = seq // blk
    n_chunks = seq // LANES

    def setup():
        _moba_setup(q_ref, k_ref, v_ref, kaug_ref, qaug_ref, vaug_ref, kmean_ref, n_blocks, n_chunks)

    def score_matmul(i):
        return lax.dot_general(qaug_ref[i * blk:(i + 1) * blk, :], kaug_ref[0:(i + 1) * blk, :], _NT,
                               preferred_element_type=F32)

    def finish(i, a):
        rows = slice(i * blk, (i + 1) * blk)
        n = (i + 1) * blk
        parts = [a[:, n - blk:] + bown_ref[...]]
        if i >= 1:
            parts.insert(0, a[:, n - 2 * blk:n - blk] + bprev_ref[...])
        if i >= 2:
            parts.insert(0, a[:, :n - 2 * blk])
        a = parts[0] if len(parts) == 1 else jnp.concatenate(parts, axis=1)
        o_ref[rows, :] = _softmax_pv_folded(a, vaug_ref[0:n, :], g_ref[rows, :])

    return setup, n_blocks, score_matmul, finish


def _moba_setup(q_ref, k_ref, v_ref, kaug_ref, qaug_ref, vaug_ref, kmean_ref, n_blocks, n_chunks):
    blk = MOBA_BLOCK
    seq = q_ref.shape[0]
    kaug_ref[:, :LANES] = k_ref[...]
    vaug_ref[:, :LANES] = v_ref[...]
    vaug_ref[:, LANES:] = jnp.ones((seq, LANES), BF16)
    row8 = lax.broadcasted_iota(jnp.int32, (8, LANES), 0)
    km = jnp.zeros((8, LANES), F32)
    for j in range(n_blocks):
        mean_j = jnp.mean(k_ref[j * blk:(j + 1) * blk, :].astype(F32), axis=0, keepdims=True)
        km = jnp.where(row8 == j, mean_j, km)
    kmean_ref[...] = jnp.concatenate([km, jnp.zeros((LANES - 8, LANES), F32)], axis=0).astype(BF16)

    gate_t = lax.dot_general(kmean_ref[...], q_ref[...], _NT, preferred_element_type=F32)[0:8, :]
    jrow = lax.broadcasted_iota(jnp.int32, gate_t.shape, 0)
    qpos = lax.broadcasted_iota(jnp.int32, gate_t.shape, 1)
    qblk = jnp.right_shift(qpos, int(math.log2(blk)))
    gm = jnp.where(jrow < qblk, gate_t, NEG)
    sel_bias = jnp.zeros(gate_t.shape, F32)
    for j in range(n_blocks):
        gj = gm[j:j + 1, :]
        beats = (gm > gj) | ((gm == gj) & (jrow < j))
        rank = jnp.sum(beats.astype(F32), axis=0, keepdims=True)
        allowed = ((rank < MOBA_TOPK) & (qblk[0:1, :] > j)) | (qblk[0:1, :] == j)
        sel_bias = jnp.where(jrow == j, jnp.where(allowed, 0.0, NEG), sel_bias)
    assert 8 * n_chunks == LANES
    packed_t = jnp.concatenate([sel_bias[:, c * LANES:(c + 1) * LANES] for c in range(n_chunks)], axis=0).T
    lane_chunk = jnp.right_shift(lax.broadcasted_iota(jnp.int32, (LANES, LANES), 1), 3)
    qaug_ref[:, :LANES] = q_ref[...]
    for c in range(n_chunks):
        qaug_ref[c * LANES:(c + 1) * LANES, LANES:] = jnp.where(lane_chunk == c, packed_t, 0.0).astype(BF16)


def _moba_scratch(seq):
    blk = MOBA_BLOCK
    return [pltpu.VMEM((seq, 2 * LANES), BF16),
            pltpu.VMEM((seq, 2 * LANES), BF16),
            pltpu.VMEM((seq, 2 * LANES), BF16),
            pltpu.VMEM((LANES, LANES), BF16),
            pltpu.VMEM((blk, blk), F32),
            pltpu.VMEM((blk, blk), F32)]


def _fox_stream(q_ref, k_ref, v_ref, g_ref, kc_ref, o_ref, qaug_ref, kaug_ref, vaug_ref, causal_ref):
    seq = q_ref.shape[0]

    def setup():
        col = lax.broadcasted_iota(jnp.int32, (seq, LANES), 1)
        qaug_ref[:, :LANES] = q_ref[...]
        qaug_ref[:, LANES:] = jnp.where(col < 3, 1.0, 0.0).astype(BF16)
        kaug_ref[:, :LANES] = k_ref[...]
        kaug_ref[:, LANES:] = kc_ref[...]
        vaug_ref[:, :LANES] = v_ref[...]
        vaug_ref[:, LANES:] = jnp.ones((seq, LANES), BF16)
        r = lax.broadcasted_iota(jnp.int32, (Q_TILE, Q_TILE), 0)
        c = lax.broadcasted_iota(jnp.int32, (Q_TILE, Q_TILE), 1)
        causal_ref[...] = jnp.where(c <= r, 0.0, NEG)

    def score_matmul(i):
        return lax.dot_general(qaug_ref[i * Q_TILE:(i + 1) * Q_TILE, :], kaug_ref[0:(i + 1) * Q_TILE, :],
                               _NT, preferred_element_type=F32)

    def finish(i, a):
        rows = slice(i * Q_TILE, (i + 1) * Q_TILE)
        n = (i + 1) * Q_TILE
        own = a[:, n - Q_TILE:] + causal_ref[...]
        a = own if i == 0 else jnp.concatenate([a[:, :n - Q_TILE], own], axis=1)
        o_ref[rows, :] = _softmax_pv_folded(a, vaug_ref[0:n, :], g_ref[rows, :])

    return setup, seq // Q_TILE, score_matmul, finish


def _fox_scratch(seq):
    return [pltpu.VMEM((seq, 2 * LANES), BF16), pltpu.VMEM((seq, 2 * LANES), BF16),
            pltpu.VMEM((seq, 2 * LANES), BF16), pltpu.VMEM((Q_TILE, Q_TILE), F32)]


FOX_LEAD_EVENTS = 8


def _causal_attention_kernel(*refs):
    rb_ref, qa, ka, va, ga, qb, kb, vb, gb, kc, oa, ob = refs[:12]
    n_moba = len(_moba_scratch(qa.shape[0]))
    moba_scratch, fox_scratch = refs[12:12 + n_moba], refs[12 + n_moba:]
    kaug_ref, _, _, _, bown_ref, bprev_ref = moba_scratch
    h = pl.program_id(0)

    @pl.when(pl.program_id(1) == 0)
    def _():
        _moba_head_constants(rb_ref, h, kaug_ref, bown_ref, bprev_ref)

    _run_interleaved([_stream_events(*_fox_stream(qb, kb, vb, gb, kc, ob, *fox_scratch)),
                      _stream_events(*_moba_stream(qa, ka, va, ga, oa, *moba_scratch))],
                     lead=FOX_LEAD_EVENTS)


def _causal_attention(p, rel_bias, kc, batch, seq, moba_offsets, fox_offsets):
    nb = seq // MOBA_BLOCK
    assert seq % MOBA_BLOCK == 0 and MOBA_TOPK < 8 and nb <= 8 and MOBA_BLOCK > T5_MAX_DIST
    assert N_HEADS_MOBA == N_HEADS_FOX
    full = lambda off: pl.BlockSpec((None, seq, LANES), lambda h, b: (off + h, b, 0))
    out = pl.BlockSpec((None, seq, LANES), lambda h, b: (h, b, 0))
    out_sds = jax.ShapeDtypeStruct((N_HEADS_MOBA, batch * seq, LANES), BF16)
    return pl.pallas_call(
        _causal_attention_kernel,
        grid=(N_HEADS_MOBA, batch),
        in_specs=([pl.BlockSpec(memory_space=pltpu.SMEM)]
                  + [full(off) for off in moba_offsets] + [full(off) for off in fox_offsets]
                  + [pl.BlockSpec((None, None, seq, LANES), lambda h, b: (b, h, 0, 0))]),
        out_specs=[out, out],
        out_shape=[out_sds, out_sds],
        scratch_shapes=_moba_scratch(seq) + _fox_scratch(seq),
        compiler_params=_cparams(2),
        name="causal_attention",
    )(rel_bias, *([p] * 8), kc)


def _mem_kernel(q_ref, k_ref, v_ref, g_ref, o_ref, vaug_ref):
    n_heads, seq, _ = q_ref.shape
    tiles_per_head = seq // MEM_Q_TILE

    @pl.when(pl.program_id(0) == 0)
    def _ones():
        vaug_ref[:, :, LANES:] = jnp.ones(v_ref.shape, BF16)

    vaug_ref[:, :, :LANES] = v_ref[...]

    def tile(t):
        h, i = divmod(t, tiles_per_head)
        return h, slice(i * MEM_Q_TILE, (i + 1) * MEM_Q_TILE)

    def score_matmul(t):
        h, rows = tile(t)
        return lax.dot_general(q_ref[h, rows, :], k_ref[h], _NT, preferred_element_type=F32)

    def finish(t, a):
        h, rows = tile(t)
        o_ref[h, rows, :] = _softmax_pv_folded(a, vaug_ref[h], g_ref[h, rows, :])

    _run_interleaved([_stream_events(lambda: None, n_heads * tiles_per_head, score_matmul, finish)])


def _mem_attention(p, mkv, batch, seq, n_mem, q0, g0):
    nh = N_HEADS_MEM
    assert q0 % nh == 0 and g0 % nh == 0
    heads = lambda off: pl.BlockSpec((nh, seq, LANES), lambda b: (off // nh, b, 0))
    kv = lambda blk: pl.BlockSpec((nh, n_mem, LANES), lambda b: (blk, b, 0))
    return pl.pallas_call(
        _mem_kernel,
        grid=(batch,),
        in_specs=[heads(q0), kv(0), kv(1), heads(g0)],
        out_specs=pl.BlockSpec((nh, seq, LANES), lambda b: (0, b, 0)),
        out_shape=jax.ShapeDtypeStruct((nh, batch * seq, LANES), BF16),
        scratch_shapes=[pltpu.VMEM((nh, n_mem, 2 * LANES), BF16)],
        compiler_params=_cparams(1),
        name="mem_attention",
    )(p, mkv, mkv, p)


def _out_proj_kernel(ya_ref, yb_ref, ym_ref, w_ref, x_ref, g_ref, o_ref, y_ref, wb_ref, *, row_chunk):
    @pl.when(pl.program_id(0) == 0)
    def _cast_weights():
        def body(c, carry):
            rows = pl.ds(pl.multiple_of(c * row_chunk, row_chunk), row_chunk)
            wb_ref[rows, :] = w_ref[rows, :].astype(BF16)
            return carry
        lax.fori_loop(0, w_ref.shape[0] // row_chunk, body, 0)

    col = 0
    for src in (ya_ref, yb_ref, ym_ref):
        for c in range(src.shape[0]):
            y_ref[:, col * LANES:(col + 1) * LANES] = src[c]
            col += 1
    o = jnp.dot(y_ref[...], wb_ref[...], preferred_element_type=F32)
    ms = jnp.mean(o * o, axis=-1, keepdims=True)
    o_ref[...] = x_ref[...] + o * lax.rsqrt(ms + EPS) * g_ref[...]


def _out_proj(ya, yb, ym, w3, layer, x2, g, *, tm):
    m, d = x2.shape
    dw = w3.shape[1]
    heads = lambda y: pl.BlockSpec((y.shape[0], tm, LANES), lambda i: (0, i, 0))
    return pl.pallas_call(
        functools.partial(_out_proj_kernel, row_chunk=256),
        grid=(m // tm,),
        in_specs=[heads(ya), heads(yb), heads(ym),
                  pl.BlockSpec((None, dw, d), lambda i: (layer, 0, 0), pipeline_mode=pl.Buffered(1)),
                  pl.BlockSpec((tm, d), lambda i: (i, 0)),
                  pl.BlockSpec((1, d), lambda i: (0, 0))],
        out_specs=pl.BlockSpec((tm, d), lambda i: (i, 0)),
        out_shape=jax.ShapeDtypeStruct((m, d), F32),
        scratch_shapes=[pltpu.VMEM((tm, dw), BF16), pltpu.VMEM((dw, d), BF16)],
        compiler_params=_cparams(1),
        name="out_proj",
    )(ya, yb, ym, w3, x2, g)


def kernel(x, mem, w_in, b_forget, w_mem_kv, w_out, g_pre, g_post, g_mem, rel_bias):
    batch, seq, d = x.shape
    n_mem = mem.shape[1]
    depth = w_in.shape[0]
    wa, wb, wm = N_HEADS_MOBA * HEAD_DIM, N_HEADS_FOX * HEAD_DIM, N_HEADS_MEM * HEAD_DIM
    f0 = 4 * wa + 4 * wb
    f1 = f0 + N_HEADS_FOX
    ha, hb, hm = N_HEADS_MOBA, N_HEADS_FOX, N_HEADS_MEM
    qa0, ka0, va0, ga0 = 0, ha, 2 * ha, 3 * ha
    qb0, kb0, vb0, gb0 = 4 * ha, 4 * ha + hb, 4 * ha + 2 * hb, 4 * ha + 3 * hb
    qm0, gm0 = 4 * ha + 4 * hb, 4 * ha + 4 * hb + hm

    x2 = x.reshape(batch * seq, d)
    mem2 = mem.reshape(batch * n_mem, d)
    for layer in range(depth):
        b_vec = jnp.pad(b_forget[layer], (0, LANES - N_HEADS_FOX)).reshape(1, LANES)

        n_in = w_in.shape[2]
        assert n_in == f1 + 2 * wm
        w_t3 = jnp.transpose(w_in, (0, 2, 1)).reshape(depth * n_in, d // LANES, LANES)
        w_main_t, w_gate_t = _w_repack(w_t3, layer, n_in, f0, N_HEADS_FOX, rows=512)
        p, fb = _norm_proj(x2, g_pre[layer].reshape(1, d), w_main_t, w_gate_t, w_transposed=True,
                           tm=1024, tn=1024)
        mkv, _ = _norm_proj(mem2, g_mem[layer].reshape(1, d), w_mem_kv[layer], w_transposed=False,
                            tm=batch * n_mem, tn=2 * wm)
        kc = _forget_scan(fb, b_vec, batch, seq, N_HEADS_FOX)

        ya, yb = _causal_attention(p, rel_bias, kc, batch, seq,
                                   (qa0, ka0, va0, ga0), (qb0, kb0, vb0, gb0))
        ym = _mem_attention(p, mkv, batch, seq, n_mem, qm0, g0=gm0)
        x2 = _out_proj(ya, yb, ym, w_out, layer, x2, g_post[layer].reshape(1, d), tm=512)
    return x2.reshape(batch, seq, d)
```

```python
import functools
import math

import jax
import jax.numpy as jnp
from jax import lax
from jax.experimental import pallas as pl
from jax.experimental.pallas import tpu as pltpu

HEAD_DIM = 128
N_HEADS_MOBA = 6
N_HEADS_FOX = 6
N_HEADS_MEM = 4
MOBA_BLOCK = 256
MOBA_TOPK = 3
T5_BUCKETS = 32
T5_MAX_DIST = 128
EPS = 1e-6
NEG = -1e30

LANES = 128
Q_TILE = MOBA_BLOCK
MEM_Q_TILE = 512
VMEM_LIMIT_BYTES = 56 * 1024 * 1024

F32 = jnp.float32
BF16 = jnp.bfloat16
_NT = (((1,), (1,)), ((), ()))


def _cparams(n_axes):
    return pltpu.CompilerParams(dimension_semantics=("arbitrary",) * n_axes,
                                vmem_limit_bytes=VMEM_LIMIT_BYTES)


def _norm_proj_kernel(*refs, n_col_blocks, w_transposed, with_gate_logits, row_chunk):
    if with_gate_logits:
        x_ref, g_ref, w_ref, wf_ref, p_ref, f_ref, h_ref = refs
    else:
        x_ref, g_ref, w_ref, p_ref, h_ref = refs
    tm = x_ref.shape[0]

    @pl.when(pl.program_id(1) == 0)
    def _normalize():
        def body(c, carry):
            rows = pl.ds(pl.multiple_of(c * row_chunk, row_chunk), row_chunk)
            xf = x_ref[rows, :]
            ms = jnp.mean(xf * xf, axis=-1, keepdims=True)
            h_ref[rows, :] = (xf * lax.rsqrt(ms + EPS) * g_ref[...]).astype(BF16)
            return carry
        lax.fori_loop(0, tm // row_chunk, body, 0)
        if with_gate_logits:
            f_ref[...] = lax.dot_general(h_ref[...], wf_ref[...], _NT, preferred_element_type=F32)

    w = w_ref[...].astype(BF16)
    if w_transposed:
        acc = lax.dot_general(h_ref[...], w, _NT, preferred_element_type=F32)
    else:
        acc = jnp.dot(h_ref[...], w, preferred_element_type=F32)
    for c in range(n_col_blocks):
        p_ref[c] = acc[:, c * LANES:(c + 1) * LANES].astype(BF16)


def _norm_proj(x2, g, w, wf_t=None, *, w_transposed, tm, tn):
    m, d = x2.shape
    n = w.shape[0] if w_transposed else w.shape[1]
    ncb = tn // LANES
    with_f = wf_t is not None
    assert n % tn == 0 and m % tm == 0
    w_spec = (pl.BlockSpec((tn, d), lambda i, j: (j, 0)) if w_transposed
              else pl.BlockSpec((d, tn), lambda i, j: (0, j)))
    in_specs = [pl.BlockSpec((tm, d), lambda i, j: (i, 0)),
                pl.BlockSpec((1, d), lambda i, j: (0, 0)), w_spec]
    out_shape = [jax.ShapeDtypeStruct((n // LANES, m, LANES), BF16)]
    out_specs = [pl.BlockSpec((ncb, tm, LANES), lambda i, j: (j, i, 0))]
    args = [x2, g, w]
    if with_f:
        in_specs.append(pl.BlockSpec((LANES, d), lambda i, j: (0, 0)))
        out_shape.append(jax.ShapeDtypeStruct((m, LANES), F32))
        out_specs.append(pl.BlockSpec((tm, LANES), lambda i, j: (i, 0)))
        args.append(wf_t)
    res = pl.pallas_call(
        functools.partial(_norm_proj_kernel, n_col_blocks=ncb, w_transposed=w_transposed,
                          with_gate_logits=with_f, row_chunk=256),
        grid=(m // tm, n // tn),
        in_specs=in_specs, out_specs=out_specs, out_shape=out_shape,
        scratch_shapes=[pltpu.VMEM((tm, d), BF16)],
        compiler_params=_cparams(2),
        name="norm_proj_gate" if with_f else "norm_proj",
    )(*args)
    return res if with_f else (res[0], None)


def _w_repack_kernel(src_ref, main_ref, gate_ref, buf_ref, gbuf_ref, sem_ref, gsem_ref, *,
                     rows, base_row, n_aligned_blocks, n_gate):
    t = pl.program_id(0)
    n_steps = pl.num_programs(0)
    n_slices = src_ref.shape[1]

    def block_copies(step, slot):
        row0 = base_row + step * rows + jnp.where(step >= n_aligned_blocks, n_gate, 0)
        return [pltpu.make_async_copy(src_ref.at[pl.ds(row0, rows), s, :],
                                      buf_ref.at[slot, :, pl.ds(s * LANES, LANES)],
                                      sem_ref.at[slot]) for s in range(n_slices)]

    def gate_copies():
        gate_row0 = base_row + n_aligned_blocks * rows
        return [pltpu.make_async_copy(src_ref.at[pl.ds(gate_row0, 8), s, :],
                                      gbuf_ref.at[:, pl.ds(s * LANES, LANES)],
                                      gsem_ref.at[0]) for s in range(n_slices)]

    @pl.when(t == 0)
    def _first():
        for cp in gate_copies() + block_copies(0, 0):
            cp.start()

    @pl.when(t + 1 < n_steps)
    def _prefetch():
        for cp in block_copies(t + 1, (t + 1) % 2):
            cp.start()

    @pl.when(t == 0)
    def _gate_rows():
        for cp in gate_copies():
            cp.wait()
        row = lax.broadcasted_iota(jnp.int32, gbuf_ref.shape, 0)
        g8 = jnp.where(row < n_gate, gbuf_ref[...], 0.0)
        gate_ref[...] = jnp.concatenate(
            [g8, jnp.zeros((gate_ref.shape[0] - 8, g8.shape[1]), F32)], axis=0).astype(BF16)

    for cp in block_copies(t, t % 2):
        cp.wait()
    main_ref[...] = buf_ref[t % 2].astype(BF16)


def _w_repack(w_t3, layer, n_total, n_aligned, n_gate, *, rows):
    n_slices = w_t3.shape[1]
    d = n_slices * LANES
    n_main = n_total - n_gate
    assert n_aligned % rows == 0 and n_main % rows == 0 and n_gate <= 8
    return pl.pallas_call(
        functools.partial(_w_repack_kernel, rows=rows, base_row=layer * n_total,
                          n_aligned_blocks=n_aligned // rows, n_gate=n_gate),
        grid=(n_main // rows,),
        in_specs=[pl.BlockSpec(memory_space=pl.ANY)],
        out_specs=[pl.BlockSpec((rows, d), lambda t: (t, 0)),
                   pl.BlockSpec((LANES, d), lambda t: (0, 0))],
        out_shape=[jax.ShapeDtypeStruct((n_main, d), BF16),
                   jax.ShapeDtypeStruct((LANES, d), BF16)],
        scratch_shapes=[pltpu.VMEM((2, rows, d), F32), pltpu.VMEM((8, d), F32),
                        pltpu.SemaphoreType.DMA((2,)), pltpu.SemaphoreType.DMA((1,))],
        compiler_params=_cparams(1),
        name="w_repack",
    )(w_t3)


def _split3(v):
    hi = v.astype(BF16).astype(F32)
    mid = (v - hi).astype(BF16).astype(F32)
    lo = (v - hi - mid).astype(BF16).astype(F32)
    return hi, mid, lo


def _forget_scan_kernel(f_ref, b_ref, kc_ref, *, n_heads):
    z = f_ref[...] + b_ref[...]
    logf = jnp.minimum(z, 0.0) - jnp.log1p(jnp.exp(-jnp.abs(z)))
    x = logf.T[0:8, :]
    s_len = x.shape[1]
    lane = lax.broadcasted_iota(jnp.int32, x.shape, 1)
    shift = 1
    while shift < s_len:
        x = x + jnp.where(lane >= shift, pltpu.roll(x, shift, 1), 0.0)
        shift *= 2
    row = lax.broadcasted_iota(jnp.int32, x.shape, 0)
    x = jnp.where(row < n_heads, x, 0.0)
    terms = _split3(x * (-1.0 / HEAD_DIM ** -0.5))
    pad = jnp.zeros((LANES - 8 * len(terms), s_len), F32)
    kc_ref[...] = jnp.concatenate(list(terms) + [pad], axis=0).T.astype(BF16)


def _forget_scan(fb, bvec, batch, seq, n_heads):
    assert n_heads <= 8
    return pl.pallas_call(
        functools.partial(_forget_scan_kernel, n_heads=n_heads),
        grid=(batch,),
        in_specs=[pl.BlockSpec((seq, LANES), lambda b: (b, 0)),
                  pl.BlockSpec((1, LANES), lambda b: (0, 0))],
        out_specs=pl.BlockSpec((None, seq, LANES), lambda b: (b, 0, 0)),
        out_shape=jax.ShapeDtypeStruct((batch, seq, LANES), BF16),
        compiler_params=_cparams(1),
        name="forget_scan",
    )(fb, bvec)


LOG2E = math.log2(math.e)


def _softmax_pv_folded(a, v_aug, g):
    m = jnp.max(a, axis=1, keepdims=True)
    p = jnp.exp2((a - m) * (HEAD_DIM ** -0.5 * LOG2E)).astype(BF16)
    acc = jnp.dot(p, v_aug, preferred_element_type=F32)
    gf = g.astype(F32)
    silu = gf * (1.0 / (1.0 + jnp.exp(-gf)))
    return (acc[:, :LANES] / acc[:, LANES:] * silu).astype(BF16)


def _stream_events(setup, n_tiles, score_matmul, finish):
    scores = {}

    def score(i):
        return lambda: scores.__setitem__(i, score_matmul(i))

    def fin(i):
        return lambda: finish(i, scores.pop(i))

    events = [setup, score(0)]
    for i in range(n_tiles):
        if i + 1 < n_tiles:
            events.append(score(i + 1))
        events.append(fin(i))
    return events


def _run_interleaved(streams, lead=0):
    queues = [list(ev) for ev in streams]
    for _ in range(min(lead, len(queues[0]))):
        queues[0].pop(0)()
    while any(queues):
        for q in queues:
            if q:
                q.pop(0)()


def _t5_bias(rb_ref, h, dist):
    max_exact = T5_BUCKETS // 2
    n = jnp.maximum(dist, 0)
    nf = jnp.maximum(n, 1).astype(F32)
    large = max_exact + (jnp.log(nf / max_exact) / math.log(T5_MAX_DIST / max_exact)
                         * (T5_BUCKETS - max_exact)).astype(jnp.int32)
    large = jnp.minimum(large, T5_BUCKETS - 1)
    bucket = jnp.where(n < max_exact, n, large)
    bias = jnp.zeros(dist.shape, F32)
    for kk in range(T5_BUCKETS):
        bias = jnp.where(bucket == kk, rb_ref[h, kk], bias)
    return bias


def _moba_head_constants(rb_ref, h, kaug_ref, bown_ref, bprev_ref):
    inv_scale = 1.0 / HEAD_DIM ** -0.5
    blk = MOBA_BLOCK
    seq = kaug_ref.shape[0]
    far_bias = rb_ref[h, T5_BUCKETS - 1]
    srow = lax.broadcasted_iota(jnp.int32, (seq, LANES), 0)
    col = lax.broadcasted_iota(jnp.int32, (seq, LANES), 1)
    kaug_ref[:, LANES:] = jnp.where((col & 7) * blk == srow - (srow & (blk - 1)), 1.0, 0.0).astype(BF16)
    r = lax.broadcasted_iota(jnp.int32, (blk, blk), 0)
    c = lax.broadcasted_iota(jnp.int32, (blk, blk), 1)
    dist = r - c
    bown_ref[...] = jnp.where(dist >= 0, (_t5_bias(rb_ref, h, dist) - far_bias) * inv_scale, NEG)
    bprev_ref[...] = (_t5_bias(rb_ref, h, dist + blk) - far_bias) * inv_scale


def _moba_stream(q_ref, k_ref, v_ref, g_ref, o_ref,
                 kaug_ref, qaug_ref, vaug_ref, kmean_ref, bown_ref, bprev_ref):
    blk = MOBA_BLOCK
    seq = q_ref.shape[0]
    n_blocks = seq // blk
    n_chunks = seq // LANES

    def setup():
        _moba_setup(q_ref, k_ref, v_ref, kaug_ref, qaug_ref, vaug_ref, kmean_ref, n_blocks, n_chunks)

    def score_matmul(i):
        return lax.dot_general(qaug_ref[i * blk:(i + 1) * blk, :], kaug_ref[0:(i + 1) * blk, :], _NT,
                               preferred_element_type=F32)

    def finish(i, a):
        rows = slice(i * blk, (i + 1) * blk)
        n = (i + 1) * blk
        parts = [a[:, n - blk:] + bown_ref[...]]
        if i >= 1:
            parts.insert(0, a[:, n - 2 * blk:n - blk] + bprev_ref[...])
        if i >= 2:
            parts.insert(0, a[:, :n - 2 * blk])
        a = parts[0] if len(parts) == 1 else jnp.concatenate(parts, axis=1)
        o_ref[rows, :] = _softmax_pv_folded(a, vaug_ref[0:n, :], g_ref[rows, :])

    return setup, n_blocks, score_matmul, finish


def _moba_setup(q_ref, k_ref, v_ref, kaug_ref, qaug_ref, vaug_ref, kmean_ref, n_blocks, n_chunks):
    blk = MOBA_BLOCK
    seq = q_ref.shape[0]
    kaug_ref[:, :LANES] = k_ref[...]
    vaug_ref[:, :LANES] = v_ref[...]
    vaug_ref[:, LANES:] = jnp.ones((seq, LANES), BF16)
    row8 = lax.broadcasted_iota(jnp.int32, (8, LANES), 0)
    km = jnp.zeros((8, LANES), F32)
    for j in range(n_blocks):
        mean_j = jnp.mean(k_ref[j * blk:(j + 1) * blk, :].astype(F32), axis=0, keepdims=True)
        km = jnp.where(row8 == j, mean_j, km)
    kmean_ref[...] = jnp.concatenate([km, jnp.zeros((LANES - 8, LANES), F32)], axis=0).astype(BF16)

    gate_t = lax.dot_general(kmean_ref[...], q_ref[...], _NT, preferred_element_type=F32)[0:8, :]
    jrow = lax.broadcasted_iota(jnp.int32, gate_t.shape, 0)
    qpos = lax.broadcasted_iota(jnp.int32, gate_t.shape, 1)
    qblk = jnp.right_shift(qpos, int(math.log2(blk)))
    gm = jnp.where(jrow < qblk, gate_t, NEG)
    sel_bias = jnp.zeros(gate_t.shape, F32)
    for j in range(n_blocks):
        gj = gm[j:j + 1, :]
        beats = (gm > gj) | ((gm == gj) & (jrow < j))
        rank = jnp.sum(beats.astype(F32), axis=0, keepdims=True)
        allowed = ((rank < MOBA_TOPK) & (qblk[0:1, :] > j)) | (qblk[0:1, :] == j)
        sel_bias = jnp.where(jrow == j, jnp.where(allowed, 0.0, NEG), sel_bias)
    assert 8 * n_chunks == LANES
    packed_t = jnp.concatenate([sel_bias[:, c * LANES:(c + 1) * LANES] for c in range(n_chunks)], axis=0).T
    lane_chunk = jnp.right_shift(lax.broadcasted_iota(jnp.int32, (LANES, LANES), 1), 3)
    qaug_ref[:, :LANES] = q_ref[...]
    for c in range(n_chunks):
        qaug_ref[c * LANES:(c + 1) * LANES, LANES:] = jnp.where(lane_chunk == c, packed_t, 0.0).astype(BF16)


def _moba_scratch(seq):
    blk = MOBA_BLOCK
    return [pltpu.VMEM((seq, 2 * LANES), BF16),
            pltpu.VMEM((seq, 2 * LANES), BF16),
            pltpu.VMEM((seq, 2 * LANES), BF16),
            pltpu.VMEM((LANES, LANES), BF16),
            pltpu.VMEM((blk, blk), F32),
            pltpu.VMEM((blk, blk), F32)]


def _fox_stream(h, q_ref, k_ref, v_ref, g_ref, kc_ref, o_ref, qaug_ref, kaug_ref, vaug_ref, causal_ref):
    seq = q_ref.shape[0]

    def setup():
        col = lax.broadcasted_iota(jnp.int32, (seq, LANES), 1)
        qaug_ref[:, :LANES] = q_ref[...]
        qaug_ref[:, LANES:] = jnp.where(((col & 7) == h) & (col < 24), 1.0, 0.0).astype(BF16)
        kaug_ref[:, :LANES] = k_ref[...]
        kaug_ref[:, LANES:] = kc_ref[...]
        vaug_ref[:, :LANES] = v_ref[...]
        vaug_ref[:, LANES:] = jnp.ones((seq, LANES), BF16)
        r = lax.broadcasted_iota(jnp.int32, (Q_TILE, Q_TILE), 0)
        c = lax.broadcasted_iota(jnp.int32, (Q_TILE, Q_TILE), 1)
        causal_ref[...] = jnp.where(c <= r, 0.0, NEG)

    def score_matmul(i):
        return lax.dot_general(qaug_ref[i * Q_TILE:(i + 1) * Q_TILE, :], kaug_ref[0:(i + 1) * Q_TILE, :],
                               _NT, preferred_element_type=F32)

    def finish(i, a):
        rows = slice(i * Q_TILE, (i + 1) * Q_TILE)
        n = (i + 1) * Q_TILE
        own = a[:, n - Q_TILE:] + causal_ref[...]
        a = own if i == 0 else jnp.concatenate([a[:, :n - Q_TILE], own], axis=1)
        o_ref[rows, :] = _softmax_pv_folded(a, vaug_ref[0:n, :], g_ref[rows, :])

    return setup, seq // Q_TILE, score_matmul, finish


def _fox_scratch(seq):
    return [pltpu.VMEM((seq, 2 * LANES), BF16), pltpu.VMEM((seq, 2 * LANES), BF16),
            pltpu.VMEM((seq, 2 * LANES), BF16), pltpu.VMEM((Q_TILE, Q_TILE), F32)]


FOX_LEAD_EVENTS = 8


def _causal_attention_kernel(*refs):
    rb_ref, qa, ka, va, ga, qb, kb, vb, gb, kc, oa, ob = refs[:12]
    n_moba = len(_moba_scratch(qa.shape[0]))
    moba_scratch, fox_scratch = refs[12:12 + n_moba], refs[12 + n_moba:]
    kaug_ref, _, _, _, bown_ref, bprev_ref = moba_scratch
    h = pl.program_id(0)

    @pl.when(pl.program_id(1) == 0)
    def _():
        _moba_head_constants(rb_ref, h, kaug_ref, bown_ref, bprev_ref)

    _run_interleaved([_stream_events(*_fox_stream(h, qb, kb, vb, gb, kc, ob, *fox_scratch)),
                      _stream_events(*_moba_stream(qa, ka, va, ga, oa, *moba_scratch))],
                     lead=FOX_LEAD_EVENTS)


def _causal_attention(p, rel_bias, kc, batch, seq, moba_offsets, fox_offsets):
    nb = seq // MOBA_BLOCK
    assert seq % MOBA_BLOCK == 0 and MOBA_TOPK < 8 and nb <= 8 and MOBA_BLOCK > T5_MAX_DIST
    assert N_HEADS_MOBA == N_HEADS_FOX
    full = lambda off: pl.BlockSpec((None, seq, LANES), lambda h, b: (off + h, b, 0))
    out = pl.BlockSpec((None, seq, LANES), lambda h, b: (h, b, 0))
    out_sds = jax.ShapeDtypeStruct((N_HEADS_MOBA, batch * seq, LANES), BF16)
    return pl.pallas_call(
        _causal_attention_kernel,
        grid=(N_HEADS_MOBA, batch),
        in_specs=([pl.BlockSpec(memory_space=pltpu.SMEM)]
                  + [full(off) for off in moba_offsets] + [full(off) for off in fox_offsets]
                  + [pl.BlockSpec((None, seq, LANES), lambda h, b: (b, 0, 0))]),
        out_specs=[out, out],
        out_shape=[out_sds, out_sds],
        scratch_shapes=_moba_scratch(seq) + _fox_scratch(seq),
        compiler_params=_cparams(2),
        name="causal_attention",
    )(rel_bias, *([p] * 8), kc)


def _mem_kernel(q_ref, k_ref, v_ref, g_ref, o_ref, vaug_ref):
    n_heads, seq, _ = q_ref.shape
    tiles_per_head = seq // MEM_Q_TILE

    @pl.when(pl.program_id(0) == 0)
    def _ones():
        vaug_ref[:, :, LANES:] = jnp.ones(v_ref.shape, BF16)

    vaug_ref[:, :, :LANES] = v_ref[...]

    def tile(t):
        h, i = divmod(t, tiles_per_head)
        return h, slice(i * MEM_Q_TILE, (i + 1) * MEM_Q_TILE)

    def score_matmul(t):
        h, rows = tile(t)
        return lax.dot_general(q_ref[h, rows, :], k_ref[h], _NT, preferred_element_type=F32)

    def finish(t, a):
        h, rows = tile(t)
        o_ref[h, rows, :] = _softmax_pv_folded(a, vaug_ref[h], g_ref[h, rows, :])

    _run_interleaved([_stream_events(lambda: None, n_heads * tiles_per_head, score_matmul, finish)])


def _mem_attention(p, mkv, batch, seq, n_mem, q0, g0):
    nh = N_HEADS_MEM
    assert q0 % nh == 0 and g0 % nh == 0
    heads = lambda off: pl.BlockSpec((nh, seq, LANES), lambda b: (off // nh, b, 0))
    kv = lambda blk: pl.BlockSpec((nh, n_mem, LANES), lambda b: (blk, b, 0))
    return pl.pallas_call(
        _mem_kernel,
        grid=(batch,),
        in_specs=[heads(q0), kv(0), kv(1), heads(g0)],
        out_specs=pl.BlockSpec((nh, seq, LANES), lambda b: (0, b, 0)),
        out_shape=jax.ShapeDtypeStruct((nh, batch * seq, LANES), BF16),
        scratch_shapes=[pltpu.VMEM((nh, n_mem, 2 * LANES), BF16)],
        compiler_params=_cparams(1),
        name="mem_attention",
    )(p, mkv, mkv, p)


def _out_proj_kernel(ya_ref, yb_ref, ym_ref, w_ref, x_ref, g_ref, o_ref, y_ref, wb_ref, *, row_chunk):
    @pl.when(pl.program_id(0) == 0)
    def _cast_weights():
        def body(c, carry):
            rows = pl.ds(pl.multiple_of(c * row_chunk, row_chunk), row_chunk)
            wb_ref[rows, :] = w_ref[rows, :].astype(BF16)
            return carry
        lax.fori_loop(0, w_ref.shape[0] // row_chunk, body, 0)

    col = 0
    for src in (ya_ref, yb_ref, ym_ref):
        for c in range(src.shape[0]):
            y_ref[:, col * LANES:(col + 1) * LANES] = src[c]
            col += 1
    half = y_ref.shape[0] // 2
    halves = [slice(0, half), slice(half, 2 * half)]
    outs = [jnp.dot(y_ref[rows, :], wb_ref[...], preferred_element_type=F32) for rows in halves]
    for rows, o in zip(halves, outs):
        ms = jnp.mean(o * o, axis=-1, keepdims=True)
        o_ref[rows, :] = x_ref[rows, :] + o * lax.rsqrt(ms + EPS) * g_ref[...]


def _out_proj(ya, yb, ym, w3, layer, x2, g, *, tm):
    m, d = x2.shape
    dw = w3.shape[1]
    heads = lambda y: pl.BlockSpec((y.shape[0], tm, LANES), lambda i: (0, i, 0))
    return pl.pallas_call(
        functools.partial(_out_proj_kernel, row_chunk=256),
        grid=(m // tm,),
        in_specs=[heads(ya), heads(yb), heads(ym),
                  pl.BlockSpec((None, dw, d), lambda i: (layer, 0, 0), pipeline_mode=pl.Buffered(1)),
                  pl.BlockSpec((tm, d), lambda i: (i, 0)),
                  pl.BlockSpec((1, d), lambda i: (0, 0))],
        out_specs=pl.BlockSpec((tm, d), lambda i: (i, 0)),
        out_shape=jax.ShapeDtypeStruct((m, d), F32),
        scratch_shapes=[pltpu.VMEM((tm, dw), BF16), pltpu.VMEM((dw, d), BF16)],
        compiler_params=_cparams(1),
        name="out_proj",
    )(ya, yb, ym, w3, x2, g)


def kernel(x, mem, w_in, b_forget, w_mem_kv, w_out, g_pre, g_post, g_mem, rel_bias):
    batch, seq, d = x.shape
    n_mem = mem.shape[1]
    depth = w_in.shape[0]
    wa, wb, wm = N_HEADS_MOBA * HEAD_DIM, N_HEADS_FOX * HEAD_DIM, N_HEADS_MEM * HEAD_DIM
    f0 = 4 * wa + 4 * wb
    f1 = f0 + N_HEADS_FOX
    ha, hb, hm = N_HEADS_MOBA, N_HEADS_FOX, N_HEADS_MEM
    qa0, ka0, va0, ga0 = 0, ha, 2 * ha, 3 * ha
    qb0, kb0, vb0, gb0 = 4 * ha, 4 * ha + hb, 4 * ha + 2 * hb, 4 * ha + 3 * hb
    qm0, gm0 = 4 * ha + 4 * hb, 4 * ha + 4 * hb + hm

    x2 = x.reshape(batch * seq, d)
    mem2 = mem.reshape(batch * n_mem, d)
    for layer in range(depth):
        b_vec = jnp.pad(b_forget[layer], (0, LANES - N_HEADS_FOX)).reshape(1, LANES)

        n_in = w_in.shape[2]
        assert n_in == f1 + 2 * wm
        w_t3 = jnp.transpose(w_in, (0, 2, 1)).reshape(depth * n_in, d // LANES, LANES)
        w_main_t, w_gate_t = _w_repack(w_t3, layer, n_in, f0, N_HEADS_FOX, rows=512)
        p, fb = _norm_proj(x2, g_pre[layer].reshape(1, d), w_main_t, w_gate_t, w_transposed=True,
                           tm=1024, tn=1792)
        mkv, _ = _norm_proj(mem2, g_mem[layer].reshape(1, d), w_mem_kv[layer], w_transposed=False,
                            tm=512, tn=512)
        kc = _forget_scan(fb, b_vec, batch, seq, N_HEADS_FOX)

        ya, yb = _causal_attention(p, rel_bias, kc, batch, seq,
                                   (qa0, ka0, va0, ga0), (qb0, kb0, vb0, gb0))
        ym = _mem_attention(p, mkv, batch, seq, n_mem, qm0, g0=gm0)
        x2 = _out_proj(ya, yb, ym, w_out, layer, x2, g_post[layer].reshape(1, d), tm=512)
    return x2.reshape(batch, seq, d)
```

```python
import functools
import math

import jax
import jax.numpy as jnp
from jax import lax
from jax.experimental import pallas as pl
from jax.experimental.pallas import tpu as pltpu

HEAD_DIM = 128
N_HEADS_MOBA = 6
N_HEADS_FOX = 6
N_HEADS_MEM = 4
MOBA_BLOCK = 256
MOBA_TOPK = 3
T5_BUCKETS = 32
T5_MAX_DIST = 128
EPS = 1e-6
NEG = -1e30

LANES = 128
Q_TILE = MOBA_BLOCK
MEM_Q_TILE = 512
VMEM_LIMIT_BYTES = 56 * 1024 * 1024

F32 = jnp.float32
BF16 = jnp.bfloat16
_NT = (((1,), (1,)), ((), ()))


def _cparams(n_axes):
    return pltpu.CompilerParams(dimension_semantics=("arbitrary",) * n_axes,
                                vmem_limit_bytes=VMEM_LIMIT_BYTES)


def _norm_proj_kernel(*refs, n_col_blocks, n_row_tiles, n_col_tiles, w_transposed, with_gate_logits):
    if with_gate_logits:
        x_ref, g_ref, w_ref, wf_ref, p_ref, f_ref, h_even, h_odd = refs
    else:
        x_ref, g_ref, w_ref, p_ref, h_even, h_odd = refs
    i = pl.program_id(0)
    j = pl.program_id(1)
    chunk = x_ref.shape[0] // n_col_tiles
    rows = pl.ds(pl.multiple_of(j * chunk, chunk), chunk)

    def normalize(h_ref):
        xf = x_ref[rows, :]
        ms = jnp.mean(xf * xf, axis=-1, keepdims=True)
        h_ref[rows, :] = (xf * lax.rsqrt(ms + EPS) * g_ref[...]).astype(BF16)

    def matmul(h_ref):
        w = w_ref[...].astype(BF16)
        if w_transposed:
            acc = lax.dot_general(h_ref[...], w, _NT, preferred_element_type=F32)
        else:
            acc = jnp.dot(h_ref[...], w, preferred_element_type=F32)
        for c in range(n_col_blocks):
            p_ref[c] = acc[:, c * LANES:(c + 1) * LANES].astype(BF16)
        if with_gate_logits:
            f_ref[rows, :] = lax.dot_general(h_ref[rows, :], wf_ref[...], _NT, preferred_element_type=F32)

    for parity, (h_cur, h_prev) in enumerate([(h_even, h_odd), (h_odd, h_even)]):
        is_parity = (i % 2) == parity

        if parity == 0:
            @pl.when(i == 0)
            def _first():
                normalize(h_cur)

        @pl.when(is_parity & (i > 0) & (i < n_row_tiles))
        def _both():
            normalize(h_cur)
            matmul(h_prev)

        if parity == n_row_tiles % 2:
            @pl.when(i == n_row_tiles)
            def _last():
                matmul(h_prev)


def _norm_proj(x2, g, w, wf_t=None, *, w_transposed, tm, tn):
    m, d = x2.shape
    n = w.shape[0] if w_transposed else w.shape[1]
    ncb = tn // LANES
    n_i, n_j = m // tm, n // tn
    with_f = wf_t is not None
    assert n % tn == 0 and m % tm == 0 and tm % (16 * n_j) == 0
    out_row = lambda i: jnp.maximum(i - 1, 0)
    w_spec = (pl.BlockSpec((tn, d), lambda i, j: (j, 0)) if w_transposed
              else pl.BlockSpec((d, tn), lambda i, j: (0, j)))
    in_specs = [pl.BlockSpec((tm, d), lambda i, j: (jnp.minimum(i, n_i - 1), 0)),
                pl.BlockSpec((1, d), lambda i, j: (0, 0)), w_spec]
    out_shape = [jax.ShapeDtypeStruct((n // LANES, m, LANES), BF16)]
    out_specs = [pl.BlockSpec((ncb, tm, LANES), lambda i, j: (jnp.where(i == 0, 0, j), out_row(i), 0))]
    args = [x2, g, w]
    if with_f:
        in_specs.append(pl.BlockSpec((LANES, d), lambda i, j: (0, 0)))
        out_shape.append(jax.ShapeDtypeStruct((m, LANES), F32))
        out_specs.append(pl.BlockSpec((tm, LANES), lambda i, j: (out_row(i), 0)))
        args.append(wf_t)
    res = pl.pallas_call(
        functools.partial(_norm_proj_kernel, n_col_blocks=ncb, n_row_tiles=n_i, n_col_tiles=n_j,
                          w_transposed=w_transposed, with_gate_logits=with_f),
        grid=(n_i + 1, n_j),
        in_specs=in_specs, out_specs=out_specs, out_shape=out_shape,
        scratch_shapes=[pltpu.VMEM((tm, d), BF16), pltpu.VMEM((tm, d), BF16)],
        compiler_params=_cparams(2),
        name="norm_proj_gate" if with_f else "norm_proj",
    )(*args)
    return res if with_f else (res[0], None)


def _w_repack_kernel(src_ref, main_ref, gate_ref, buf_ref, gbuf_ref, sem_ref, gsem_ref, *,
                     rows, base_row, n_aligned_blocks, n_gate):
    t = pl.program_id(0)
    n_steps = pl.num_programs(0)
    n_slices = src_ref.shape[1]

    def block_copies(step, slot):
        row0 = base_row + step * rows + jnp.where(step >= n_aligned_blocks, n_gate, 0)
        return [pltpu.make_async_copy(src_ref.at[pl.ds(row0, rows), s, :],
                                      buf_ref.at[slot, :, pl.ds(s * LANES, LANES)],
                                      sem_ref.at[slot]) for s in range(n_slices)]

    def gate_copies():
        gate_row0 = base_row + n_aligned_blocks * rows
        return [pltpu.make_async_copy(src_ref.at[pl.ds(gate_row0, 8), s, :],
                                      gbuf_ref.at[:, pl.ds(s * LANES, LANES)],
                                      gsem_ref.at[0]) for s in range(n_slices)]

    @pl.when(t == 0)
    def _first():
        for cp in gate_copies() + block_copies(0, 0):
            cp.start()

    @pl.when(t + 1 < n_steps)
    def _prefetch():
        for cp in block_copies(t + 1, (t + 1) % 2):
            cp.start()

    @pl.when(t == 0)
    def _gate_rows():
        for cp in gate_copies():
            cp.wait()
        row = lax.broadcasted_iota(jnp.int32, gbuf_ref.shape, 0)
        g8 = jnp.where(row < n_gate, gbuf_ref[...], 0.0)
        gate_ref[...] = jnp.concatenate(
            [g8, jnp.zeros((gate_ref.shape[0] - 8, g8.shape[1]), F32)], axis=0).astype(BF16)

    for cp in block_copies(t, t % 2):
        cp.wait()
    main_ref[...] = buf_ref[t % 2].astype(BF16)


def _w_repack(w_t3, layer, n_total, n_aligned, n_gate, *, rows):
    n_slices = w_t3.shape[1]
    d = n_slices * LANES
    n_main = n_total - n_gate
    assert n_aligned % rows == 0 and n_main % rows == 0 and n_gate <= 8
    return pl.pallas_call(
        functools.partial(_w_repack_kernel, rows=rows, base_row=layer * n_total,
                          n_aligned_blocks=n_aligned // rows, n_gate=n_gate),
        grid=(n_main // rows,),
        in_specs=[pl.BlockSpec(memory_space=pl.ANY)],
        out_specs=[pl.BlockSpec((rows, d), lambda t: (t, 0)),
                   pl.BlockSpec((LANES, d), lambda t: (0, 0))],
        out_shape=[jax.ShapeDtypeStruct((n_main, d), BF16),
                   jax.ShapeDtypeStruct((LANES, d), BF16)],
        scratch_shapes=[pltpu.VMEM((2, rows, d), F32), pltpu.VMEM((8, d), F32),
                        pltpu.SemaphoreType.DMA((2,)), pltpu.SemaphoreType.DMA((1,))],
        compiler_params=_cparams(1),
        name="w_repack",
    )(w_t3)


def _split3(v):
    hi = v.astype(BF16).astype(F32)
    mid = (v - hi).astype(BF16).astype(F32)
    lo = (v - hi - mid).astype(BF16).astype(F32)
    return hi, mid, lo


def _forget_scan_kernel(f_ref, b_ref, kc_ref, *, n_heads):
    z = f_ref[...] + b_ref[...]
    logf = jnp.minimum(z, 0.0) - jnp.log1p(jnp.exp(-jnp.abs(z)))
    x = logf.T[0:8, :]
    s_len = x.shape[1]
    lane = lax.broadcasted_iota(jnp.int32, x.shape, 1)
    shift = 1
    while shift < s_len:
        x = x + jnp.where(lane >= shift, pltpu.roll(x, shift, 1), 0.0)
        shift *= 2
    row = lax.broadcasted_iota(jnp.int32, x.shape, 0)
    x = jnp.where(row < n_heads, x, 0.0)
    terms = _split3(x * (-1.0 / HEAD_DIM ** -0.5))
    pad = jnp.zeros((LANES - 8 * len(terms), s_len), F32)
    kc_ref[...] = jnp.concatenate(list(terms) + [pad], axis=0).T.astype(BF16)


def _forget_scan(fb, bvec, batch, seq, n_heads):
    assert n_heads <= 8
    return pl.pallas_call(
        functools.partial(_forget_scan_kernel, n_heads=n_heads),
        grid=(batch,),
        in_specs=[pl.BlockSpec((seq, LANES), lambda b: (b, 0)),
                  pl.BlockSpec((1, LANES), lambda b: (0, 0))],
        out_specs=pl.BlockSpec((None, seq, LANES), lambda b: (b, 0, 0)),
        out_shape=jax.ShapeDtypeStruct((batch, seq, LANES), BF16),
        compiler_params=_cparams(1),
        name="forget_scan",
    )(fb, bvec)


LOG2E = math.log2(math.e)


def _softmax_pv_folded(a, v_aug, g):
    m = jnp.max(a, axis=1, keepdims=True)
    p = jnp.exp2((a - m) * (HEAD_DIM ** -0.5 * LOG2E)).astype(BF16)
    acc = jnp.dot(p, v_aug, preferred_element_type=F32)
    gf = g.astype(F32)
    silu = gf * (1.0 / (1.0 + jnp.exp(-gf)))
    return (acc[:, :LANES] / acc[:, LANES:] * silu).astype(BF16)


SCORE_AHEAD = 1


def _stream_events(setup, n_tiles, score_matmul, finish):
    scores = {}

    def score(i):
        return lambda: scores.__setitem__(i, score_matmul(i))

    def fin(i):
        return lambda: finish(i, scores.pop(i))

    events = [setup] + [score(i) for i in range(min(SCORE_AHEAD, n_tiles))]
    for i in range(n_tiles):
        if i + SCORE_AHEAD < n_tiles:
            events.append(score(i + SCORE_AHEAD))
        events.append(fin(i))
    return events


def _run_interleaved(streams, lead=0):
    queues = [list(ev) for ev in streams]
    for _ in range(min(lead, len(queues[0]))):
        queues[0].pop(0)()
    while any(queues):
        for q in queues:
            if q:
                q.pop(0)()


def _t5_bias(rb_ref, h, dist):
    max_exact = T5_BUCKETS // 2
    n = jnp.maximum(dist, 0)
    nf = jnp.maximum(n, 1).astype(F32)
    large = max_exact + (jnp.log(nf / max_exact) / math.log(T5_MAX_DIST / max_exact)
                         * (T5_BUCKETS - max_exact)).astype(jnp.int32)
    large = jnp.minimum(large, T5_BUCKETS - 1)
    bucket = jnp.where(n < max_exact, n, large)
    bias = jnp.zeros(dist.shape, F32)
    for kk in range(T5_BUCKETS):
        bias = jnp.where(bucket == kk, rb_ref[h, kk], bias)
    return bias


def _moba_head_constants(rb_ref, h, kaug_ref, bown_ref, bprev_ref):
    inv_scale = 1.0 / HEAD_DIM ** -0.5
    blk = MOBA_BLOCK
    seq = kaug_ref.shape[0]
    far_bias = rb_ref[h, T5_BUCKETS - 1]
    srow = lax.broadcasted_iota(jnp.int32, (seq, LANES), 0)
    col = lax.broadcasted_iota(jnp.int32, (seq, LANES), 1)
    kaug_ref[:, LANES:] = jnp.where((col & 7) * blk == srow - (srow & (blk - 1)), 1.0, 0.0).astype(BF16)
    t = T5_MAX_DIST
    assert blk == 2 * t
    dist = lax.broadcasted_iota(jnp.int32, (t, t), 0) - lax.broadcasted_iota(jnp.int32, (t, t), 1)
    near = jnp.where(dist >= 0, (_t5_bias(rb_ref, h, dist) - far_bias) * inv_scale, NEG)
    mid = (_t5_bias(rb_ref, h, dist + t) - far_bias) * inv_scale
    bown_ref[0:t, 0:t] = near
    bown_ref[0:t, t:blk] = jnp.full((t, t), NEG, F32)
    bown_ref[t:blk, 0:t] = mid
    bown_ref[t:blk, t:blk] = near
    bprev_ref[...] = jnp.zeros((blk, blk), F32)
    bprev_ref[0:t, t:blk] = mid


def _moba_stream(q_ref, k_ref, v_ref, g_ref, o_ref,
                 kaug_ref, qaug_ref, vaug_ref, kmean_ref, bown_ref, bprev_ref):
    blk = MOBA_BLOCK
    seq = q_ref.shape[0]
    n_blocks = seq // blk
    n_chunks = seq // LANES

    def setup():
        _moba_setup(q_ref, k_ref, v_ref, kaug_ref, qaug_ref, vaug_ref, kmean_ref, n_blocks, n_chunks)

    def score_matmul(i):
        return lax.dot_general(qaug_ref[i * blk:(i + 1) * blk, :], kaug_ref[0:(i + 1) * blk, :], _NT,
                               preferred_element_type=F32)

    def finish(i, a):
        rows = slice(i * blk, (i + 1) * blk)
        n = (i + 1) * blk
        parts = [a[:, n - blk:] + bown_ref[...]]
        if i >= 1:
            parts.insert(0, a[:, n - 2 * blk:n - blk] + bprev_ref[...])
        if i >= 2:
            parts.insert(0, a[:, :n - 2 * blk])
        a = parts[0] if len(parts) == 1 else jnp.concatenate(parts, axis=1)
        o_ref[rows, :] = _softmax_pv_folded(a, vaug_ref[0:n, :], g_ref[rows, :])

    return setup, n_blocks, score_matmul, finish


def _moba_setup(q_ref, k_ref, v_ref, kaug_ref, qaug_ref, vaug_ref, kmean_ref, n_blocks, n_chunks):
    blk = MOBA_BLOCK
    seq = q_ref.shape[0]
    kaug_ref[:, :LANES] = k_ref[...]
    vaug_ref[:, :LANES] = v_ref[...]
    vaug_ref[:, LANES:] = jnp.ones((seq, LANES), BF16)
    row8 = lax.broadcasted_iota(jnp.int32, (8, LANES), 0)
    km = jnp.zeros((8, LANES), F32)
    for j in range(n_blocks):
        mean_j = jnp.mean(k_ref[j * blk:(j + 1) * blk, :].astype(F32), axis=0, keepdims=True)
        km = jnp.where(row8 == j, mean_j, km)
    kmean_ref[...] = jnp.concatenate([km, jnp.zeros((LANES - 8, LANES), F32)], axis=0).astype(BF16)

    gate_t = lax.dot_general(kmean_ref[...], q_ref[...], _NT, preferred_element_type=F32)[0:8, :]
    jrow = lax.broadcasted_iota(jnp.int32, gate_t.shape, 0)
    qpos = lax.broadcasted_iota(jnp.int32, gate_t.shape, 1)
    qblk = jnp.right_shift(qpos, int(math.log2(blk)))
    gm = jnp.where(jrow < qblk, gate_t, NEG)
    sel_bias = jnp.zeros(gate_t.shape, F32)
    for j in range(n_blocks):
        gj = gm[j:j + 1, :]
        beats = (gm > gj) | ((gm == gj) & (jrow < j))
        rank = jnp.sum(beats.astype(F32), axis=0, keepdims=True)
        allowed = ((rank < MOBA_TOPK) & (qblk[0:1, :] > j)) | (qblk[0:1, :] == j)
        sel_bias = jnp.where(jrow == j, jnp.where(allowed, 0.0, NEG), sel_bias)
    assert 8 * n_chunks == LANES
    packed_t = jnp.concatenate([sel_bias[:, c * LANES:(c + 1) * LANES] for c in range(n_chunks)], axis=0).T
    lane_chunk = jnp.right_shift(lax.broadcasted_iota(jnp.int32, (LANES, LANES), 1), 3)
    qaug_ref[:, :LANES] = q_ref[...]
    for c in range(n_chunks):
        qaug_ref[c * LANES:(c + 1) * LANES, LANES:] = jnp.where(lane_chunk == c, packed_t, 0.0).astype(BF16)


def _moba_scratch(seq):
    blk = MOBA_BLOCK
    return [pltpu.VMEM((seq, 2 * LANES), BF16),
            pltpu.VMEM((seq, 2 * LANES), BF16),
            pltpu.VMEM((seq, 2 * LANES), BF16),
            pltpu.VMEM((LANES, LANES), BF16),
            pltpu.VMEM((blk, blk), F32),
            pltpu.VMEM((blk, blk), F32)]


def _fox_stream(h, q_ref, k_ref, v_ref, g_ref, kc_ref, o_ref, qaug_ref, kaug_ref, vaug_ref, causal_ref):
    seq = q_ref.shape[0]

    def setup():
        col = lax.broadcasted_iota(jnp.int32, (seq, LANES), 1)
        qaug_ref[:, :LANES] = q_ref[...]
        qaug_ref[:, LANES:] = jnp.where(((col & 7) == h) & (col < 24), 1.0, 0.0).astype(BF16)
        kaug_ref[:, :LANES] = k_ref[...]
        kaug_ref[:, LANES:] = kc_ref[...]
        vaug_ref[:, :LANES] = v_ref[...]
        vaug_ref[:, LANES:] = jnp.ones((seq, LANES), BF16)
        r = lax.broadcasted_iota(jnp.int32, (Q_TILE, Q_TILE), 0)
        c = lax.broadcasted_iota(jnp.int32, (Q_TILE, Q_TILE), 1)
        causal_ref[...] = jnp.where(c <= r, 0.0, NEG)

    def score_matmul(i):
        return lax.dot_general(qaug_ref[i * Q_TILE:(i + 1) * Q_TILE, :], kaug_ref[0:(i + 1) * Q_TILE, :],
                               _NT, preferred_element_type=F32)

    def finish(i, a):
        rows = slice(i * Q_TILE, (i + 1) * Q_TILE)
        n = (i + 1) * Q_TILE
        own = a[:, n - Q_TILE:] + causal_ref[...]
        a = own if i == 0 else jnp.concatenate([a[:, :n - Q_TILE], own], axis=1)
        o_ref[rows, :] = _softmax_pv_folded(a, vaug_ref[0:n, :], g_ref[rows, :])

    return setup, seq // Q_TILE, score_matmul, finish


def _fox_scratch(seq):
    return [pltpu.VMEM((seq, 2 * LANES), BF16), pltpu.VMEM((seq, 2 * LANES), BF16),
            pltpu.VMEM((seq, 2 * LANES), BF16), pltpu.VMEM((Q_TILE, Q_TILE), F32)]


FOX_LEAD_EVENTS = 8


def _causal_attention_kernel(*refs):
    rb_ref, qa, ka, va, ga, qb, kb, vb, gb, kc, oa, ob = refs[:12]
    n_moba = len(_moba_scratch(qa.shape[0]))
    moba_scratch, fox_scratch = refs[12:12 + n_moba], refs[12 + n_moba:]
    kaug_ref, _, _, _, bown_ref, bprev_ref = moba_scratch
    h = pl.program_id(0)

    @pl.when(pl.program_id(1) == 0)
    def _():
        _moba_head_constants(rb_ref, h, kaug_ref, bown_ref, bprev_ref)

    _run_interleaved([_stream_events(*_fox_stream(h, qb, kb, vb, gb, kc, ob, *fox_scratch)),
                      _stream_events(*_moba_stream(qa, ka, va, ga, oa, *moba_scratch))],
                     lead=FOX_LEAD_EVENTS)


def _causal_attention(p, rel_bias, kc, batch, seq, moba_offsets, fox_offsets):
    nb = seq // MOBA_BLOCK
    assert seq % MOBA_BLOCK == 0 and MOBA_TOPK < 8 and nb <= 8 and MOBA_BLOCK > T5_MAX_DIST
    assert N_HEADS_MOBA == N_HEADS_FOX
    full = lambda off: pl.BlockSpec((None, seq, LANES), lambda h, b: (off + h, b, 0))
    out = pl.BlockSpec((None, seq, LANES), lambda h, b: (h, b, 0))
    out_sds = jax.ShapeDtypeStruct((N_HEADS_MOBA, batch * seq, LANES), BF16)
    return pl.pallas_call(
        _causal_attention_kernel,
        grid=(N_HEADS_MOBA, batch),
        in_specs=([pl.BlockSpec(memory_space=pltpu.SMEM)]
                  + [full(off) for off in moba_offsets] + [full(off) for off in fox_offsets]
                  + [pl.BlockSpec((None, seq, LANES), lambda h, b: (b, 0, 0))]),
        out_specs=[out, out],
        out_shape=[out_sds, out_sds],
        scratch_shapes=_moba_scratch(seq) + _fox_scratch(seq),
        compiler_params=_cparams(2),
        name="causal_attention",
    )(rel_bias, *([p] * 8), kc)


def _mem_kernel(q_ref, k_ref, v_ref, g_ref, o_ref, vaug_ref):
    n_heads, seq, _ = q_ref.shape
    tiles_per_head = seq // MEM_Q_TILE

    @pl.when(pl.program_id(0) == 0)
    def _ones():
        vaug_ref[:, :, LANES:] = jnp.ones(v_ref.shape, BF16)

    vaug_ref[:, :, :LANES] = v_ref[...]

    def tile(t):
        h, i = divmod(t, tiles_per_head)
        return h, slice(i * MEM_Q_TILE, (i + 1) * MEM_Q_TILE)

    def score_matmul(t):
        h, rows = tile(t)
        return lax.dot_general(q_ref[h, rows, :], k_ref[h], _NT, preferred_element_type=F32)

    def finish(t, a):
        h, rows = tile(t)
        o_ref[h, rows, :] = _softmax_pv_folded(a, vaug_ref[h], g_ref[h, rows, :])

    _run_interleaved([_stream_events(lambda: None, n_heads * tiles_per_head, score_matmul, finish)])


def _mem_attention(p, mkv, batch, seq, n_mem, q0, g0):
    nh = N_HEADS_MEM
    assert q0 % nh == 0 and g0 % nh == 0
    heads = lambda off: pl.BlockSpec((nh, seq, LANES), lambda b: (off // nh, b, 0))
    kv = lambda blk: pl.BlockSpec((nh, n_mem, LANES), lambda b: (blk, b, 0))
    return pl.pallas_call(
        _mem_kernel,
        grid=(batch,),
        in_specs=[heads(q0), kv(0), kv(1), heads(g0)],
        out_specs=pl.BlockSpec((nh, seq, LANES), lambda b: (0, b, 0)),
        out_shape=jax.ShapeDtypeStruct((nh, batch * seq, LANES), BF16),
        scratch_shapes=[pltpu.VMEM((nh, n_mem, 2 * LANES), BF16)],
        compiler_params=_cparams(1),
        name="mem_attention",
    )(p, mkv, mkv, p)


def _out_proj_kernel(ya_ref, yb_ref, ym_ref, w_ref, x_ref, g_ref, o_ref, wb_ref, *, row_chunk):
    @pl.when(pl.program_id(0) == 0)
    def _cast_weights():
        def body(c, carry):
            rows = pl.ds(pl.multiple_of(c * row_chunk, row_chunk), row_chunk)
            wb_ref[rows, :] = w_ref[rows, :].astype(BF16)
            return carry
        lax.fori_loop(0, w_ref.shape[0] // row_chunk, body, 0)

    y = jnp.concatenate([src[c] for src in (ya_ref, yb_ref, ym_ref) for c in range(src.shape[0])], axis=1)
    o = jnp.dot(y, wb_ref[...], preferred_element_type=F32)
    ms = jnp.mean(o * o, axis=-1, keepdims=True)
    o_ref[...] = x_ref[...] + o * lax.rsqrt(ms + EPS) * g_ref[...]


def _out_proj(ya, yb, ym, w3, layer, x2, g, *, tm):
    m, d = x2.shape
    dw = w3.shape[1]
    heads = lambda y: pl.BlockSpec((y.shape[0], tm, LANES), lambda i: (0, i, 0))
    return pl.pallas_call(
        functools.partial(_out_proj_kernel, row_chunk=256),
        grid=(m // tm,),
        in_specs=[heads(ya), heads(yb), heads(ym),
                  pl.BlockSpec((None, dw, d), lambda i: (layer, 0, 0), pipeline_mode=pl.Buffered(1)),
                  pl.BlockSpec((tm, d), lambda i: (i, 0)),
                  pl.BlockSpec((1, d), lambda i: (0, 0))],
        out_specs=pl.BlockSpec((tm, d), lambda i: (i, 0)),
        out_shape=jax.ShapeDtypeStruct((m, d), F32),
        scratch_shapes=[pltpu.VMEM((dw, d), BF16)],
        compiler_params=_cparams(1),
        name="out_proj",
    )(ya, yb, ym, w3, x2, g)


def kernel(x, mem, w_in, b_forget, w_mem_kv, w_out, g_pre, g_post, g_mem, rel_bias):
    batch, seq, d = x.shape
    n_mem = mem.shape[1]
    depth = w_in.shape[0]
    wa, wb, wm = N_HEADS_MOBA * HEAD_DIM, N_HEADS_FOX * HEAD_DIM, N_HEADS_MEM * HEAD_DIM
    f0 = 4 * wa + 4 * wb
    f1 = f0 + N_HEADS_FOX
    ha, hb, hm = N_HEADS_MOBA, N_HEADS_FOX, N_HEADS_MEM
    qa0, ka0, va0, ga0 = 0, ha, 2 * ha, 3 * ha
    qb0, kb0, vb0, gb0 = 4 * ha, 4 * ha + hb, 4 * ha + 2 * hb, 4 * ha + 3 * hb
    qm0, gm0 = 4 * ha + 4 * hb, 4 * ha + 4 * hb + hm

    x2 = x.reshape(batch * seq, d)
    mem2 = mem.reshape(batch * n_mem, d)
    for layer in range(depth):
        b_vec = jnp.pad(b_forget[layer], (0, LANES - N_HEADS_FOX)).reshape(1, LANES)

        n_in = w_in.shape[2]
        assert n_in == f1 + 2 * wm
        w_t3 = jnp.transpose(w_in, (0, 2, 1)).reshape(depth * n_in, d // LANES, LANES)
        w_main_t, w_gate_t = _w_repack(w_t3, layer, n_in, f0, N_HEADS_FOX, rows=512)
        p, fb = _norm_proj(x2, g_pre[layer].reshape(1, d), w_main_t, w_gate_t, w_transposed=True,
                           tm=1024, tn=1792)
        mkv, _ = _norm_proj(mem2, g_mem[layer].reshape(1, d), w_mem_kv[layer], w_transposed=False,
                            tm=512, tn=512)
        kc = _forget_scan(fb, b_vec, batch, seq, N_HEADS_FOX)

        ya, yb = _causal_attention(p, rel_bias, kc, batch, seq,
                                   (qa0, ka0, va0, ga0), (qb0, kb0, vb0, gb0))
        ym = _mem_attention(p, mkv, batch, seq, n_mem, qm0, g0=gm0)
        x2 = _out_proj(ya, yb, ym, w_out, layer, x2, g_post[layer].reshape(1, d), tm=512)
    return x2.reshape(batch, seq, d)
```

```python
import functools
import math

import jax
import jax.numpy as jnp
from jax import lax
from jax.experimental import pallas as pl
from jax.experimental.pallas import tpu as pltpu

HEAD_DIM = 128
N_HEADS_MOBA = 6
N_HEADS_FOX = 6
N_HEADS_MEM = 4
MOBA_BLOCK = 256
MOBA_TOPK = 3
T5_BUCKETS = 32
T5_MAX_DIST = 128
EPS = 1e-6
NEG = -1e30

LANES = 128
SUBLANES = 8
VMEM_LIMIT_BYTES = 56 * 1024 * 1024

IN_PROJ_TILE = (1024, 1792)
OUT_PROJ_ROWS = 512
REPACK_ROWS = 512
W_OUT_CAST_ROWS = 128
Q_TILE = MOBA_BLOCK
MEM_Q_TILE = 512

F32 = jnp.float32
BF16 = jnp.bfloat16
_NT = (((1,), (1,)), ((), ()))


def _cparams(n_axes):
    return pltpu.CompilerParams(dimension_semantics=("arbitrary",) * n_axes,
                                vmem_limit_bytes=VMEM_LIMIT_BYTES)


def _norm_proj_kernel(*refs, n_col_blocks, w_transposed, with_gate_logits, row_chunk):
    if with_gate_logits:
        x_ref, g_ref, w_ref, wf_ref, p_ref, f_ref, h_ref = refs
    else:
        x_ref, g_ref, w_ref, p_ref, h_ref = refs
    tm = x_ref.shape[0]

    @pl.when(pl.program_id(1) == 0)
    def _normalize():
        for c in range(tm // row_chunk):
            rows = slice(c * row_chunk, (c + 1) * row_chunk)
            xf = x_ref[rows, :]
            ms = jnp.mean(xf * xf, axis=-1, keepdims=True)
            h = (xf * lax.rsqrt(ms + EPS) * g_ref[...]).astype(BF16)
            h_ref[rows, :] = h
            if with_gate_logits:
                f_ref[rows, :] = lax.dot_general(h, wf_ref[...], _NT, preferred_element_type=F32)

    w = w_ref[...].astype(BF16)
    if w_transposed:
        acc = lax.dot_general(h_ref[...], w, _NT, preferred_element_type=F32)
    else:
        acc = jnp.dot(h_ref[...], w, preferred_element_type=F32)
    for c in range(n_col_blocks):
        p_ref[c] = acc[:, c * LANES:(c + 1) * LANES].astype(BF16)


def _norm_proj(x2, g, w, wf_t=None, *, w_transposed, tm, tn):
    m, d = x2.shape
    n = w.shape[0] if w_transposed else w.shape[1]
    ncb = tn // LANES
    with_f = wf_t is not None
    assert n % tn == 0 and m % tm == 0
    w_spec = (pl.BlockSpec((tn, d), lambda i, j: (j, 0)) if w_transposed
              else pl.BlockSpec((d, tn), lambda i, j: (0, j)))
    in_specs = [pl.BlockSpec((tm, d), lambda i, j: (i, 0)),
                pl.BlockSpec((1, d), lambda i, j: (0, 0)), w_spec]
    out_shape = [jax.ShapeDtypeStruct((n // LANES, m, LANES), BF16)]
    out_specs = [pl.BlockSpec((ncb, tm, LANES), lambda i, j: (j, i, 0))]
    args = [x2, g, w]
    if with_f:
        in_specs.append(pl.BlockSpec((LANES, d), lambda i, j: (0, 0)))
        out_shape.append(jax.ShapeDtypeStruct((m, LANES), F32))
        out_specs.append(pl.BlockSpec((tm, LANES), lambda i, j: (i, 0)))
        args.append(wf_t)
    res = pl.pallas_call(
        functools.partial(_norm_proj_kernel, n_col_blocks=ncb, w_transposed=w_transposed,
                          with_gate_logits=with_f, row_chunk=256),
        grid=(m // tm, n // tn),
        in_specs=in_specs, out_specs=out_specs, out_shape=out_shape,
        scratch_shapes=[pltpu.VMEM((tm, d), BF16)],
        compiler_params=_cparams(2),
        name="norm_proj_gate" if with_f else "norm_proj",
    )(*args)
    return res if with_f else (res[0], None)


def _w_repack_kernel(src_ref, main_ref, gate_ref, buf_ref, gbuf_ref, sem_ref, gsem_ref, *,
                     rows, base_row, n_aligned_blocks, n_gate):
    t = pl.program_id(0)
    n_steps = pl.num_programs(0)
    n_slices = src_ref.shape[1]

    def block_copies(step, slot):
        row0 = base_row + step * rows + jnp.where(step >= n_aligned_blocks, n_gate, 0)
        return [pltpu.make_async_copy(src_ref.at[pl.ds(row0, rows), s, :],
                                      buf_ref.at[slot, :, pl.ds(s * LANES, LANES)],
                                      sem_ref.at[slot]) for s in range(n_slices)]

    def gate_copies():
        gate_row0 = base_row + n_aligned_blocks * rows
        return [pltpu.make_async_copy(src_ref.at[pl.ds(gate_row0, SUBLANES), s, :],
                                      gbuf_ref.at[:, pl.ds(s * LANES, LANES)],
                                      gsem_ref.at[0]) for s in range(n_slices)]

    @pl.when(t == 0)
    def _first():
        for cp in gate_copies() + block_copies(0, 0):
            cp.start()

    @pl.when(t + 1 < n_steps)
    def _prefetch():
        for cp in block_copies(t + 1, (t + 1) % 2):
            cp.start()

    @pl.when(t == 0)
    def _gate_rows():
        for cp in gate_copies():
            cp.wait()
        row = lax.broadcasted_iota(jnp.int32, gbuf_ref.shape, 0)
        gate = jnp.where(row < n_gate, gbuf_ref[...], 0.0)
        gate_ref[...] = jnp.concatenate(
            [gate, jnp.zeros((gate_ref.shape[0] - SUBLANES, gate.shape[1]), F32)], axis=0).astype(BF16)

    for cp in block_copies(t, t % 2):
        cp.wait()
    main_ref[...] = buf_ref[t % 2].astype(BF16)


def _w_repack(w_t3, layer, n_total, n_aligned, n_gate, *, rows):
    n_slices = w_t3.shape[1]
    d = n_slices * LANES
    n_main = n_total - n_gate
    assert n_aligned % rows == 0 and n_main % rows == 0 and n_gate <= SUBLANES
    return pl.pallas_call(
        functools.partial(_w_repack_kernel, rows=rows, base_row=layer * n_total,
                          n_aligned_blocks=n_aligned // rows, n_gate=n_gate),
        grid=(n_main // rows,),
        in_specs=[pl.BlockSpec(memory_space=pl.ANY)],
        out_specs=[pl.BlockSpec((rows, d), lambda t: (t, 0)),
                   pl.BlockSpec((LANES, d), lambda t: (0, 0))],
        out_shape=[jax.ShapeDtypeStruct((n_main, d), BF16),
                   jax.ShapeDtypeStruct((LANES, d), BF16)],
        scratch_shapes=[pltpu.VMEM((2, rows, d), F32), pltpu.VMEM((SUBLANES, d), F32),
                        pltpu.SemaphoreType.DMA((2,)), pltpu.SemaphoreType.DMA((1,))],
        compiler_params=_cparams(1),
        name="w_repack",
    )(w_t3)


N_SPLIT = 3


def _split3(v):
    hi = v.astype(BF16).astype(F32)
    mid = (v - hi).astype(BF16).astype(F32)
    lo = (v - hi - mid).astype(BF16).astype(F32)
    return hi, mid, lo


def _forget_bias_columns(f_ref, b_ref, kc_ref, n_heads):
    z = f_ref[...] + b_ref[...]
    logf = jnp.minimum(z, 0.0) - jnp.log1p(jnp.exp(-jnp.abs(z)))
    x = logf.T[0:SUBLANES, :]
    s_len = x.shape[1]
    lane = lax.broadcasted_iota(jnp.int32, x.shape, 1)
    shift = 1
    while shift < s_len:
        x = x + jnp.where(lane >= shift, pltpu.roll(x, shift, 1), 0.0)
        shift *= 2
    row = lax.broadcasted_iota(jnp.int32, x.shape, 0)
    x = jnp.where(row < n_heads, x, 0.0)
    terms = _split3(x * (-1.0 / HEAD_DIM ** -0.5))
    pad = jnp.zeros((LANES - SUBLANES * N_SPLIT, s_len), F32)
    kc_ref[...] = jnp.concatenate(list(terms) + [pad], axis=0).T.astype(BF16)


LOG2E = math.log2(math.e)


def _softmax_pv_folded(a, v_aug, g):
    m = jnp.max(a, axis=1, keepdims=True)
    p = jnp.exp2((a - m) * (HEAD_DIM ** -0.5 * LOG2E)).astype(BF16)
    acc = jnp.dot(p, v_aug, preferred_element_type=F32)
    gf = g.astype(F32)
    silu = gf * (1.0 / (1.0 + jnp.exp(-gf)))
    return (acc[:, :LANES] / acc[:, LANES:] * silu).astype(BF16)


SCORE_AHEAD = 1


def _stream_events(setup, n_tiles, score_matmul, finish):
    scores = {}

    def score(i):
        return lambda: scores.__setitem__(i, score_matmul(i))

    def fin(i):
        return lambda: finish(i, scores.pop(i))

    events = [setup] + [score(i) for i in range(min(SCORE_AHEAD, n_tiles))]
    for i in range(n_tiles):
        if i + SCORE_AHEAD < n_tiles:
            events.append(score(i + SCORE_AHEAD))
        events.append(fin(i))
    return events


def _run_interleaved(streams, lead=0):
    queues = [list(ev) for ev in streams]
    for _ in range(min(lead, len(queues[0]))):
        queues[0].pop(0)()
    while any(queues):
        for q in queues:
            if q:
                q.pop(0)()


def _t5_bias(rb_ref, h, dist):
    max_exact = T5_BUCKETS // 2
    n = jnp.maximum(dist, 0)
    nf = jnp.maximum(n, 1).astype(F32)
    val = jnp.log(nf / max_exact) / math.log(T5_MAX_DIST / max_exact) * (T5_BUCKETS - max_exact)
    bias = jnp.zeros(dist.shape, F32)
    for kk in range(T5_BUCKETS):
        if kk < max_exact:
            hit = n == kk
        elif kk < T5_BUCKETS - 1:
            hit = (n >= max_exact) & (val >= kk - max_exact) & (val < kk - max_exact + 1)
        else:
            hit = (n >= max_exact) & (val >= kk - max_exact)
        bias = jnp.where(hit, rb_ref[h, kk], bias)
    return bias


def _moba_head_constants(rb_ref, h, kaug_ref, bown_ref, bprev_ref):
    inv_scale = 1.0 / HEAD_DIM ** -0.5
    blk = MOBA_BLOCK
    seq = kaug_ref.shape[0]
    far_bias = rb_ref[h, T5_BUCKETS - 1]
    srow = lax.broadcasted_iota(jnp.int32, (seq, LANES), 0)
    col = lax.broadcasted_iota(jnp.int32, (seq, LANES), 1)
    kaug_ref[:, LANES:] = jnp.where((col & (SUBLANES - 1)) * blk == srow - (srow & (blk - 1)),
                                    1.0, 0.0).astype(BF16)
    t = T5_MAX_DIST
    assert blk == 2 * t
    dist = lax.broadcasted_iota(jnp.int32, (t, t), 0) - lax.broadcasted_iota(jnp.int32, (t, t), 1)
    near = jnp.where(dist >= 0, (_t5_bias(rb_ref, h, dist) - far_bias) * inv_scale, NEG)
    mid = (_t5_bias(rb_ref, h, dist + t) - far_bias) * inv_scale
    bown_ref[0:t, 0:t] = near
    bown_ref[0:t, t:blk] = jnp.full((t, t), NEG, F32)
    bown_ref[t:blk, 0:t] = mid
    bown_ref[t:blk, t:blk] = near
    bprev_ref[...] = jnp.zeros((blk, blk), F32)
    bprev_ref[0:t, t:blk] = mid


def _moba_stream(q_ref, k_ref, v_ref, g_ref, o_ref,
                 kaug_ref, qaug_ref, vaug_ref, kmean_ref, bown_ref, bprev_ref):
    blk = MOBA_BLOCK
    seq = q_ref.shape[0]
    n_blocks = seq // blk
    n_chunks = seq // LANES

    def setup():
        _moba_setup(q_ref, k_ref, v_ref, kaug_ref, qaug_ref, vaug_ref, kmean_ref, n_blocks, n_chunks)

    def score_matmul(i):
        return lax.dot_general(qaug_ref[i * blk:(i + 1) * blk, :], kaug_ref[0:(i + 1) * blk, :], _NT,
                               preferred_element_type=F32)

    def finish(i, a):
        rows = slice(i * blk, (i + 1) * blk)
        n = (i + 1) * blk
        parts = [a[:, n - blk:] + bown_ref[...]]
        if i >= 1:
            parts.insert(0, a[:, n - 2 * blk:n - blk] + bprev_ref[...])
        if i >= 2:
            parts.insert(0, a[:, :n - 2 * blk])
        a = parts[0] if len(parts) == 1 else jnp.concatenate(parts, axis=1)
        o_ref[rows, :] = _softmax_pv_folded(a, vaug_ref[0:n, :], g_ref[rows, :])

    return setup, n_blocks, score_matmul, finish


def _moba_setup(q_ref, k_ref, v_ref, kaug_ref, qaug_ref, vaug_ref, kmean_ref, n_blocks, n_chunks):
    blk = MOBA_BLOCK
    seq = q_ref.shape[0]
    kaug_ref[:, :LANES] = k_ref[...]
    vaug_ref[:, :LANES] = v_ref[...]
    vaug_ref[:, LANES:] = jnp.ones((seq, LANES), BF16)
    blk_row = lax.broadcasted_iota(jnp.int32, (SUBLANES, LANES), 0)
    km = jnp.zeros((SUBLANES, LANES), F32)
    for j in range(n_blocks):
        mean_j = jnp.mean(k_ref[j * blk:(j + 1) * blk, :].astype(F32), axis=0, keepdims=True)
        km = jnp.where(blk_row == j, mean_j, km)
    kmean_ref[...] = jnp.concatenate([km, jnp.zeros((LANES - SUBLANES, LANES), F32)], axis=0).astype(BF16)

    gate_t = lax.dot_general(kmean_ref[...], q_ref[...], _NT, preferred_element_type=F32)[0:SUBLANES, :]
    jrow = lax.broadcasted_iota(jnp.int32, gate_t.shape, 0)
    qpos = lax.broadcasted_iota(jnp.int32, gate_t.shape, 1)
    qblk = jnp.right_shift(qpos, int(math.log2(blk)))
    gm = jnp.where(jrow < qblk, gate_t, NEG)
    sel_bias = jnp.zeros(gate_t.shape, F32)
    for j in range(n_blocks):
        gj = gm[j:j + 1, :]
        beats = (gm > gj) | ((gm == gj) & (jrow < j))
        rank = jnp.sum(beats.astype(F32), axis=0, keepdims=True)
        allowed = ((rank < MOBA_TOPK) & (qblk[0:1, :] > j)) | (qblk[0:1, :] == j)
        sel_bias = jnp.where(jrow == j, jnp.where(allowed, 0.0, NEG), sel_bias)
    assert SUBLANES * n_chunks == LANES
    packed_t = jnp.concatenate([sel_bias[:, c * LANES:(c + 1) * LANES] for c in range(n_chunks)], axis=0).T
    lane_chunk = jnp.right_shift(lax.broadcasted_iota(jnp.int32, (LANES, LANES), 1),
                                 int(math.log2(SUBLANES)))
    qaug_ref[:, :LANES] = q_ref[...]
    for c in range(n_chunks):
        qaug_ref[c * LANES:(c + 1) * LANES, LANES:] = jnp.where(lane_chunk == c, packed_t, 0.0).astype(BF16)


def _moba_scratch(seq):
    blk = MOBA_BLOCK
    return [pltpu.VMEM((seq, 2 * LANES), BF16),
            pltpu.VMEM((seq, 2 * LANES), BF16),
            pltpu.VMEM((seq, 2 * LANES), BF16),
            pltpu.VMEM((LANES, LANES), BF16),
            pltpu.VMEM((blk, blk), F32),
            pltpu.VMEM((blk, blk), F32)]


def _fox_stream(h, q_ref, k_ref, v_ref, g_ref, kc_ref, o_ref, qaug_ref, kaug_ref, vaug_ref, causal_ref):
    seq = q_ref.shape[0]

    def setup():
        col = lax.broadcasted_iota(jnp.int32, (seq, LANES), 1)
        qaug_ref[:, :LANES] = q_ref[...]
        qaug_ref[:, LANES:] = jnp.where(((col & (SUBLANES - 1)) == h) & (col < SUBLANES * N_SPLIT),
                                        1.0, 0.0).astype(BF16)
        kaug_ref[:, :LANES] = k_ref[...]
        kaug_ref[:, LANES:] = kc_ref[...]
        vaug_ref[:, :LANES] = v_ref[...]
        vaug_ref[:, LANES:] = jnp.ones((seq, LANES), BF16)
        r = lax.broadcasted_iota(jnp.int32, (Q_TILE, Q_TILE), 0)
        c = lax.broadcasted_iota(jnp.int32, (Q_TILE, Q_TILE), 1)
        causal_ref[...] = jnp.where(c <= r, 0.0, NEG)

    def score_matmul(i):
        return lax.dot_general(qaug_ref[i * Q_TILE:(i + 1) * Q_TILE, :], kaug_ref[0:(i + 1) * Q_TILE, :],
                               _NT, preferred_element_type=F32)

    def finish(i, a):
        rows = slice(i * Q_TILE, (i + 1) * Q_TILE)
        n = (i + 1) * Q_TILE
        own = a[:, n - Q_TILE:] + causal_ref[...]
        a = own if i == 0 else jnp.concatenate([a[:, :n - Q_TILE], own], axis=1)
        o_ref[rows, :] = _softmax_pv_folded(a, vaug_ref[0:n, :], g_ref[rows, :])

    return setup, seq // Q_TILE, score_matmul, finish


def _fox_scratch(seq):
    return [pltpu.VMEM((seq, 2 * LANES), BF16), pltpu.VMEM((seq, 2 * LANES), BF16),
            pltpu.VMEM((seq, 2 * LANES), BF16), pltpu.VMEM((Q_TILE, Q_TILE), F32)]


FOX_LEAD_EVENTS = 8


def _causal_attention_kernel(*refs):
    rb_ref, qa, ka, va, ga, qb, kb, vb, gb, kc, wo_ref, oa, ob, wob_ref = refs[:14]
    n_moba = len(_moba_scratch(qa.shape[0]))
    moba_scratch, fox_scratch = refs[14:14 + n_moba], refs[14 + n_moba:]
    kaug_ref, _, _, _, bown_ref, bprev_ref = moba_scratch
    h = pl.program_id(0)

    @pl.when(pl.program_id(1) == 0)
    def _():
        _moba_head_constants(rb_ref, h, kaug_ref, bown_ref, bprev_ref)

    wob_ref[...] = wo_ref[...].astype(BF16)
    _run_interleaved([_stream_events(*_fox_stream(h, qb, kb, vb, gb, kc, ob, *fox_scratch)),
                      _stream_events(*_moba_stream(qa, ka, va, ga, oa, *moba_scratch))],
                     lead=FOX_LEAD_EVENTS)


def _causal_attention(p, rel_bias, kc, w_out3, layer, batch, seq, moba_offsets, fox_offsets):
    nb = seq // MOBA_BLOCK
    assert seq % MOBA_BLOCK == 0 and MOBA_TOPK < SUBLANES and nb <= SUBLANES and MOBA_BLOCK > T5_MAX_DIST
    assert N_HEADS_MOBA == N_HEADS_FOX <= SUBLANES
    dw, d = w_out3.shape[1:]
    n_slabs = dw // W_OUT_CAST_ROWS
    assert dw % W_OUT_CAST_ROWS == 0 and n_slabs <= N_HEADS_MOBA * batch
    slab = lambda h, b: jnp.minimum(h * batch + b, n_slabs - 1)
    full = lambda off: pl.BlockSpec((None, seq, LANES), lambda h, b: (off + h, b, 0))
    out = pl.BlockSpec((None, seq, LANES), lambda h, b: (h, b, 0))
    out_sds = jax.ShapeDtypeStruct((N_HEADS_MOBA, batch * seq, LANES), BF16)
    return pl.pallas_call(
        _causal_attention_kernel,
        grid=(N_HEADS_MOBA, batch),
        in_specs=([pl.BlockSpec(memory_space=pltpu.SMEM)]
                  + [full(off) for off in moba_offsets] + [full(off) for off in fox_offsets]
                  + [pl.BlockSpec((None, seq, LANES), lambda h, b: (b, 0, 0)),
                     pl.BlockSpec((None, W_OUT_CAST_ROWS, d), lambda h, b: (layer, slab(h, b), 0))]),
        out_specs=[out, out, pl.BlockSpec((W_OUT_CAST_ROWS, d), lambda h, b: (slab(h, b), 0))],
        out_shape=[out_sds, out_sds, jax.ShapeDtypeStruct((dw, d), BF16)],
        scratch_shapes=_moba_scratch(seq) + _fox_scratch(seq),
        compiler_params=_cparams(2),
        name="causal_attention",
    )(rel_bias, *([p] * 8), kc, w_out3)


def _mem_kernel(q_ref, k_ref, v_ref, g_ref, f_ref, b_ref, o_ref, kc_ref, vaug_ref, *, n_fox_heads):
    n_heads, seq, _ = q_ref.shape
    tiles_per_head = seq // MEM_Q_TILE

    _forget_bias_columns(f_ref, b_ref, kc_ref, n_fox_heads)
    vaug_ref[:, :, LANES:] = jnp.ones(v_ref.shape, BF16)
    vaug_ref[:, :, :LANES] = v_ref[...]

    def tile(t):
        h, i = divmod(t, tiles_per_head)
        return h, slice(i * MEM_Q_TILE, (i + 1) * MEM_Q_TILE)

    def score_matmul(t):
        h, rows = tile(t)
        return lax.dot_general(q_ref[h, rows, :], k_ref[h], _NT, preferred_element_type=F32)

    def finish(t, a):
        h, rows = tile(t)
        o_ref[h, rows, :] = _softmax_pv_folded(a, vaug_ref[h], g_ref[h, rows, :])

    _run_interleaved([_stream_events(lambda: None, n_heads * tiles_per_head, score_matmul, finish)])


def _mem_attention(p, mkv, fb, bvec, batch, seq, n_mem, q0, g0):
    nh = N_HEADS_MEM
    assert q0 % nh == 0 and g0 % nh == 0 and N_HEADS_FOX <= SUBLANES
    heads = lambda off: pl.BlockSpec((nh, seq, LANES), lambda b: (off // nh, b, 0))
    kv = lambda blk: pl.BlockSpec((nh, n_mem, LANES), lambda b: (blk, b, 0))
    return pl.pallas_call(
        functools.partial(_mem_kernel, n_fox_heads=N_HEADS_FOX),
        grid=(batch,),
        in_specs=[heads(q0), kv(0), kv(1), heads(g0),
                  pl.BlockSpec((seq, LANES), lambda b: (b, 0)),
                  pl.BlockSpec((1, LANES), lambda b: (0, 0))],
        out_specs=[pl.BlockSpec((nh, seq, LANES), lambda b: (0, b, 0)),
                   pl.BlockSpec((None, seq, LANES), lambda b: (b, 0, 0))],
        out_shape=[jax.ShapeDtypeStruct((nh, batch * seq, LANES), BF16),
                   jax.ShapeDtypeStruct((batch, seq, LANES), BF16)],
        scratch_shapes=[pltpu.VMEM((nh, n_mem, 2 * LANES), BF16)],
        compiler_params=_cparams(1),
        name="mem_attention",
    )(p, mkv, mkv, p, fb, bvec)


def _out_proj_kernel(ya_ref, yb_ref, ym_ref, w_ref, x_ref, g_ref, o_ref):
    y = jnp.concatenate([src[c] for src in (ya_ref, yb_ref, ym_ref) for c in range(src.shape[0])], axis=1)
    o = jnp.dot(y, w_ref[...], preferred_element_type=F32)
    ms = jnp.mean(o * o, axis=-1, keepdims=True)
    o_ref[...] = x_ref[...] + o * lax.rsqrt(ms + EPS) * g_ref[...]


def _out_proj(ya, yb, ym, w, x2, g, *, tm):
    m, d = x2.shape
    heads = lambda y: pl.BlockSpec((y.shape[0], tm, LANES), lambda i: (0, i, 0))
    return pl.pallas_call(
        _out_proj_kernel,
        grid=(m // tm,),
        in_specs=[heads(ya), heads(yb), heads(ym),
                  pl.BlockSpec(w.shape, lambda i: (0, 0), pipeline_mode=pl.Buffered(1)),
                  pl.BlockSpec((tm, d), lambda i: (i, 0)),
                  pl.BlockSpec((1, d), lambda i: (0, 0))],
        out_specs=pl.BlockSpec((tm, d), lambda i: (i, 0)),
        out_shape=jax.ShapeDtypeStruct((m, d), F32),
        compiler_params=_cparams(1),
        name="out_proj",
    )(ya, yb, ym, w, x2, g)


def kernel(x, mem, w_in, b_forget, w_mem_kv, w_out, g_pre, g_post, g_mem, rel_bias):
    batch, seq, d = x.shape
    n_mem = mem.shape[1]
    depth = w_in.shape[0]
    wa, wb, wm = N_HEADS_MOBA * HEAD_DIM, N_HEADS_FOX * HEAD_DIM, N_HEADS_MEM * HEAD_DIM
    f0 = 4 * wa + 4 * wb
    f1 = f0 + N_HEADS_FOX
    ha, hb, hm = N_HEADS_MOBA, N_HEADS_FOX, N_HEADS_MEM
    qa0, ka0, va0, ga0 = 0, ha, 2 * ha, 3 * ha
    qb0, kb0, vb0, gb0 = 4 * ha, 4 * ha + hb, 4 * ha + 2 * hb, 4 * ha + 3 * hb
    qm0, gm0 = 4 * ha + 4 * hb, 4 * ha + 4 * hb + hm

    x2 = x.reshape(batch * seq, d)
    mem2 = mem.reshape(batch * n_mem, d)
    for layer in range(depth):
        b_vec = jnp.pad(b_forget[layer], (0, LANES - N_HEADS_FOX)).reshape(1, LANES)

        n_in = w_in.shape[2]
        assert n_in == f1 + 2 * wm
        w_t3 = jnp.transpose(w_in, (0, 2, 1)).reshape(depth * n_in, d // LANES, LANES)
        w_main_t, w_gate_t = _w_repack(w_t3, layer, n_in, f0, N_HEADS_FOX, rows=REPACK_ROWS)
        p, fb = _norm_proj(x2, g_pre[layer].reshape(1, d), w_main_t, w_gate_t, w_transposed=True,
                           tm=IN_PROJ_TILE[0], tn=IN_PROJ_TILE[1])
        mkv, _ = _norm_proj(mem2, g_mem[layer].reshape(1, d), w_mem_kv[layer], w_transposed=False,
                            tm=batch * n_mem, tn=2 * wm)
        ym, kc = _mem_attention(p, mkv, fb, b_vec, batch, seq, n_mem, qm0, gm0)
        ya, yb, w_out_bf = _causal_attention(p, rel_bias, kc, w_out, layer, batch, seq,
                                             (qa0, ka0, va0, ga0), (qb0, kb0, vb0, gb0))
        x2 = _out_proj(ya, yb, ym, w_out_bf, x2, g_post[layer].reshape(1, d), tm=OUT_PROJ_ROWS)
    return x2.reshape(batch, seq, d)
```

```python
import functools
import math

import jax
import jax.numpy as jnp
from jax import lax
from jax.experimental import pallas as pl
from jax.experimental.pallas import tpu as pltpu

HEAD_DIM = 128
N_HEADS_MOBA = 6
N_HEADS_FOX = 6
N_HEADS_MEM = 4
MOBA_BLOCK = 256
MOBA_TOPK = 3
T5_BUCKETS = 32
T5_MAX_DIST = 128
EPS = 1e-6
NEG = -1e30

LANES = 128
SUBLANES = 8
VMEM_LIMIT_BYTES = 56 * 1024 * 1024

IN_PROJ_TILE = (1024, 1792)
OUT_PROJ_ROWS = 512
REPACK_ROWS = 512
W_OUT_CAST_ROWS = 128
Q_TILE = MOBA_BLOCK
MEM_Q_TILE = 512

F32 = jnp.float32
BF16 = jnp.bfloat16
_NT = (((1,), (1,)), ((), ()))


def _cparams(n_axes):
    return pltpu.CompilerParams(dimension_semantics=("arbitrary",) * n_axes,
                                vmem_limit_bytes=VMEM_LIMIT_BYTES)


def _norm_proj_kernel(*refs, n_col_blocks, w_transposed, with_gate_logits, row_chunk):
    if with_gate_logits:
        x_ref, g_ref, w_ref, wf_ref, p_ref, f_ref, h_ref = refs
    else:
        x_ref, g_ref, w_ref, p_ref, h_ref = refs
    tm = x_ref.shape[0]

    @pl.when(pl.program_id(1) == 0)
    def _normalize():
        for c in range(tm // row_chunk):
            rows = slice(c * row_chunk, (c + 1) * row_chunk)
            xf = x_ref[rows, :]
            ms = jnp.mean(xf * xf, axis=-1, keepdims=True)
            h = (xf * lax.rsqrt(ms + EPS) * g_ref[...]).astype(BF16)
            h_ref[rows, :] = h
            if with_gate_logits:
                f_ref[rows, :] = lax.dot_general(h, wf_ref[...], _NT, preferred_element_type=F32)

    w = w_ref[...].astype(BF16)
    if w_transposed:
        acc = lax.dot_general(h_ref[...], w, _NT, preferred_element_type=F32)
    else:
        acc = jnp.dot(h_ref[...], w, preferred_element_type=F32)
    for c in range(n_col_blocks):
        p_ref[c] = acc[:, c * LANES:(c + 1) * LANES].astype(BF16)


def _norm_proj(x2, g, w, wf_t=None, *, w_transposed, tm, tn):
    m, d = x2.shape
    n = w.shape[0] if w_transposed else w.shape[1]
    ncb = tn // LANES
    with_f = wf_t is not None
    assert n % tn == 0 and m % tm == 0
    w_spec = (pl.BlockSpec((tn, d), lambda i, j: (j, 0)) if w_transposed
              else pl.BlockSpec((d, tn), lambda i, j: (0, j)))
    in_specs = [pl.BlockSpec((tm, d), lambda i, j: (i, 0)),
                pl.BlockSpec((1, d), lambda i, j: (0, 0)), w_spec]
    out_shape = [jax.ShapeDtypeStruct((n // LANES, m, LANES), BF16)]
    out_specs = [pl.BlockSpec((ncb, tm, LANES), lambda i, j: (j, i, 0))]
    args = [x2, g, w]
    if with_f:
        in_specs.append(pl.BlockSpec((LANES, d), lambda i, j: (0, 0)))
        out_shape.append(jax.ShapeDtypeStruct((m, LANES), F32))
        out_specs.append(pl.BlockSpec((tm, LANES), lambda i, j: (i, 0)))
        args.append(wf_t)
    res = pl.pallas_call(
        functools.partial(_norm_proj_kernel, n_col_blocks=ncb, w_transposed=w_transposed,
                          with_gate_logits=with_f, row_chunk=256),
        grid=(m // tm, n // tn),
        in_specs=in_specs, out_specs=out_specs, out_shape=out_shape,
        scratch_shapes=[pltpu.VMEM((tm, d), BF16)],
        compiler_params=_cparams(2),
        name="norm_proj_gate" if with_f else "norm_proj",
    )(*args)
    return res if with_f else (res[0], None)


def _w_repack_kernel(src_ref, mem_ref, gm_ref, wm_ref, main_ref, gate_ref, mkv_ref,
                     buf_ref, gbuf_ref, hm_ref, sem_ref, gsem_ref, *,
                     rows, base_row, n_aligned_blocks, n_gate, n_mem_tiles, row_chunk):
    t = pl.program_id(0)
    n_steps = pl.num_programs(0)
    n_slices = src_ref.shape[1]

    def block_copies(step, slot):
        row0 = base_row + step * rows + jnp.where(step >= n_aligned_blocks, n_gate, 0)
        return [pltpu.make_async_copy(src_ref.at[pl.ds(row0, rows), s, :],
                                      buf_ref.at[slot, :, pl.ds(s * LANES, LANES)],
                                      sem_ref.at[slot]) for s in range(n_slices)]

    def gate_copies():
        gate_row0 = base_row + n_aligned_blocks * rows
        return [pltpu.make_async_copy(src_ref.at[pl.ds(gate_row0, SUBLANES), s, :],
                                      gbuf_ref.at[:, pl.ds(s * LANES, LANES)],
                                      gsem_ref.at[0]) for s in range(n_slices)]

    @pl.when(t == 0)
    def _first():
        for cp in gate_copies() + block_copies(0, 0):
            cp.start()

    @pl.when(t + 1 < n_steps)
    def _prefetch():
        for cp in block_copies(t + 1, (t + 1) % 2):
            cp.start()

    @pl.when(t == 0)
    def _gate_rows_and_mem_norm():
        for cp in gate_copies():
            cp.wait()
        row = lax.broadcasted_iota(jnp.int32, gbuf_ref.shape, 0)
        gate = jnp.where(row < n_gate, gbuf_ref[...], 0.0)
        gate_ref[...] = jnp.concatenate(
            [gate, jnp.zeros((gate_ref.shape[0] - SUBLANES, gate.shape[1]), F32)], axis=0).astype(BF16)
        for c in range(mem_ref.shape[0] // row_chunk):
            r = slice(c * row_chunk, (c + 1) * row_chunk)
            xf = mem_ref[r, :]
            ms = jnp.mean(xf * xf, axis=-1, keepdims=True)
            hm_ref[r, :] = (xf * lax.rsqrt(ms + EPS) * gm_ref[...]).astype(BF16)

    @pl.when(t < n_mem_tiles)
    def _mem_projection():
        mkv_ref[...] = jnp.dot(hm_ref[...], wm_ref[...].astype(BF16),
                               preferred_element_type=F32).astype(BF16)

    for cp in block_copies(t, t % 2):
        cp.wait()
    main_ref[...] = buf_ref[t % 2].astype(BF16)


def _w_repack_mem_proj(w_t3, layer, n_total, n_aligned, n_gate, mem2, g_mem, w_mem3, *, rows):
    n_slices = w_t3.shape[1]
    d = n_slices * LANES
    n_main = n_total - n_gate
    m_mem = mem2.shape[0]
    n_mem_tiles = w_mem3.shape[2] // LANES
    n_steps = n_main // rows
    assert n_aligned % rows == 0 and n_main % rows == 0 and n_gate <= SUBLANES
    assert w_mem3.shape[2] % LANES == 0 and n_mem_tiles <= n_steps
    mem_tile = lambda t: jnp.minimum(t, n_mem_tiles - 1)
    return pl.pallas_call(
        functools.partial(_w_repack_kernel, rows=rows, base_row=layer * n_total,
                          n_aligned_blocks=n_aligned // rows, n_gate=n_gate,
                          n_mem_tiles=n_mem_tiles, row_chunk=256),
        grid=(n_steps,),
        in_specs=[pl.BlockSpec(memory_space=pl.ANY),
                  pl.BlockSpec((m_mem, d), lambda t: (0, 0), pipeline_mode=pl.Buffered(1)),
                  pl.BlockSpec((1, d), lambda t: (0, 0)),
                  pl.BlockSpec((None, d, LANES), lambda t: (layer, 0, mem_tile(t)))],
        out_specs=[pl.BlockSpec((rows, d), lambda t: (t, 0)),
                   pl.BlockSpec((LANES, d), lambda t: (0, 0)),
                   pl.BlockSpec((None, m_mem, LANES), lambda t: (mem_tile(t), 0, 0))],
        out_shape=[jax.ShapeDtypeStruct((n_main, d), BF16),
                   jax.ShapeDtypeStruct((LANES, d), BF16),
                   jax.ShapeDtypeStruct((n_mem_tiles, m_mem, LANES), BF16)],
        scratch_shapes=[pltpu.VMEM((2, rows, d), F32), pltpu.VMEM((SUBLANES, d), F32),
                        pltpu.VMEM((m_mem, d), BF16),
                        pltpu.SemaphoreType.DMA((2,)), pltpu.SemaphoreType.DMA((1,))],
        compiler_params=_cparams(1),
        name="w_repack_mem_proj",
    )(w_t3, mem2, g_mem, w_mem3)


N_SPLIT = 3


def _split3(v):
    hi = v.astype(BF16).astype(F32)
    mid = (v - hi).astype(BF16).astype(F32)
    lo = (v - hi - mid).astype(BF16).astype(F32)
    return hi, mid, lo


def _forget_bias_columns(f_ref, b_ref, kc_ref, n_heads):
    z = f_ref[...] + b_ref[...]
    logf = jnp.minimum(z, 0.0) - jnp.log1p(jnp.exp(-jnp.abs(z)))
    x = logf.T[0:SUBLANES, :]
    s_len = x.shape[1]
    lane = lax.broadcasted_iota(jnp.int32, x.shape, 1)
    shift = 1
    while shift < s_len:
        x = x + jnp.where(lane >= shift, pltpu.roll(x, shift, 1), 0.0)
        shift *= 2
    row = lax.broadcasted_iota(jnp.int32, x.shape, 0)
    x = jnp.where(row < n_heads, x, 0.0)
    terms = _split3(x * (-1.0 / HEAD_DIM ** -0.5))
    pad = jnp.zeros((LANES - SUBLANES * N_SPLIT, s_len), F32)
    kc_ref[...] = jnp.concatenate(list(terms) + [pad], axis=0).T.astype(BF16)


LOG2E = math.log2(math.e)


def _softmax_pv_folded(a, v_aug, g):
    m = jnp.max(a, axis=1, keepdims=True)
    p = jnp.exp2((a - m) * (HEAD_DIM ** -0.5 * LOG2E)).astype(BF16)
    acc = jnp.dot(p, v_aug, preferred_element_type=F32)
    gf = g.astype(F32)
    silu = gf * (1.0 / (1.0 + jnp.exp(-gf)))
    return (acc[:, :LANES] / acc[:, LANES:] * silu).astype(BF16)


SCORE_AHEAD = 1


def _stream_events(setup, n_tiles, score_matmul, finish):
    scores = {}

    def score(i):
        return lambda: scores.__setitem__(i, score_matmul(i))

    def fin(i):
        return lambda: finish(i, scores.pop(i))

    events = [setup] + [score(i) for i in range(min(SCORE_AHEAD, n_tiles))]
    for i in range(n_tiles):
        if i + SCORE_AHEAD < n_tiles:
            events.append(score(i + SCORE_AHEAD))
        events.append(fin(i))
    return events


def _run_interleaved(streams, lead=0):
    queues = [list(ev) for ev in streams]
    for _ in range(min(lead, len(queues[0]))):
        queues[0].pop(0)()
    while any(queues):
        for q in queues:
            if q:
                q.pop(0)()


def _t5_bias(rb_ref, h, dist):
    max_exact = T5_BUCKETS // 2
    n = jnp.maximum(dist, 0)
    nf = jnp.maximum(n, 1).astype(F32)
    val = jnp.log(nf / max_exact) / math.log(T5_MAX_DIST / max_exact) * (T5_BUCKETS - max_exact)
    bias = jnp.zeros(dist.shape, F32)
    for kk in range(T5_BUCKETS):
        if kk < max_exact:
            hit = n == kk
        elif kk < T5_BUCKETS - 1:
            hit = (n >= max_exact) & (val >= kk - max_exact) & (val < kk - max_exact + 1)
        else:
            hit = (n >= max_exact) & (val >= kk - max_exact)
        bias = jnp.where(hit, rb_ref[h, kk], bias)
    return bias


def _moba_head_constants(rb_ref, h, kaug_ref, bown_ref, bprev_ref):
    inv_scale = 1.0 / HEAD_DIM ** -0.5
    blk = MOBA_BLOCK
    seq = kaug_ref.shape[0]
    far_bias = rb_ref[h, T5_BUCKETS - 1]
    srow = lax.broadcasted_iota(jnp.int32, (seq, LANES), 0)
    col = lax.broadcasted_iota(jnp.int32, (seq, LANES), 1)
    kaug_ref[:, LANES:] = jnp.where((col & (SUBLANES - 1)) * blk == srow - (srow & (blk - 1)),
                                    1.0, 0.0).astype(BF16)
    t = T5_MAX_DIST
    assert blk == 2 * t
    dist = lax.broadcasted_iota(jnp.int32, (t, t), 0) - lax.broadcasted_iota(jnp.int32, (t, t), 1)
    near = jnp.where(dist >= 0, (_t5_bias(rb_ref, h, dist) - far_bias) * inv_scale, NEG)
    mid = (_t5_bias(rb_ref, h, dist + t) - far_bias) * inv_scale
    bown_ref[0:t, 0:t] = near
    bown_ref[0:t, t:blk] = jnp.full((t, t), NEG, F32)
    bown_ref[t:blk, 0:t] = mid
    bown_ref[t:blk, t:blk] = near
    bprev_ref[...] = jnp.zeros((blk, blk), F32)
    bprev_ref[0:t, t:blk] = mid


def _moba_stream(q_ref, k_ref, v_ref, g_ref, o_ref,
                 kaug_ref, qaug_ref, vaug_ref, kmean_ref, bown_ref, bprev_ref):
    blk = MOBA_BLOCK
    seq = q_ref.shape[0]
    n_blocks = seq // blk
    n_chunks = seq // LANES

    def setup():
        _moba_setup(q_ref, k_ref, v_ref, kaug_ref, qaug_ref, vaug_ref, kmean_ref, n_blocks, n_chunks)

    def score_matmul(i):
        return lax.dot_general(qaug_ref[i * blk:(i + 1) * blk, :], kaug_ref[0:(i + 1) * blk, :], _NT,
                               preferred_element_type=F32)

    def finish(i, a):
        rows = slice(i * blk, (i + 1) * blk)
        n = (i + 1) * blk
        parts = [a[:, n - blk:] + bown_ref[...]]
        if i >= 1:
            parts.insert(0, a[:, n - 2 * blk:n - blk] + bprev_ref[...])
        if i >= 2:
            parts.insert(0, a[:, :n - 2 * blk])
        a = parts[0] if len(parts) == 1 else jnp.concatenate(parts, axis=1)
        o_ref[rows, :] = _softmax_pv_folded(a, vaug_ref[0:n, :], g_ref[rows, :])

    return setup, n_blocks, score_matmul, finish


def _moba_setup(q_ref, k_ref, v_ref, kaug_ref, qaug_ref, vaug_ref, kmean_ref, n_blocks, n_chunks):
    blk = MOBA_BLOCK
    seq = q_ref.shape[0]
    kaug_ref[:, :LANES] = k_ref[...]
    vaug_ref[:, :LANES] = v_ref[...]
    vaug_ref[:, LANES:] = jnp.ones((seq, LANES), BF16)
    blk_row = lax.broadcasted_iota(jnp.int32, (SUBLANES, LANES), 0)
    km = jnp.zeros((SUBLANES, LANES), F32)
    for j in range(n_blocks):
        mean_j = jnp.mean(k_ref[j * blk:(j + 1) * blk, :].astype(F32), axis=0, keepdims=True)
        km = jnp.where(blk_row == j, mean_j, km)
    kmean_ref[...] = jnp.concatenate([km, jnp.zeros((LANES - SUBLANES, LANES), F32)], axis=0).astype(BF16)

    gate_t = lax.dot_general(kmean_ref[...], q_ref[...], _NT, preferred_element_type=F32)[0:SUBLANES, :]
    jrow = lax.broadcasted_iota(jnp.int32, gate_t.shape, 0)
    qpos = lax.broadcasted_iota(jnp.int32, gate_t.shape, 1)
    qblk = jnp.right_shift(qpos, int(math.log2(blk)))
    gm = jnp.where(jrow < qblk, gate_t, NEG)
    sel_bias = jnp.zeros(gate_t.shape, F32)
    for j in range(n_blocks):
        gj = gm[j:j + 1, :]
        beats = (gm > gj) | ((gm == gj) & (jrow < j))
        rank = jnp.sum(beats.astype(F32), axis=0, keepdims=True)
        allowed = ((rank < MOBA_TOPK) & (qblk[0:1, :] > j)) | (qblk[0:1, :] == j)
        sel_bias = jnp.where(jrow == j, jnp.where(allowed, 0.0, NEG), sel_bias)
    assert SUBLANES * n_chunks == LANES
    packed_t = jnp.concatenate([sel_bias[:, c * LANES:(c + 1) * LANES] for c in range(n_chunks)], axis=0).T
    lane_chunk = jnp.right_shift(lax.broadcasted_iota(jnp.int32, (LANES, LANES), 1),
                                 int(math.log2(SUBLANES)))
    qaug_ref[:, :LANES] = q_ref[...]
    for c in range(n_chunks):
        qaug_ref[c * LANES:(c + 1) * LANES, LANES:] = jnp.where(lane_chunk == c, packed_t, 0.0).astype(BF16)


def _moba_scratch(seq):
    blk = MOBA_BLOCK
    return [pltpu.VMEM((seq, 2 * LANES), BF16),
            pltpu.VMEM((seq, 2 * LANES), BF16),
            pltpu.VMEM((seq, 2 * LANES), BF16),
            pltpu.VMEM((LANES, LANES), BF16),
            pltpu.VMEM((blk, blk), F32),
            pltpu.VMEM((blk, blk), F32)]


def _fox_stream(h, q_ref, k_ref, v_ref, g_ref, kc_ref, o_ref, qaug_ref, kaug_ref, vaug_ref, causal_ref):
    seq = q_ref.shape[0]

    def setup():
        col = lax.broadcasted_iota(jnp.int32, (seq, LANES), 1)
        qaug_ref[:, :LANES] = q_ref[...]
        qaug_ref[:, LANES:] = jnp.where(((col & (SUBLANES - 1)) == h) & (col < SUBLANES * N_SPLIT),
                                        1.0, 0.0).astype(BF16)
        kaug_ref[:, :LANES] = k_ref[...]
        kaug_ref[:, LANES:] = kc_ref[...]
        vaug_ref[:, :LANES] = v_ref[...]
        vaug_ref[:, LANES:] = jnp.ones((seq, LANES), BF16)
        r = lax.broadcasted_iota(jnp.int32, (Q_TILE, Q_TILE), 0)
        c = lax.broadcasted_iota(jnp.int32, (Q_TILE, Q_TILE), 1)
        causal_ref[...] = jnp.where(c <= r, 0.0, NEG)

    def score_matmul(i):
        return lax.dot_general(qaug_ref[i * Q_TILE:(i + 1) * Q_TILE, :], kaug_ref[0:(i + 1) * Q_TILE, :],
                               _NT, preferred_element_type=F32)

    def finish(i, a):
        rows = slice(i * Q_TILE, (i + 1) * Q_TILE)
        n = (i + 1) * Q_TILE
        own = a[:, n - Q_TILE:] + causal_ref[...]
        a = own if i == 0 else jnp.concatenate([a[:, :n - Q_TILE], own], axis=1)
        o_ref[rows, :] = _softmax_pv_folded(a, vaug_ref[0:n, :], g_ref[rows, :])

    return setup, seq // Q_TILE, score_matmul, finish


def _fox_scratch(seq):
    return [pltpu.VMEM((seq, 2 * LANES), BF16), pltpu.VMEM((seq, 2 * LANES), BF16),
            pltpu.VMEM((seq, 2 * LANES), BF16), pltpu.VMEM((Q_TILE, Q_TILE), F32)]


FOX_LEAD_EVENTS = 8


def _causal_attention_kernel(*refs):
    rb_ref, qa, ka, va, ga, qb, kb, vb, gb, kc, wo_ref, oa, ob, wob_ref = refs[:14]
    n_moba = len(_moba_scratch(qa.shape[0]))
    moba_scratch, fox_scratch = refs[14:14 + n_moba], refs[14 + n_moba:]
    kaug_ref, _, _, _, bown_ref, bprev_ref = moba_scratch
    h = pl.program_id(0)

    @pl.when(pl.program_id(1) == 0)
    def _():
        _moba_head_constants(rb_ref, h, kaug_ref, bown_ref, bprev_ref)

    wob_ref[...] = wo_ref[...].astype(BF16)
    _run_interleaved([_stream_events(*_fox_stream(h, qb, kb, vb, gb, kc, ob, *fox_scratch)),
                      _stream_events(*_moba_stream(qa, ka, va, ga, oa, *moba_scratch))],
                     lead=FOX_LEAD_EVENTS)


def _causal_attention(p, rel_bias, kc, w_out3, layer, batch, seq, moba_offsets, fox_offsets):
    nb = seq // MOBA_BLOCK
    assert seq % MOBA_BLOCK == 0 and MOBA_TOPK < SUBLANES and nb <= SUBLANES and MOBA_BLOCK > T5_MAX_DIST
    assert N_HEADS_MOBA == N_HEADS_FOX <= SUBLANES
    dw, d = w_out3.shape[1:]
    n_slabs = dw // W_OUT_CAST_ROWS
    assert dw % W_OUT_CAST_ROWS == 0 and n_slabs <= N_HEADS_MOBA * batch
    slab = lambda h, b: jnp.minimum(h * batch + b, n_slabs - 1)
    full = lambda off: pl.BlockSpec((None, seq, LANES), lambda h, b: (off + h, b, 0))
    out = pl.BlockSpec((None, seq, LANES), lambda h, b: (h, b, 0))
    out_sds = jax.ShapeDtypeStruct((N_HEADS_MOBA, batch * seq, LANES), BF16)
    return pl.pallas_call(
        _causal_attention_kernel,
        grid=(N_HEADS_MOBA, batch),
        in_specs=([pl.BlockSpec(memory_space=pltpu.SMEM)]
                  + [full(off) for off in moba_offsets] + [full(off) for off in fox_offsets]
                  + [pl.BlockSpec((None, seq, LANES), lambda h, b: (b, 0, 0)),
                     pl.BlockSpec((None, W_OUT_CAST_ROWS, d), lambda h, b: (layer, slab(h, b), 0))]),
        out_specs=[out, out, pl.BlockSpec((W_OUT_CAST_ROWS, d), lambda h, b: (slab(h, b), 0))],
        out_shape=[out_sds, out_sds, jax.ShapeDtypeStruct((dw, d), BF16)],
        scratch_shapes=_moba_scratch(seq) + _fox_scratch(seq),
        compiler_params=_cparams(2),
        name="causal_attention",
    )(rel_bias, *([p] * 8), kc, w_out3)


def _mem_kernel(q_ref, k_ref, v_ref, g_ref, f_ref, b_ref, o_ref, kc_ref, vaug_ref, *, n_fox_heads):
    n_heads, seq, _ = q_ref.shape
    tiles_per_head = seq // MEM_Q_TILE

    _forget_bias_columns(f_ref, b_ref, kc_ref, n_fox_heads)
    vaug_ref[:, :, LANES:] = jnp.ones(v_ref.shape, BF16)
    vaug_ref[:, :, :LANES] = v_ref[...]

    def tile(t):
        h, i = divmod(t, tiles_per_head)
        return h, slice(i * MEM_Q_TILE, (i + 1) * MEM_Q_TILE)

    def score_matmul(t):
        h, rows = tile(t)
        return lax.dot_general(q_ref[h, rows, :], k_ref[h], _NT, preferred_element_type=F32)

    def finish(t, a):
        h, rows = tile(t)
        o_ref[h, rows, :] = _softmax_pv_folded(a, vaug_ref[h], g_ref[h, rows, :])

    _run_interleaved([_stream_events(lambda: None, n_heads * tiles_per_head, score_matmul, finish)])


def _mem_attention(p, mkv, fb, bvec, batch, seq, n_mem, q0, g0):
    nh = N_HEADS_MEM
    assert q0 % nh == 0 and g0 % nh == 0 and N_HEADS_FOX <= SUBLANES
    heads = lambda off: pl.BlockSpec((nh, seq, LANES), lambda b: (off // nh, b, 0))
    kv = lambda blk: pl.BlockSpec((nh, n_mem, LANES), lambda b: (blk, b, 0))
    return pl.pallas_call(
        functools.partial(_mem_kernel, n_fox_heads=N_HEADS_FOX),
        grid=(batch,),
        in_specs=[heads(q0), kv(0), kv(1), heads(g0),
                  pl.BlockSpec((seq, LANES), lambda b: (b, 0)),
                  pl.BlockSpec((1, LANES), lambda b: (0, 0))],
        out_specs=[pl.BlockSpec((nh, seq, LANES), lambda b: (0, b, 0)),
                   pl.BlockSpec((None, seq, LANES), lambda b: (b, 0, 0))],
        out_shape=[jax.ShapeDtypeStruct((nh, batch * seq, LANES), BF16),
                   jax.ShapeDtypeStruct((batch, seq, LANES), BF16)],
        scratch_shapes=[pltpu.VMEM((nh, n_mem, 2 * LANES), BF16)],
        compiler_params=_cparams(1),
        name="mem_attention",
    )(p, mkv, mkv, p, fb, bvec)


def _out_proj_kernel(ya_ref, yb_ref, ym_ref, w_ref, x_ref, g_ref, o_ref):
    y = jnp.concatenate([src[c] for src in (ya_ref, yb_ref, ym_ref) for c in range(src.shape[0])], axis=1)
    o = jnp.dot(y, w_ref[...], preferred_element_type=F32)
    ms = jnp.mean(o * o, axis=-1, keepdims=True)
    o_ref[...] = x_ref[...] + o * lax.rsqrt(ms + EPS) * g_ref[...]


def _out_proj(ya, yb, ym, w, x2, g, *, tm):
    m, d = x2.shape
    heads = lambda y: pl.BlockSpec((y.shape[0], tm, LANES), lambda i: (0, i, 0))
    return pl.pallas_call(
        _out_proj_kernel,
        grid=(m // tm,),
        in_specs=[heads(ya), heads(yb), heads(ym),
                  pl.BlockSpec(w.shape, lambda i: (0, 0), pipeline_mode=pl.Buffered(1)),
                  pl.BlockSpec((tm, d), lambda i: (i, 0)),
                  pl.BlockSpec((1, d), lambda i: (0, 0))],
        out_specs=pl.BlockSpec((tm, d), lambda i: (i, 0)),
        out_shape=jax.ShapeDtypeStruct((m, d), F32),
        compiler_params=_cparams(1),
        name="out_proj",
    )(ya, yb, ym, w, x2, g)


def kernel(x, mem, w_in, b_forget, w_mem_kv, w_out, g_pre, g_post, g_mem, rel_bias):
    batch, seq, d = x.shape
    n_mem = mem.shape[1]
    depth = w_in.shape[0]
    wa, wb, wm = N_HEADS_MOBA * HEAD_DIM, N_HEADS_FOX * HEAD_DIM, N_HEADS_MEM * HEAD_DIM
    f0 = 4 * wa + 4 * wb
    f1 = f0 + N_HEADS_FOX
    ha, hb, hm = N_HEADS_MOBA, N_HEADS_FOX, N_HEADS_MEM
    qa0, ka0, va0, ga0 = 0, ha, 2 * ha, 3 * ha
    qb0, kb0, vb0, gb0 = 4 * ha, 4 * ha + hb, 4 * ha + 2 * hb, 4 * ha + 3 * hb
    qm0, gm0 = 4 * ha + 4 * hb, 4 * ha + 4 * hb + hm

    x2 = x.reshape(batch * seq, d)
    mem2 = mem.reshape(batch * n_mem, d)
    for layer in range(depth):
        b_vec = jnp.pad(b_forget[layer], (0, LANES - N_HEADS_FOX)).reshape(1, LANES)

        n_in = w_in.shape[2]
        assert n_in == f1 + 2 * wm
        w_t3 = jnp.transpose(w_in, (0, 2, 1)).reshape(depth * n_in, d // LANES, LANES)
        w_main_t, w_gate_t, mkv = _w_repack_mem_proj(w_t3, layer, n_in, f0, N_HEADS_FOX, mem2,
                                                     g_mem[layer].reshape(1, d), w_mem_kv, rows=REPACK_ROWS)
        p, fb = _norm_proj(x2, g_pre[layer].reshape(1, d), w_main_t, w_gate_t, w_transposed=True,
                           tm=IN_PROJ_TILE[0], tn=IN_PROJ_TILE[1])
        ym, kc = _mem_attention(p, mkv, fb, b_vec, batch, seq, n_mem, qm0, gm0)
        ya, yb, w_out_bf = _causal_attention(p, rel_bias, kc, w_out, layer, batch, seq,
                                             (qa0, ka0, va0, ga0), (qb0, kb0, vb0, gb0))
        x2 = _out_proj(ya, yb, ym, w_out_bf, x2, g_post[layer].reshape(1, d), tm=OUT_PROJ_ROWS)
    return x2.reshape(batch, seq, d)
```

```python
import functools
import math

import jax
import jax.numpy as jnp
from jax import lax
from jax.experimental import pallas as pl
from jax.experimental.pallas import tpu as pltpu

HEAD_DIM = 128
N_HEADS_MOBA = 6
N_HEADS_FOX = 6
N_HEADS_MEM = 4
MOBA_BLOCK = 256
MOBA_TOPK = 3
T5_BUCKETS = 32
T5_MAX_DIST = 128
EPS = 1e-6
NEG = -1e30

LANES = 128
SUBLANES = 8
VMEM_LIMIT_BYTES = 56 * 1024 * 1024

IN_PROJ_TILE = (1024, 1792)
OUT_PROJ_ROWS = 512
REPACK_ROWS = 512
W_OUT_CAST_ROWS = 128
Q_TILE = MOBA_BLOCK
MEM_Q_TILE = 1024

F32 = jnp.float32
BF16 = jnp.bfloat16
_NT = (((1,), (1,)), ((), ()))


def _cparams(n_axes):
    return pltpu.CompilerParams(dimension_semantics=("arbitrary",) * n_axes,
                                vmem_limit_bytes=VMEM_LIMIT_BYTES)


def _norm_proj_kernel(x_ref, g_ref, w_ref, wf_ref, p_ref, f_ref, h_ref, *, n_col_blocks, row_chunk):
    tm = x_ref.shape[0]

    @pl.when(pl.program_id(1) == 0)
    def _normalize():
        for c in range(tm // row_chunk):
            rows = slice(c * row_chunk, (c + 1) * row_chunk)
            xf = x_ref[rows, :]
            ms = jnp.mean(xf * xf, axis=-1, keepdims=True)
            h = (xf * lax.rsqrt(ms + EPS) * g_ref[...]).astype(BF16)
            h_ref[rows, :] = h
            f_ref[rows, :] = lax.dot_general(h, wf_ref[...], _NT, preferred_element_type=F32)

    acc = lax.dot_general(h_ref[...], w_ref[...], _NT, preferred_element_type=F32)
    for c in range(n_col_blocks):
        p_ref[c] = acc[:, c * LANES:(c + 1) * LANES].astype(BF16)


def _norm_proj(x2, g, w_t, wf_t, *, tm, tn):
    m, d = x2.shape
    n = w_t.shape[0]
    ncb = tn // LANES
    assert n % tn == 0 and m % tm == 0
    return pl.pallas_call(
        functools.partial(_norm_proj_kernel, n_col_blocks=ncb, row_chunk=256),
        grid=(m // tm, n // tn),
        in_specs=[pl.BlockSpec((tm, d), lambda i, j: (i, 0)),
                  pl.BlockSpec((1, d), lambda i, j: (0, 0)),
                  pl.BlockSpec((tn, d), lambda i, j: (j, 0)),
                  pl.BlockSpec((LANES, d), lambda i, j: (0, 0))],
        out_specs=[pl.BlockSpec((ncb, tm, LANES), lambda i, j: (j, i, 0)),
                   pl.BlockSpec((tm, LANES), lambda i, j: (i, 0))],
        out_shape=[jax.ShapeDtypeStruct((n // LANES, m, LANES), BF16),
                   jax.ShapeDtypeStruct((m, LANES), F32)],
        scratch_shapes=[pltpu.VMEM((tm, d), BF16)],
        compiler_params=_cparams(2),
        name="norm_proj_gate",
    )(x2, g, w_t, wf_t)


def _w_repack_kernel(src_ref, mem_ref, gm_ref, wm_ref, main_ref, gate_ref, mkv_ref,
                     buf_ref, gbuf_ref, hm_ref, sem_ref, gsem_ref, *,
                     rows, base_row, n_aligned_blocks, n_gate, n_mem_tiles, row_chunk):
    t = pl.program_id(0)
    n_steps = pl.num_programs(0)
    n_slices = src_ref.shape[1]

    def block_copies(step, slot):
        row0 = base_row + step * rows + jnp.where(step >= n_aligned_blocks, n_gate, 0)
        return [pltpu.make_async_copy(src_ref.at[pl.ds(row0, rows), s, :],
                                      buf_ref.at[slot, :, pl.ds(s * LANES, LANES)],
                                      sem_ref.at[slot]) for s in range(n_slices)]

    def gate_copies():
        gate_row0 = base_row + n_aligned_blocks * rows
        return [pltpu.make_async_copy(src_ref.at[pl.ds(gate_row0, SUBLANES), s, :],
                                      gbuf_ref.at[:, pl.ds(s * LANES, LANES)],
                                      gsem_ref.at[0]) for s in range(n_slices)]

    @pl.when(t == 0)
    def _first():
        for cp in gate_copies() + block_copies(0, 0):
            cp.start()

    @pl.when(t + 1 < n_steps)
    def _prefetch():
        for cp in block_copies(t + 1, (t + 1) % 2):
            cp.start()

    @pl.when(t == 0)
    def _gate_rows_and_mem_norm():
        for cp in gate_copies():
            cp.wait()
        row = lax.broadcasted_iota(jnp.int32, gbuf_ref.shape, 0)
        gate = jnp.where(row < n_gate, gbuf_ref[...], 0.0)
        gate_ref[...] = jnp.concatenate(
            [gate, jnp.zeros((gate_ref.shape[0] - SUBLANES, gate.shape[1]), F32)], axis=0).astype(BF16)
        for c in range(mem_ref.shape[0] // row_chunk):
            r = slice(c * row_chunk, (c + 1) * row_chunk)
            xf = mem_ref[r, :]
            ms = jnp.mean(xf * xf, axis=-1, keepdims=True)
            hm_ref[r, :] = (xf * lax.rsqrt(ms + EPS) * gm_ref[...]).astype(BF16)

    @pl.when(t < n_mem_tiles)
    def _mem_projection():
        mkv_ref[...] = jnp.dot(hm_ref[...], wm_ref[...].astype(BF16),
                               preferred_element_type=F32).astype(BF16)

    for cp in block_copies(t, t % 2):
        cp.wait()
    main_ref[...] = buf_ref[t % 2].astype(BF16)


def _w_repack_mem_proj(w_t3, layer, n_total, n_aligned, n_gate, mem2, g_mem, w_mem3, *, rows):
    n_slices = w_t3.shape[1]
    d = n_slices * LANES
    n_main = n_total - n_gate
    m_mem = mem2.shape[0]
    n_mem_tiles = w_mem3.shape[2] // LANES
    n_steps = n_main // rows
    assert n_aligned % rows == 0 and n_main % rows == 0 and n_gate <= SUBLANES
    assert w_mem3.shape[2] % LANES == 0 and n_mem_tiles <= n_steps
    mem_tile = lambda t: jnp.minimum(t, n_mem_tiles - 1)
    return pl.pallas_call(
        functools.partial(_w_repack_kernel, rows=rows, base_row=layer * n_total,
                          n_aligned_blocks=n_aligned // rows, n_gate=n_gate,
                          n_mem_tiles=n_mem_tiles, row_chunk=256),
        grid=(n_steps,),
        in_specs=[pl.BlockSpec(memory_space=pl.ANY),
                  pl.BlockSpec((m_mem, d), lambda t: (0, 0), pipeline_mode=pl.Buffered(1)),
                  pl.BlockSpec((1, d), lambda t: (0, 0)),
                  pl.BlockSpec((None, d, LANES), lambda t: (layer, 0, mem_tile(t)))],
        out_specs=[pl.BlockSpec((rows, d), lambda t: (t, 0)),
                   pl.BlockSpec((LANES, d), lambda t: (0, 0)),
                   pl.BlockSpec((None, m_mem, LANES), lambda t: (mem_tile(t), 0, 0))],
        out_shape=[jax.ShapeDtypeStruct((n_main, d), BF16),
                   jax.ShapeDtypeStruct((LANES, d), BF16),
                   jax.ShapeDtypeStruct((n_mem_tiles, m_mem, LANES), BF16)],
        scratch_shapes=[pltpu.VMEM((2, rows, d), F32), pltpu.VMEM((SUBLANES, d), F32),
                        pltpu.VMEM((m_mem, d), BF16),
                        pltpu.SemaphoreType.DMA((2,)), pltpu.SemaphoreType.DMA((1,))],
        compiler_params=_cparams(1),
        name="w_repack_mem_proj",
    )(w_t3, mem2, g_mem, w_mem3)


N_SPLIT = 3


def _split3(v):
    hi = v.astype(BF16).astype(F32)
    mid = (v - hi).astype(BF16).astype(F32)
    lo = (v - hi - mid).astype(BF16).astype(F32)
    return hi, mid, lo


def _forget_bias_columns(f_ref, b_ref, kc_ref, n_heads):
    z = f_ref[...] + b_ref[...]
    logf = jnp.minimum(z, 0.0) - jnp.log1p(jnp.exp(-jnp.abs(z)))
    x = logf.T[0:SUBLANES, :]
    s_len = x.shape[1]
    lane = lax.broadcasted_iota(jnp.int32, x.shape, 1)
    shift = 1
    while shift < s_len:
        x = x + jnp.where(lane >= shift, pltpu.roll(x, shift, 1), 0.0)
        shift *= 2
    row = lax.broadcasted_iota(jnp.int32, x.shape, 0)
    x = jnp.where(row < n_heads, x, 0.0)
    terms = _split3(x * (-1.0 / HEAD_DIM ** -0.5))
    pad = jnp.zeros((LANES - SUBLANES * N_SPLIT, s_len), F32)
    kc_ref[...] = jnp.concatenate(list(terms) + [pad], axis=0).T.astype(BF16)


LOG2E = math.log2(math.e)


def _softmax_pv_folded(a, v_aug, g):
    m = jnp.max(a, axis=1, keepdims=True)
    p = jnp.exp2((a - m) * (HEAD_DIM ** -0.5 * LOG2E)).astype(BF16)
    acc = jnp.dot(p, v_aug, preferred_element_type=F32)
    gf = g.astype(F32)
    silu = gf * (1.0 / (1.0 + jnp.exp(-gf)))
    return (acc[:, :LANES] / acc[:, LANES:] * silu).astype(BF16)


SCORE_AHEAD = 1


def _stream_events(setup, n_tiles, score_matmul, finish):
    scores = {}

    def score(i):
        return lambda: scores.__setitem__(i, score_matmul(i))

    def fin(i):
        return lambda: finish(i, scores.pop(i))

    events = [setup] + [score(i) for i in range(min(SCORE_AHEAD, n_tiles))]
    for i in range(n_tiles):
        if i + SCORE_AHEAD < n_tiles:
            events.append(score(i + SCORE_AHEAD))
        events.append(fin(i))
    return events


def _run_interleaved(streams, lead=0):
    queues = [list(ev) for ev in streams]
    for _ in range(min(lead, len(queues[0]))):
        queues[0].pop(0)()
    while any(queues):
        for q in queues:
            if q:
                q.pop(0)()


def _t5_bias(rb_ref, h, dist):
    max_exact = T5_BUCKETS // 2
    n = jnp.maximum(dist, 0)
    nf = jnp.maximum(n, 1).astype(F32)
    val = jnp.log(nf / max_exact) / math.log(T5_MAX_DIST / max_exact) * (T5_BUCKETS - max_exact)
    bias = jnp.zeros(dist.shape, F32)
    for kk in range(T5_BUCKETS):
        if kk < max_exact:
            hit = n == kk
        elif kk < T5_BUCKETS - 1:
            hit = (n >= max_exact) & (val >= kk - max_exact) & (val < kk - max_exact + 1)
        else:
            hit = (n >= max_exact) & (val >= kk - max_exact)
        bias = jnp.where(hit, rb_ref[h, kk], bias)
    return bias


def _moba_head_constants(rb_ref, h, kaug_ref, bown_ref, bprev_ref):
    inv_scale = 1.0 / HEAD_DIM ** -0.5
    blk = MOBA_BLOCK
    seq = kaug_ref.shape[0]
    far_bias = rb_ref[h, T5_BUCKETS - 1]
    srow = lax.broadcasted_iota(jnp.int32, (seq, LANES), 0)
    col = lax.broadcasted_iota(jnp.int32, (seq, LANES), 1)
    kaug_ref[:, LANES:] = jnp.where((col & (SUBLANES - 1)) * blk == srow - (srow & (blk - 1)),
                                    1.0, 0.0).astype(BF16)
    t = T5_MAX_DIST
    assert blk == 2 * t
    dist = lax.broadcasted_iota(jnp.int32, (t, t), 0) - lax.broadcasted_iota(jnp.int32, (t, t), 1)
    near = jnp.where(dist >= 0, (_t5_bias(rb_ref, h, dist) - far_bias) * inv_scale, NEG)
    mid = (_t5_bias(rb_ref, h, dist + t) - far_bias) * inv_scale
    bown_ref[0:t, 0:t] = near
    bown_ref[0:t, t:blk] = jnp.full((t, t), NEG, F32)
    bown_ref[t:blk, 0:t] = mid
    bown_ref[t:blk, t:blk] = near
    bprev_ref[...] = jnp.zeros((blk, blk), F32)
    bprev_ref[0:t, t:blk] = mid


def _moba_stream(q_ref, k_ref, v_ref, g_ref, o_ref,
                 kaug_ref, qaug_ref, vaug_ref, kmean_ref, bown_ref, bprev_ref):
    blk = MOBA_BLOCK
    seq = q_ref.shape[0]
    n_blocks = seq // blk
    n_chunks = seq // LANES

    def setup():
        _moba_setup(q_ref, k_ref, v_ref, kaug_ref, qaug_ref, vaug_ref, kmean_ref, n_blocks, n_chunks)

    def score_matmul(i):
        return lax.dot_general(qaug_ref[i * blk:(i + 1) * blk, :], kaug_ref[0:(i + 1) * blk, :], _NT,
                               preferred_element_type=F32)

    def finish(i, a):
        rows = slice(i * blk, (i + 1) * blk)
        n = (i + 1) * blk
        parts = [a[:, n - blk:] + bown_ref[...]]
        if i >= 1:
            parts.insert(0, a[:, n - 2 * blk:n - blk] + bprev_ref[...])
        if i >= 2:
            parts.insert(0, a[:, :n - 2 * blk])
        a = parts[0] if len(parts) == 1 else jnp.concatenate(parts, axis=1)
        o_ref[rows, :] = _softmax_pv_folded(a, vaug_ref[0:n, :], g_ref[rows, :])

    return setup, n_blocks, score_matmul, finish


def _moba_setup(q_ref, k_ref, v_ref, kaug_ref, qaug_ref, vaug_ref, kmean_ref, n_blocks, n_chunks):
    blk = MOBA_BLOCK
    seq = q_ref.shape[0]
    kaug_ref[:, :LANES] = k_ref[...]
    vaug_ref[:, :LANES] = v_ref[...]
    vaug_ref[:, LANES:] = jnp.ones((seq, LANES), BF16)
    blk_row = lax.broadcasted_iota(jnp.int32, (SUBLANES, LANES), 0)
    km = jnp.zeros((SUBLANES, LANES), F32)
    for j in range(n_blocks):
        mean_j = jnp.mean(k_ref[j * blk:(j + 1) * blk, :].astype(F32), axis=0, keepdims=True)
        km = jnp.where(blk_row == j, mean_j, km)
    kmean_ref[...] = jnp.concatenate([km, jnp.zeros((LANES - SUBLANES, LANES), F32)], axis=0).astype(BF16)

    gate_t = lax.dot_general(kmean_ref[...], q_ref[...], _NT, preferred_element_type=F32)[0:SUBLANES, :]
    jrow = lax.broadcasted_iota(jnp.int32, gate_t.shape, 0)
    qpos = lax.broadcasted_iota(jnp.int32, gate_t.shape, 1)
    qblk = jnp.right_shift(qpos, int(math.log2(blk)))
    gm = jnp.where(jrow < qblk, gate_t, NEG)
    sel_bias = jnp.zeros(gate_t.shape, F32)
    for j in range(n_blocks):
        gj = gm[j:j + 1, :]
        beats = (gm > gj) | ((gm == gj) & (jrow < j))
        rank = jnp.sum(beats.astype(F32), axis=0, keepdims=True)
        allowed = ((rank < MOBA_TOPK) & (qblk[0:1, :] > j)) | (qblk[0:1, :] == j)
        sel_bias = jnp.where(jrow == j, jnp.where(allowed, 0.0, NEG), sel_bias)
    assert SUBLANES * n_chunks == LANES
    packed_t = jnp.concatenate([sel_bias[:, c * LANES:(c + 1) * LANES] for c in range(n_chunks)], axis=0).T
    lane_chunk = jnp.right_shift(lax.broadcasted_iota(jnp.int32, (LANES, LANES), 1),
                                 int(math.log2(SUBLANES)))
    qaug_ref[:, :LANES] = q_ref[...]
    for c in range(n_chunks):
        qaug_ref[c * LANES:(c + 1) * LANES, LANES:] = jnp.where(lane_chunk == c, packed_t, 0.0).astype(BF16)


def _moba_scratch(seq):
    blk = MOBA_BLOCK
    return [pltpu.VMEM((seq, 2 * LANES), BF16),
            pltpu.VMEM((seq, 2 * LANES), BF16),
            pltpu.VMEM((seq, 2 * LANES), BF16),
            pltpu.VMEM((LANES, LANES), BF16),
            pltpu.VMEM((blk, blk), F32),
            pltpu.VMEM((blk, blk), F32)]


def _fox_stream(h, q_ref, k_ref, v_ref, g_ref, kc_ref, o_ref, qaug_ref, kaug_ref, vaug_ref, causal_ref):
    seq = q_ref.shape[0]

    def setup():
        col = lax.broadcasted_iota(jnp.int32, (seq, LANES), 1)
        qaug_ref[:, :LANES] = q_ref[...]
        qaug_ref[:, LANES:] = jnp.where(((col & (SUBLANES - 1)) == h) & (col < SUBLANES * N_SPLIT),
                                        1.0, 0.0).astype(BF16)
        kaug_ref[:, :LANES] = k_ref[...]
        kaug_ref[:, LANES:] = kc_ref[...]
        vaug_ref[:, :LANES] = v_ref[...]
        vaug_ref[:, LANES:] = jnp.ones((seq, LANES), BF16)
        r = lax.broadcasted_iota(jnp.int32, (Q_TILE, Q_TILE), 0)
        c = lax.broadcasted_iota(jnp.int32, (Q_TILE, Q_TILE), 1)
        causal_ref[...] = jnp.where(c <= r, 0.0, NEG)

    def score_matmul(i):
        return lax.dot_general(qaug_ref[i * Q_TILE:(i + 1) * Q_TILE, :], kaug_ref[0:(i + 1) * Q_TILE, :],
                               _NT, preferred_element_type=F32)

    def finish(i, a):
        rows = slice(i * Q_TILE, (i + 1) * Q_TILE)
        n = (i + 1) * Q_TILE
        own = a[:, n - Q_TILE:] + causal_ref[...]
        a = own if i == 0 else jnp.concatenate([a[:, :n - Q_TILE], own], axis=1)
        o_ref[rows, :] = _softmax_pv_folded(a, vaug_ref[0:n, :], g_ref[rows, :])

    return setup, seq // Q_TILE, score_matmul, finish


def _fox_scratch(seq):
    return [pltpu.VMEM((seq, 2 * LANES), BF16), pltpu.VMEM((seq, 2 * LANES), BF16),
            pltpu.VMEM((seq, 2 * LANES), BF16), pltpu.VMEM((Q_TILE, Q_TILE), F32)]


FOX_LEAD_EVENTS = 8


def _causal_attention_kernel(*refs):
    rb_ref, qa, ka, va, ga, qb, kb, vb, gb, kc, wo_ref, oa, ob, wob_ref = refs[:14]
    n_moba = len(_moba_scratch(qa.shape[0]))
    moba_scratch, fox_scratch = refs[14:14 + n_moba], refs[14 + n_moba:]
    kaug_ref, _, _, _, bown_ref, bprev_ref = moba_scratch
    h = pl.program_id(0)

    @pl.when(pl.program_id(1) == 0)
    def _():
        _moba_head_constants(rb_ref, h, kaug_ref, bown_ref, bprev_ref)

    wob_ref[...] = wo_ref[...].astype(BF16)
    _run_interleaved([_stream_events(*_fox_stream(h, qb, kb, vb, gb, kc, ob, *fox_scratch)),
                      _stream_events(*_moba_stream(qa, ka, va, ga, oa, *moba_scratch))],
                     lead=FOX_LEAD_EVENTS)


def _causal_attention(p, rel_bias, kc, w_out3, layer, batch, seq, moba_offsets, fox_offsets):
    nb = seq // MOBA_BLOCK
    assert seq % MOBA_BLOCK == 0 and MOBA_TOPK < SUBLANES and nb <= SUBLANES and MOBA_BLOCK > T5_MAX_DIST
    assert N_HEADS_MOBA == N_HEADS_FOX <= SUBLANES
    dw, d = w_out3.shape[1:]
    n_slabs = dw // W_OUT_CAST_ROWS
    assert dw % W_OUT_CAST_ROWS == 0 and n_slabs <= N_HEADS_MOBA * batch
    slab = lambda h, b: jnp.minimum(h * batch + b, n_slabs - 1)
    full = lambda off: pl.BlockSpec((None, seq, LANES), lambda h, b: (off + h, b, 0))
    out = pl.BlockSpec((None, seq, LANES), lambda h, b: (h, b, 0))
    out_sds = jax.ShapeDtypeStruct((N_HEADS_MOBA, batch * seq, LANES), BF16)
    return pl.pallas_call(
        _causal_attention_kernel,
        grid=(N_HEADS_MOBA, batch),
        in_specs=([pl.BlockSpec(memory_space=pltpu.SMEM)]
                  + [full(off) for off in moba_offsets] + [full(off) for off in fox_offsets]
                  + [pl.BlockSpec((None, seq, LANES), lambda h, b: (b, 0, 0)),
                     pl.BlockSpec((None, W_OUT_CAST_ROWS, d), lambda h, b: (layer, slab(h, b), 0))]),
        out_specs=[out, out, pl.BlockSpec((W_OUT_CAST_ROWS, d), lambda h, b: (slab(h, b), 0))],
        out_shape=[out_sds, out_sds, jax.ShapeDtypeStruct((dw, d), BF16)],
        scratch_shapes=_moba_scratch(seq) + _fox_scratch(seq),
        compiler_params=_cparams(2),
        name="causal_attention",
    )(rel_bias, *([p] * 8), kc, w_out3)


def _mem_kernel(q_ref, k_ref, v_ref, g_ref, f_ref, b_ref, o_ref, kc_ref, vaug_ref, *, n_fox_heads):
    n_heads, seq, _ = q_ref.shape
    tiles_per_head = seq // MEM_Q_TILE

    _forget_bias_columns(f_ref, b_ref, kc_ref, n_fox_heads)
    vaug_ref[:, :, LANES:] = jnp.ones(v_ref.shape, BF16)
    vaug_ref[:, :, :LANES] = v_ref[...]

    def tile(t):
        h, i = divmod(t, tiles_per_head)
        return h, slice(i * MEM_Q_TILE, (i + 1) * MEM_Q_TILE)

    def score_matmul(t):
        h, rows = tile(t)
        return lax.dot_general(q_ref[h, rows, :], k_ref[h], _NT, preferred_element_type=F32)

    def finish(t, a):
        h, rows = tile(t)
        o_ref[h, rows, :] = _softmax_pv_folded(a, vaug_ref[h], g_ref[h, rows, :])

    _run_interleaved([_stream_events(lambda: None, n_heads * tiles_per_head, score_matmul, finish)])


def _mem_attention(p, mkv, fb, bvec, batch, seq, n_mem, q0, g0):
    nh = N_HEADS_MEM
    assert q0 % nh == 0 and g0 % nh == 0 and N_HEADS_FOX <= SUBLANES
    heads = lambda off: pl.BlockSpec((nh, seq, LANES), lambda b: (off // nh, b, 0))
    kv = lambda blk: pl.BlockSpec((nh, n_mem, LANES), lambda b: (blk, b, 0))
    return pl.pallas_call(
        functools.partial(_mem_kernel, n_fox_heads=N_HEADS_FOX),
        grid=(batch,),
        in_specs=[heads(q0), kv(0), kv(1), heads(g0),
                  pl.BlockSpec((seq, LANES), lambda b: (b, 0)),
                  pl.BlockSpec((1, LANES), lambda b: (0, 0))],
        out_specs=[pl.BlockSpec((nh, seq, LANES), lambda b: (0, b, 0)),
                   pl.BlockSpec((None, seq, LANES), lambda b: (b, 0, 0))],
        out_shape=[jax.ShapeDtypeStruct((nh, batch * seq, LANES), BF16),
                   jax.ShapeDtypeStruct((batch, seq, LANES), BF16)],
        scratch_shapes=[pltpu.VMEM((nh, n_mem, 2 * LANES), BF16)],
        compiler_params=_cparams(1),
        name="mem_attention",
    )(p, mkv, mkv, p, fb, bvec)


def _out_proj_kernel(ya_ref, yb_ref, ym_ref, w_ref, x_ref, g_ref, o_ref):
    y = jnp.concatenate([src[c] for src in (ya_ref, yb_ref, ym_ref) for c in range(src.shape[0])], axis=1)
    o = jnp.dot(y, w_ref[...], preferred_element_type=F32)
    ms = jnp.mean(o * o, axis=-1, keepdims=True)
    o_ref[...] = x_ref[...] + o * lax.rsqrt(ms + EPS) * g_ref[...]


def _out_proj(ya, yb, ym, w, x2, g, *, tm):
    m, d = x2.shape
    heads = lambda y: pl.BlockSpec((y.shape[0], tm, LANES), lambda i: (0, i, 0))
    return pl.pallas_call(
        _out_proj_kernel,
        grid=(m // tm,),
        in_specs=[heads(ya), heads(yb), heads(ym),
                  pl.BlockSpec(w.shape, lambda i: (0, 0), pipeline_mode=pl.Buffered(1)),
                  pl.BlockSpec((tm, d), lambda i: (i, 0)),
                  pl.BlockSpec((1, d), lambda i: (0, 0))],
        out_specs=pl.BlockSpec((tm, d), lambda i: (i, 0)),
        out_shape=jax.ShapeDtypeStruct((m, d), F32),
        compiler_params=_cparams(1),
        name="out_proj",
    )(ya, yb, ym, w, x2, g)


def kernel(x, mem, w_in, b_forget, w_mem_kv, w_out, g_pre, g_post, g_mem, rel_bias):
    batch, seq, d = x.shape
    n_mem = mem.shape[1]
    depth = w_in.shape[0]
    wa, wb, wm = N_HEADS_MOBA * HEAD_DIM, N_HEADS_FOX * HEAD_DIM, N_HEADS_MEM * HEAD_DIM
    f0 = 4 * wa + 4 * wb
    f1 = f0 + N_HEADS_FOX
    ha, hb, hm = N_HEADS_MOBA, N_HEADS_FOX, N_HEADS_MEM
    qa0, ka0, va0, ga0 = 0, ha, 2 * ha, 3 * ha
    qb0, kb0, vb0, gb0 = 4 * ha, 4 * ha + hb, 4 * ha + 2 * hb, 4 * ha + 3 * hb
    qm0, gm0 = 4 * ha + 4 * hb, 4 * ha + 4 * hb + hm

    x2 = x.reshape(batch * seq, d)
    mem2 = mem.reshape(batch * n_mem, d)
    for layer in range(depth):
        b_vec = jnp.pad(b_forget[layer], (0, LANES - N_HEADS_FOX)).reshape(1, LANES)

        n_in = w_in.shape[2]
        assert n_in == f1 + 2 * wm
        w_t3 = jnp.transpose(w_in, (0, 2, 1)).reshape(depth * n_in, d // LANES, LANES)
        w_main_t, w_gate_t, mkv = _w_repack_mem_proj(w_t3, layer, n_in, f0, N_HEADS_FOX, mem2,
                                                     g_mem[layer].reshape(1, d), w_mem_kv, rows=REPACK_ROWS)
        p, fb = _norm_proj(x2, g_pre[layer].reshape(1, d), w_main_t, w_gate_t,
                           tm=IN_PROJ_TILE[0], tn=IN_PROJ_TILE[1])
        ym, kc = _mem_attention(p, mkv, fb, b_vec, batch, seq, n_mem, qm0, gm0)
        ya, yb, w_out_bf = _causal_attention(p, rel_bias, kc, w_out, layer, batch, seq,
                                             (qa0, ka0, va0, ga0), (qb0, kb0, vb0, gb0))
        x2 = _out_proj(ya, yb, ym, w_out_bf, x2, g_post[layer].reshape(1, d), tm=OUT_PROJ_ROWS)
    return x2.reshape(batch, seq, d)
```

```python
import functools
import math

import jax
import jax.numpy as jnp
from jax import lax
from jax.experimental import pallas as pl
from jax.experimental.pallas import tpu as pltpu

HEAD_DIM = 128
N_HEADS_MOBA = 6
N_HEADS_FOX = 6
N_HEADS_MEM = 4
MOBA_BLOCK = 256
MOBA_TOPK = 3
T5_BUCKETS = 32
T5_MAX_DIST = 128
EPS = 1e-6
NEG = -1e30

LANES = 128
SUBLANES = 8
VMEM_LIMIT_BYTES = 56 * 1024 * 1024

IN_PROJ_TILE = (1024, 1792)
OUT_PROJ_ROWS = 512
HEAD_GROUP = 4
W_OUT_CAST_ROWS = 128
Q_TILE = MOBA_BLOCK
MEM_Q_TILE = 1024

F32 = jnp.float32
BF16 = jnp.bfloat16
_NT = (((1,), (1,)), ((), ()))


def _cparams(n_axes):
    return pltpu.CompilerParams(dimension_semantics=("arbitrary",) * n_axes,
                                vmem_limit_bytes=VMEM_LIMIT_BYTES)


def _norm_proj_kernel(x_ref, g_ref, w_ref, wf_ref, p_ref, f_ref, h_ref, *, n_col_blocks, row_chunk):
    tm = x_ref.shape[0]

    @pl.when(pl.program_id(1) == 0)
    def _normalize():
        for c in range(tm // row_chunk):
            rows = slice(c * row_chunk, (c + 1) * row_chunk)
            xf = x_ref[rows, :]
            ms = jnp.mean(xf * xf, axis=-1, keepdims=True)
            h = (xf * lax.rsqrt(ms + EPS) * g_ref[...]).astype(BF16)
            h_ref[rows, :] = h
            f_ref[rows, :] = lax.dot_general(h, wf_ref[...], _NT, preferred_element_type=F32)

    acc = lax.dot_general(h_ref[...], w_ref[...], _NT, preferred_element_type=F32)
    for c in range(n_col_blocks):
        p_ref[c] = acc[:, c * LANES:(c + 1) * LANES].astype(BF16)


def _norm_proj(x2, g, w_t, wf_t, *, tm, tn):
    m, d = x2.shape
    n = w_t.shape[0]
    ncb = tn // LANES
    assert n % tn == 0 and m % tm == 0
    return pl.pallas_call(
        functools.partial(_norm_proj_kernel, n_col_blocks=ncb, row_chunk=256),
        grid=(m // tm, n // tn),
        in_specs=[pl.BlockSpec((tm, d), lambda i, j: (i, 0)),
                  pl.BlockSpec((1, d), lambda i, j: (0, 0)),
                  pl.BlockSpec((tn, d), lambda i, j: (j, 0)),
                  pl.BlockSpec((LANES, d), lambda i, j: (0, 0))],
        out_specs=[pl.BlockSpec((ncb, tm, LANES), lambda i, j: (j, i, 0)),
                   pl.BlockSpec((tm, LANES), lambda i, j: (i, 0))],
        out_shape=[jax.ShapeDtypeStruct((n // LANES, m, LANES), BF16),
                   jax.ShapeDtypeStruct((m, LANES), F32)],
        scratch_shapes=[pltpu.VMEM((tm, d), BF16)],
        compiler_params=_cparams(2),
        name="norm_proj_gate",
    )(x2, g, w_t, wf_t)


def _w_repack_kernel(src_ref, mem_ref, gm_ref, wm_ref, main_ref, gate_ref, mkv_ref,
                     buf_ref, gbuf_ref, hm_ref, sem_ref, gsem_ref, *,
                     base_row, n_heads, n_aligned, n_gate, n_mem_tiles, row_chunk):
    t = pl.program_id(0)
    n_steps = pl.num_programs(0)
    n_slices = src_ref.shape[1]

    def source_row(step, g):
        head_blk = (step // n_heads) * (HEAD_GROUP * n_heads) + g * n_heads + step % n_heads
        blk = jnp.where(step < 2 * n_heads, head_blk, HEAD_GROUP * step + g)
        return base_row + blk * LANES + jnp.where(blk * LANES >= n_aligned, n_gate, 0)

    def block_copies(step, slot):
        return [pltpu.make_async_copy(src_ref.at[pl.ds(source_row(step, g), LANES), s, :],
                                      buf_ref.at[slot, pl.ds(g * LANES, LANES), pl.ds(s * LANES, LANES)],
                                      sem_ref.at[slot])
                for g in range(HEAD_GROUP) for s in range(n_slices)]

    def gate_copies():
        gate_row0 = base_row + n_aligned
        return [pltpu.make_async_copy(src_ref.at[pl.ds(gate_row0, SUBLANES), s, :],
                                      gbuf_ref.at[:, pl.ds(s * LANES, LANES)],
                                      gsem_ref.at[0]) for s in range(n_slices)]

    @pl.when(t == 0)
    def _first():
        for cp in gate_copies() + block_copies(0, 0):
            cp.start()

    @pl.when(t + 1 < n_steps)
    def _prefetch():
        for cp in block_copies(t + 1, (t + 1) % 2):
            cp.start()

    @pl.when(t == 0)
    def _gate_rows_and_mem_norm():
        for cp in gate_copies():
            cp.wait()
        row = lax.broadcasted_iota(jnp.int32, gbuf_ref.shape, 0)
        gate = jnp.where(row < n_gate, gbuf_ref[...], 0.0)
        gate_ref[...] = jnp.concatenate(
            [gate, jnp.zeros((gate_ref.shape[0] - SUBLANES, gate.shape[1]), F32)], axis=0).astype(BF16)
        for c in range(mem_ref.shape[0] // row_chunk):
            r = slice(c * row_chunk, (c + 1) * row_chunk)
            xf = mem_ref[r, :]
            ms = jnp.mean(xf * xf, axis=-1, keepdims=True)
            hm_ref[r, :] = (xf * lax.rsqrt(ms + EPS) * gm_ref[...]).astype(BF16)

    @pl.when(t < n_mem_tiles)
    def _mem_projection():
        mkv_ref[...] = jnp.dot(hm_ref[...], wm_ref[...].astype(BF16),
                               preferred_element_type=F32).astype(BF16)

    for cp in block_copies(t, t % 2):
        cp.wait()
    main_ref[...] = buf_ref[t % 2].astype(BF16)


def _w_repack_mem_proj(w_t3, layer, n_total, n_heads, n_gate, mem2, g_mem, w_mem3):
    n_slices = w_t3.shape[1]
    d = n_slices * LANES
    rows = HEAD_GROUP * LANES
    n_aligned = 2 * n_heads * rows
    n_main = n_total - n_gate
    m_mem = mem2.shape[0]
    n_mem_tiles = w_mem3.shape[2] // LANES
    n_steps = n_main // rows
    assert n_main % rows == 0 and n_aligned <= n_main and n_gate <= SUBLANES
    assert w_mem3.shape[2] % LANES == 0 and n_mem_tiles <= n_steps
    mem_tile = lambda t: jnp.minimum(t, n_mem_tiles - 1)
    return pl.pallas_call(
        functools.partial(_w_repack_kernel, base_row=layer * n_total, n_heads=n_heads,
                          n_aligned=n_aligned, n_gate=n_gate,
                          n_mem_tiles=n_mem_tiles, row_chunk=256),
        grid=(n_steps,),
        in_specs=[pl.BlockSpec(memory_space=pl.ANY),
                  pl.BlockSpec((m_mem, d), lambda t: (0, 0), pipeline_mode=pl.Buffered(1)),
                  pl.BlockSpec((1, d), lambda t: (0, 0)),
                  pl.BlockSpec((None, d, LANES), lambda t: (layer, 0, mem_tile(t)))],
        out_specs=[pl.BlockSpec((rows, d), lambda t: (t, 0)),
                   pl.BlockSpec((LANES, d), lambda t: (0, 0)),
                   pl.BlockSpec((None, m_mem, LANES), lambda t: (mem_tile(t), 0, 0))],
        out_shape=[jax.ShapeDtypeStruct((n_main, d), BF16),
                   jax.ShapeDtypeStruct((LANES, d), BF16),
                   jax.ShapeDtypeStruct((n_mem_tiles, m_mem, LANES), BF16)],
        scratch_shapes=[pltpu.VMEM((2, rows, d), F32), pltpu.VMEM((SUBLANES, d), F32),
                        pltpu.VMEM((m_mem, d), BF16),
                        pltpu.SemaphoreType.DMA((2,)), pltpu.SemaphoreType.DMA((1,))],
        compiler_params=_cparams(1),
        name="w_repack_mem_proj",
    )(w_t3, mem2, g_mem, w_mem3)


N_SPLIT = 3


def _split3(v):
    hi = v.astype(BF16).astype(F32)
    mid = (v - hi).astype(BF16).astype(F32)
    lo = (v - hi - mid).astype(BF16).astype(F32)
    return hi, mid, lo


def _forget_bias_columns(f_ref, b_ref, kc_ref, n_heads):
    z = f_ref[...] + b_ref[...]
    logf = jnp.minimum(z, 0.0) - jnp.log1p(jnp.exp(-jnp.abs(z)))
    x = logf.T[0:SUBLANES, :]
    s_len = x.shape[1]
    lane = lax.broadcasted_iota(jnp.int32, x.shape, 1)
    shift = 1
    while shift < s_len:
        x = x + jnp.where(lane >= shift, pltpu.roll(x, shift, 1), 0.0)
        shift *= 2
    row = lax.broadcasted_iota(jnp.int32, x.shape, 0)
    x = jnp.where(row < n_heads, x, 0.0)
    terms = _split3(x * (-1.0 / HEAD_DIM ** -0.5))
    pad = jnp.zeros((LANES - SUBLANES * N_SPLIT, s_len), F32)
    kc_ref[...] = jnp.concatenate(list(terms) + [pad], axis=0).T.astype(BF16)


LOG2E = math.log2(math.e)


def _softmax_pv_folded(a, v_aug, g):
    m = jnp.max(a, axis=1, keepdims=True)
    p = jnp.exp2((a - m) * (HEAD_DIM ** -0.5 * LOG2E)).astype(BF16)
    acc = jnp.dot(p, v_aug, preferred_element_type=F32)
    gf = g.astype(F32)
    silu = gf * (1.0 / (1.0 + jnp.exp(-gf)))
    return (acc[:, :LANES] / acc[:, LANES:] * silu).astype(BF16)


SCORE_AHEAD = 1


def _stream_events(setup, n_tiles, score_matmul, finish):
    scores = {}

    def score(i):
        return lambda: scores.__setitem__(i, score_matmul(i))

    def fin(i):
        return lambda: finish(i, scores.pop(i))

    events = [setup] + [score(i) for i in range(min(SCORE_AHEAD, n_tiles))]
    for i in range(n_tiles):
        if i + SCORE_AHEAD < n_tiles:
            events.append(score(i + SCORE_AHEAD))
        events.append(fin(i))
    return events


def _run_interleaved(streams, lead=0):
    queues = [list(ev) for ev in streams]
    for _ in range(min(lead, len(queues[0]))):
        queues[0].pop(0)()
    while any(queues):
        for q in queues:
            if q:
                q.pop(0)()


def _t5_bias(rb_ref, h, dist):
    max_exact = T5_BUCKETS // 2
    n = jnp.maximum(dist, 0)
    nf = jnp.maximum(n, 1).astype(F32)
    val = jnp.log(nf / max_exact) / math.log(T5_MAX_DIST / max_exact) * (T5_BUCKETS - max_exact)
    bias = jnp.zeros(dist.shape, F32)
    for kk in range(T5_BUCKETS):
        if kk < max_exact:
            hit = n == kk
        elif kk < T5_BUCKETS - 1:
            hit = (n >= max_exact) & (val >= kk - max_exact) & (val < kk - max_exact + 1)
        else:
            hit = (n >= max_exact) & (val >= kk - max_exact)
        bias = jnp.where(hit, rb_ref[h, kk], bias)
    return bias


def _moba_head_constants(rb_ref, h, kaug_ref, bown_ref, bprev_ref):
    inv_scale = 1.0 / HEAD_DIM ** -0.5
    blk = MOBA_BLOCK
    seq = kaug_ref.shape[0]
    far_bias = rb_ref[h, T5_BUCKETS - 1]
    srow = lax.broadcasted_iota(jnp.int32, (seq, LANES), 0)
    col = lax.broadcasted_iota(jnp.int32, (seq, LANES), 1)
    kaug_ref[:, LANES:] = jnp.where((col & (SUBLANES - 1)) * blk == srow - (srow & (blk - 1)),
                                    1.0, 0.0).astype(BF16)
    t = T5_MAX_DIST
    assert blk == 2 * t
    dist = lax.broadcasted_iota(jnp.int32, (t, t), 0) - lax.broadcasted_iota(jnp.int32, (t, t), 1)
    near = jnp.where(dist >= 0, (_t5_bias(rb_ref, h, dist) - far_bias) * inv_scale, NEG)
    mid = (_t5_bias(rb_ref, h, dist + t) - far_bias) * inv_scale
    bown_ref[0:t, 0:t] = near
    bown_ref[0:t, t:blk] = jnp.full((t, t), NEG, F32)
    bown_ref[t:blk, 0:t] = mid
    bown_ref[t:blk, t:blk] = near
    bprev_ref[...] = jnp.zeros((blk, blk), F32)
    bprev_ref[0:t, t:blk] = mid


def _moba_stream(q_ref, k_ref, v_ref, g_ref, o_ref,
                 kaug_ref, qaug_ref, vaug_ref, kmean_ref, bown_ref, bprev_ref):
    blk = MOBA_BLOCK
    seq = q_ref.shape[0]
    n_blocks = seq // blk
    n_chunks = seq // LANES

    def setup():
        _moba_setup(q_ref, k_ref, v_ref, kaug_ref, qaug_ref, vaug_ref, kmean_ref, n_blocks, n_chunks)

    def score_matmul(i):
        return lax.dot_general(qaug_ref[i * blk:(i + 1) * blk, :], kaug_ref[0:(i + 1) * blk, :], _NT,
                               preferred_element_type=F32)

    def finish(i, a):
        rows = slice(i * blk, (i + 1) * blk)
        n = (i + 1) * blk
        parts = [a[:, n - blk:] + bown_ref[...]]
        if i >= 1:
            parts.insert(0, a[:, n - 2 * blk:n - blk] + bprev_ref[...])
        if i >= 2:
            parts.insert(0, a[:, :n - 2 * blk])
        a = parts[0] if len(parts) == 1 else jnp.concatenate(parts, axis=1)
        o_ref[rows, :] = _softmax_pv_folded(a, vaug_ref[0:n, :], g_ref[rows, :])

    return setup, n_blocks, score_matmul, finish


def _moba_setup(q_ref, k_ref, v_ref, kaug_ref, qaug_ref, vaug_ref, kmean_ref, n_blocks, n_chunks):
    blk = MOBA_BLOCK
    seq = q_ref.shape[0]
    kaug_ref[:, :LANES] = k_ref[...]
    vaug_ref[:, :LANES] = v_ref[...]
    vaug_ref[:, LANES:] = jnp.ones((seq, LANES), BF16)
    blk_row = lax.broadcasted_iota(jnp.int32, (SUBLANES, LANES), 0)
    km = jnp.zeros((SUBLANES, LANES), F32)
    for j in range(n_blocks):
        mean_j = jnp.mean(k_ref[j * blk:(j + 1) * blk, :].astype(F32), axis=0, keepdims=True)
        km = jnp.where(blk_row == j, mean_j, km)
    kmean_ref[...] = jnp.concatenate([km, jnp.zeros((LANES - SUBLANES, LANES), F32)], axis=0).astype(BF16)

    gate_t = lax.dot_general(kmean_ref[...], q_ref[...], _NT, preferred_element_type=F32)[0:SUBLANES, :]
    jrow = lax.broadcasted_iota(jnp.int32, gate_t.shape, 0)
    qpos = lax.broadcasted_iota(jnp.int32, gate_t.shape, 1)
    qblk = jnp.right_shift(qpos, int(math.log2(blk)))
    gm = jnp.where(jrow < qblk, gate_t, NEG)
    sel_bias = jnp.zeros(gate_t.shape, F32)
    for j in range(n_blocks):
        gj = gm[j:j + 1, :]
        beats = (gm > gj) | ((gm == gj) & (jrow < j))
        rank = jnp.sum(beats.astype(F32), axis=0, keepdims=True)
        allowed = ((rank < MOBA_TOPK) & (qblk[0:1, :] > j)) | (qblk[0:1, :] == j)
        sel_bias = jnp.where(jrow == j, jnp.where(allowed, 0.0, NEG), sel_bias)
    assert SUBLANES * n_chunks == LANES
    packed_t = jnp.concatenate([sel_bias[:, c * LANES:(c + 1) * LANES] for c in range(n_chunks)], axis=0).T
    lane_chunk = jnp.right_shift(lax.broadcasted_iota(jnp.int32, (LANES, LANES), 1),
                                 int(math.log2(SUBLANES)))
    qaug_ref[:, :LANES] = q_ref[...]
    for c in range(n_chunks):
        qaug_ref[c * LANES:(c + 1) * LANES, LANES:] = jnp.where(lane_chunk == c, packed_t, 0.0).astype(BF16)


def _moba_scratch(seq):
    blk = MOBA_BLOCK
    return [pltpu.VMEM((seq, 2 * LANES), BF16),
            pltpu.VMEM((seq, 2 * LANES), BF16),
            pltpu.VMEM((seq, 2 * LANES), BF16),
            pltpu.VMEM((LANES, LANES), BF16),
            pltpu.VMEM((blk, blk), F32),
            pltpu.VMEM((blk, blk), F32)]


def _fox_stream(h, q_ref, k_ref, v_ref, g_ref, kc_ref, o_ref, qaug_ref, kaug_ref, vaug_ref, causal_ref):
    seq = q_ref.shape[0]

    def setup():
        col = lax.broadcasted_iota(jnp.int32, (seq, LANES), 1)
        qaug_ref[:, :LANES] = q_ref[...]
        qaug_ref[:, LANES:] = jnp.where(((col & (SUBLANES - 1)) == h) & (col < SUBLANES * N_SPLIT),
                                        1.0, 0.0).astype(BF16)
        kaug_ref[:, :LANES] = k_ref[...]
        kaug_ref[:, LANES:] = kc_ref[...]
        vaug_ref[:, :LANES] = v_ref[...]
        vaug_ref[:, LANES:] = jnp.ones((seq, LANES), BF16)
        r = lax.broadcasted_iota(jnp.int32, (Q_TILE, Q_TILE), 0)
        c = lax.broadcasted_iota(jnp.int32, (Q_TILE, Q_TILE), 1)
        causal_ref[...] = jnp.where(c <= r, 0.0, NEG)

    def score_matmul(i):
        return lax.dot_general(qaug_ref[i * Q_TILE:(i + 1) * Q_TILE, :], kaug_ref[0:(i + 1) * Q_TILE, :],
                               _NT, preferred_element_type=F32)

    def finish(i, a):
        rows = slice(i * Q_TILE, (i + 1) * Q_TILE)
        n = (i + 1) * Q_TILE
        own = a[:, n - Q_TILE:] + causal_ref[...]
        a = own if i == 0 else jnp.concatenate([a[:, :n - Q_TILE], own], axis=1)
        o_ref[rows, :] = _softmax_pv_folded(a, vaug_ref[0:n, :], g_ref[rows, :])

    return setup, seq // Q_TILE, score_matmul, finish


def _fox_scratch(seq):
    return [pltpu.VMEM((seq, 2 * LANES), BF16), pltpu.VMEM((seq, 2 * LANES), BF16),
            pltpu.VMEM((seq, 2 * LANES), BF16), pltpu.VMEM((Q_TILE, Q_TILE), F32)]


FOX_LEAD_EVENTS = 8


def _causal_attention_kernel(*refs):
    rb_ref, moba_qkvg, fox_qkvg, kc, wo_ref, oa, ob, wob_ref = refs[:8]
    qa, ka, va, ga = (moba_qkvg.at[i] for i in range(HEAD_GROUP))
    qb, kb, vb, gb = (fox_qkvg.at[i] for i in range(HEAD_GROUP))
    n_moba = len(_moba_scratch(qa.shape[0]))
    moba_scratch, fox_scratch = refs[8:8 + n_moba], refs[8 + n_moba:]
    kaug_ref, _, _, _, bown_ref, bprev_ref = moba_scratch
    h = pl.program_id(0)

    @pl.when(pl.program_id(1) == 0)
    def _():
        _moba_head_constants(rb_ref, h, kaug_ref, bown_ref, bprev_ref)

    wob_ref[...] = wo_ref[...].astype(BF16)
    _run_interleaved([_stream_events(*_fox_stream(h, qb, kb, vb, gb, kc, ob, *fox_scratch)),
                      _stream_events(*_moba_stream(qa, ka, va, ga, oa, *moba_scratch))],
                     lead=FOX_LEAD_EVENTS)


def _causal_attention(p, rel_bias, kc, w_out3, layer, batch, seq):
    nb = seq // MOBA_BLOCK
    assert seq % MOBA_BLOCK == 0 and MOBA_TOPK < SUBLANES and nb <= SUBLANES and MOBA_BLOCK > T5_MAX_DIST
    assert N_HEADS_MOBA == N_HEADS_FOX <= SUBLANES
    dw, d = w_out3.shape[1:]
    n_slabs = dw // W_OUT_CAST_ROWS
    assert dw % W_OUT_CAST_ROWS == 0 and n_slabs <= N_HEADS_MOBA * batch
    slab = lambda h, b: jnp.minimum(h * batch + b, n_slabs - 1)
    head = lambda first: pl.BlockSpec((HEAD_GROUP, seq, LANES), lambda h, b: (first + h, b, 0))
    out = pl.BlockSpec((None, seq, LANES), lambda h, b: (h, b, 0))
    out_sds = jax.ShapeDtypeStruct((N_HEADS_MOBA, batch * seq, LANES), BF16)
    return pl.pallas_call(
        _causal_attention_kernel,
        grid=(N_HEADS_MOBA, batch),
        in_specs=[pl.BlockSpec(memory_space=pltpu.SMEM), head(0), head(N_HEADS_MOBA),
                  pl.BlockSpec((None, seq, LANES), lambda h, b: (b, 0, 0)),
                  pl.BlockSpec((None, W_OUT_CAST_ROWS, d), lambda h, b: (layer, slab(h, b), 0))],
        out_specs=[out, out, pl.BlockSpec((W_OUT_CAST_ROWS, d), lambda h, b: (slab(h, b), 0))],
        out_shape=[out_sds, out_sds, jax.ShapeDtypeStruct((dw, d), BF16)],
        scratch_shapes=_moba_scratch(seq) + _fox_scratch(seq),
        compiler_params=_cparams(2),
        name="causal_attention",
    )(rel_bias, p, p, kc, w_out3)


def _mem_kernel(q_ref, k_ref, v_ref, g_ref, f_ref, b_ref, o_ref, kc_ref, vaug_ref, *, n_fox_heads):
    n_heads, seq, _ = q_ref.shape
    tiles_per_head = seq // MEM_Q_TILE

    _forget_bias_columns(f_ref, b_ref, kc_ref, n_fox_heads)
    vaug_ref[:, :, LANES:] = jnp.ones(v_ref.shape, BF16)
    vaug_ref[:, :, :LANES] = v_ref[...]

    def tile(t):
        h, i = divmod(t, tiles_per_head)
        return h, slice(i * MEM_Q_TILE, (i + 1) * MEM_Q_TILE)

    def score_matmul(t):
        h, rows = tile(t)
        return lax.dot_general(q_ref[h, rows, :], k_ref[h], _NT, preferred_element_type=F32)

    def finish(t, a):
        h, rows = tile(t)
        o_ref[h, rows, :] = _softmax_pv_folded(a, vaug_ref[h], g_ref[h, rows, :])

    _run_interleaved([_stream_events(lambda: None, n_heads * tiles_per_head, score_matmul, finish)])


def _mem_attention(p, mkv, fb, bvec, batch, seq, n_mem, q0, g0):
    nh = N_HEADS_MEM
    assert q0 % nh == 0 and g0 % nh == 0 and N_HEADS_FOX <= SUBLANES
    heads = lambda off: pl.BlockSpec((nh, seq, LANES), lambda b: (off // nh, b, 0))
    kv = lambda blk: pl.BlockSpec((nh, n_mem, LANES), lambda b: (blk, b, 0))
    return pl.pallas_call(
        functools.partial(_mem_kernel, n_fox_heads=N_HEADS_FOX),
        grid=(batch,),
        in_specs=[heads(q0), kv(0), kv(1), heads(g0),
                  pl.BlockSpec((seq, LANES), lambda b: (b, 0)),
                  pl.BlockSpec((1, LANES), lambda b: (0, 0))],
        out_specs=[pl.BlockSpec((nh, seq, LANES), lambda b: (0, b, 0)),
                   pl.BlockSpec((None, seq, LANES), lambda b: (b, 0, 0))],
        out_shape=[jax.ShapeDtypeStruct((nh, batch * seq, LANES), BF16),
                   jax.ShapeDtypeStruct((batch, seq, LANES), BF16)],
        scratch_shapes=[pltpu.VMEM((nh, n_mem, 2 * LANES), BF16)],
        compiler_params=_cparams(1),
        name="mem_attention",
    )(p, mkv, mkv, p, fb, bvec)


def _out_proj_kernel(ya_ref, yb_ref, ym_ref, w_ref, x_ref, g_ref, o_ref):
    y = jnp.concatenate([src[c] for src in (ya_ref, yb_ref, ym_ref) for c in range(src.shape[0])], axis=1)
    o = jnp.dot(y, w_ref[...], preferred_element_type=F32)
    ms = jnp.mean(o * o, axis=-1, keepdims=True)
    o_ref[...] = x_ref[...] + o * lax.rsqrt(ms + EPS) * g_ref[...]


def _out_proj(ya, yb, ym, w, x2, g, *, tm):
    m, d = x2.shape
    heads = lambda y: pl.BlockSpec((y.shape[0], tm, LANES), lambda i: (0, i, 0))
    return pl.pallas_call(
        _out_proj_kernel,
        grid=(m // tm,),
        in_specs=[heads(ya), heads(yb), heads(ym),
                  pl.BlockSpec(w.shape, lambda i: (0, 0), pipeline_mode=pl.Buffered(1)),
                  pl.BlockSpec((tm, d), lambda i: (i, 0)),
                  pl.BlockSpec((1, d), lambda i: (0, 0))],
        out_specs=pl.BlockSpec((tm, d), lambda i: (i, 0)),
        out_shape=jax.ShapeDtypeStruct((m, d), F32),
        compiler_params=_cparams(1),
        name="out_proj",
    )(ya, yb, ym, w, x2, g)


def kernel(x, mem, w_in, b_forget, w_mem_kv, w_out, g_pre, g_post, g_mem, rel_bias):
    batch, seq, d = x.shape
    n_mem = mem.shape[1]
    depth = w_in.shape[0]
    assert N_HEADS_MOBA == N_HEADS_FOX
    n_causal = N_HEADS_MOBA + N_HEADS_FOX
    wm = N_HEADS_MEM * HEAD_DIM
    n_in = w_in.shape[2]
    assert n_in == n_causal * HEAD_GROUP * HEAD_DIM + N_HEADS_FOX + 2 * wm
    qm0 = n_causal * HEAD_GROUP
    gm0 = qm0 + N_HEADS_MEM

    x2 = x.reshape(batch * seq, d)
    mem2 = mem.reshape(batch * n_mem, d)
    for layer in range(depth):
        b_vec = jnp.pad(b_forget[layer], (0, LANES - N_HEADS_FOX)).reshape(1, LANES)
        w_t3 = jnp.transpose(w_in, (0, 2, 1)).reshape(depth * n_in, d // LANES, LANES)
        w_main_t, w_gate_t, mkv = _w_repack_mem_proj(w_t3, layer, n_in, N_HEADS_MOBA, N_HEADS_FOX, mem2,
                                                     g_mem[layer].reshape(1, d), w_mem_kv)
        p, fb = _norm_proj(x2, g_pre[layer].reshape(1, d), w_main_t, w_gate_t,
                           tm=IN_PROJ_TILE[0], tn=IN_PROJ_TILE[1])
        ym, kc = _mem_attention(p, mkv, fb, b_vec, batch, seq, n_mem, qm0, gm0)
        ya, yb, w_out_bf = _causal_attention(p, rel_bias, kc, w_out, layer, batch, seq)
        x2 = _out_proj(ya, yb, ym, w_out_bf, x2, g_post[layer].reshape(1, d), tm=OUT_PROJ_ROWS)
    return x2.reshape(batch, seq, d)
```

```python
import functools
import math

import jax
import jax.numpy as jnp
from jax import lax
from jax.experimental import pallas as pl
from jax.experimental.pallas import tpu as pltpu

HEAD_DIM = 128
N_HEADS_MOBA = 6
N_HEADS_FOX = 6
N_HEADS_MEM = 4
MOBA_BLOCK = 256
MOBA_TOPK = 3
T5_BUCKETS = 32
T5_MAX_DIST = 128
EPS = 1e-6
NEG = -1e30

LANES = 128
SUBLANES = 8
VMEM_LIMIT_BYTES = 56 * 1024 * 1024

IN_PROJ_TILE = (1024, 1792)
OUT_PROJ_ROWS = 512
HEAD_GROUP = 4
W_OUT_CAST_ROWS = 128
Q_TILE = MOBA_BLOCK
MEM_Q_TILE = 1024

F32 = jnp.float32
BF16 = jnp.bfloat16
_NT = (((1,), (1,)), ((), ()))


def _cparams(n_axes):
    return pltpu.CompilerParams(dimension_semantics=("arbitrary",) * n_axes,
                                vmem_limit_bytes=VMEM_LIMIT_BYTES)


def _norm_proj_kernel(x_ref, g_ref, w_ref, wf_ref, p_ref, f_ref, h_ref, *, n_col_blocks, row_chunk):
    tm = x_ref.shape[0]

    @pl.when(pl.program_id(1) == 0)
    def _normalize():
        for c in range(tm // row_chunk):
            rows = slice(c * row_chunk, (c + 1) * row_chunk)
            xf = x_ref[rows, :]
            ms = jnp.mean(xf * xf, axis=-1, keepdims=True)
            h = (xf * lax.rsqrt(ms + EPS) * g_ref[...]).astype(BF16)
            h_ref[rows, :] = h
            f_ref[rows, :] = lax.dot_general(h, wf_ref[...], _NT, preferred_element_type=F32)

    acc = lax.dot_general(h_ref[...], w_ref[...], _NT, preferred_element_type=F32)
    for c in range(n_col_blocks):
        p_ref[c] = acc[:, c * LANES:(c + 1) * LANES].astype(BF16)


def _norm_proj(x2, g, w_t, wf_t, *, tm, tn):
    m, d = x2.shape
    n = w_t.shape[0]
    ncb = tn // LANES
    assert n % tn == 0 and m % tm == 0
    return pl.pallas_call(
        functools.partial(_norm_proj_kernel, n_col_blocks=ncb, row_chunk=256),
        grid=(m // tm, n // tn),
        in_specs=[pl.BlockSpec((tm, d), lambda i, j: (i, 0)),
                  pl.BlockSpec((1, d), lambda i, j: (0, 0)),
                  pl.BlockSpec((tn, d), lambda i, j: (j, 0)),
                  pl.BlockSpec((LANES, d), lambda i, j: (0, 0))],
        out_specs=[pl.BlockSpec((ncb, tm, LANES), lambda i, j: (j, i, 0)),
                   pl.BlockSpec((tm, LANES), lambda i, j: (i, 0))],
        out_shape=[jax.ShapeDtypeStruct((n // LANES, m, LANES), BF16),
                   jax.ShapeDtypeStruct((m, LANES), F32)],
        scratch_shapes=[pltpu.VMEM((tm, d), BF16)],
        compiler_params=_cparams(2),
        name="norm_proj_gate",
    )(x2, g, w_t, wf_t)


def _w_repack_kernel(src_ref, mem_ref, gm_ref, wm_ref, main_ref, gate_ref, mkv_ref,
                     buf_ref, gbuf_ref, hm_ref, sem_ref, gsem_ref, *,
                     base_row, n_heads, n_aligned, n_gate, n_mem_tiles, row_chunk):
    t = pl.program_id(0)
    n_steps = pl.num_programs(0)
    n_slices = src_ref.shape[1]

    def source_row(step, g):
        head_blk = (step // n_heads) * (HEAD_GROUP * n_heads) + g * n_heads + step % n_heads
        blk = jnp.where(step < 2 * n_heads, head_blk, HEAD_GROUP * step + g)
        return base_row + blk * LANES + jnp.where(blk * LANES >= n_aligned, n_gate, 0)

    def block_copies(step, slot):
        return [pltpu.make_async_copy(src_ref.at[pl.ds(source_row(step, g), LANES), s, :],
                                      buf_ref.at[slot, pl.ds(g * LANES, LANES), pl.ds(s * LANES, LANES)],
                                      sem_ref.at[slot])
                for g in range(HEAD_GROUP) for s in range(n_slices)]

    def gate_copies():
        gate_row0 = base_row + n_aligned
        return [pltpu.make_async_copy(src_ref.at[pl.ds(gate_row0, SUBLANES), s, :],
                                      gbuf_ref.at[:, pl.ds(s * LANES, LANES)],
                                      gsem_ref.at[0]) for s in range(n_slices)]

    @pl.when(t == 0)
    def _first():
        for cp in gate_copies() + block_copies(0, 0):
            cp.start()

    @pl.when(t + 1 < n_steps)
    def _prefetch():
        for cp in block_copies(t + 1, (t + 1) % 2):
            cp.start()

    @pl.when(t == 0)
    def _gate_rows_and_mem_norm():
        for cp in gate_copies():
            cp.wait()
        row = lax.broadcasted_iota(jnp.int32, gbuf_ref.shape, 0)
        gate = jnp.where(row < n_gate, gbuf_ref[...], 0.0)
        gate_ref[...] = jnp.concatenate(
            [gate, jnp.zeros((gate_ref.shape[0] - SUBLANES, gate.shape[1]), F32)], axis=0).astype(BF16)
        for c in range(mem_ref.shape[0] // row_chunk):
            r = slice(c * row_chunk, (c + 1) * row_chunk)
            xf = mem_ref[r, :]
            ms = jnp.mean(xf * xf, axis=-1, keepdims=True)
            hm_ref[r, :] = (xf * lax.rsqrt(ms + EPS) * gm_ref[...]).astype(BF16)

    @pl.when(t < n_mem_tiles)
    def _mem_projection():
        mkv_ref[...] = jnp.dot(hm_ref[...], wm_ref[...].astype(BF16),
                               preferred_element_type=F32).astype(BF16)

    for cp in block_copies(t, t % 2):
        cp.wait()
    main_ref[...] = buf_ref[t % 2].astype(BF16)


def _w_repack_mem_proj(w_t3, layer, n_total, n_heads, n_gate, mem2, g_mem, w_mem3):
    n_slices = w_t3.shape[1]
    d = n_slices * LANES
    rows = HEAD_GROUP * LANES
    n_aligned = 2 * n_heads * rows
    n_main = n_total - n_gate
    m_mem = mem2.shape[0]
    n_mem_tiles = w_mem3.shape[2] // LANES
    n_steps = n_main // rows
    assert n_main % rows == 0 and n_aligned <= n_main and n_gate <= SUBLANES
    assert w_mem3.shape[2] % LANES == 0 and n_mem_tiles <= n_steps
    mem_tile = lambda t: jnp.minimum(t, n_mem_tiles - 1)
    return pl.pallas_call(
        functools.partial(_w_repack_kernel, base_row=layer * n_total, n_heads=n_heads,
                          n_aligned=n_aligned, n_gate=n_gate,
                          n_mem_tiles=n_mem_tiles, row_chunk=256),
        grid=(n_steps,),
        in_specs=[pl.BlockSpec(memory_space=pl.ANY),
                  pl.BlockSpec((m_mem, d), lambda t: (0, 0), pipeline_mode=pl.Buffered(1)),
                  pl.BlockSpec((1, d), lambda t: (0, 0)),
                  pl.BlockSpec((None, d, LANES), lambda t: (layer, 0, mem_tile(t)))],
        out_specs=[pl.BlockSpec((rows, d), lambda t: (t, 0)),
                   pl.BlockSpec((LANES, d), lambda t: (0, 0)),
                   pl.BlockSpec((None, m_mem, LANES), lambda t: (mem_tile(t), 0, 0))],
        out_shape=[jax.ShapeDtypeStruct((n_main, d), BF16),
                   jax.ShapeDtypeStruct((LANES, d), BF16),
                   jax.ShapeDtypeStruct((n_mem_tiles, m_mem, LANES), BF16)],
        scratch_shapes=[pltpu.VMEM((2, rows, d), F32), pltpu.VMEM((SUBLANES, d), F32),
                        pltpu.VMEM((m_mem, d), BF16),
                        pltpu.SemaphoreType.DMA((2,)), pltpu.SemaphoreType.DMA((1,))],
        compiler_params=_cparams(1),
        name="w_repack_mem_proj",
    )(w_t3, mem2, g_mem, w_mem3)


N_SPLIT = 3


def _split3(v):
    hi = v.astype(BF16).astype(F32)
    mid = (v - hi).astype(BF16).astype(F32)
    lo = (v - hi - mid).astype(BF16).astype(F32)
    return hi, mid, lo


def _forget_bias_columns(f_ref, b_ref, kc_ref, n_heads):
    z = f_ref[...] + b_ref[...]
    logf = jnp.minimum(z, 0.0) - jnp.log1p(jnp.exp(-jnp.abs(z)))
    x = logf.T[0:SUBLANES, :]
    s_len = x.shape[1]
    lane = lax.broadcasted_iota(jnp.int32, x.shape, 1)
    shift = 1
    while shift < s_len:
        x = x + jnp.where(lane >= shift, pltpu.roll(x, shift, 1), 0.0)
        shift *= 2
    row = lax.broadcasted_iota(jnp.int32, x.shape, 0)
    x = jnp.where(row < n_heads, x, 0.0)
    terms = _split3(x * (-1.0 / HEAD_DIM ** -0.5))
    pad = jnp.zeros((LANES - SUBLANES * N_SPLIT, s_len), F32)
    kc_ref[...] = jnp.concatenate(list(terms) + [pad], axis=0).T.astype(BF16)


LOG2E = math.log2(math.e)


def _softmax_pv_folded(a, v_aug, g):
    m = jnp.max(a, axis=1, keepdims=True)
    p = jnp.exp2((a - m) * (HEAD_DIM ** -0.5 * LOG2E)).astype(BF16)
    acc = jnp.dot(p, v_aug, preferred_element_type=F32)
    gf = g.astype(F32)
    silu = gf * (1.0 / (1.0 + jnp.exp(-gf)))
    return (acc[:, :LANES] / acc[:, LANES:] * silu).astype(BF16)


SCORE_AHEAD = 1


def _stream_events(setup, n_tiles, score_matmul, finish):
    scores = {}

    def score(i):
        return lambda: scores.__setitem__(i, score_matmul(i))

    def fin(i):
        return lambda: finish(i, scores.pop(i))

    events = [setup] + [score(i) for i in range(min(SCORE_AHEAD, n_tiles))]
    for i in range(n_tiles):
        if i + SCORE_AHEAD < n_tiles:
            events.append(score(i + SCORE_AHEAD))
        events.append(fin(i))
    return events


def _run_interleaved(streams, lead=0):
    queues = [list(ev) for ev in streams]
    for _ in range(min(lead, len(queues[0]))):
        queues[0].pop(0)()
    while any(queues):
        for q in queues:
            if q:
                q.pop(0)()


def _t5_bias(rb_ref, h, dist):
    max_exact = T5_BUCKETS // 2
    n = jnp.maximum(dist, 0)
    nf = jnp.maximum(n, 1).astype(F32)
    val = jnp.log(nf / max_exact) / math.log(T5_MAX_DIST / max_exact) * (T5_BUCKETS - max_exact)
    bias = jnp.zeros(dist.shape, F32)
    for kk in range(T5_BUCKETS):
        if kk < max_exact:
            hit = n == kk
        elif kk < T5_BUCKETS - 1:
            hit = (n >= max_exact) & (val >= kk - max_exact) & (val < kk - max_exact + 1)
        else:
            hit = (n >= max_exact) & (val >= kk - max_exact)
        bias = jnp.where(hit, rb_ref[h, kk], bias)
    return bias


def _moba_head_constants(rb_ref, h, kaug_ref, bown_ref, bprev_ref):
    inv_scale = 1.0 / HEAD_DIM ** -0.5
    blk = MOBA_BLOCK
    seq = kaug_ref.shape[0]
    far_bias = rb_ref[h, T5_BUCKETS - 1]
    srow = lax.broadcasted_iota(jnp.int32, (seq, LANES), 0)
    col = lax.broadcasted_iota(jnp.int32, (seq, LANES), 1)
    kaug_ref[:, LANES:] = jnp.where((col & (SUBLANES - 1)) * blk == srow - (srow & (blk - 1)),
                                    1.0, 0.0).astype(BF16)
    t = T5_MAX_DIST
    assert blk == 2 * t
    dist = lax.broadcasted_iota(jnp.int32, (t, t), 0) - lax.broadcasted_iota(jnp.int32, (t, t), 1)
    near = jnp.where(dist >= 0, (_t5_bias(rb_ref, h, dist) - far_bias) * inv_scale, NEG)
    mid = (_t5_bias(rb_ref, h, dist + t) - far_bias) * inv_scale
    bown_ref[0:t, 0:t] = near
    bown_ref[0:t, t:blk] = jnp.full((t, t), NEG, F32)
    bown_ref[t:blk, 0:t] = mid
    bown_ref[t:blk, t:blk] = near
    bprev_ref[...] = jnp.zeros((blk, blk), F32)
    bprev_ref[0:t, t:blk] = mid


def _moba_stream(q_ref, k_ref, v_ref, g_ref, o_ref,
                 kaug_ref, qaug_ref, vaug_ref, kmean_ref, bown_ref, bprev_ref):
    blk = MOBA_BLOCK
    seq = q_ref.shape[0]
    n_blocks = seq // blk
    n_chunks = seq // LANES

    def setup():
        _moba_setup(q_ref, k_ref, v_ref, kaug_ref, qaug_ref, vaug_ref, kmean_ref, n_blocks, n_chunks)

    def score_matmul(i):
        return lax.dot_general(qaug_ref[i * blk:(i + 1) * blk, :], kaug_ref[0:(i + 1) * blk, :], _NT,
                               preferred_element_type=F32)

    def finish(i, a):
        rows = slice(i * blk, (i + 1) * blk)
        n = (i + 1) * blk
        parts = [a[:, n - blk:] + bown_ref[...]]
        if i >= 1:
            parts.insert(0, a[:, n - 2 * blk:n - blk] + bprev_ref[...])
        if i >= 2:
            parts.insert(0, a[:, :n - 2 * blk])
        a = parts[0] if len(parts) == 1 else jnp.concatenate(parts, axis=1)
        o_ref[rows, :] = _softmax_pv_folded(a, vaug_ref[0:n, :], g_ref[rows, :])

    return setup, n_blocks, score_matmul, finish


def _moba_setup(q_ref, k_ref, v_ref, kaug_ref, qaug_ref, vaug_ref, kmean_ref, n_blocks, n_chunks):
    blk = MOBA_BLOCK
    seq = q_ref.shape[0]
    kaug_ref[:, :LANES] = k_ref[...]
    vaug_ref[:, :LANES] = v_ref[...]
    vaug_ref[:, LANES:] = jnp.ones((seq, LANES), BF16)
    blk_row = lax.broadcasted_iota(jnp.int32, (SUBLANES, LANES), 0)
    km = jnp.zeros((SUBLANES, LANES), F32)
    for j in range(n_blocks):
        mean_j = jnp.mean(k_ref[j * blk:(j + 1) * blk, :].astype(F32), axis=0, keepdims=True)
        km = jnp.where(blk_row == j, mean_j, km)
    kmean_ref[...] = jnp.concatenate([km, jnp.zeros((LANES - SUBLANES, LANES), F32)], axis=0).astype(BF16)

    gate_t = lax.dot_general(kmean_ref[...], q_ref[...], _NT, preferred_element_type=F32)[0:SUBLANES, :]
    jrow = lax.broadcasted_iota(jnp.int32, gate_t.shape, 0)
    qpos = lax.broadcasted_iota(jnp.int32, gate_t.shape, 1)
    qblk = jnp.right_shift(qpos, int(math.log2(blk)))
    gm = jnp.where(jrow < qblk, gate_t, NEG)
    sel_bias = jnp.zeros(gate_t.shape, F32)
    for j in range(n_blocks):
        gj = gm[j:j + 1, :]
        beats = (gm > gj) | ((gm == gj) & (jrow < j))
        rank = jnp.sum(beats.astype(F32), axis=0, keepdims=True)
        allowed = ((rank < MOBA_TOPK) & (qblk[0:1, :] > j)) | (qblk[0:1, :] == j)
        sel_bias = jnp.where(jrow == j, jnp.where(allowed, 0.0, NEG), sel_bias)
    assert SUBLANES * n_chunks == LANES
    packed_t = jnp.concatenate([sel_bias[:, c * LANES:(c + 1) * LANES] for c in range(n_chunks)], axis=0).T
    lane_chunk = jnp.right_shift(lax.broadcasted_iota(jnp.int32, (LANES, LANES), 1),
                                 int(math.log2(SUBLANES)))
    qaug_ref[:, :LANES] = q_ref[...]
    for c in range(n_chunks):
        qaug_ref[c * LANES:(c + 1) * LANES, LANES:] = jnp.where(lane_chunk == c, packed_t, 0.0).astype(BF16)


def _moba_scratch(seq):
    blk = MOBA_BLOCK
    return [pltpu.VMEM((seq, 2 * LANES), BF16),
            pltpu.VMEM((seq, 2 * LANES), BF16),
            pltpu.VMEM((seq, 2 * LANES), BF16),
            pltpu.VMEM((LANES, LANES), BF16),
            pltpu.VMEM((blk, blk), F32),
            pltpu.VMEM((blk, blk), F32)]


def _fox_stream(h, q_ref, k_ref, v_ref, g_ref, kc_ref, o_ref, qaug_ref, kaug_ref, vaug_ref, causal_ref):
    seq = q_ref.shape[0]

    def setup():
        col = lax.broadcasted_iota(jnp.int32, (seq, LANES), 1)
        qaug_ref[:, :LANES] = q_ref[...]
        qaug_ref[:, LANES:] = jnp.where(((col & (SUBLANES - 1)) == h) & (col < SUBLANES * N_SPLIT),
                                        1.0, 0.0).astype(BF16)
        kaug_ref[:, :LANES] = k_ref[...]
        kaug_ref[:, LANES:] = kc_ref[...]
        vaug_ref[:, :LANES] = v_ref[...]
        vaug_ref[:, LANES:] = jnp.ones((seq, LANES), BF16)
        r = lax.broadcasted_iota(jnp.int32, (Q_TILE, Q_TILE), 0)
        c = lax.broadcasted_iota(jnp.int32, (Q_TILE, Q_TILE), 1)
        causal_ref[...] = jnp.where(c <= r, 0.0, NEG)

    def score_matmul(i):
        return lax.dot_general(qaug_ref[i * Q_TILE:(i + 1) * Q_TILE, :], kaug_ref[0:(i + 1) * Q_TILE, :],
                               _NT, preferred_element_type=F32)

    def finish(i, a):
        rows = slice(i * Q_TILE, (i + 1) * Q_TILE)
        n = (i + 1) * Q_TILE
        own = a[:, n - Q_TILE:] + causal_ref[...]
        a = own if i == 0 else jnp.concatenate([a[:, :n - Q_TILE], own], axis=1)
        o_ref[rows, :] = _softmax_pv_folded(a, vaug_ref[0:n, :], g_ref[rows, :])

    return setup, seq // Q_TILE, score_matmul, finish


def _fox_scratch(seq):
    return [pltpu.VMEM((seq, 2 * LANES), BF16), pltpu.VMEM((seq, 2 * LANES), BF16),
            pltpu.VMEM((seq, 2 * LANES), BF16), pltpu.VMEM((Q_TILE, Q_TILE), F32)]


FOX_LEAD_EVENTS = 11


def _causal_attention_kernel(*refs):
    rb_ref, moba_qkvg, fox_qkvg, kc, wo_ref, oa, ob, wob_ref = refs[:8]
    qa, ka, va, ga = (moba_qkvg.at[i] for i in range(HEAD_GROUP))
    qb, kb, vb, gb = (fox_qkvg.at[i] for i in range(HEAD_GROUP))
    n_moba = len(_moba_scratch(qa.shape[0]))
    moba_scratch, fox_scratch = refs[8:8 + n_moba], refs[8 + n_moba:]
    kaug_ref, _, _, _, bown_ref, bprev_ref = moba_scratch
    h = pl.program_id(0)

    @pl.when(pl.program_id(1) == 0)
    def _():
        _moba_head_constants(rb_ref, h, kaug_ref, bown_ref, bprev_ref)

    wob_ref[...] = wo_ref[...].astype(BF16)
    _run_interleaved([_stream_events(*_fox_stream(h, qb, kb, vb, gb, kc, ob, *fox_scratch)),
                      _stream_events(*_moba_stream(qa, ka, va, ga, oa, *moba_scratch))],
                     lead=FOX_LEAD_EVENTS)


def _causal_attention(p, rel_bias, kc, w_out3, layer, batch, seq):
    nb = seq // MOBA_BLOCK
    assert seq % MOBA_BLOCK == 0 and MOBA_TOPK < SUBLANES and nb <= SUBLANES and MOBA_BLOCK > T5_MAX_DIST
    assert N_HEADS_MOBA == N_HEADS_FOX <= SUBLANES
    dw, d = w_out3.shape[1:]
    n_slabs = dw // W_OUT_CAST_ROWS
    assert dw % W_OUT_CAST_ROWS == 0 and n_slabs <= N_HEADS_MOBA * batch
    slab = lambda h, b: jnp.minimum(h * batch + b, n_slabs - 1)
    head = lambda first: pl.BlockSpec((HEAD_GROUP, seq, LANES), lambda h, b: (first + h, b, 0))
    out = pl.BlockSpec((None, seq, LANES), lambda h, b: (h, b, 0))
    out_sds = jax.ShapeDtypeStruct((N_HEADS_MOBA, batch * seq, LANES), BF16)
    return pl.pallas_call(
        _causal_attention_kernel,
        grid=(N_HEADS_MOBA, batch),
        in_specs=[pl.BlockSpec(memory_space=pltpu.SMEM), head(0), head(N_HEADS_MOBA),
                  pl.BlockSpec((None, seq, LANES), lambda h, b: (b, 0, 0)),
                  pl.BlockSpec((None, W_OUT_CAST_ROWS, d), lambda h, b: (layer, slab(h, b), 0))],
        out_specs=[out, out, pl.BlockSpec((W_OUT_CAST_ROWS, d), lambda h, b: (slab(h, b), 0))],
        out_shape=[out_sds, out_sds, jax.ShapeDtypeStruct((dw, d), BF16)],
        scratch_shapes=_moba_scratch(seq) + _fox_scratch(seq),
        compiler_params=_cparams(2),
        name="causal_attention",
    )(rel_bias, p, p, kc, w_out3)


def _mem_kernel(q_ref, k_ref, v_ref, g_ref, f_ref, b_ref, o_ref, kc_ref, vaug_ref, *, n_fox_heads):
    n_heads, seq, _ = q_ref.shape
    tiles_per_head = seq // MEM_Q_TILE

    _forget_bias_columns(f_ref, b_ref, kc_ref, n_fox_heads)
    vaug_ref[:, :, LANES:] = jnp.ones(v_ref.shape, BF16)
    vaug_ref[:, :, :LANES] = v_ref[...]

    def tile(t):
        h, i = divmod(t, tiles_per_head)
        return h, slice(i * MEM_Q_TILE, (i + 1) * MEM_Q_TILE)

    def score_matmul(t):
        h, rows = tile(t)
        return lax.dot_general(q_ref[h, rows, :], k_ref[h], _NT, preferred_element_type=F32)

    def finish(t, a):
        h, rows = tile(t)
        o_ref[h, rows, :] = _softmax_pv_folded(a, vaug_ref[h], g_ref[h, rows, :])

    _run_interleaved([_stream_events(lambda: None, n_heads * tiles_per_head, score_matmul, finish)])


def _mem_attention(p, mkv, fb, bvec, batch, seq, n_mem, q0, g0):
    nh = N_HEADS_MEM
    assert q0 % nh == 0 and g0 % nh == 0 and N_HEADS_FOX <= SUBLANES
    heads = lambda off: pl.BlockSpec((nh, seq, LANES), lambda b: (off // nh, b, 0))
    kv = lambda blk: pl.BlockSpec((nh, n_mem, LANES), lambda b: (blk, b, 0))
    return pl.pallas_call(
        functools.partial(_mem_kernel, n_fox_heads=N_HEADS_FOX),
        grid=(batch,),
        in_specs=[heads(q0), kv(0), kv(1), heads(g0),
                  pl.BlockSpec((seq, LANES), lambda b: (b, 0)),
                  pl.BlockSpec((1, LANES), lambda b: (0, 0))],
        out_specs=[pl.BlockSpec((nh, seq, LANES), lambda b: (0, b, 0)),
                   pl.BlockSpec((None, seq, LANES), lambda b: (b, 0, 0))],
        out_shape=[jax.ShapeDtypeStruct((nh, batch * seq, LANES), BF16),
                   jax.ShapeDtypeStruct((batch, seq, LANES), BF16)],
        scratch_shapes=[pltpu.VMEM((nh, n_mem, 2 * LANES), BF16)],
        compiler_params=_cparams(1),
        name="mem_attention",
    )(p, mkv, mkv, p, fb, bvec)


def _out_proj_kernel(ya_ref, yb_ref, ym_ref, w_ref, x_ref, g_ref, o_ref):
    y = jnp.concatenate([src[c] for src in (ya_ref, yb_ref, ym_ref) for c in range(src.shape[0])], axis=1)
    o = jnp.dot(y, w_ref[...], preferred_element_type=F32)
    ms = jnp.mean(o * o, axis=-1, keepdims=True)
    o_ref[...] = x_ref[...] + o * lax.rsqrt(ms + EPS) * g_ref[...]


def _out_proj(ya, yb, ym, w, x2, g, *, tm):
    m, d = x2.shape
    heads = lambda y: pl.BlockSpec((y.shape[0], tm, LANES), lambda i: (0, i, 0))
    return pl.pallas_call(
        _out_proj_kernel,
        grid=(m // tm,),
        in_specs=[heads(ya), heads(yb), heads(ym),
                  pl.BlockSpec(w.shape, lambda i: (0, 0), pipeline_mode=pl.Buffered(1)),
                  pl.BlockSpec((tm, d), lambda i: (i, 0)),
                  pl.BlockSpec((1, d), lambda i: (0, 0))],
        out_specs=pl.BlockSpec((tm, d), lambda i: (i, 0)),
        out_shape=jax.ShapeDtypeStruct((m, d), F32),
        compiler_params=_cparams(1),
        name="out_proj",
    )(ya, yb, ym, w, x2, g)


def kernel(x, mem, w_in, b_forget, w_mem_kv, w_out, g_pre, g_post, g_mem, rel_bias):
    batch, seq, d = x.shape
    n_mem = mem.shape[1]
    depth = w_in.shape[0]
    assert N_HEADS_MOBA == N_HEADS_FOX
    n_causal = N_HEADS_MOBA + N_HEADS_FOX
    wm = N_HEADS_MEM * HEAD_DIM
    n_in = w_in.shape[2]
    assert n_in == n_causal * HEAD_GROUP * HEAD_DIM + N_HEADS_FOX + 2 * wm
    qm0 = n_causal * HEAD_GROUP
    gm0 = qm0 + N_HEADS_MEM

    x2 = x.reshape(batch * seq, d)
    mem2 = mem.reshape(batch * n_mem, d)
    for layer in range(depth):
        b_vec = jnp.pad(b_forget[layer], (0, LANES - N_HEADS_FOX)).reshape(1, LANES)
        w_t3 = jnp.transpose(w_in, (0, 2, 1)).reshape(depth * n_in, d // LANES, LANES)
        w_main_t, w_gate_t, mkv = _w_repack_mem_proj(w_t3, layer, n_in, N_HEADS_MOBA, N_HEADS_FOX, mem2,
                                                     g_mem[layer].reshape(1, d), w_mem_kv)
        p, fb = _norm_proj(x2, g_pre[layer].reshape(1, d), w_main_t, w_gate_t,
                           tm=IN_PROJ_TILE[0], tn=IN_PROJ_TILE[1])
        ym, kc = _mem_attention(p, mkv, fb, b_vec, batch, seq, n_mem, qm0, gm0)
        ya, yb, w_out_bf = _causal_attention(p, rel_bias, kc, w_out, layer, batch, seq)
        x2 = _out_proj(ya, yb, ym, w_out_bf, x2, g_post[layer].reshape(1, d), tm=OUT_PROJ_ROWS)
    return x2.reshape(batch, seq, d)
```

```python
import functools
import math

import jax
import jax.numpy as jnp
from jax import lax
from jax.experimental import pallas as pl
from jax.experimental.pallas import tpu as pltpu

HEAD_DIM = 128
N_HEADS_MOBA = 6
N_HEADS_FOX = 6
N_HEADS_MEM = 4
MOBA_BLOCK = 256
MOBA_TOPK = 3
T5_BUCKETS = 32
T5_MAX_DIST = 128
EPS = 1e-6
NEG = -1e30

LANES = 128
SUBLANES = 8
VMEM_LIMIT_BYTES = 56 * 1024 * 1024

IN_PROJ_TILE = (1024, 1792)
NORM_ROW_CHUNK = 256
OUT_PROJ_ROWS = 512
HEAD_GROUP = 4
W_OUT_CAST_ROWS = 128
Q_TILE = MOBA_BLOCK
MEM_Q_TILE = 1024

F32 = jnp.float32
BF16 = jnp.bfloat16
_NT = (((1,), (1,)), ((), ()))


def _cparams(n_axes):
    return pltpu.CompilerParams(dimension_semantics=("arbitrary",) * n_axes,
                                vmem_limit_bytes=VMEM_LIMIT_BYTES)


def _norm_proj_kernel(x_ref, g_ref, w_ref, wf_ref, p_ref, f_ref, h_ref, *, n_col_blocks, row_chunk):
    tm = x_ref.shape[0]

    @pl.when(pl.program_id(1) == 0)
    def _normalize():
        for c in range(tm // row_chunk):
            rows = slice(c * row_chunk, (c + 1) * row_chunk)
            xf = x_ref[rows, :]
            ms = jnp.mean(xf * xf, axis=-1, keepdims=True)
            h = (xf * lax.rsqrt(ms + EPS) * g_ref[...]).astype(BF16)
            h_ref[rows, :] = h
            f_ref[rows, :] = lax.dot_general(h, wf_ref[...], _NT, preferred_element_type=F32)

    acc = lax.dot_general(h_ref[...], w_ref[...], _NT, preferred_element_type=F32)
    for c in range(n_col_blocks):
        p_ref[c] = acc[:, c * LANES:(c + 1) * LANES].astype(BF16)


def _norm_proj(x2, g, w_t, wf_t, *, tm, tn):
    m, d = x2.shape
    n = w_t.shape[0]
    ncb = tn // LANES
    assert n % tn == 0 and m % tm == 0
    return pl.pallas_call(
        functools.partial(_norm_proj_kernel, n_col_blocks=ncb, row_chunk=NORM_ROW_CHUNK),
        grid=(m // tm, n // tn),
        in_specs=[pl.BlockSpec((tm, d), lambda i, j: (i, 0)),
                  pl.BlockSpec((1, d), lambda i, j: (0, 0)),
                  pl.BlockSpec((tn, d), lambda i, j: (j, 0)),
                  pl.BlockSpec((LANES, d), lambda i, j: (0, 0))],
        out_specs=[pl.BlockSpec((ncb, tm, LANES), lambda i, j: (j, i, 0)),
                   pl.BlockSpec((tm, LANES), lambda i, j: (i, 0))],
        out_shape=[jax.ShapeDtypeStruct((n // LANES, m, LANES), BF16),
                   jax.ShapeDtypeStruct((m, LANES), F32)],
        scratch_shapes=[pltpu.VMEM((tm, d), BF16)],
        compiler_params=_cparams(2),
        name="norm_proj_gate",
    )(x2, g, w_t, wf_t)


def _w_repack_kernel(src_ref, mem_ref, gm_ref, wm_ref, main_ref, gate_ref, mkv_ref,
                     buf_ref, gbuf_ref, hm_ref, sem_ref, gsem_ref, *,
                     base_row, n_heads, n_aligned, n_gate, n_mem_tiles, row_chunk):
    t = pl.program_id(0)
    n_steps = pl.num_programs(0)
    n_slices = src_ref.shape[1]

    def source_row(step, g):
        head_blk = (step // n_heads) * (HEAD_GROUP * n_heads) + g * n_heads + step % n_heads
        blk = jnp.where(step < 2 * n_heads, head_blk, HEAD_GROUP * step + g)
        return base_row + blk * LANES + jnp.where(blk * LANES >= n_aligned, n_gate, 0)

    def block_copies(step, slot):
        return [pltpu.make_async_copy(src_ref.at[pl.ds(source_row(step, g), LANES), s, :],
                                      buf_ref.at[slot, pl.ds(g * LANES, LANES), pl.ds(s * LANES, LANES)],
                                      sem_ref.at[slot])
                for g in range(HEAD_GROUP) for s in range(n_slices)]

    def gate_copies():
        gate_row0 = base_row + n_aligned
        return [pltpu.make_async_copy(src_ref.at[pl.ds(gate_row0, SUBLANES), s, :],
                                      gbuf_ref.at[:, pl.ds(s * LANES, LANES)],
                                      gsem_ref.at[0]) for s in range(n_slices)]

    @pl.when(t == 0)
    def _first():
        for cp in gate_copies() + block_copies(0, 0):
            cp.start()

    @pl.when(t + 1 < n_steps)
    def _prefetch():
        for cp in block_copies(t + 1, (t + 1) % 2):
            cp.start()

    @pl.when(t == 0)
    def _gate_rows_and_mem_norm():
        for cp in gate_copies():
            cp.wait()
        row = lax.broadcasted_iota(jnp.int32, gbuf_ref.shape, 0)
        gate = jnp.where(row < n_gate, gbuf_ref[...], 0.0)
        gate_ref[...] = jnp.concatenate(
            [gate, jnp.zeros((gate_ref.shape[0] - SUBLANES, gate.shape[1]), F32)], axis=0).astype(BF16)
        for c in range(mem_ref.shape[0] // row_chunk):
            r = slice(c * row_chunk, (c + 1) * row_chunk)
            xf = mem_ref[r, :]
            ms = jnp.mean(xf * xf, axis=-1, keepdims=True)
            hm_ref[r, :] = (xf * lax.rsqrt(ms + EPS) * gm_ref[...]).astype(BF16)

    @pl.when(t < n_mem_tiles)
    def _mem_projection():
        mkv_ref[...] = jnp.dot(hm_ref[...], wm_ref[...].astype(BF16),
                               preferred_element_type=F32).astype(BF16)

    for cp in block_copies(t, t % 2):
        cp.wait()
    main_ref[...] = buf_ref[t % 2].astype(BF16)


def _w_repack_mem_proj(w_t3, layer, n_total, n_heads, n_gate, mem2, g_mem, w_mem3):
    n_slices = w_t3.shape[1]
    d = n_slices * LANES
    rows = HEAD_GROUP * LANES
    n_aligned = 2 * n_heads * rows
    n_main = n_total - n_gate
    m_mem = mem2.shape[0]
    n_mem_tiles = w_mem3.shape[2] // LANES
    n_steps = n_main // rows
    assert n_main % rows == 0 and n_aligned <= n_main and n_gate <= SUBLANES
    assert w_mem3.shape[2] % LANES == 0 and n_mem_tiles <= n_steps
    mem_tile = lambda t: jnp.minimum(t, n_mem_tiles - 1)
    return pl.pallas_call(
        functools.partial(_w_repack_kernel, base_row=layer * n_total, n_heads=n_heads,
                          n_aligned=n_aligned, n_gate=n_gate,
                          n_mem_tiles=n_mem_tiles, row_chunk=NORM_ROW_CHUNK),
        grid=(n_steps,),
        in_specs=[pl.BlockSpec(memory_space=pl.ANY),
                  pl.BlockSpec((m_mem, d), lambda t: (0, 0), pipeline_mode=pl.Buffered(1)),
                  pl.BlockSpec((1, d), lambda t: (0, 0)),
                  pl.BlockSpec((None, d, LANES), lambda t: (layer, 0, mem_tile(t)))],
        out_specs=[pl.BlockSpec((rows, d), lambda t: (t, 0)),
                   pl.BlockSpec((LANES, d), lambda t: (0, 0)),
                   pl.BlockSpec((None, m_mem, LANES), lambda t: (mem_tile(t), 0, 0))],
        out_shape=[jax.ShapeDtypeStruct((n_main, d), BF16),
                   jax.ShapeDtypeStruct((LANES, d), BF16),
                   jax.ShapeDtypeStruct((n_mem_tiles, m_mem, LANES), BF16)],
        scratch_shapes=[pltpu.VMEM((2, rows, d), F32), pltpu.VMEM((SUBLANES, d), F32),
                        pltpu.VMEM((m_mem, d), BF16),
                        pltpu.SemaphoreType.DMA((2,)), pltpu.SemaphoreType.DMA((1,))],
        compiler_params=_cparams(1),
        name="w_repack_mem_proj",
    )(w_t3, mem2, g_mem, w_mem3)


N_SPLIT = 3


def _split3(v):
    hi = v.astype(BF16).astype(F32)
    mid = (v - hi).astype(BF16).astype(F32)
    lo = (v - hi - mid).astype(BF16).astype(F32)
    return hi, mid, lo


def _forget_bias_columns(f_ref, b_ref, layer, kc_ref, n_heads):
    head_lane = lax.broadcasted_iota(jnp.int32, (1, LANES), 1)
    bias = jnp.zeros((1, LANES), F32)
    for hh in range(n_heads):
        bias = jnp.where(head_lane == hh, b_ref[layer, hh], bias)
    z = f_ref[...] + bias
    logf = jnp.minimum(z, 0.0) - jnp.log1p(jnp.exp(-jnp.abs(z)))
    x = logf.T[0:SUBLANES, :]
    s_len = x.shape[1]
    lane = lax.broadcasted_iota(jnp.int32, x.shape, 1)
    shift = 1
    while shift < s_len:
        x = x + jnp.where(lane >= shift, pltpu.roll(x, shift, 1), 0.0)
        shift *= 2
    row = lax.broadcasted_iota(jnp.int32, x.shape, 0)
    x = jnp.where(row < n_heads, x, 0.0)
    terms = _split3(x * (-1.0 / HEAD_DIM ** -0.5))
    pad = jnp.zeros((LANES - SUBLANES * N_SPLIT, s_len), F32)
    kc_ref[...] = jnp.concatenate(list(terms) + [pad], axis=0).T.astype(BF16)


LOG2E = math.log2(math.e)


def _softmax_pv_folded(a, v_aug, g):
    m = jnp.max(a, axis=1, keepdims=True)
    p = jnp.exp2((a - m) * (HEAD_DIM ** -0.5 * LOG2E)).astype(BF16)
    acc = jnp.dot(p, v_aug, preferred_element_type=F32)
    gf = g.astype(F32)
    silu = gf * (1.0 / (1.0 + jnp.exp(-gf)))
    return (acc[:, :LANES] / acc[:, LANES:] * silu).astype(BF16)


SCORE_AHEAD = 1


def _stream_events(setup, n_tiles, score_matmul, finish):
    scores = {}

    def score(i):
        return lambda: scores.__setitem__(i, score_matmul(i))

    def fin(i):
        return lambda: finish(i, scores.pop(i))

    events = [setup] + [score(i) for i in range(min(SCORE_AHEAD, n_tiles))]
    for i in range(n_tiles):
        if i + SCORE_AHEAD < n_tiles:
            events.append(score(i + SCORE_AHEAD))
        events.append(fin(i))
    return events


def _run_interleaved(streams, lead=0):
    queues = [list(ev) for ev in streams]
    for _ in range(min(lead, len(queues[0]))):
        queues[0].pop(0)()
    while any(queues):
        for q in queues:
            if q:
                q.pop(0)()


def _t5_bias(rb_ref, h, dist):
    max_exact = T5_BUCKETS // 2
    n = jnp.maximum(dist, 0)
    nf = jnp.maximum(n, 1).astype(F32)
    val = jnp.log(nf / max_exact) / math.log(T5_MAX_DIST / max_exact) * (T5_BUCKETS - max_exact)
    bias = jnp.zeros(dist.shape, F32)
    for kk in range(T5_BUCKETS):
        if kk < max_exact:
            hit = n == kk
        elif kk < T5_BUCKETS - 1:
            hit = (n >= max_exact) & (val >= kk - max_exact) & (val < kk - max_exact + 1)
        else:
            hit = (n >= max_exact) & (val >= kk - max_exact)
        bias = jnp.where(hit, rb_ref[h, kk], bias)
    return bias


def _moba_head_constants(rb_ref, h, kaug_ref, bown_ref, bprev_ref):
    inv_scale = 1.0 / HEAD_DIM ** -0.5
    blk = MOBA_BLOCK
    seq = kaug_ref.shape[0]
    far_bias = rb_ref[h, T5_BUCKETS - 1]
    srow = lax.broadcasted_iota(jnp.int32, (seq, LANES), 0)
    col = lax.broadcasted_iota(jnp.int32, (seq, LANES), 1)
    kaug_ref[:, LANES:] = jnp.where((col & (SUBLANES - 1)) * blk == srow - (srow & (blk - 1)),
                                    1.0, 0.0).astype(BF16)
    t = T5_MAX_DIST
    assert blk == 2 * t
    dist = lax.broadcasted_iota(jnp.int32, (t, t), 0) - lax.broadcasted_iota(jnp.int32, (t, t), 1)
    near = jnp.where(dist >= 0, (_t5_bias(rb_ref, h, dist) - far_bias) * inv_scale, NEG)
    mid = (_t5_bias(rb_ref, h, dist + t) - far_bias) * inv_scale
    bown_ref[0:t, 0:t] = near
    bown_ref[0:t, t:blk] = jnp.full((t, t), NEG, F32)
    bown_ref[t:blk, 0:t] = mid
    bown_ref[t:blk, t:blk] = near
    bprev_ref[...] = jnp.zeros((blk, blk), F32)
    bprev_ref[0:t, t:blk] = mid


def _moba_stream(q_ref, k_ref, v_ref, g_ref, o_ref,
                 kaug_ref, qaug_ref, vaug_ref, kmean_ref, bown_ref, bprev_ref):
    blk = MOBA_BLOCK
    seq = q_ref.shape[0]
    n_blocks = seq // blk
    n_chunks = seq // LANES

    def setup():
        _moba_setup(q_ref, k_ref, v_ref, kaug_ref, qaug_ref, vaug_ref, kmean_ref, n_blocks, n_chunks)

    def score_matmul(i):
        return lax.dot_general(qaug_ref[i * blk:(i + 1) * blk, :], kaug_ref[0:(i + 1) * blk, :], _NT,
                               preferred_element_type=F32)

    def finish(i, a):
        rows = slice(i * blk, (i + 1) * blk)
        n = (i + 1) * blk
        parts = [a[:, n - blk:] + bown_ref[...]]
        if i >= 1:
            parts.insert(0, a[:, n - 2 * blk:n - blk] + bprev_ref[...])
        if i >= 2:
            parts.insert(0, a[:, :n - 2 * blk])
        a = parts[0] if len(parts) == 1 else jnp.concatenate(parts, axis=1)
        o_ref[rows, :] = _softmax_pv_folded(a, vaug_ref[0:n, :], g_ref[rows, :])

    return setup, n_blocks, score_matmul, finish


def _moba_setup(q_ref, k_ref, v_ref, kaug_ref, qaug_ref, vaug_ref, kmean_ref, n_blocks, n_chunks):
    blk = MOBA_BLOCK
    seq = q_ref.shape[0]
    kaug_ref[:, :LANES] = k_ref[...]
    vaug_ref[:, :LANES] = v_ref[...]
    vaug_ref[:, LANES:] = jnp.ones((seq, LANES), BF16)
    blk_row = lax.broadcasted_iota(jnp.int32, (SUBLANES, LANES), 0)
    km = jnp.zeros((SUBLANES, LANES), F32)
    for j in range(n_blocks):
        mean_j = jnp.mean(k_ref[j * blk:(j + 1) * blk, :].astype(F32), axis=0, keepdims=True)
        km = jnp.where(blk_row == j, mean_j, km)
    kmean_ref[...] = jnp.concatenate([km, jnp.zeros((LANES - SUBLANES, LANES), F32)], axis=0).astype(BF16)

    gate_t = lax.dot_general(kmean_ref[...], q_ref[...], _NT, preferred_element_type=F32)[0:SUBLANES, :]
    jrow = lax.broadcasted_iota(jnp.int32, gate_t.shape, 0)
    qpos = lax.broadcasted_iota(jnp.int32, gate_t.shape, 1)
    qblk = jnp.right_shift(qpos, int(math.log2(blk)))
    gm = jnp.where(jrow < qblk, gate_t, NEG)
    sel_bias = jnp.zeros(gate_t.shape, F32)
    for j in range(n_blocks):
        gj = gm[j:j + 1, :]
        beats = (gm > gj) | ((gm == gj) & (jrow < j))
        rank = jnp.sum(beats.astype(F32), axis=0, keepdims=True)
        allowed = ((rank < MOBA_TOPK) & (qblk[0:1, :] > j)) | (qblk[0:1, :] == j)
        sel_bias = jnp.where(jrow == j, jnp.where(allowed, 0.0, NEG), sel_bias)
    assert SUBLANES * n_chunks == LANES
    packed_t = jnp.concatenate([sel_bias[:, c * LANES:(c + 1) * LANES] for c in range(n_chunks)], axis=0).T
    lane_chunk = jnp.right_shift(lax.broadcasted_iota(jnp.int32, (LANES, LANES), 1),
                                 int(math.log2(SUBLANES)))
    qaug_ref[:, :LANES] = q_ref[...]
    for c in range(n_chunks):
        qaug_ref[c * LANES:(c + 1) * LANES, LANES:] = jnp.where(lane_chunk == c, packed_t, 0.0).astype(BF16)


def _moba_scratch(seq):
    blk = MOBA_BLOCK
    return [pltpu.VMEM((seq, 2 * LANES), BF16),
            pltpu.VMEM((seq, 2 * LANES), BF16),
            pltpu.VMEM((seq, 2 * LANES), BF16),
            pltpu.VMEM((LANES, LANES), BF16),
            pltpu.VMEM((blk, blk), F32),
            pltpu.VMEM((blk, blk), F32)]


def _fox_stream(h, q_ref, k_ref, v_ref, g_ref, kc_ref, o_ref, qaug_ref, kaug_ref, vaug_ref, causal_ref):
    seq = q_ref.shape[0]

    def setup():
        col = lax.broadcasted_iota(jnp.int32, (seq, LANES), 1)
        qaug_ref[:, :LANES] = q_ref[...]
        qaug_ref[:, LANES:] = jnp.where(((col & (SUBLANES - 1)) == h) & (col < SUBLANES * N_SPLIT),
                                        1.0, 0.0).astype(BF16)
        kaug_ref[:, :LANES] = k_ref[...]
        kaug_ref[:, LANES:] = kc_ref[...]
        vaug_ref[:, :LANES] = v_ref[...]
        vaug_ref[:, LANES:] = jnp.ones((seq, LANES), BF16)
        r = lax.broadcasted_iota(jnp.int32, (Q_TILE, Q_TILE), 0)
        c = lax.broadcasted_iota(jnp.int32, (Q_TILE, Q_TILE), 1)
        causal_ref[...] = jnp.where(c <= r, 0.0, NEG)

    def score_matmul(i):
        return lax.dot_general(qaug_ref[i * Q_TILE:(i + 1) * Q_TILE, :], kaug_ref[0:(i + 1) * Q_TILE, :],
                               _NT, preferred_element_type=F32)

    def finish(i, a):
        rows = slice(i * Q_TILE, (i + 1) * Q_TILE)
        n = (i + 1) * Q_TILE
        own = a[:, n - Q_TILE:] + causal_ref[...]
        a = own if i == 0 else jnp.concatenate([a[:, :n - Q_TILE], own], axis=1)
        o_ref[rows, :] = _softmax_pv_folded(a, vaug_ref[0:n, :], g_ref[rows, :])

    return setup, seq // Q_TILE, score_matmul, finish


def _fox_scratch(seq):
    return [pltpu.VMEM((seq, 2 * LANES), BF16), pltpu.VMEM((seq, 2 * LANES), BF16),
            pltpu.VMEM((seq, 2 * LANES), BF16), pltpu.VMEM((Q_TILE, Q_TILE), F32)]


FOX_LEAD_EVENTS = 11


def _causal_attention_kernel(*refs):
    rb_ref, moba_qkvg, fox_qkvg, kc, wo_ref, oa, ob, wob_ref = refs[:8]
    qa, ka, va, ga = (moba_qkvg.at[i] for i in range(HEAD_GROUP))
    qb, kb, vb, gb = (fox_qkvg.at[i] for i in range(HEAD_GROUP))
    n_moba = len(_moba_scratch(qa.shape[0]))
    moba_scratch, fox_scratch = refs[8:8 + n_moba], refs[8 + n_moba:]
    kaug_ref, _, _, _, bown_ref, bprev_ref = moba_scratch
    h = pl.program_id(0)

    @pl.when(pl.program_id(1) == 0)
    def _():
        _moba_head_constants(rb_ref, h, kaug_ref, bown_ref, bprev_ref)

    wob_ref[...] = wo_ref[...].astype(BF16)
    _run_interleaved([_stream_events(*_fox_stream(h, qb, kb, vb, gb, kc, ob, *fox_scratch)),
                      _stream_events(*_moba_stream(qa, ka, va, ga, oa, *moba_scratch))],
                     lead=FOX_LEAD_EVENTS)


def _causal_attention(p, rel_bias, kc, w_out3, layer, batch, seq):
    nb = seq // MOBA_BLOCK
    assert seq % MOBA_BLOCK == 0 and MOBA_TOPK < SUBLANES and nb <= SUBLANES and MOBA_BLOCK > T5_MAX_DIST
    assert N_HEADS_MOBA == N_HEADS_FOX <= SUBLANES
    dw, d = w_out3.shape[1:]
    n_slabs = dw // W_OUT_CAST_ROWS
    assert dw % W_OUT_CAST_ROWS == 0 and n_slabs <= N_HEADS_MOBA * batch
    slab = lambda h, b: jnp.minimum(h * batch + b, n_slabs - 1)
    head = lambda first: pl.BlockSpec((HEAD_GROUP, seq, LANES), lambda h, b: (first + h, b, 0))
    out = pl.BlockSpec((None, seq, LANES), lambda h, b: (h, b, 0))
    out_sds = jax.ShapeDtypeStruct((N_HEADS_MOBA, batch * seq, LANES), BF16)
    return pl.pallas_call(
        _causal_attention_kernel,
        grid=(N_HEADS_MOBA, batch),
        in_specs=[pl.BlockSpec(memory_space=pltpu.SMEM), head(0), head(N_HEADS_MOBA),
                  pl.BlockSpec((None, seq, LANES), lambda h, b: (b, 0, 0)),
                  pl.BlockSpec((None, W_OUT_CAST_ROWS, d), lambda h, b: (layer, slab(h, b), 0))],
        out_specs=[out, out, pl.BlockSpec((W_OUT_CAST_ROWS, d), lambda h, b: (slab(h, b), 0))],
        out_shape=[out_sds, out_sds, jax.ShapeDtypeStruct((dw, d), BF16)],
        scratch_shapes=_moba_scratch(seq) + _fox_scratch(seq),
        compiler_params=_cparams(2),
        name="causal_attention",
    )(rel_bias, p, p, kc, w_out3)


def _mem_kernel(b_ref, q_ref, k_ref, v_ref, g_ref, f_ref, o_ref, kc_ref, vaug_ref, *, layer, n_fox_heads):
    n_heads, seq, _ = q_ref.shape
    tiles_per_head = seq // MEM_Q_TILE

    _forget_bias_columns(f_ref, b_ref, layer, kc_ref, n_fox_heads)
    vaug_ref[:, :, LANES:] = jnp.ones(v_ref.shape, BF16)
    vaug_ref[:, :, :LANES] = v_ref[...]

    def tile(t):
        h, i = divmod(t, tiles_per_head)
        return h, slice(i * MEM_Q_TILE, (i + 1) * MEM_Q_TILE)

    def score_matmul(t):
        h, rows = tile(t)
        return lax.dot_general(q_ref[h, rows, :], k_ref[h], _NT, preferred_element_type=F32)

    def finish(t, a):
        h, rows = tile(t)
        o_ref[h, rows, :] = _softmax_pv_folded(a, vaug_ref[h], g_ref[h, rows, :])

    _run_interleaved([_stream_events(lambda: None, n_heads * tiles_per_head, score_matmul, finish)])


def _mem_attention(p, mkv, fb, b_forget, layer, batch, seq, n_mem, q0, g0):
    nh = N_HEADS_MEM
    assert q0 % nh == 0 and g0 % nh == 0 and N_HEADS_FOX <= SUBLANES
    heads = lambda off: pl.BlockSpec((nh, seq, LANES), lambda b: (off // nh, b, 0))
    kv = lambda blk: pl.BlockSpec((nh, n_mem, LANES), lambda b: (blk, b, 0))
    return pl.pallas_call(
        functools.partial(_mem_kernel, layer=layer, n_fox_heads=N_HEADS_FOX),
        grid=(batch,),
        in_specs=[pl.BlockSpec(memory_space=pltpu.SMEM),
                  heads(q0), kv(0), kv(1), heads(g0),
                  pl.BlockSpec((seq, LANES), lambda b: (b, 0))],
        out_specs=[pl.BlockSpec((nh, seq, LANES), lambda b: (0, b, 0)),
                   pl.BlockSpec((None, seq, LANES), lambda b: (b, 0, 0))],
        out_shape=[jax.ShapeDtypeStruct((nh, batch * seq, LANES), BF16),
                   jax.ShapeDtypeStruct((batch, seq, LANES), BF16)],
        scratch_shapes=[pltpu.VMEM((nh, n_mem, 2 * LANES), BF16)],
        compiler_params=_cparams(1),
        name="mem_attention",
    )(b_forget, p, mkv, mkv, p, fb)


def _out_proj_kernel(ya_ref, yb_ref, ym_ref, w_ref, x_ref, g_ref, o_ref):
    y = jnp.concatenate([src[c] for src in (ya_ref, yb_ref, ym_ref) for c in range(src.shape[0])], axis=1)
    o = jnp.dot(y, w_ref[...], preferred_element_type=F32)
    ms = jnp.mean(o * o, axis=-1, keepdims=True)
    o_ref[...] = x_ref[...] + o * lax.rsqrt(ms + EPS) * g_ref[...]


def _out_proj(ya, yb, ym, w, x2, g, *, tm):
    m, d = x2.shape
    heads = lambda y: pl.BlockSpec((y.shape[0], tm, LANES), lambda i: (0, i, 0))
    return pl.pallas_call(
        _out_proj_kernel,
        grid=(m // tm,),
        in_specs=[heads(ya), heads(yb), heads(ym),
                  pl.BlockSpec(w.shape, lambda i: (0, 0), pipeline_mode=pl.Buffered(1)),
                  pl.BlockSpec((tm, d), lambda i: (i, 0)),
                  pl.BlockSpec((1, d), lambda i: (0, 0))],
        out_specs=pl.BlockSpec((tm, d), lambda i: (i, 0)),
        out_shape=jax.ShapeDtypeStruct((m, d), F32),
        compiler_params=_cparams(1),
        name="out_proj",
    )(ya, yb, ym, w, x2, g)


def kernel(x, mem, w_in, b_forget, w_mem_kv, w_out, g_pre, g_post, g_mem, rel_bias):
    batch, seq, d = x.shape
    n_mem = mem.shape[1]
    depth = w_in.shape[0]
    assert N_HEADS_MOBA == N_HEADS_FOX
    n_causal = N_HEADS_MOBA + N_HEADS_FOX
    wm = N_HEADS_MEM * HEAD_DIM
    n_in = w_in.shape[2]
    assert n_in == n_causal * HEAD_GROUP * HEAD_DIM + N_HEADS_FOX + 2 * wm
    qm0 = n_causal * HEAD_GROUP
    gm0 = qm0 + N_HEADS_MEM

    x2 = x.reshape(batch * seq, d)
    mem2 = mem.reshape(batch * n_mem, d)
    for layer in range(depth):
        w_t3 = jnp.transpose(w_in, (0, 2, 1)).reshape(depth * n_in, d // LANES, LANES)
        w_main_t, w_gate_t, mkv = _w_repack_mem_proj(w_t3, layer, n_in, N_HEADS_MOBA, N_HEADS_FOX, mem2,
                                                     g_mem[layer].reshape(1, d), w_mem_kv)
        p, fb = _norm_proj(x2, g_pre[layer].reshape(1, d), w_main_t, w_gate_t,
                           tm=IN_PROJ_TILE[0], tn=IN_PROJ_TILE[1])
        ym, kc = _mem_attention(p, mkv, fb, b_forget, layer, batch, seq, n_mem, qm0, gm0)
        ya, yb, w_out_bf = _causal_attention(p, rel_bias, kc, w_out, layer, batch, seq)
        x2 = _out_proj(ya, yb, ym, w_out_bf, x2, g_post[layer].reshape(1, d), tm=OUT_PROJ_ROWS)
    return x2.reshape(batch, seq, d)
```

```python
import functools
import math

import jax
import jax.numpy as jnp
from jax import lax
from jax.experimental import pallas as pl
from jax.experimental.pallas import tpu as pltpu

HEAD_DIM = 128
N_HEADS_MOBA = 6
N_HEADS_FOX = 6
N_HEADS_MEM = 4
MOBA_BLOCK = 256
MOBA_TOPK = 3
T5_BUCKETS = 32
T5_MAX_DIST = 128
EPS = 1e-6
NEG = -1e30

LANES = 128
SUBLANES = 8
VMEM_LIMIT_BYTES = 56 * 1024 * 1024

IN_PROJ_TILE = (1024, 1792)
NORM_ROW_CHUNK = 256
OUT_PROJ_ROWS = 512
HEAD_GROUP = 4
W_OUT_CAST_ROWS = 128
Q_TILE = MOBA_BLOCK
MEM_Q_TILE = 1024

F32 = jnp.float32
BF16 = jnp.bfloat16
_NT = (((1,), (1,)), ((), ()))


def _cparams(n_axes):
    return pltpu.CompilerParams(dimension_semantics=("arbitrary",) * n_axes,
                                vmem_limit_bytes=VMEM_LIMIT_BYTES)


def _norm_proj_kernel(x_ref, g_ref, w_ref, wf_ref, p_ref, f_ref, h_ref, *, n_col_blocks, row_chunk):
    tm = x_ref.shape[0]
    j = pl.program_id(1)

    def project(rows):
        acc = lax.dot_general(h_ref[rows, :], w_ref[...], _NT, preferred_element_type=F32)
        for c in range(n_col_blocks):
            p_ref[c, rows, :] = acc[:, c * LANES:(c + 1) * LANES].astype(BF16)

    @pl.when(j == 0)
    def _normalize_and_project():
        for c in range(tm // row_chunk):
            rows = slice(c * row_chunk, (c + 1) * row_chunk)
            xf = x_ref[rows, :]
            ms = jnp.mean(xf * xf, axis=-1, keepdims=True)
            h = (xf * lax.rsqrt(ms + EPS) * g_ref[...]).astype(BF16)
            h_ref[rows, :] = h
            f_ref[rows, :] = lax.dot_general(h, wf_ref[...], _NT, preferred_element_type=F32)
            project(rows)

    @pl.when(j > 0)
    def _project():
        project(slice(0, tm))


def _norm_proj(x2, g, w_t, wf_t, *, tm, tn):
    m, d = x2.shape
    n = w_t.shape[0]
    ncb = tn // LANES
    assert n % tn == 0 and m % tm == 0
    return pl.pallas_call(
        functools.partial(_norm_proj_kernel, n_col_blocks=ncb, row_chunk=NORM_ROW_CHUNK),
        grid=(m // tm, n // tn),
        in_specs=[pl.BlockSpec((tm, d), lambda i, j: (i, 0)),
                  pl.BlockSpec((1, d), lambda i, j: (0, 0)),
                  pl.BlockSpec((tn, d), lambda i, j: (j, 0)),
                  pl.BlockSpec((LANES, d), lambda i, j: (0, 0))],
        out_specs=[pl.BlockSpec((ncb, tm, LANES), lambda i, j: (j, i, 0)),
                   pl.BlockSpec((tm, LANES), lambda i, j: (i, 0))],
        out_shape=[jax.ShapeDtypeStruct((n // LANES, m, LANES), BF16),
                   jax.ShapeDtypeStruct((m, LANES), F32)],
        scratch_shapes=[pltpu.VMEM((tm, d), BF16)],
        compiler_params=_cparams(2),
        name="norm_proj_gate",
    )(x2, g, w_t, wf_t)


def _w_repack_kernel(src_ref, mem_ref, gm_ref, wm_ref, main_ref, gate_ref, mkv_ref,
                     buf_ref, gbuf_ref, hm_ref, sem_ref, gsem_ref, *,
                     base_row, n_heads, n_aligned, n_gate, n_mem_tiles, row_chunk):
    t = pl.program_id(0)
    n_steps = pl.num_programs(0)
    n_slices = src_ref.shape[1]

    def source_row(step, g):
        head_blk = (step // n_heads) * (HEAD_GROUP * n_heads) + g * n_heads + step % n_heads
        blk = jnp.where(step < 2 * n_heads, head_blk, HEAD_GROUP * step + g)
        return base_row + blk * LANES + jnp.where(blk * LANES >= n_aligned, n_gate, 0)

    def block_copies(step, slot):
        return [pltpu.make_async_copy(src_ref.at[pl.ds(source_row(step, g), LANES), s, :],
                                      buf_ref.at[slot, pl.ds(g * LANES, LANES), pl.ds(s * LANES, LANES)],
                                      sem_ref.at[slot])
                for g in range(HEAD_GROUP) for s in range(n_slices)]

    def gate_copies():
        gate_row0 = base_row + n_aligned
        return [pltpu.make_async_copy(src_ref.at[pl.ds(gate_row0, SUBLANES), s, :],
                                      gbuf_ref.at[:, pl.ds(s * LANES, LANES)],
                                      gsem_ref.at[0]) for s in range(n_slices)]

    @pl.when(t == 0)
    def _first():
        for cp in gate_copies() + block_copies(0, 0):
            cp.start()

    @pl.when(t + 1 < n_steps)
    def _prefetch():
        for cp in block_copies(t + 1, (t + 1) % 2):
            cp.start()

    @pl.when(t == 0)
    def _gate_rows_and_mem_norm():
        for cp in gate_copies():
            cp.wait()
        row = lax.broadcasted_iota(jnp.int32, gbuf_ref.shape, 0)
        gate = jnp.where(row < n_gate, gbuf_ref[...], 0.0)
        gate_ref[...] = jnp.concatenate(
            [gate, jnp.zeros((gate_ref.shape[0] - SUBLANES, gate.shape[1]), F32)], axis=0).astype(BF16)
        for c in range(mem_ref.shape[0] // row_chunk):
            r = slice(c * row_chunk, (c + 1) * row_chunk)
            xf = mem_ref[r, :]
            ms = jnp.mean(xf * xf, axis=-1, keepdims=True)
            hm_ref[r, :] = (xf * lax.rsqrt(ms + EPS) * gm_ref[...]).astype(BF16)

    @pl.when(t < n_mem_tiles)
    def _mem_projection():
        mkv_ref[...] = jnp.dot(hm_ref[...], wm_ref[...].astype(BF16),
                               preferred_element_type=F32).astype(BF16)

    for cp in block_copies(t, t % 2):
        cp.wait()
    main_ref[...] = buf_ref[t % 2].astype(BF16)


def _w_repack_mem_proj(w_t3, layer, n_total, n_heads, n_gate, mem2, g_mem, w_mem3):
    n_slices = w_t3.shape[1]
    d = n_slices * LANES
    rows = HEAD_GROUP * LANES
    n_aligned = 2 * n_heads * rows
    n_main = n_total - n_gate
    m_mem = mem2.shape[0]
    n_mem_tiles = w_mem3.shape[2] // LANES
    n_steps = n_main // rows
    assert n_main % rows == 0 and n_aligned <= n_main and n_gate <= SUBLANES
    assert w_mem3.shape[2] % LANES == 0 and n_mem_tiles <= n_steps
    mem_tile = lambda t: jnp.minimum(t, n_mem_tiles - 1)
    return pl.pallas_call(
        functools.partial(_w_repack_kernel, base_row=layer * n_total, n_heads=n_heads,
                          n_aligned=n_aligned, n_gate=n_gate,
                          n_mem_tiles=n_mem_tiles, row_chunk=NORM_ROW_CHUNK),
        grid=(n_steps,),
        in_specs=[pl.BlockSpec(memory_space=pl.ANY),
                  pl.BlockSpec((m_mem, d), lambda t: (0, 0), pipeline_mode=pl.Buffered(1)),
                  pl.BlockSpec((1, d), lambda t: (0, 0)),
                  pl.BlockSpec((None, d, LANES), lambda t: (layer, 0, mem_tile(t)))],
        out_specs=[pl.BlockSpec((rows, d), lambda t: (t, 0)),
                   pl.BlockSpec((LANES, d), lambda t: (0, 0)),
                   pl.BlockSpec((None, m_mem, LANES), lambda t: (mem_tile(t), 0, 0))],
        out_shape=[jax.ShapeDtypeStruct((n_main, d), BF16),
                   jax.ShapeDtypeStruct((LANES, d), BF16),
                   jax.ShapeDtypeStruct((n_mem_tiles, m_mem, LANES), BF16)],
        scratch_shapes=[pltpu.VMEM((2, rows, d), F32), pltpu.VMEM((SUBLANES, d), F32),
                        pltpu.VMEM((m_mem, d), BF16),
                        pltpu.SemaphoreType.DMA((2,)), pltpu.SemaphoreType.DMA((1,))],
        compiler_params=_cparams(1),
        name="w_repack_mem_proj",
    )(w_t3, mem2, g_mem, w_mem3)


N_SPLIT = 3


def _split3(v):
    hi = v.astype(BF16).astype(F32)
    mid = (v - hi).astype(BF16).astype(F32)
    lo = (v - hi - mid).astype(BF16).astype(F32)
    return hi, mid, lo


def _forget_bias_columns(f_ref, b_ref, layer, kc_ref, n_heads):
    head_lane = lax.broadcasted_iota(jnp.int32, (1, LANES), 1)
    bias = jnp.zeros((1, LANES), F32)
    for hh in range(n_heads):
        bias = jnp.where(head_lane == hh, b_ref[layer, hh], bias)
    z = f_ref[...] + bias
    logf = jnp.minimum(z, 0.0) - jnp.log1p(jnp.exp(-jnp.abs(z)))
    x = logf.T[0:SUBLANES, :]
    s_len = x.shape[1]
    lane = lax.broadcasted_iota(jnp.int32, x.shape, 1)
    shift = 1
    while shift < s_len:
        x = x + jnp.where(lane >= shift, pltpu.roll(x, shift, 1), 0.0)
        shift *= 2
    row = lax.broadcasted_iota(jnp.int32, x.shape, 0)
    x = jnp.where(row < n_heads, x, 0.0)
    terms = _split3(x * (-1.0 / HEAD_DIM ** -0.5))
    pad = jnp.zeros((LANES - SUBLANES * N_SPLIT, s_len), F32)
    kc_ref[...] = jnp.concatenate(list(terms) + [pad], axis=0).T.astype(BF16)


LOG2E = math.log2(math.e)


def _softmax_pv_folded(a, v_aug, g):
    m = jnp.max(a, axis=1, keepdims=True)
    p = jnp.exp2((a - m) * (HEAD_DIM ** -0.5 * LOG2E)).astype(BF16)
    acc = jnp.dot(p, v_aug, preferred_element_type=F32)
    gf = g.astype(F32)
    silu = gf * (1.0 / (1.0 + jnp.exp(-gf)))
    return (acc[:, :LANES] / acc[:, LANES:] * silu).astype(BF16)


SCORE_AHEAD = 1


def _stream_events(setup, n_tiles, score_matmul, finish):
    scores = {}

    def score(i):
        return lambda: scores.__setitem__(i, score_matmul(i))

    def fin(i):
        return lambda: finish(i, scores.pop(i))

    events = [setup] + [score(i) for i in range(min(SCORE_AHEAD, n_tiles))]
    for i in range(n_tiles):
        if i + SCORE_AHEAD < n_tiles:
            events.append(score(i + SCORE_AHEAD))
        events.append(fin(i))
    return events


def _run_interleaved(streams, lead=0):
    queues = [list(ev) for ev in streams]
    for _ in range(min(lead, len(queues[0]))):
        queues[0].pop(0)()
    while any(queues):
        for q in queues:
            if q:
                q.pop(0)()


def _t5_bias(rb_ref, h, dist):
    max_exact = T5_BUCKETS // 2
    n = jnp.maximum(dist, 0)
    nf = jnp.maximum(n, 1).astype(F32)
    val = jnp.log(nf / max_exact) / math.log(T5_MAX_DIST / max_exact) * (T5_BUCKETS - max_exact)
    bias = jnp.zeros(dist.shape, F32)
    for kk in range(T5_BUCKETS):
        if kk < max_exact:
            hit = n == kk
        elif kk < T5_BUCKETS - 1:
            hit = (n >= max_exact) & (val >= kk - max_exact) & (val < kk - max_exact + 1)
        else:
            hit = (n >= max_exact) & (val >= kk - max_exact)
        bias = jnp.where(hit, rb_ref[h, kk], bias)
    return bias


def _moba_head_constants(rb_ref, h, kaug_ref, bown_ref, bprev_ref):
    inv_scale = 1.0 / HEAD_DIM ** -0.5
    blk = MOBA_BLOCK
    seq = kaug_ref.shape[0]
    far_bias = rb_ref[h, T5_BUCKETS - 1]
    srow = lax.broadcasted_iota(jnp.int32, (seq, LANES), 0)
    col = lax.broadcasted_iota(jnp.int32, (seq, LANES), 1)
    kaug_ref[:, LANES:] = jnp.where((col & (SUBLANES - 1)) * blk == srow - (srow & (blk - 1)),
                                    1.0, 0.0).astype(BF16)
    t = T5_MAX_DIST
    assert blk == 2 * t
    dist = lax.broadcasted_iota(jnp.int32, (t, t), 0) - lax.broadcasted_iota(jnp.int32, (t, t), 1)
    near = jnp.where(dist >= 0, (_t5_bias(rb_ref, h, dist) - far_bias) * inv_scale, NEG)
    mid = (_t5_bias(rb_ref, h, dist + t) - far_bias) * inv_scale
    bown_ref[0:t, 0:t] = near
    bown_ref[0:t, t:blk] = jnp.full((t, t), NEG, F32)
    bown_ref[t:blk, 0:t] = mid
    bown_ref[t:blk, t:blk] = near
    bprev_ref[...] = jnp.zeros((blk, blk), F32)
    bprev_ref[0:t, t:blk] = mid


def _moba_stream(q_ref, k_ref, v_ref, g_ref, o_ref,
                 kaug_ref, qaug_ref, vaug_ref, kmean_ref, bown_ref, bprev_ref):
    blk = MOBA_BLOCK
    seq = q_ref.shape[0]
    n_blocks = seq // blk
    n_chunks = seq // LANES

    def setup():
        _moba_setup(q_ref, k_ref, v_ref, kaug_ref, qaug_ref, vaug_ref, kmean_ref, n_blocks, n_chunks)

    def score_matmul(i):
        return lax.dot_general(qaug_ref[i * blk:(i + 1) * blk, :], kaug_ref[0:(i + 1) * blk, :], _NT,
                               preferred_element_type=F32)

    def finish(i, a):
        rows = slice(i * blk, (i + 1) * blk)
        n = (i + 1) * blk
        parts = [a[:, n - blk:] + bown_ref[...]]
        if i >= 1:
            parts.insert(0, a[:, n - 2 * blk:n - blk] + bprev_ref[...])
        if i >= 2:
            parts.insert(0, a[:, :n - 2 * blk])
        a = parts[0] if len(parts) == 1 else jnp.concatenate(parts, axis=1)
        o_ref[rows, :] = _softmax_pv_folded(a, vaug_ref[0:n, :], g_ref[rows, :])

    return setup, n_blocks, score_matmul, finish


def _moba_setup(q_ref, k_ref, v_ref, kaug_ref, qaug_ref, vaug_ref, kmean_ref, n_blocks, n_chunks):
    blk = MOBA_BLOCK
    seq = q_ref.shape[0]
    kaug_ref[:, :LANES] = k_ref[...]
    vaug_ref[:, :LANES] = v_ref[...]
    vaug_ref[:, LANES:] = jnp.ones((seq, LANES), BF16)
    blk_row = lax.broadcasted_iota(jnp.int32, (SUBLANES, LANES), 0)
    km = jnp.zeros((SUBLANES, LANES), F32)
    for j in range(n_blocks):
        mean_j = jnp.mean(k_ref[j * blk:(j + 1) * blk, :].astype(F32), axis=0, keepdims=True)
        km = jnp.where(blk_row == j, mean_j, km)
    kmean_ref[...] = jnp.concatenate([km, jnp.zeros((LANES - SUBLANES, LANES), F32)], axis=0).astype(BF16)

    gate_t = lax.dot_general(kmean_ref[...], q_ref[...], _NT, preferred_element_type=F32)[0:SUBLANES, :]
    jrow = lax.broadcasted_iota(jnp.int32, gate_t.shape, 0)
    qpos = lax.broadcasted_iota(jnp.int32, gate_t.shape, 1)
    qblk = jnp.right_shift(qpos, int(math.log2(blk)))
    gm = jnp.where(jrow < qblk, gate_t, NEG)
    sel_bias = jnp.zeros(gate_t.shape, F32)
    for j in range(n_blocks):
        gj = gm[j:j + 1, :]
        beats = (gm > gj) | ((gm == gj) & (jrow < j))
        rank = jnp.sum(beats.astype(F32), axis=0, keepdims=True)
        allowed = ((rank < MOBA_TOPK) & (qblk[0:1, :] > j)) | (qblk[0:1, :] == j)
        sel_bias = jnp.where(jrow == j, jnp.where(allowed, 0.0, NEG), sel_bias)
    assert SUBLANES * n_chunks == LANES
    packed_t = jnp.concatenate([sel_bias[:, c * LANES:(c + 1) * LANES] for c in range(n_chunks)], axis=0).T
    lane_chunk = jnp.right_shift(lax.broadcasted_iota(jnp.int32, (LANES, LANES), 1),
                                 int(math.log2(SUBLANES)))
    qaug_ref[:, :LANES] = q_ref[...]
    for c in range(n_chunks):
        qaug_ref[c * LANES:(c + 1) * LANES, LANES:] = jnp.where(lane_chunk == c, packed_t, 0.0).astype(BF16)


def _moba_scratch(seq):
    blk = MOBA_BLOCK
    return [pltpu.VMEM((seq, 2 * LANES), BF16),
            pltpu.VMEM((seq, 2 * LANES), BF16),
            pltpu.VMEM((seq, 2 * LANES), BF16),
            pltpu.VMEM((LANES, LANES), BF16),
            pltpu.VMEM((blk, blk), F32),
            pltpu.VMEM((blk, blk), F32)]


def _fox_stream(h, q_ref, k_ref, v_ref, g_ref, kc_ref, o_ref, qaug_ref, kaug_ref, vaug_ref, causal_ref):
    seq = q_ref.shape[0]

    def setup():
        col = lax.broadcasted_iota(jnp.int32, (seq, LANES), 1)
        qaug_ref[:, :LANES] = q_ref[...]
        qaug_ref[:, LANES:] = jnp.where(((col & (SUBLANES - 1)) == h) & (col < SUBLANES * N_SPLIT),
                                        1.0, 0.0).astype(BF16)
        kaug_ref[:, :LANES] = k_ref[...]
        kaug_ref[:, LANES:] = kc_ref[...]
        vaug_ref[:, :LANES] = v_ref[...]
        vaug_ref[:, LANES:] = jnp.ones((seq, LANES), BF16)
        r = lax.broadcasted_iota(jnp.int32, (Q_TILE, Q_TILE), 0)
        c = lax.broadcasted_iota(jnp.int32, (Q_TILE, Q_TILE), 1)
        causal_ref[...] = jnp.where(c <= r, 0.0, NEG)

    def score_matmul(i):
        return lax.dot_general(qaug_ref[i * Q_TILE:(i + 1) * Q_TILE, :], kaug_ref[0:(i + 1) * Q_TILE, :],
                               _NT, preferred_element_type=F32)

    def finish(i, a):
        rows = slice(i * Q_TILE, (i + 1) * Q_TILE)
        n = (i + 1) * Q_TILE
        own = a[:, n - Q_TILE:] + causal_ref[...]
        a = own if i == 0 else jnp.concatenate([a[:, :n - Q_TILE], own], axis=1)
        o_ref[rows, :] = _softmax_pv_folded(a, vaug_ref[0:n, :], g_ref[rows, :])

    return setup, seq // Q_TILE, score_matmul, finish


def _fox_scratch(seq):
    return [pltpu.VMEM((seq, 2 * LANES), BF16), pltpu.VMEM((seq, 2 * LANES), BF16),
            pltpu.VMEM((seq, 2 * LANES), BF16), pltpu.VMEM((Q_TILE, Q_TILE), F32)]


FOX_LEAD_EVENTS = 11


def _causal_attention_kernel(*refs):
    rb_ref, moba_qkvg, fox_qkvg, kc, wo_ref, oa, ob, wob_ref = refs[:8]
    qa, ka, va, ga = (moba_qkvg.at[i] for i in range(HEAD_GROUP))
    qb, kb, vb, gb = (fox_qkvg.at[i] for i in range(HEAD_GROUP))
    n_moba = len(_moba_scratch(qa.shape[0]))
    moba_scratch, fox_scratch = refs[8:8 + n_moba], refs[8 + n_moba:]
    kaug_ref, _, _, _, bown_ref, bprev_ref = moba_scratch
    h = pl.program_id(0)

    @pl.when(pl.program_id(1) == 0)
    def _():
        _moba_head_constants(rb_ref, h, kaug_ref, bown_ref, bprev_ref)

    wob_ref[...] = wo_ref[...].astype(BF16)
    _run_interleaved([_stream_events(*_fox_stream(h, qb, kb, vb, gb, kc, ob, *fox_scratch)),
                      _stream_events(*_moba_stream(qa, ka, va, ga, oa, *moba_scratch))],
                     lead=FOX_LEAD_EVENTS)


def _causal_attention(p, rel_bias, kc, w_out3, layer, batch, seq):
    nb = seq // MOBA_BLOCK
    assert seq % MOBA_BLOCK == 0 and MOBA_TOPK < SUBLANES and nb <= SUBLANES and MOBA_BLOCK > T5_MAX_DIST
    assert N_HEADS_MOBA == N_HEADS_FOX <= SUBLANES
    dw, d = w_out3.shape[1:]
    n_slabs = dw // W_OUT_CAST_ROWS
    assert dw % W_OUT_CAST_ROWS == 0 and n_slabs <= N_HEADS_MOBA * batch
    slab = lambda h, b: jnp.minimum(h * batch + b, n_slabs - 1)
    head = lambda first: pl.BlockSpec((HEAD_GROUP, seq, LANES), lambda h, b: (first + h, b, 0))
    out = pl.BlockSpec((None, seq, LANES), lambda h, b: (h, b, 0))
    out_sds = jax.ShapeDtypeStruct((N_HEADS_MOBA, batch * seq, LANES), BF16)
    return pl.pallas_call(
        _causal_attention_kernel,
        grid=(N_HEADS_MOBA, batch),
        in_specs=[pl.BlockSpec(memory_space=pltpu.SMEM), head(0), head(N_HEADS_MOBA),
                  pl.BlockSpec((None, seq, LANES), lambda h, b: (b, 0, 0)),
                  pl.BlockSpec((None, W_OUT_CAST_ROWS, d), lambda h, b: (layer, slab(h, b), 0))],
        out_specs=[out, out, pl.BlockSpec((W_OUT_CAST_ROWS, d), lambda h, b: (slab(h, b), 0))],
        out_shape=[out_sds, out_sds, jax.ShapeDtypeStruct((dw, d), BF16)],
        scratch_shapes=_moba_scratch(seq) + _fox_scratch(seq),
        compiler_params=_cparams(2),
        name="causal_attention",
    )(rel_bias, p, p, kc, w_out3)


def _mem_kernel(b_ref, q_ref, k_ref, v_ref, g_ref, f_ref, o_ref, kc_ref, vaug_ref, *, layer, n_fox_heads):
    n_heads, seq, _ = q_ref.shape
    tiles_per_head = seq // MEM_Q_TILE

    _forget_bias_columns(f_ref, b_ref, layer, kc_ref, n_fox_heads)
    vaug_ref[:, :, LANES:] = jnp.ones(v_ref.shape, BF16)
    vaug_ref[:, :, :LANES] = v_ref[...]

    def tile(t):
        h, i = divmod(t, tiles_per_head)
        return h, slice(i * MEM_Q_TILE, (i + 1) * MEM_Q_TILE)

    def score_matmul(t):
        h, rows = tile(t)
        return lax.dot_general(q_ref[h, rows, :], k_ref[h], _NT, preferred_element_type=F32)

    def finish(t, a):
        h, rows = tile(t)
        o_ref[h, rows, :] = _softmax_pv_folded(a, vaug_ref[h], g_ref[h, rows, :])

    _run_interleaved([_stream_events(lambda: None, n_heads * tiles_per_head, score_matmul, finish)])


def _mem_attention(p, mkv, fb, b_forget, layer, batch, seq, n_mem, q0, g0):
    nh = N_HEADS_MEM
    assert q0 % nh == 0 and g0 % nh == 0 and N_HEADS_FOX <= SUBLANES
    heads = lambda off: pl.BlockSpec((nh, seq, LANES), lambda b: (off // nh, b, 0))
    kv = lambda blk: pl.BlockSpec((nh, n_mem, LANES), lambda b: (blk, b, 0))
    return pl.pallas_call(
        functools.partial(_mem_kernel, layer=layer, n_fox_heads=N_HEADS_FOX),
        grid=(batch,),
        in_specs=[pl.BlockSpec(memory_space=pltpu.SMEM),
                  heads(q0), kv(0), kv(1), heads(g0),
                  pl.BlockSpec((seq, LANES), lambda b: (b, 0))],
        out_specs=[pl.BlockSpec((nh, seq, LANES), lambda b: (0, b, 0)),
                   pl.BlockSpec((None, seq, LANES), lambda b: (b, 0, 0))],
        out_shape=[jax.ShapeDtypeStruct((nh, batch * seq, LANES), BF16),
                   jax.ShapeDtypeStruct((batch, seq, LANES), BF16)],
        scratch_shapes=[pltpu.VMEM((nh, n_mem, 2 * LANES), BF16)],
        compiler_params=_cparams(1),
        name="mem_attention",
    )(b_forget, p, mkv, mkv, p, fb)


def _out_proj_kernel(ya_ref, yb_ref, ym_ref, w_ref, x_ref, g_ref, o_ref):
    y = jnp.concatenate([src[c] for src in (ya_ref, yb_ref, ym_ref) for c in range(src.shape[0])], axis=1)
    o = jnp.dot(y, w_ref[...], preferred_element_type=F32)
    ms = jnp.mean(o * o, axis=-1, keepdims=True)
    o_ref[...] = x_ref[...] + o * lax.rsqrt(ms + EPS) * g_ref[...]


def _out_proj(ya, yb, ym, w, x2, g, *, tm):
    m, d = x2.shape
    heads = lambda y: pl.BlockSpec((y.shape[0], tm, LANES), lambda i: (0, i, 0))
    return pl.pallas_call(
        _out_proj_kernel,
        grid=(m // tm,),
        in_specs=[heads(ya), heads(yb), heads(ym),
                  pl.BlockSpec(w.shape, lambda i: (0, 0), pipeline_mode=pl.Buffered(1)),
                  pl.BlockSpec((tm, d), lambda i: (i, 0)),
                  pl.BlockSpec((1, d), lambda i: (0, 0))],
        out_specs=pl.BlockSpec((tm, d), lambda i: (i, 0)),
        out_shape=jax.ShapeDtypeStruct((m, d), F32),
        compiler_params=_cparams(1),
        name="out_proj",
    )(ya, yb, ym, w, x2, g)


def kernel(x, mem, w_in, b_forget, w_mem_kv, w_out, g_pre, g_post, g_mem, rel_bias):
    batch, seq, d = x.shape
    n_mem = mem.shape[1]
    depth = w_in.shape[0]
    assert N_HEADS_MOBA == N_HEADS_FOX
    n_causal = N_HEADS_MOBA + N_HEADS_FOX
    wm = N_HEADS_MEM * HEAD_DIM
    n_in = w_in.shape[2]
    assert n_in == n_causal * HEAD_GROUP * HEAD_DIM + N_HEADS_FOX + 2 * wm
    qm0 = n_causal * HEAD_GROUP
    gm0 = qm0 + N_HEADS_MEM

    x2 = x.reshape(batch * seq, d)
    mem2 = mem.reshape(batch * n_mem, d)
    for layer in range(depth):
        w_t3 = jnp.transpose(w_in, (0, 2, 1)).reshape(depth * n_in, d // LANES, LANES)
        w_main_t, w_gate_t, mkv = _w_repack_mem_proj(w_t3, layer, n_in, N_HEADS_MOBA, N_HEADS_FOX, mem2,
                                                     g_mem[layer].reshape(1, d), w_mem_kv)
        p, fb = _norm_proj(x2, g_pre[layer].reshape(1, d), w_main_t, w_gate_t,
                           tm=IN_PROJ_TILE[0], tn=IN_PROJ_TILE[1])
        ym, kc = _mem_attention(p, mkv, fb, b_forget, layer, batch, seq, n_mem, qm0, gm0)
        ya, yb, w_out_bf = _causal_attention(p, rel_bias, kc, w_out, layer, batch, seq)
        x2 = _out_proj(ya, yb, ym, w_out_bf, x2, g_post[layer].reshape(1, d), tm=OUT_PROJ_ROWS)
    return x2.reshape(batch, seq, d)
```

```python
import functools
import math

import jax
import jax.numpy as jnp
from jax import lax
from jax.experimental import pallas as pl
from jax.experimental.pallas import tpu as pltpu

HEAD_DIM = 128
N_HEADS_MOBA = 6
N_HEADS_FOX = 6
N_HEADS_MEM = 4
MOBA_BLOCK = 256
MOBA_TOPK = 3
T5_BUCKETS = 32
T5_MAX_DIST = 128
EPS = 1e-6
NEG = -1e30

LANES = 128
SUBLANES = 8
VMEM_LIMIT_BYTES = 56 * 1024 * 1024

IN_PROJ_TILE = (1024, 1792)
NORM_ROW_CHUNK = 256
OUT_PROJ_ROWS = 1024
HEAD_GROUP = 4
W_OUT_CAST_ROWS = 128
Q_TILE = MOBA_BLOCK
MEM_Q_TILE = 1024

F32 = jnp.float32
BF16 = jnp.bfloat16
_NT = (((1,), (1,)), ((), ()))


def _cparams(n_axes):
    return pltpu.CompilerParams(dimension_semantics=("arbitrary",) * n_axes,
                                vmem_limit_bytes=VMEM_LIMIT_BYTES)


def _norm_proj_kernel(x_ref, g_ref, w_ref, wf_ref, p_ref, f_ref, h_ref, *, n_col_blocks, row_chunk):
    tm = x_ref.shape[0]
    j = pl.program_id(1)

    def project(rows):
        acc = lax.dot_general(h_ref[rows, :], w_ref[...], _NT, preferred_element_type=F32)
        for c in range(n_col_blocks):
            p_ref[c, rows, :] = acc[:, c * LANES:(c + 1) * LANES].astype(BF16)

    @pl.when(j == 0)
    def _normalize_and_project():
        for c in range(tm // row_chunk):
            rows = slice(c * row_chunk, (c + 1) * row_chunk)
            xf = x_ref[rows, :]
            ms = jnp.mean(xf * xf, axis=-1, keepdims=True)
            h = (xf * lax.rsqrt(ms + EPS) * g_ref[...]).astype(BF16)
            h_ref[rows, :] = h
            f_ref[rows, :] = lax.dot_general(h, wf_ref[...], _NT, preferred_element_type=F32)
            project(rows)

    @pl.when(j > 0)
    def _project():
        project(slice(0, tm))


def _norm_proj(x2, g, w_t, wf_t, *, tm, tn):
    m, d = x2.shape
    n = w_t.shape[0]
    ncb = tn // LANES
    assert n % tn == 0 and m % tm == 0
    return pl.pallas_call(
        functools.partial(_norm_proj_kernel, n_col_blocks=ncb, row_chunk=NORM_ROW_CHUNK),
        grid=(m // tm, n // tn),
        in_specs=[pl.BlockSpec((tm, d), lambda i, j: (i, 0)),
                  pl.BlockSpec((1, d), lambda i, j: (0, 0)),
                  pl.BlockSpec((tn, d), lambda i, j: (j, 0)),
                  pl.BlockSpec((LANES, d), lambda i, j: (0, 0))],
        out_specs=[pl.BlockSpec((ncb, tm, LANES), lambda i, j: (j, i, 0)),
                   pl.BlockSpec((tm, LANES), lambda i, j: (i, 0))],
        out_shape=[jax.ShapeDtypeStruct((n // LANES, m, LANES), BF16),
                   jax.ShapeDtypeStruct((m, LANES), F32)],
        scratch_shapes=[pltpu.VMEM((tm, d), BF16)],
        compiler_params=_cparams(2),
        name="norm_proj_gate",
    )(x2, g, w_t, wf_t)


def _w_repack_kernel(src_ref, mem_ref, gm_ref, wm_ref, main_ref, gate_ref, mkv_ref,
                     buf_ref, gbuf_ref, hm_ref, sem_ref, gsem_ref, *,
                     base_row, n_heads, n_aligned, n_gate, n_mem_tiles, row_chunk):
    t = pl.program_id(0)
    n_steps = pl.num_programs(0)
    n_slices = src_ref.shape[1]

    def source_row(step, g):
        head_blk = (step // n_heads) * (HEAD_GROUP * n_heads) + g * n_heads + step % n_heads
        blk = jnp.where(step < 2 * n_heads, head_blk, HEAD_GROUP * step + g)
        return base_row + blk * LANES + jnp.where(blk * LANES >= n_aligned, n_gate, 0)

    def block_copies(step, slot):
        return [pltpu.make_async_copy(src_ref.at[pl.ds(source_row(step, g), LANES), s, :],
                                      buf_ref.at[slot, pl.ds(g * LANES, LANES), pl.ds(s * LANES, LANES)],
                                      sem_ref.at[slot])
                for g in range(HEAD_GROUP) for s in range(n_slices)]

    def gate_copies():
        gate_row0 = base_row + n_aligned
        return [pltpu.make_async_copy(src_ref.at[pl.ds(gate_row0, SUBLANES), s, :],
                                      gbuf_ref.at[:, pl.ds(s * LANES, LANES)],
                                      gsem_ref.at[0]) for s in range(n_slices)]

    @pl.when(t == 0)
    def _first():
        for cp in gate_copies() + block_copies(0, 0):
            cp.start()

    @pl.when(t + 1 < n_steps)
    def _prefetch():
        for cp in block_copies(t + 1, (t + 1) % 2):
            cp.start()

    @pl.when(t == 0)
    def _gate_rows_and_mem_norm():
        for cp in gate_copies():
            cp.wait()
        row = lax.broadcasted_iota(jnp.int32, gbuf_ref.shape, 0)
        gate = jnp.where(row < n_gate, gbuf_ref[...], 0.0)
        gate_ref[...] = jnp.concatenate(
            [gate, jnp.zeros((gate_ref.shape[0] - SUBLANES, gate.shape[1]), F32)], axis=0).astype(BF16)
        for c in range(mem_ref.shape[0] // row_chunk):
            r = slice(c * row_chunk, (c + 1) * row_chunk)
            xf = mem_ref[r, :]
            ms = jnp.mean(xf * xf, axis=-1, keepdims=True)
            hm_ref[r, :] = (xf * lax.rsqrt(ms + EPS) * gm_ref[...]).astype(BF16)

    @pl.when(t < n_mem_tiles)
    def _mem_projection():
        mkv_ref[...] = jnp.dot(hm_ref[...], wm_ref[...].astype(BF16),
                               preferred_element_type=F32).astype(BF16)

    for cp in block_copies(t, t % 2):
        cp.wait()
    main_ref[...] = buf_ref[t % 2].astype(BF16)


def _w_repack_mem_proj(w_t3, layer, n_total, n_heads, n_gate, mem2, g_mem, w_mem3):
    n_slices = w_t3.shape[1]
    d = n_slices * LANES
    rows = HEAD_GROUP * LANES
    n_aligned = 2 * n_heads * rows
    n_main = n_total - n_gate
    m_mem = mem2.shape[0]
    n_mem_tiles = w_mem3.shape[2] // LANES
    n_steps = n_main // rows
    assert n_main % rows == 0 and n_aligned <= n_main and n_gate <= SUBLANES
    assert w_mem3.shape[2] % LANES == 0 and n_mem_tiles <= n_steps
    mem_tile = lambda t: jnp.minimum(t, n_mem_tiles - 1)
    return pl.pallas_call(
        functools.partial(_w_repack_kernel, base_row=layer * n_total, n_heads=n_heads,
                          n_aligned=n_aligned, n_gate=n_gate,
                          n_mem_tiles=n_mem_tiles, row_chunk=NORM_ROW_CHUNK),
        grid=(n_steps,),
        in_specs=[pl.BlockSpec(memory_space=pl.ANY),
                  pl.BlockSpec((m_mem, d), lambda t: (0, 0), pipeline_mode=pl.Buffered(1)),
                  pl.BlockSpec((1, d), lambda t: (0, 0)),
                  pl.BlockSpec((None, d, LANES), lambda t: (layer, 0, mem_tile(t)))],
        out_specs=[pl.BlockSpec((rows, d), lambda t: (t, 0)),
                   pl.BlockSpec((LANES, d), lambda t: (0, 0)),
                   pl.BlockSpec((None, m_mem, LANES), lambda t: (mem_tile(t), 0, 0))],
        out_shape=[jax.ShapeDtypeStruct((n_main, d), BF16),
                   jax.ShapeDtypeStruct((LANES, d), BF16),
                   jax.ShapeDtypeStruct((n_mem_tiles, m_mem, LANES), BF16)],
        scratch_shapes=[pltpu.VMEM((2, rows, d), F32), pltpu.VMEM((SUBLANES, d), F32),
                        pltpu.VMEM((m_mem, d), BF16),
                        pltpu.SemaphoreType.DMA((2,)), pltpu.SemaphoreType.DMA((1,))],
        compiler_params=_cparams(1),
        name="w_repack_mem_proj",
    )(w_t3, mem2, g_mem, w_mem3)


N_SPLIT = 3


def _split3(v):
    hi = v.astype(BF16).astype(F32)
    mid = (v - hi).astype(BF16).astype(F32)
    lo = (v - hi - mid).astype(BF16).astype(F32)
    return hi, mid, lo


def _forget_bias_columns(f_ref, b_ref, layer, kc_ref, n_heads):
    head_lane = lax.broadcasted_iota(jnp.int32, (1, LANES), 1)
    bias = jnp.zeros((1, LANES), F32)
    for hh in range(n_heads):
        bias = jnp.where(head_lane == hh, b_ref[layer, hh], bias)
    z = f_ref[...] + bias
    logf = jnp.minimum(z, 0.0) - jnp.log1p(jnp.exp(-jnp.abs(z)))
    x = logf.T[0:SUBLANES, :]
    s_len = x.shape[1]
    lane = lax.broadcasted_iota(jnp.int32, x.shape, 1)
    shift = 1
    while shift < s_len:
        x = x + jnp.where(lane >= shift, pltpu.roll(x, shift, 1), 0.0)
        shift *= 2
    row = lax.broadcasted_iota(jnp.int32, x.shape, 0)
    x = jnp.where(row < n_heads, x, 0.0)
    terms = _split3(x * (-1.0 / HEAD_DIM ** -0.5))
    pad = jnp.zeros((LANES - SUBLANES * N_SPLIT, s_len), F32)
    kc_ref[...] = jnp.concatenate(list(terms) + [pad], axis=0).T.astype(BF16)


LOG2E = math.log2(math.e)


def _softmax_pv_folded(a, v_aug, g):
    m = jnp.max(a, axis=1, keepdims=True)
    p = jnp.exp2((a - m) * (HEAD_DIM ** -0.5 * LOG2E)).astype(BF16)
    acc = jnp.dot(p, v_aug, preferred_element_type=F32)
    gf = g.astype(F32)
    silu = gf * (1.0 / (1.0 + jnp.exp(-gf)))
    return (acc[:, :LANES] / acc[:, LANES:] * silu).astype(BF16)


SCORE_AHEAD = 1


def _stream_events(setup, n_tiles, score_matmul, finish):
    scores = {}

    def score(i):
        return lambda: scores.__setitem__(i, score_matmul(i))

    def fin(i):
        return lambda: finish(i, scores.pop(i))

    events = [setup] + [score(i) for i in range(min(SCORE_AHEAD, n_tiles))]
    for i in range(n_tiles):
        if i + SCORE_AHEAD < n_tiles:
            events.append(score(i + SCORE_AHEAD))
        events.append(fin(i))
    return events


def _run_interleaved(streams, lead=0):
    queues = [list(ev) for ev in streams]
    for _ in range(min(lead, len(queues[0]))):
        queues[0].pop(0)()
    while any(queues):
        for q in queues:
            if q:
                q.pop(0)()


def _t5_bias(rb_ref, h, dist):
    max_exact = T5_BUCKETS // 2
    n = jnp.maximum(dist, 0)
    nf = jnp.maximum(n, 1).astype(F32)
    val = jnp.log(nf / max_exact) / math.log(T5_MAX_DIST / max_exact) * (T5_BUCKETS - max_exact)
    bias = jnp.zeros(dist.shape, F32)
    for kk in range(T5_BUCKETS):
        if kk < max_exact:
            hit = n == kk
        elif kk < T5_BUCKETS - 1:
            hit = (n >= max_exact) & (val >= kk - max_exact) & (val < kk - max_exact + 1)
        else:
            hit = (n >= max_exact) & (val >= kk - max_exact)
        bias = jnp.where(hit, rb_ref[h, kk], bias)
    return bias


def _moba_head_constants(rb_ref, h, kaug_ref, bown_ref, bprev_ref):
    inv_scale = 1.0 / HEAD_DIM ** -0.5
    blk = MOBA_BLOCK
    seq = kaug_ref.shape[0]
    far_bias = rb_ref[h, T5_BUCKETS - 1]
    srow = lax.broadcasted_iota(jnp.int32, (seq, LANES), 0)
    col = lax.broadcasted_iota(jnp.int32, (seq, LANES), 1)
    kaug_ref[:, LANES:] = jnp.where((col & (SUBLANES - 1)) * blk == srow - (srow & (blk - 1)),
                                    1.0, 0.0).astype(BF16)
    t = T5_MAX_DIST
    assert blk == 2 * t
    dist = lax.broadcasted_iota(jnp.int32, (t, t), 0) - lax.broadcasted_iota(jnp.int32, (t, t), 1)
    near = jnp.where(dist >= 0, (_t5_bias(rb_ref, h, dist) - far_bias) * inv_scale, NEG)
    mid = (_t5_bias(rb_ref, h, dist + t) - far_bias) * inv_scale
    bown_ref[0:t, 0:t] = near
    bown_ref[0:t, t:blk] = jnp.full((t, t), NEG, F32)
    bown_ref[t:blk, 0:t] = mid
    bown_ref[t:blk, t:blk] = near
    bprev_ref[...] = jnp.zeros((blk, blk), F32)
    bprev_ref[0:t, t:blk] = mid


def _moba_stream(q_ref, k_ref, v_ref, g_ref, o_ref,
                 kaug_ref, qaug_ref, vaug_ref, kmean_ref, bown_ref, bprev_ref):
    blk = MOBA_BLOCK
    seq = q_ref.shape[0]
    n_blocks = seq // blk
    n_chunks = seq // LANES

    def setup():
        _moba_setup(q_ref, k_ref, v_ref, kaug_ref, qaug_ref, vaug_ref, kmean_ref, n_blocks, n_chunks)

    def score_matmul(i):
        return lax.dot_general(qaug_ref[i * blk:(i + 1) * blk, :], kaug_ref[0:(i + 1) * blk, :], _NT,
                               preferred_element_type=F32)

    def finish(i, a):
        rows = slice(i * blk, (i + 1) * blk)
        n = (i + 1) * blk
        parts = [a[:, n - blk:] + bown_ref[...]]
        if i >= 1:
            parts.insert(0, a[:, n - 2 * blk:n - blk] + bprev_ref[...])
        if i >= 2:
            parts.insert(0, a[:, :n - 2 * blk])
        a = parts[0] if len(parts) == 1 else jnp.concatenate(parts, axis=1)
        o_ref[rows, :] = _softmax_pv_folded(a, vaug_ref[0:n, :], g_ref[rows, :])

    return setup, n_blocks, score_matmul, finish


def _moba_setup(q_ref, k_ref, v_ref, kaug_ref, qaug_ref, vaug_ref, kmean_ref, n_blocks, n_chunks):
    blk = MOBA_BLOCK
    seq = q_ref.shape[0]
    kaug_ref[:, :LANES] = k_ref[...]
    vaug_ref[:, :LANES] = v_ref[...]
    vaug_ref[:, LANES:] = jnp.ones((seq, LANES), BF16)
    blk_row = lax.broadcasted_iota(jnp.int32, (SUBLANES, LANES), 0)
    km = jnp.zeros((SUBLANES, LANES), F32)
    for j in range(n_blocks):
        mean_j = jnp.mean(k_ref[j * blk:(j + 1) * blk, :].astype(F32), axis=0, keepdims=True)
        km = jnp.where(blk_row == j, mean_j, km)
    kmean_ref[...] = jnp.concatenate([km, jnp.zeros((LANES - SUBLANES, LANES), F32)], axis=0).astype(BF16)

    gate_t = lax.dot_general(kmean_ref[...], q_ref[...], _NT, preferred_element_type=F32)[0:SUBLANES, :]
    jrow = lax.broadcasted_iota(jnp.int32, gate_t.shape, 0)
    qpos = lax.broadcasted_iota(jnp.int32, gate_t.shape, 1)
    qblk = jnp.right_shift(qpos, int(math.log2(blk)))
    gm = jnp.where(jrow < qblk, gate_t, NEG)
    sel_bias = jnp.zeros(gate_t.shape, F32)
    for j in range(n_blocks):
        gj = gm[j:j + 1, :]
        beats = (gm > gj) | ((gm == gj) & (jrow < j))
        rank = jnp.sum(beats.astype(F32), axis=0, keepdims=True)
        allowed = ((rank < MOBA_TOPK) & (qblk[0:1, :] > j)) | (qblk[0:1, :] == j)
        sel_bias = jnp.where(jrow == j, jnp.where(allowed, 0.0, NEG), sel_bias)
    assert SUBLANES * n_chunks == LANES
    packed_t = jnp.concatenate([sel_bias[:, c * LANES:(c + 1) * LANES] for c in range(n_chunks)], axis=0).T
    lane_chunk = jnp.right_shift(lax.broadcasted_iota(jnp.int32, (LANES, LANES), 1),
                                 int(math.log2(SUBLANES)))
    qaug_ref[:, :LANES] = q_ref[...]
    for c in range(n_chunks):
        qaug_ref[c * LANES:(c + 1) * LANES, LANES:] = jnp.where(lane_chunk == c, packed_t, 0.0).astype(BF16)


def _moba_scratch(seq):
    blk = MOBA_BLOCK
    return [pltpu.VMEM((seq, 2 * LANES), BF16),
            pltpu.VMEM((seq, 2 * LANES), BF16),
            pltpu.VMEM((seq, 2 * LANES), BF16),
            pltpu.VMEM((LANES, LANES), BF16),
            pltpu.VMEM((blk, blk), F32),
            pltpu.VMEM((blk, blk), F32)]


def _fox_stream(h, q_ref, k_ref, v_ref, g_ref, kc_ref, o_ref, qaug_ref, kaug_ref, vaug_ref, causal_ref):
    seq = q_ref.shape[0]

    def setup():
        col = lax.broadcasted_iota(jnp.int32, (seq, LANES), 1)
        qaug_ref[:, :LANES] = q_ref[...]
        qaug_ref[:, LANES:] = jnp.where(((col & (SUBLANES - 1)) == h) & (col < SUBLANES * N_SPLIT),
                                        1.0, 0.0).astype(BF16)
        kaug_ref[:, :LANES] = k_ref[...]
        kaug_ref[:, LANES:] = kc_ref[...]
        vaug_ref[:, :LANES] = v_ref[...]
        vaug_ref[:, LANES:] = jnp.ones((seq, LANES), BF16)
        r = lax.broadcasted_iota(jnp.int32, (Q_TILE, Q_TILE), 0)
        c = lax.broadcasted_iota(jnp.int32, (Q_TILE, Q_TILE), 1)
        causal_ref[...] = jnp.where(c <= r, 0.0, NEG)

    def score_matmul(i):
        return lax.dot_general(qaug_ref[i * Q_TILE:(i + 1) * Q_TILE, :], kaug_ref[0:(i + 1) * Q_TILE, :],
                               _NT, preferred_element_type=F32)

    def finish(i, a):
        rows = slice(i * Q_TILE, (i + 1) * Q_TILE)
        n = (i + 1) * Q_TILE
        own = a[:, n - Q_TILE:] + causal_ref[...]
        a = own if i == 0 else jnp.concatenate([a[:, :n - Q_TILE], own], axis=1)
        o_ref[rows, :] = _softmax_pv_folded(a, vaug_ref[0:n, :], g_ref[rows, :])

    return setup, seq // Q_TILE, score_matmul, finish


def _fox_scratch(seq):
    return [pltpu.VMEM((seq, 2 * LANES), BF16), pltpu.VMEM((seq, 2 * LANES), BF16),
            pltpu.VMEM((seq, 2 * LANES), BF16), pltpu.VMEM((Q_TILE, Q_TILE), F32)]


FOX_LEAD_EVENTS = 11


def _causal_attention_kernel(*refs):
    rb_ref, moba_qkvg, fox_qkvg, kc, wo_ref, oa, ob, wob_ref = refs[:8]
    qa, ka, va, ga = (moba_qkvg.at[i] for i in range(HEAD_GROUP))
    qb, kb, vb, gb = (fox_qkvg.at[i] for i in range(HEAD_GROUP))
    n_moba = len(_moba_scratch(qa.shape[0]))
    moba_scratch, fox_scratch = refs[8:8 + n_moba], refs[8 + n_moba:]
    kaug_ref, _, _, _, bown_ref, bprev_ref = moba_scratch
    h = pl.program_id(0)

    @pl.when(pl.program_id(1) == 0)
    def _():
        _moba_head_constants(rb_ref, h, kaug_ref, bown_ref, bprev_ref)

    wob_ref[...] = wo_ref[...].astype(BF16)
    _run_interleaved([_stream_events(*_fox_stream(h, qb, kb, vb, gb, kc, ob, *fox_scratch)),
                      _stream_events(*_moba_stream(qa, ka, va, ga, oa, *moba_scratch))],
                     lead=FOX_LEAD_EVENTS)


def _causal_attention(p, rel_bias, kc, w_out3, layer, batch, seq):
    nb = seq // MOBA_BLOCK
    assert seq % MOBA_BLOCK == 0 and MOBA_TOPK < SUBLANES and nb <= SUBLANES and MOBA_BLOCK > T5_MAX_DIST
    assert N_HEADS_MOBA == N_HEADS_FOX <= SUBLANES
    dw, d = w_out3.shape[1:]
    n_slabs = dw // W_OUT_CAST_ROWS
    assert dw % W_OUT_CAST_ROWS == 0 and n_slabs <= N_HEADS_MOBA * batch
    slab = lambda h, b: jnp.minimum(h * batch + b, n_slabs - 1)
    head = lambda first: pl.BlockSpec((HEAD_GROUP, seq, LANES), lambda h, b: (first + h, b, 0))
    out = pl.BlockSpec((None, seq, LANES), lambda h, b: (h, b, 0))
    out_sds = jax.ShapeDtypeStruct((N_HEADS_MOBA, batch * seq, LANES), BF16)
    return pl.pallas_call(
        _causal_attention_kernel,
        grid=(N_HEADS_MOBA, batch),
        in_specs=[pl.BlockSpec(memory_space=pltpu.SMEM), head(0), head(N_HEADS_MOBA),
                  pl.BlockSpec((None, seq, LANES), lambda h, b: (b, 0, 0)),
                  pl.BlockSpec((None, W_OUT_CAST_ROWS, d), lambda h, b: (layer, slab(h, b), 0))],
        out_specs=[out, out, pl.BlockSpec((W_OUT_CAST_ROWS, d), lambda h, b: (slab(h, b), 0))],
        out_shape=[out_sds, out_sds, jax.ShapeDtypeStruct((dw, d), BF16)],
        scratch_shapes=_moba_scratch(seq) + _fox_scratch(seq),
        compiler_params=_cparams(2),
        name="causal_attention",
    )(rel_bias, p, p, kc, w_out3)


def _mem_kernel(b_ref, q_ref, k_ref, v_ref, g_ref, f_ref, o_ref, kc_ref, vaug_ref, *, layer, n_fox_heads):
    n_heads, seq, _ = q_ref.shape
    tiles_per_head = seq // MEM_Q_TILE

    _forget_bias_columns(f_ref, b_ref, layer, kc_ref, n_fox_heads)
    vaug_ref[:, :, LANES:] = jnp.ones(v_ref.shape, BF16)
    vaug_ref[:, :, :LANES] = v_ref[...]

    def tile(t):
        h, i = divmod(t, tiles_per_head)
        return h, slice(i * MEM_Q_TILE, (i + 1) * MEM_Q_TILE)

    def score_matmul(t):
        h, rows = tile(t)
        return lax.dot_general(q_ref[h, rows, :], k_ref[h], _NT, preferred_element_type=F32)

    def finish(t, a):
        h, rows = tile(t)
        o_ref[h, rows, :] = _softmax_pv_folded(a, vaug_ref[h], g_ref[h, rows, :])

    _run_interleaved([_stream_events(lambda: None, n_heads * tiles_per_head, score_matmul, finish)])


def _mem_attention(p, mkv, fb, b_forget, layer, batch, seq, n_mem, q0, g0):
    nh = N_HEADS_MEM
    assert q0 % nh == 0 and g0 % nh == 0 and N_HEADS_FOX <= SUBLANES
    heads = lambda off: pl.BlockSpec((nh, seq, LANES), lambda b: (off // nh, b, 0))
    kv = lambda blk: pl.BlockSpec((nh, n_mem, LANES), lambda b: (blk, b, 0))
    return pl.pallas_call(
        functools.partial(_mem_kernel, layer=layer, n_fox_heads=N_HEADS_FOX),
        grid=(batch,),
        in_specs=[pl.BlockSpec(memory_space=pltpu.SMEM),
                  heads(q0), kv(0), kv(1), heads(g0),
                  pl.BlockSpec((seq, LANES), lambda b: (b, 0))],
        out_specs=[pl.BlockSpec((nh, seq, LANES), lambda b: (0, b, 0)),
                   pl.BlockSpec((None, seq, LANES), lambda b: (b, 0, 0))],
        out_shape=[jax.ShapeDtypeStruct((nh, batch * seq, LANES), BF16),
                   jax.ShapeDtypeStruct((batch, seq, LANES), BF16)],
        scratch_shapes=[pltpu.VMEM((nh, n_mem, 2 * LANES), BF16)],
        compiler_params=_cparams(1),
        name="mem_attention",
    )(b_forget, p, mkv, mkv, p, fb)


def _out_proj_kernel(ya_ref, yb_ref, ym_ref, w_ref, x_ref, g_ref, o_ref):
    tm = x_ref.shape[0]
    for c in range(tm // NORM_ROW_CHUNK):
        rows = slice(c * NORM_ROW_CHUNK, (c + 1) * NORM_ROW_CHUNK)
        y = jnp.concatenate([src[k, rows, :] for src in (ya_ref, yb_ref, ym_ref) for k in range(src.shape[0])],
                            axis=1)
        o = jnp.dot(y, w_ref[...], preferred_element_type=F32)
        ms = jnp.mean(o * o, axis=-1, keepdims=True)
        o_ref[rows, :] = x_ref[rows, :] + o * lax.rsqrt(ms + EPS) * g_ref[...]


def _out_proj(ya, yb, ym, w, x2, g, *, tm):
    m, d = x2.shape
    heads = lambda y: pl.BlockSpec((y.shape[0], tm, LANES), lambda i: (0, i, 0))
    return pl.pallas_call(
        _out_proj_kernel,
        grid=(m // tm,),
        in_specs=[heads(ya), heads(yb), heads(ym),
                  pl.BlockSpec(w.shape, lambda i: (0, 0), pipeline_mode=pl.Buffered(1)),
                  pl.BlockSpec((tm, d), lambda i: (i, 0)),
                  pl.BlockSpec((1, d), lambda i: (0, 0))],
        out_specs=pl.BlockSpec((tm, d), lambda i: (i, 0)),
        out_shape=jax.ShapeDtypeStruct((m, d), F32),
        compiler_params=_cparams(1),
        name="out_proj",
    )(ya, yb, ym, w, x2, g)


def kernel(x, mem, w_in, b_forget, w_mem_kv, w_out, g_pre, g_post, g_mem, rel_bias):
    batch, seq, d = x.shape
    n_mem = mem.shape[1]
    depth = w_in.shape[0]
    assert N_HEADS_MOBA == N_HEADS_FOX
    n_causal = N_HEADS_MOBA + N_HEADS_FOX
    wm = N_HEADS_MEM * HEAD_DIM
    n_in = w_in.shape[2]
    assert n_in == n_causal * HEAD_GROUP * HEAD_DIM + N_HEADS_FOX + 2 * wm
    qm0 = n_causal * HEAD_GROUP
    gm0 = qm0 + N_HEADS_MEM

    x2 = x.reshape(batch * seq, d)
    mem2 = mem.reshape(batch * n_mem, d)
    for layer in range(depth):
        w_t3 = jnp.transpose(w_in, (0, 2, 1)).reshape(depth * n_in, d // LANES, LANES)
        w_main_t, w_gate_t, mkv = _w_repack_mem_proj(w_t3, layer, n_in, N_HEADS_MOBA, N_HEADS_FOX, mem2,
                                                     g_mem[layer].reshape(1, d), w_mem_kv)
        p, fb = _norm_proj(x2, g_pre[layer].reshape(1, d), w_main_t, w_gate_t,
                           tm=IN_PROJ_TILE[0], tn=IN_PROJ_TILE[1])
        ym, kc = _mem_attention(p, mkv, fb, b_forget, layer, batch, seq, n_mem, qm0, gm0)
        ya, yb, w_out_bf = _causal_attention(p, rel_bias, kc, w_out, layer, batch, seq)
        x2 = _out_proj(ya, yb, ym, w_out_bf, x2, g_post[layer].reshape(1, d), tm=OUT_PROJ_ROWS)
    return x2.reshape(batch, seq, d)
```

```python
import functools
import math

import jax
import jax.numpy as jnp
from jax import lax
from jax.experimental import pallas as pl
from jax.experimental.pallas import tpu as pltpu

HEAD_DIM = 128
N_HEADS_MOBA = 6
N_HEADS_FOX = 6
N_HEADS_MEM = 4
MOBA_BLOCK = 256
MOBA_TOPK = 3
T5_BUCKETS = 32
T5_MAX_DIST = 128
EPS = 1e-6
NEG = -1e30

LANES = 128
SUBLANES = 8
VMEM_LIMIT_BYTES = 56 * 1024 * 1024

IN_PROJ_TILE = (1024, 1792)
NORM_ROW_CHUNK = 256
OUT_PROJ_ROWS = 512
HEAD_GROUP = 4
W_OUT_CAST_ROWS = 128
Q_TILE = MOBA_BLOCK
MEM_Q_TILE = 1024

F32 = jnp.float32
BF16 = jnp.bfloat16
_NT = (((1,), (1,)), ((), ()))


def _cparams(n_axes):
    return pltpu.CompilerParams(dimension_semantics=("arbitrary",) * n_axes,
                                vmem_limit_bytes=VMEM_LIMIT_BYTES)


def _norm_proj_kernel(xlo_ref, xhi_ref, g_ref, w_ref, wf_ref, p_ref, f_ref, h_ref, *, n_col_blocks,
                      row_chunk):
    half = xlo_ref.shape[0]
    tm = 2 * half
    j = pl.program_id(1)

    def project(rows):
        acc = lax.dot_general(h_ref[rows, :], w_ref[...], _NT, preferred_element_type=F32)
        for c in range(n_col_blocks):
            p_ref[c, rows, :] = acc[:, c * LANES:(c + 1) * LANES].astype(BF16)

    @pl.when(j == 0)
    def _normalize_and_project():
        for c in range(tm // row_chunk):
            rows = slice(c * row_chunk, (c + 1) * row_chunk)
            x_ref, r0 = (xlo_ref, c * row_chunk) if c * row_chunk < half else (xhi_ref, c * row_chunk - half)
            xf = x_ref[r0:r0 + row_chunk, :]
            ms = jnp.mean(xf * xf, axis=-1, keepdims=True)
            h = (xf * lax.rsqrt(ms + EPS) * g_ref[...]).astype(BF16)
            h_ref[rows, :] = h
            f_ref[rows, :] = lax.dot_general(h, wf_ref[...], _NT, preferred_element_type=F32)
            project(rows)

    @pl.when(j > 0)
    def _project():
        project(slice(0, tm))


def _norm_proj(x2, g, w_t, wf_t, *, tm, tn):
    m, d = x2.shape
    n = w_t.shape[0]
    ncb = tn // LANES
    n_i, n_j = m // tm, n // tn
    assert n % tn == 0 and m % tm == 0 and (tm // 2) % NORM_ROW_CHUNK == 0 and n_j >= 2

    def x_half(k):
        def index(i, j):
            tile = jnp.minimum(i + jnp.where(j >= n_j - 2 + k, 1, 0), n_i - 1)
            return (2 * tile + k, 0)
        return pl.BlockSpec((tm // 2, d), index)

    return pl.pallas_call(
        functools.partial(_norm_proj_kernel, n_col_blocks=ncb, row_chunk=NORM_ROW_CHUNK),
        grid=(n_i, n_j),
        in_specs=[x_half(0), x_half(1),
                  pl.BlockSpec((1, d), lambda i, j: (0, 0)),
                  pl.BlockSpec((tn, d), lambda i, j: (j, 0)),
                  pl.BlockSpec((LANES, d), lambda i, j: (0, 0))],
        out_specs=[pl.BlockSpec((ncb, tm, LANES), lambda i, j: (j, i, 0)),
                   pl.BlockSpec((tm, LANES), lambda i, j: (i, 0))],
        out_shape=[jax.ShapeDtypeStruct((n // LANES, m, LANES), BF16),
                   jax.ShapeDtypeStruct((m, LANES), F32)],
        scratch_shapes=[pltpu.VMEM((tm, d), BF16)],
        compiler_params=_cparams(2),
        name="norm_proj_gate",
    )(x2, x2, g, w_t, wf_t)


def _w_repack_kernel(src_ref, mem_ref, gm_ref, wm_ref, main_ref, gate_ref, mkv_ref,
                     buf_ref, gbuf_ref, hm_ref, sem_ref, gsem_ref, *,
                     base_row, n_heads, n_aligned, n_gate, n_mem_tiles, row_chunk):
    t = pl.program_id(0)
    n_steps = pl.num_programs(0)
    n_slices = src_ref.shape[1]

    def source_row(step, g):
        head_blk = (step // n_heads) * (HEAD_GROUP * n_heads) + g * n_heads + step % n_heads
        blk = jnp.where(step < 2 * n_heads, head_blk, HEAD_GROUP * step + g)
        return base_row + blk * LANES + jnp.where(blk * LANES >= n_aligned, n_gate, 0)

    def block_copies(step, slot):
        return [pltpu.make_async_copy(src_ref.at[pl.ds(source_row(step, g), LANES), s, :],
                                      buf_ref.at[slot, pl.ds(g * LANES, LANES), pl.ds(s * LANES, LANES)],
                                      sem_ref.at[slot])
                for g in range(HEAD_GROUP) for s in range(n_slices)]

    def gate_copies():
        gate_row0 = base_row + n_aligned
        return [pltpu.make_async_copy(src_ref.at[pl.ds(gate_row0, SUBLANES), s, :],
                                      gbuf_ref.at[:, pl.ds(s * LANES, LANES)],
                                      gsem_ref.at[0]) for s in range(n_slices)]

    @pl.when(t == 0)
    def _first():
        for cp in gate_copies() + block_copies(0, 0):
            cp.start()

    @pl.when(t + 1 < n_steps)
    def _prefetch():
        for cp in block_copies(t + 1, (t + 1) % 2):
            cp.start()

    @pl.when(t == 0)
    def _gate_rows_and_mem_norm():
        for cp in gate_copies():
            cp.wait()
        row = lax.broadcasted_iota(jnp.int32, gbuf_ref.shape, 0)
        gate = jnp.where(row < n_gate, gbuf_ref[...], 0.0)
        gate_ref[...] = jnp.concatenate(
            [gate, jnp.zeros((gate_ref.shape[0] - SUBLANES, gate.shape[1]), F32)], axis=0).astype(BF16)
        for c in range(mem_ref.shape[0] // row_chunk):
            r = slice(c * row_chunk, (c + 1) * row_chunk)
            xf = mem_ref[r, :]
            ms = jnp.mean(xf * xf, axis=-1, keepdims=True)
            hm_ref[r, :] = (xf * lax.rsqrt(ms + EPS) * gm_ref[...]).astype(BF16)

    @pl.when(t < n_mem_tiles)
    def _mem_projection():
        mkv_ref[...] = jnp.dot(hm_ref[...], wm_ref[...].astype(BF16),
                               preferred_element_type=F32).astype(BF16)

    for cp in block_copies(t, t % 2):
        cp.wait()
    main_ref[...] = buf_ref[t % 2].astype(BF16)


def _w_repack_mem_proj(w_t3, layer, n_total, n_heads, n_gate, mem2, g_mem, w_mem3):
    n_slices = w_t3.shape[1]
    d = n_slices * LANES
    rows = HEAD_GROUP * LANES
    n_aligned = 2 * n_heads * rows
    n_main = n_total - n_gate
    m_mem = mem2.shape[0]
    n_mem_tiles = w_mem3.shape[2] // LANES
    n_steps = n_main // rows
    assert n_main % rows == 0 and n_aligned <= n_main and n_gate <= SUBLANES
    assert w_mem3.shape[2] % LANES == 0 and n_mem_tiles <= n_steps
    mem_tile = lambda t: jnp.minimum(t, n_mem_tiles - 1)
    return pl.pallas_call(
        functools.partial(_w_repack_kernel, base_row=layer * n_total, n_heads=n_heads,
                          n_aligned=n_aligned, n_gate=n_gate,
                          n_mem_tiles=n_mem_tiles, row_chunk=NORM_ROW_CHUNK),
        grid=(n_steps,),
        in_specs=[pl.BlockSpec(memory_space=pl.ANY),
                  pl.BlockSpec((m_mem, d), lambda t: (0, 0), pipeline_mode=pl.Buffered(1)),
                  pl.BlockSpec((1, d), lambda t: (0, 0)),
                  pl.BlockSpec((None, d, LANES), lambda t: (layer, 0, mem_tile(t)))],
        out_specs=[pl.BlockSpec((rows, d), lambda t: (t, 0)),
                   pl.BlockSpec((LANES, d), lambda t: (0, 0)),
                   pl.BlockSpec((None, m_mem, LANES), lambda t: (mem_tile(t), 0, 0))],
        out_shape=[jax.ShapeDtypeStruct((n_main, d), BF16),
                   jax.ShapeDtypeStruct((LANES, d), BF16),
                   jax.ShapeDtypeStruct((n_mem_tiles, m_mem, LANES), BF16)],
        scratch_shapes=[pltpu.VMEM((2, rows, d), F32), pltpu.VMEM((SUBLANES, d), F32),
                        pltpu.VMEM((m_mem, d), BF16),
                        pltpu.SemaphoreType.DMA((2,)), pltpu.SemaphoreType.DMA((1,))],
        compiler_params=_cparams(1),
        name="w_repack_mem_proj",
    )(w_t3, mem2, g_mem, w_mem3)


N_SPLIT = 3


def _split3(v):
    hi = v.astype(BF16).astype(F32)
    mid = (v - hi).astype(BF16).astype(F32)
    lo = (v - hi - mid).astype(BF16).astype(F32)
    return hi, mid, lo


def _forget_bias_columns(f_ref, b_ref, layer, kc_ref, n_heads):
    head_lane = lax.broadcasted_iota(jnp.int32, (1, LANES), 1)
    bias = jnp.zeros((1, LANES), F32)
    for hh in range(n_heads):
        bias = jnp.where(head_lane == hh, b_ref[layer, hh], bias)
    z = f_ref[...] + bias
    logf = jnp.minimum(z, 0.0) - jnp.log1p(jnp.exp(-jnp.abs(z)))
    x = logf.T[0:SUBLANES, :]
    s_len = x.shape[1]
    lane = lax.broadcasted_iota(jnp.int32, x.shape, 1)
    shift = 1
    while shift < s_len:
        x = x + jnp.where(lane >= shift, pltpu.roll(x, shift, 1), 0.0)
        shift *= 2
    row = lax.broadcasted_iota(jnp.int32, x.shape, 0)
    x = jnp.where(row < n_heads, x, 0.0)
    terms = _split3(x * (-1.0 / HEAD_DIM ** -0.5))
    pad = jnp.zeros((LANES - SUBLANES * N_SPLIT, s_len), F32)
    kc_ref[...] = jnp.concatenate(list(terms) + [pad], axis=0).T.astype(BF16)


LOG2E = math.log2(math.e)


def _softmax_pv_folded(a, v_aug, g):
    m = jnp.max(a, axis=1, keepdims=True)
    p = jnp.exp2((a - m) * (HEAD_DIM ** -0.5 * LOG2E)).astype(BF16)
    acc = jnp.dot(p, v_aug, preferred_element_type=F32)
    gf = g.astype(F32)
    silu = gf * (1.0 / (1.0 + jnp.exp(-gf)))
    return (acc[:, :LANES] / acc[:, LANES:] * silu).astype(BF16)


SCORE_AHEAD = 1


def _stream_events(setup, n_tiles, score_matmul, finish):
    scores = {}

    def score(i):
        return lambda: scores.__setitem__(i, score_matmul(i))

    def fin(i):
        return lambda: finish(i, scores.pop(i))

    events = [setup] + [score(i) for i in range(min(SCORE_AHEAD, n_tiles))]
    for i in range(n_tiles):
        if i + SCORE_AHEAD < n_tiles:
            events.append(score(i + SCORE_AHEAD))
        events.append(fin(i))
    return events


def _run_interleaved(streams, lead=0):
    queues = [list(ev) for ev in streams]
    for _ in range(min(lead, len(queues[0]))):
        queues[0].pop(0)()
    while any(queues):
        for q in queues:
            if q:
                q.pop(0)()


def _t5_bias(rb_ref, h, dist):
    max_exact = T5_BUCKETS // 2
    n = jnp.maximum(dist, 0)
    nf = jnp.maximum(n, 1).astype(F32)
    val = jnp.log(nf / max_exact) / math.log(T5_MAX_DIST / max_exact) * (T5_BUCKETS - max_exact)
    bias = jnp.zeros(dist.shape, F32)
    for kk in range(T5_BUCKETS):
        if kk < max_exact:
            hit = n == kk
        elif kk < T5_BUCKETS - 1:
            hit = (n >= max_exact) & (val >= kk - max_exact) & (val < kk - max_exact + 1)
        else:
            hit = (n >= max_exact) & (val >= kk - max_exact)
        bias = jnp.where(hit, rb_ref[h, kk], bias)
    return bias


def _moba_head_constants(rb_ref, h, kaug_ref, bown_ref, bprev_ref):
    inv_scale = 1.0 / HEAD_DIM ** -0.5
    blk = MOBA_BLOCK
    seq = kaug_ref.shape[0]
    far_bias = rb_ref[h, T5_BUCKETS - 1]
    srow = lax.broadcasted_iota(jnp.int32, (seq, LANES), 0)
    col = lax.broadcasted_iota(jnp.int32, (seq, LANES), 1)
    kaug_ref[:, LANES:] = jnp.where((col & (SUBLANES - 1)) * blk == srow - (srow & (blk - 1)),
                                    1.0, 0.0).astype(BF16)
    t = T5_MAX_DIST
    assert blk == 2 * t
    dist = lax.broadcasted_iota(jnp.int32, (t, t), 0) - lax.broadcasted_iota(jnp.int32, (t, t), 1)
    near = jnp.where(dist >= 0, (_t5_bias(rb_ref, h, dist) - far_bias) * inv_scale, NEG)
    mid = (_t5_bias(rb_ref, h, dist + t) - far_bias) * inv_scale
    bown_ref[0:t, 0:t] = near
    bown_ref[0:t, t:blk] = jnp.full((t, t), NEG, F32)
    bown_ref[t:blk, 0:t] = mid
    bown_ref[t:blk, t:blk] = near
    bprev_ref[...] = jnp.zeros((blk, blk), F32)
    bprev_ref[0:t, t:blk] = mid


def _moba_stream(q_ref, k_ref, v_ref, g_ref, o_ref,
                 kaug_ref, qaug_ref, vaug_ref, kmean_ref, bown_ref, bprev_ref):
    blk = MOBA_BLOCK
    seq = q_ref.shape[0]
    n_blocks = seq // blk
    n_chunks = seq // LANES

    def setup():
        _moba_setup(q_ref, k_ref, v_ref, kaug_ref, qaug_ref, vaug_ref, kmean_ref, n_blocks, n_chunks)

    def score_matmul(i):
        return lax.dot_general(qaug_ref[i * blk:(i + 1) * blk, :], kaug_ref[0:(i + 1) * blk, :], _NT,
                               preferred_element_type=F32)

    def finish(i, a):
        rows = slice(i * blk, (i + 1) * blk)
        n = (i + 1) * blk
        parts = [a[:, n - blk:] + bown_ref[...]]
        if i >= 1:
            parts.insert(0, a[:, n - 2 * blk:n - blk] + bprev_ref[...])
        if i >= 2:
            parts.insert(0, a[:, :n - 2 * blk])
        a = parts[0] if len(parts) == 1 else jnp.concatenate(parts, axis=1)
        o_ref[rows, :] = _softmax_pv_folded(a, vaug_ref[0:n, :], g_ref[rows, :])

    return setup, n_blocks, score_matmul, finish


def _moba_setup(q_ref, k_ref, v_ref, kaug_ref, qaug_ref, vaug_ref, kmean_ref, n_blocks, n_chunks):
    blk = MOBA_BLOCK
    seq = q_ref.shape[0]
    kaug_ref[:, :LANES] = k_ref[...]
    vaug_ref[:, :LANES] = v_ref[...]
    vaug_ref[:, LANES:] = jnp.ones((seq, LANES), BF16)
    blk_row = lax.broadcasted_iota(jnp.int32, (SUBLANES, LANES), 0)
    km = jnp.zeros((SUBLANES, LANES), F32)
    for j in range(n_blocks):
        mean_j = jnp.mean(k_ref[j * blk:(j + 1) * blk, :].astype(F32), axis=0, keepdims=True)
        km = jnp.where(blk_row == j, mean_j, km)
    kmean_ref[...] = jnp.concatenate([km, jnp.zeros((LANES - SUBLANES, LANES), F32)], axis=0).astype(BF16)

    gate_t = lax.dot_general(kmean_ref[...], q_ref[...], _NT, preferred_element_type=F32)[0:SUBLANES, :]
    jrow = lax.broadcasted_iota(jnp.int32, gate_t.shape, 0)
    qpos = lax.broadcasted_iota(jnp.int32, gate_t.shape, 1)
    qblk = jnp.right_shift(qpos, int(math.log2(blk)))
    gm = jnp.where(jrow < qblk, gate_t, NEG)
    sel_bias = jnp.zeros(gate_t.shape, F32)
    for j in range(n_blocks):
        gj = gm[j:j + 1, :]
        beats = (gm > gj) | ((gm == gj) & (jrow < j))
        rank = jnp.sum(beats.astype(F32), axis=0, keepdims=True)
        allowed = ((rank < MOBA_TOPK) & (qblk[0:1, :] > j)) | (qblk[0:1, :] == j)
        sel_bias = jnp.where(jrow == j, jnp.where(allowed, 0.0, NEG), sel_bias)
    assert SUBLANES * n_chunks == LANES
    packed_t = jnp.concatenate([sel_bias[:, c * LANES:(c + 1) * LANES] for c in range(n_chunks)], axis=0).T
    lane_chunk = jnp.right_shift(lax.broadcasted_iota(jnp.int32, (LANES, LANES), 1),
                                 int(math.log2(SUBLANES)))
    qaug_ref[:, :LANES] = q_ref[...]
    for c in range(n_chunks):
        qaug_ref[c * LANES:(c + 1) * LANES, LANES:] = jnp.where(lane_chunk == c, packed_t, 0.0).astype(BF16)


def _moba_scratch(seq):
    blk = MOBA_BLOCK
    return [pltpu.VMEM((seq, 2 * LANES), BF16),
            pltpu.VMEM((seq, 2 * LANES), BF16),
            pltpu.VMEM((seq, 2 * LANES), BF16),
            pltpu.VMEM((LANES, LANES), BF16),
            pltpu.VMEM((blk, blk), F32),
            pltpu.VMEM((blk, blk), F32)]


def _fox_stream(h, q_ref, k_ref, v_ref, g_ref, kc_ref, o_ref, qaug_ref, kaug_ref, vaug_ref, causal_ref):
    seq = q_ref.shape[0]

    def setup():
        col = lax.broadcasted_iota(jnp.int32, (seq, LANES), 1)
        qaug_ref[:, :LANES] = q_ref[...]
        qaug_ref[:, LANES:] = jnp.where(((col & (SUBLANES - 1)) == h) & (col < SUBLANES * N_SPLIT),
                                        1.0, 0.0).astype(BF16)
        kaug_ref[:, :LANES] = k_ref[...]
        kaug_ref[:, LANES:] = kc_ref[...]
        vaug_ref[:, :LANES] = v_ref[...]
        vaug_ref[:, LANES:] = jnp.ones((seq, LANES), BF16)
        r = lax.broadcasted_iota(jnp.int32, (Q_TILE, Q_TILE), 0)
        c = lax.broadcasted_iota(jnp.int32, (Q_TILE, Q_TILE), 1)
        causal_ref[...] = jnp.where(c <= r, 0.0, NEG)

    def score_matmul(i):
        return lax.dot_general(qaug_ref[i * Q_TILE:(i + 1) * Q_TILE, :], kaug_ref[0:(i + 1) * Q_TILE, :],
                               _NT, preferred_element_type=F32)

    def finish(i, a):
        rows = slice(i * Q_TILE, (i + 1) * Q_TILE)
        n = (i + 1) * Q_TILE
        own = a[:, n - Q_TILE:] + causal_ref[...]
        a = own if i == 0 else jnp.concatenate([a[:, :n - Q_TILE], own], axis=1)
        o_ref[rows, :] = _softmax_pv_folded(a, vaug_ref[0:n, :], g_ref[rows, :])

    return setup, seq // Q_TILE, score_matmul, finish


def _fox_scratch(seq):
    return [pltpu.VMEM((seq, 2 * LANES), BF16), pltpu.VMEM((seq, 2 * LANES), BF16),
            pltpu.VMEM((seq, 2 * LANES), BF16), pltpu.VMEM((Q_TILE, Q_TILE), F32)]


FOX_LEAD_EVENTS = 11


def _causal_attention_kernel(*refs):
    rb_ref, moba_qkvg, fox_qkvg, kc, wo_ref, oa, ob, wob_ref = refs[:8]
    qa, ka, va, ga = (moba_qkvg.at[i] for i in range(HEAD_GROUP))
    qb, kb, vb, gb = (fox_qkvg.at[i] for i in range(HEAD_GROUP))
    n_moba = len(_moba_scratch(qa.shape[0]))
    moba_scratch, fox_scratch = refs[8:8 + n_moba], refs[8 + n_moba:]
    kaug_ref, _, _, _, bown_ref, bprev_ref = moba_scratch
    h = pl.program_id(0)

    @pl.when(pl.program_id(1) == 0)
    def _():
        _moba_head_constants(rb_ref, h, kaug_ref, bown_ref, bprev_ref)

    wob_ref[...] = wo_ref[...].astype(BF16)
    _run_interleaved([_stream_events(*_fox_stream(h, qb, kb, vb, gb, kc, ob, *fox_scratch)),
                      _stream_events(*_moba_stream(qa, ka, va, ga, oa, *moba_scratch))],
                     lead=FOX_LEAD_EVENTS)


def _causal_attention(p, rel_bias, kc, w_out3, layer, batch, seq):
    nb = seq // MOBA_BLOCK
    assert seq % MOBA_BLOCK == 0 and MOBA_TOPK < SUBLANES and nb <= SUBLANES and MOBA_BLOCK > T5_MAX_DIST
    assert N_HEADS_MOBA == N_HEADS_FOX <= SUBLANES
    dw, d = w_out3.shape[1:]
    n_slabs = dw // W_OUT_CAST_ROWS
    assert dw % W_OUT_CAST_ROWS == 0 and n_slabs <= N_HEADS_MOBA * batch
    slab = lambda h, b: jnp.minimum(h * batch + b, n_slabs - 1)
    head = lambda first: pl.BlockSpec((HEAD_GROUP, seq, LANES), lambda h, b: (first + h, b, 0))
    out = pl.BlockSpec((None, seq, LANES), lambda h, b: (h, b, 0))
    out_sds = jax.ShapeDtypeStruct((N_HEADS_MOBA, batch * seq, LANES), BF16)
    return pl.pallas_call(
        _causal_attention_kernel,
        grid=(N_HEADS_MOBA, batch),
        in_specs=[pl.BlockSpec(memory_space=pltpu.SMEM), head(0), head(N_HEADS_MOBA),
                  pl.BlockSpec((None, seq, LANES), lambda h, b: (b, 0, 0)),
                  pl.BlockSpec((None, W_OUT_CAST_ROWS, d), lambda h, b: (layer, slab(h, b), 0))],
        out_specs=[out, out, pl.BlockSpec((W_OUT_CAST_ROWS, d), lambda h, b: (slab(h, b), 0))],
        out_shape=[out_sds, out_sds, jax.ShapeDtypeStruct((dw, d), BF16)],
        scratch_shapes=_moba_scratch(seq) + _fox_scratch(seq),
        compiler_params=_cparams(2),
        name="causal_attention",
    )(rel_bias, p, p, kc, w_out3)


def _mem_kernel(b_ref, q_ref, k_ref, v_ref, g_ref, f_ref, o_ref, kc_ref, vaug_ref, *, layer, n_fox_heads):
    n_heads, seq, _ = q_ref.shape
    tiles_per_head = seq // MEM_Q_TILE

    _forget_bias_columns(f_ref, b_ref, layer, kc_ref, n_fox_heads)
    vaug_ref[:, :, LANES:] = jnp.ones(v_ref.shape, BF16)
    vaug_ref[:, :, :LANES] = v_ref[...]

    def tile(t):
        h, i = divmod(t, tiles_per_head)
        return h, slice(i * MEM_Q_TILE, (i + 1) * MEM_Q_TILE)

    def score_matmul(t):
        h, rows = tile(t)
        return lax.dot_general(q_ref[h, rows, :], k_ref[h], _NT, preferred_element_type=F32)

    def finish(t, a):
        h, rows = tile(t)
        o_ref[h, rows, :] = _softmax_pv_folded(a, vaug_ref[h], g_ref[h, rows, :])

    _run_interleaved([_stream_events(lambda: None, n_heads * tiles_per_head, score_matmul, finish)])


def _mem_attention(p, mkv, fb, b_forget, layer, batch, seq, n_mem, q0, g0):
    nh = N_HEADS_MEM
    assert q0 % nh == 0 and g0 % nh == 0 and N_HEADS_FOX <= SUBLANES
    heads = lambda off: pl.BlockSpec((nh, seq, LANES), lambda b: (off // nh, b, 0))
    kv = lambda blk: pl.BlockSpec((nh, n_mem, LANES), lambda b: (blk, b, 0))
    return pl.pallas_call(
        functools.partial(_mem_kernel, layer=layer, n_fox_heads=N_HEADS_FOX),
        grid=(batch,),
        in_specs=[pl.BlockSpec(memory_space=pltpu.SMEM),
                  heads(q0), kv(0), kv(1), heads(g0),
                  pl.BlockSpec((seq, LANES), lambda b: (b, 0))],
        out_specs=[pl.BlockSpec((nh, seq, LANES), lambda b: (0, b, 0)),
                   pl.BlockSpec((None, seq, LANES), lambda b: (b, 0, 0))],
        out_shape=[jax.ShapeDtypeStruct((nh, batch * seq, LANES), BF16),
                   jax.ShapeDtypeStruct((batch, seq, LANES), BF16)],
        scratch_shapes=[pltpu.VMEM((nh, n_mem, 2 * LANES), BF16)],
        compiler_params=_cparams(1),
        name="mem_attention",
    )(b_forget, p, mkv, mkv, p, fb)


def _out_proj_kernel(ya_ref, yb_ref, ym_ref, w_ref, x_ref, g_ref, o_ref):
    y = jnp.concatenate([src[c] for src in (ya_ref, yb_ref, ym_ref) for c in range(src.shape[0])], axis=1)
    o = jnp.dot(y, w_ref[...], preferred_element_type=F32)
    ms = jnp.mean(o * o, axis=-1, keepdims=True)
    o_ref[...] = x_ref[...] + o * lax.rsqrt(ms + EPS) * g_ref[...]


def _out_proj(ya, yb, ym, w, x2, g, *, tm):
    m, d = x2.shape
    heads = lambda y: pl.BlockSpec((y.shape[0], tm, LANES), lambda i: (0, i, 0))
    return pl.pallas_call(
        _out_proj_kernel,
        grid=(m // tm,),
        in_specs=[heads(ya), heads(yb), heads(ym),
                  pl.BlockSpec(w.shape, lambda i: (0, 0), pipeline_mode=pl.Buffered(1)),
                  pl.BlockSpec((tm, d), lambda i: (i, 0)),
                  pl.BlockSpec((1, d), lambda i: (0, 0))],
        out_specs=pl.BlockSpec((tm, d), lambda i: (i, 0)),
        out_shape=jax.ShapeDtypeStruct((m, d), F32),
        compiler_params=_cparams(1),
        name="out_proj",
    )(ya, yb, ym, w, x2, g)


def kernel(x, mem, w_in, b_forget, w_mem_kv, w_out, g_pre, g_post, g_mem, rel_bias):
    batch, seq, d = x.shape
    n_mem = mem.shape[1]
    depth = w_in.shape[0]
    assert N_HEADS_MOBA == N_HEADS_FOX
    n_causal = N_HEADS_MOBA + N_HEADS_FOX
    wm = N_HEADS_MEM * HEAD_DIM
    n_in = w_in.shape[2]
    assert n_in == n_causal * HEAD_GROUP * HEAD_DIM + N_HEADS_FOX + 2 * wm
    qm0 = n_causal * HEAD_GROUP
    gm0 = qm0 + N_HEADS_MEM

    x2 = x.reshape(batch * seq, d)
    mem2 = mem.reshape(batch * n_mem, d)
    for layer in range(depth):
        w_t3 = jnp.transpose(w_in, (0, 2, 1)).reshape(depth * n_in, d // LANES, LANES)
        w_main_t, w_gate_t, mkv = _w_repack_mem_proj(w_t3, layer, n_in, N_HEADS_MOBA, N_HEADS_FOX, mem2,
                                                     g_mem[layer].reshape(1, d), w_mem_kv)
        p, fb = _norm_proj(x2, g_pre[layer].reshape(1, d), w_main_t, w_gate_t,
                           tm=IN_PROJ_TILE[0], tn=IN_PROJ_TILE[1])
        ym, kc = _mem_attention(p, mkv, fb, b_forget, layer, batch, seq, n_mem, qm0, gm0)
        ya, yb, w_out_bf = _causal_attention(p, rel_bias, kc, w_out, layer, batch, seq)
        x2 = _out_proj(ya, yb, ym, w_out_bf, x2, g_post[layer].reshape(1, d), tm=OUT_PROJ_ROWS)
    return x2.reshape(batch, seq, d)
```

```python
import functools
import math

import jax
import jax.numpy as jnp
from jax import lax
from jax.experimental import pallas as pl
from jax.experimental.pallas import tpu as pltpu

HEAD_DIM = 128
N_HEADS_MOBA = 6
N_HEADS_FOX = 6
N_HEADS_MEM = 4
MOBA_BLOCK = 256
MOBA_TOPK = 3
T5_BUCKETS = 32
T5_MAX_DIST = 128
EPS = 1e-6
NEG = -1e30

LANES = 128
SUBLANES = 8
VMEM_LIMIT_BYTES = 56 * 1024 * 1024

IN_PROJ_TILE = (1024, 1792)
NORM_ROW_CHUNK = 256
OUT_PROJ_ROWS = 1024
HEAD_GROUP = 4
W_OUT_CAST_ROWS = 128
Q_TILE = MOBA_BLOCK
MEM_Q_TILE = 1024

F32 = jnp.float32
BF16 = jnp.bfloat16
_NT = (((1,), (1,)), ((), ()))


def _cparams(n_axes):
    return pltpu.CompilerParams(dimension_semantics=("arbitrary",) * n_axes,
                                vmem_limit_bytes=VMEM_LIMIT_BYTES)


def _norm_proj_kernel(x_ref, g_ref, w_ref, wf_ref, p_ref, f_ref, h_ref, *, n_col_blocks, row_chunk):
    tm = x_ref.shape[0]
    j = pl.program_id(1)

    def project(rows):
        acc = lax.dot_general(h_ref[rows, :], w_ref[...], _NT, preferred_element_type=F32)
        for c in range(n_col_blocks):
            p_ref[c, rows, :] = acc[:, c * LANES:(c + 1) * LANES].astype(BF16)

    @pl.when(j == 0)
    def _normalize_and_project():
        for c in range(tm // row_chunk):
            rows = slice(c * row_chunk, (c + 1) * row_chunk)
            xf = x_ref[rows, :]
            ms = jnp.mean(xf * xf, axis=-1, keepdims=True)
            h = (xf * lax.rsqrt(ms + EPS) * g_ref[...]).astype(BF16)
            h_ref[rows, :] = h
            f_ref[rows, :] = lax.dot_general(h, wf_ref[...], _NT, preferred_element_type=F32)
            project(rows)

    @pl.when(j > 0)
    def _project():
        project(slice(0, tm))


def _norm_proj(x2, g, w_t, wf_t, *, tm, tn):
    m, d = x2.shape
    n = w_t.shape[0]
    ncb = tn // LANES
    assert n % tn == 0 and m % tm == 0
    return pl.pallas_call(
        functools.partial(_norm_proj_kernel, n_col_blocks=ncb, row_chunk=NORM_ROW_CHUNK),
        grid=(m // tm, n // tn),
        in_specs=[pl.BlockSpec((tm, d), lambda i, j: (i, 0)),
                  pl.BlockSpec((1, d), lambda i, j: (0, 0)),
                  pl.BlockSpec((tn, d), lambda i, j: (j, 0)),
                  pl.BlockSpec((LANES, d), lambda i, j: (0, 0))],
        out_specs=[pl.BlockSpec((ncb, tm, LANES), lambda i, j: (j, i, 0)),
                   pl.BlockSpec((tm, LANES), lambda i, j: (i, 0))],
        out_shape=[jax.ShapeDtypeStruct((n // LANES, m, LANES), BF16),
                   jax.ShapeDtypeStruct((m, LANES), F32)],
        scratch_shapes=[pltpu.VMEM((tm, d), BF16)],
        compiler_params=_cparams(2),
        name="norm_proj_gate",
    )(x2, g, w_t, wf_t)


def _w_repack_kernel(src_ref, mem_ref, gm_ref, wm_ref, main_ref, gate_ref, mkv_ref,
                     buf_ref, gbuf_ref, hm_ref, sem_ref, gsem_ref, *,
                     base_row, n_heads, n_aligned, n_gate, n_mem_tiles, row_chunk):
    t = pl.program_id(0)
    n_steps = pl.num_programs(0)
    n_slices = src_ref.shape[1]

    def source_row(step, g):
        head_blk = (step // n_heads) * (HEAD_GROUP * n_heads) + g * n_heads + step % n_heads
        blk = jnp.where(step < 2 * n_heads, head_blk, HEAD_GROUP * step + g)
        return base_row + blk * LANES + jnp.where(blk * LANES >= n_aligned, n_gate, 0)

    def block_copies(step, slot):
        return [pltpu.make_async_copy(src_ref.at[pl.ds(source_row(step, g), LANES), s, :],
                                      buf_ref.at[slot, pl.ds(g * LANES, LANES), pl.ds(s * LANES, LANES)],
                                      sem_ref.at[slot])
                for g in range(HEAD_GROUP) for s in range(n_slices)]

    def gate_copies():
        gate_row0 = base_row + n_aligned
        return [pltpu.make_async_copy(src_ref.at[pl.ds(gate_row0, SUBLANES), s, :],
                                      gbuf_ref.at[:, pl.ds(s * LANES, LANES)],
                                      gsem_ref.at[0]) for s in range(n_slices)]

    @pl.when(t == 0)
    def _first():
        for cp in gate_copies() + block_copies(0, 0):
            cp.start()

    @pl.when(t + 1 < n_steps)
    def _prefetch():
        for cp in block_copies(t + 1, (t + 1) % 2):
            cp.start()

    @pl.when(t == 0)
    def _gate_rows_and_mem_norm():
        for cp in gate_copies():
            cp.wait()
        row = lax.broadcasted_iota(jnp.int32, gbuf_ref.shape, 0)
        gate = jnp.where(row < n_gate, gbuf_ref[...], 0.0)
        gate_ref[...] = jnp.concatenate(
            [gate, jnp.zeros((gate_ref.shape[0] - SUBLANES, gate.shape[1]), F32)], axis=0).astype(BF16)
        for c in range(mem_ref.shape[0] // row_chunk):
            r = slice(c * row_chunk, (c + 1) * row_chunk)
            xf = mem_ref[r, :]
            ms = jnp.mean(xf * xf, axis=-1, keepdims=True)
            hm_ref[r, :] = (xf * lax.rsqrt(ms + EPS) * gm_ref[...]).astype(BF16)

    @pl.when(t < n_mem_tiles)
    def _mem_projection():
        mkv_ref[...] = jnp.dot(hm_ref[...], wm_ref[...].astype(BF16),
                               preferred_element_type=F32).astype(BF16)

    for cp in block_copies(t, t % 2):
        cp.wait()
    main_ref[...] = buf_ref[t % 2].astype(BF16)


def _w_repack_mem_proj(w_t3, layer, n_total, n_heads, n_gate, mem2, g_mem, w_mem3):
    n_slices = w_t3.shape[1]
    d = n_slices * LANES
    rows = HEAD_GROUP * LANES
    n_aligned = 2 * n_heads * rows
    n_main = n_total - n_gate
    m_mem = mem2.shape[0]
    n_mem_tiles = w_mem3.shape[2] // LANES
    n_steps = n_main // rows
    assert n_main % rows == 0 and n_aligned <= n_main and n_gate <= SUBLANES
    assert w_mem3.shape[2] % LANES == 0 and n_mem_tiles <= n_steps
    mem_tile = lambda t: jnp.minimum(t, n_mem_tiles - 1)
    return pl.pallas_call(
        functools.partial(_w_repack_kernel, base_row=layer * n_total, n_heads=n_heads,
                          n_aligned=n_aligned, n_gate=n_gate,
                          n_mem_tiles=n_mem_tiles, row_chunk=NORM_ROW_CHUNK),
        grid=(n_steps,),
        in_specs=[pl.BlockSpec(memory_space=pl.ANY),
                  pl.BlockSpec((m_mem, d), lambda t: (0, 0), pipeline_mode=pl.Buffered(1)),
                  pl.BlockSpec((1, d), lambda t: (0, 0)),
                  pl.BlockSpec((None, d, LANES), lambda t: (layer, 0, mem_tile(t)))],
        out_specs=[pl.BlockSpec((rows, d), lambda t: (t, 0)),
                   pl.BlockSpec((LANES, d), lambda t: (0, 0)),
                   pl.BlockSpec((None, m_mem, LANES), lambda t: (mem_tile(t), 0, 0))],
        out_shape=[jax.ShapeDtypeStruct((n_main, d), BF16),
                   jax.ShapeDtypeStruct((LANES, d), BF16),
                   jax.ShapeDtypeStruct((n_mem_tiles, m_mem, LANES), BF16)],
        scratch_shapes=[pltpu.VMEM((2, rows, d), F32), pltpu.VMEM((SUBLANES, d), F32),
                        pltpu.VMEM((m_mem, d), BF16),
                        pltpu.SemaphoreType.DMA((2,)), pltpu.SemaphoreType.DMA((1,))],
        compiler_params=_cparams(1),
        name="w_repack_mem_proj",
    )(w_t3, mem2, g_mem, w_mem3)


N_SPLIT = 3


def _split3(v):
    hi = v.astype(BF16).astype(F32)
    mid = (v - hi).astype(BF16).astype(F32)
    lo = (v - hi - mid).astype(BF16).astype(F32)
    return hi, mid, lo


def _forget_bias_columns(f_ref, b_ref, layer, kc_ref, n_heads):
    head_lane = lax.broadcasted_iota(jnp.int32, (1, LANES), 1)
    bias = jnp.zeros((1, LANES), F32)
    for hh in range(n_heads):
        bias = jnp.where(head_lane == hh, b_ref[layer, hh], bias)
    z = f_ref[...] + bias
    logf = jnp.minimum(z, 0.0) - jnp.log1p(jnp.exp(-jnp.abs(z)))
    x = logf.T[0:SUBLANES, :]
    s_len = x.shape[1]
    lane = lax.broadcasted_iota(jnp.int32, x.shape, 1)
    shift = 1
    while shift < s_len:
        x = x + jnp.where(lane >= shift, pltpu.roll(x, shift, 1), 0.0)
        shift *= 2
    row = lax.broadcasted_iota(jnp.int32, x.shape, 0)
    x = jnp.where(row < n_heads, x, 0.0)
    terms = _split3(x * (-1.0 / HEAD_DIM ** -0.5))
    pad = jnp.zeros((LANES - SUBLANES * N_SPLIT, s_len), F32)
    kc_ref[...] = jnp.concatenate(list(terms) + [pad], axis=0).T.astype(BF16)


LOG2E = math.log2(math.e)


def _softmax_pv_folded(a, v_aug, g):
    m = jnp.max(a, axis=1, keepdims=True)
    p = jnp.exp2((a - m) * (HEAD_DIM ** -0.5 * LOG2E)).astype(BF16)
    acc = jnp.dot(p, v_aug, preferred_element_type=F32)
    gf = g.astype(F32)
    silu = gf * (1.0 / (1.0 + jnp.exp(-gf)))
    return (acc[:, :LANES] / acc[:, LANES:] * silu).astype(BF16)


SCORE_AHEAD = 1


def _stream_events(setup, n_tiles, score_matmul, finish):
    scores = {}

    def score(i):
        return lambda: scores.__setitem__(i, score_matmul(i))

    def fin(i):
        return lambda: finish(i, scores.pop(i))

    events = [setup] + [score(i) for i in range(min(SCORE_AHEAD, n_tiles))]
    for i in range(n_tiles):
        if i + SCORE_AHEAD < n_tiles:
            events.append(score(i + SCORE_AHEAD))
        events.append(fin(i))
    return events


def _run_interleaved(streams, lead=0):
    queues = [list(ev) for ev in streams]
    for _ in range(min(lead, len(queues[0]))):
        queues[0].pop(0)()
    while any(queues):
        for q in queues:
            if q:
                q.pop(0)()


def _t5_bias(rb_ref, h, dist):
    max_exact = T5_BUCKETS // 2
    n = jnp.maximum(dist, 0)
    nf = jnp.maximum(n, 1).astype(F32)
    val = jnp.log(nf / max_exact) / math.log(T5_MAX_DIST / max_exact) * (T5_BUCKETS - max_exact)
    bias = jnp.zeros(dist.shape, F32)
    for kk in range(T5_BUCKETS):
        if kk < max_exact:
            hit = n == kk
        elif kk < T5_BUCKETS - 1:
            hit = (n >= max_exact) & (val >= kk - max_exact) & (val < kk - max_exact + 1)
        else:
            hit = (n >= max_exact) & (val >= kk - max_exact)
        bias = jnp.where(hit, rb_ref[h, kk], bias)
    return bias


def _moba_head_constants(rb_ref, h, kaug_ref, bown_ref, bprev_ref):
    inv_scale = 1.0 / HEAD_DIM ** -0.5
    blk = MOBA_BLOCK
    seq = kaug_ref.shape[0]
    far_bias = rb_ref[h, T5_BUCKETS - 1]
    srow = lax.broadcasted_iota(jnp.int32, (seq, LANES), 0)
    col = lax.broadcasted_iota(jnp.int32, (seq, LANES), 1)
    kaug_ref[:, LANES:] = jnp.where((col & (SUBLANES - 1)) * blk == srow - (srow & (blk - 1)),
                                    1.0, 0.0).astype(BF16)
    t = T5_MAX_DIST
    assert blk == 2 * t
    dist = lax.broadcasted_iota(jnp.int32, (t, t), 0) - lax.broadcasted_iota(jnp.int32, (t, t), 1)
    near = jnp.where(dist >= 0, (_t5_bias(rb_ref, h, dist) - far_bias) * inv_scale, NEG)
    mid = (_t5_bias(rb_ref, h, dist + t) - far_bias) * inv_scale
    bown_ref[0:t, 0:t] = near
    bown_ref[0:t, t:blk] = jnp.full((t, t), NEG, F32)
    bown_ref[t:blk, 0:t] = mid
    bown_ref[t:blk, t:blk] = near
    bprev_ref[...] = jnp.zeros((blk, blk), F32)
    bprev_ref[0:t, t:blk] = mid


def _moba_stream(q_ref, k_ref, v_ref, g_ref, o_ref,
                 kaug_ref, qaug_ref, vaug_ref, kmean_ref, bown_ref, bprev_ref):
    blk = MOBA_BLOCK
    seq = q_ref.shape[0]
    n_blocks = seq // blk
    n_chunks = seq // LANES

    def setup():
        _moba_setup(q_ref, k_ref, v_ref, kaug_ref, qaug_ref, vaug_ref, kmean_ref, n_blocks, n_chunks)

    def score_matmul(i):
        return lax.dot_general(qaug_ref[i * blk:(i + 1) * blk, :], kaug_ref[0:(i + 1) * blk, :], _NT,
                               preferred_element_type=F32)

    def finish(i, a):
        rows = slice(i * blk, (i + 1) * blk)
        n = (i + 1) * blk
        parts = [a[:, n - blk:] + bown_ref[...]]
        if i >= 1:
            parts.insert(0, a[:, n - 2 * blk:n - blk] + bprev_ref[...])
        if i >= 2:
            parts.insert(0, a[:, :n - 2 * blk])
        a = parts[0] if len(parts) == 1 else jnp.concatenate(parts, axis=1)
        o_ref[rows, :] = _softmax_pv_folded(a, vaug_ref[0:n, :], g_ref[rows, :])

    return setup, n_blocks, score_matmul, finish


def _moba_setup(q_ref, k_ref, v_ref, kaug_ref, qaug_ref, vaug_ref, kmean_ref, n_blocks, n_chunks):
    blk = MOBA_BLOCK
    seq = q_ref.shape[0]
    kaug_ref[:, :LANES] = k_ref[...]
    vaug_ref[:, :LANES] = v_ref[...]
    vaug_ref[:, LANES:] = jnp.ones((seq, LANES), BF16)
    blk_row = lax.broadcasted_iota(jnp.int32, (SUBLANES, LANES), 0)
    km = jnp.zeros((SUBLANES, LANES), F32)
    for j in range(n_blocks):
        mean_j = jnp.mean(k_ref[j * blk:(j + 1) * blk, :].astype(F32), axis=0, keepdims=True)
        km = jnp.where(blk_row == j, mean_j, km)
    kmean_ref[...] = jnp.concatenate([km, jnp.zeros((LANES - SUBLANES, LANES), F32)], axis=0).astype(BF16)

    gate_t = lax.dot_general(kmean_ref[...], q_ref[...], _NT, preferred_element_type=F32)[0:SUBLANES, :]
    jrow = lax.broadcasted_iota(jnp.int32, gate_t.shape, 0)
    qpos = lax.broadcasted_iota(jnp.int32, gate_t.shape, 1)
    qblk = jnp.right_shift(qpos, int(math.log2(blk)))
    gm = jnp.where(jrow < qblk, gate_t, NEG)
    sel_bias = jnp.zeros(gate_t.shape, F32)
    for j in range(n_blocks):
        gj = gm[j:j + 1, :]
        beats = (gm > gj) | ((gm == gj) & (jrow < j))
        rank = jnp.sum(beats.astype(F32), axis=0, keepdims=True)
        allowed = ((rank < MOBA_TOPK) & (qblk[0:1, :] > j)) | (qblk[0:1, :] == j)
        sel_bias = jnp.where(jrow == j, jnp.where(allowed, 0.0, NEG), sel_bias)
    assert SUBLANES * n_chunks == LANES
    packed_t = jnp.concatenate([sel_bias[:, c * LANES:(c + 1) * LANES] for c in range(n_chunks)], axis=0).T
    lane_chunk = jnp.right_shift(lax.broadcasted_iota(jnp.int32, (LANES, LANES), 1),
                                 int(math.log2(SUBLANES)))
    qaug_ref[:, :LANES] = q_ref[...]
    for c in range(n_chunks):
        qaug_ref[c * LANES:(c + 1) * LANES, LANES:] = jnp.where(lane_chunk == c, packed_t, 0.0).astype(BF16)


def _moba_scratch(seq):
    blk = MOBA_BLOCK
    return [pltpu.VMEM((seq, 2 * LANES), BF16),
            pltpu.VMEM((seq, 2 * LANES), BF16),
            pltpu.VMEM((seq, 2 * LANES), BF16),
            pltpu.VMEM((LANES, LANES), BF16),
            pltpu.VMEM((blk, blk), F32),
            pltpu.VMEM((blk, blk), F32)]


def _fox_stream(h, q_ref, k_ref, v_ref, g_ref, kc_ref, o_ref, qaug_ref, kaug_ref, vaug_ref, causal_ref):
    seq = q_ref.shape[0]

    def setup():
        col = lax.broadcasted_iota(jnp.int32, (seq, LANES), 1)
        qaug_ref[:, :LANES] = q_ref[...]
        qaug_ref[:, LANES:] = jnp.where(((col & (SUBLANES - 1)) == h) & (col < SUBLANES * N_SPLIT),
                                        1.0, 0.0).astype(BF16)
        kaug_ref[:, :LANES] = k_ref[...]
        kaug_ref[:, LANES:] = kc_ref[...]
        vaug_ref[:, :LANES] = v_ref[...]
        vaug_ref[:, LANES:] = jnp.ones((seq, LANES), BF16)
        r = lax.broadcasted_iota(jnp.int32, (Q_TILE, Q_TILE), 0)
        c = lax.broadcasted_iota(jnp.int32, (Q_TILE, Q_TILE), 1)
        causal_ref[...] = jnp.where(c <= r, 0.0, NEG)

    def score_matmul(i):
        return lax.dot_general(qaug_ref[i * Q_TILE:(i + 1) * Q_TILE, :], kaug_ref[0:(i + 1) * Q_TILE, :],
                               _NT, preferred_element_type=F32)

    def finish(i, a):
        rows = slice(i * Q_TILE, (i + 1) * Q_TILE)
        n = (i + 1) * Q_TILE
        own = a[:, n - Q_TILE:] + causal_ref[...]
        a = own if i == 0 else jnp.concatenate([a[:, :n - Q_TILE], own], axis=1)
        o_ref[rows, :] = _softmax_pv_folded(a, vaug_ref[0:n, :], g_ref[rows, :])

    return setup, seq // Q_TILE, score_matmul, finish


def _fox_scratch(seq):
    return [pltpu.VMEM((seq, 2 * LANES), BF16), pltpu.VMEM((seq, 2 * LANES), BF16),
            pltpu.VMEM((seq, 2 * LANES), BF16), pltpu.VMEM((Q_TILE, Q_TILE), F32)]


FOX_LEAD_EVENTS = 11


def _causal_attention_kernel(*refs):
    rb_ref, moba_qkvg, fox_qkvg, kc, wo_ref, oa, ob, wob_ref = refs[:8]
    qa, ka, va, ga = (moba_qkvg.at[i] for i in range(HEAD_GROUP))
    qb, kb, vb, gb = (fox_qkvg.at[i] for i in range(HEAD_GROUP))
    n_moba = len(_moba_scratch(qa.shape[0]))
    moba_scratch, fox_scratch = refs[8:8 + n_moba], refs[8 + n_moba:]
    kaug_ref, _, _, _, bown_ref, bprev_ref = moba_scratch
    h = pl.program_id(0)

    @pl.when(pl.program_id(1) == 0)
    def _():
        _moba_head_constants(rb_ref, h, kaug_ref, bown_ref, bprev_ref)

    wob_ref[...] = wo_ref[...].astype(BF16)
    _run_interleaved([_stream_events(*_fox_stream(h, qb, kb, vb, gb, kc, ob, *fox_scratch)),
                      _stream_events(*_moba_stream(qa, ka, va, ga, oa, *moba_scratch))],
                     lead=FOX_LEAD_EVENTS)


def _causal_attention(p, rel_bias, kc, w_out3, layer, batch, seq):
    nb = seq // MOBA_BLOCK
    assert seq % MOBA_BLOCK == 0 and MOBA_TOPK < SUBLANES and nb <= SUBLANES and MOBA_BLOCK > T5_MAX_DIST
    assert N_HEADS_MOBA == N_HEADS_FOX <= SUBLANES
    dw, d = w_out3.shape[1:]
    n_slabs = dw // W_OUT_CAST_ROWS
    assert dw % W_OUT_CAST_ROWS == 0 and n_slabs <= N_HEADS_MOBA * batch
    slab = lambda h, b: jnp.minimum(h * batch + b, n_slabs - 1)
    head = lambda first: pl.BlockSpec((HEAD_GROUP, seq, LANES), lambda h, b: (first + h, b, 0))
    out = pl.BlockSpec((None, seq, LANES), lambda h, b: (h, b, 0))
    out_sds = jax.ShapeDtypeStruct((N_HEADS_MOBA, batch * seq, LANES), BF16)
    return pl.pallas_call(
        _causal_attention_kernel,
        grid=(N_HEADS_MOBA, batch),
        in_specs=[pl.BlockSpec(memory_space=pltpu.SMEM), head(0), head(N_HEADS_MOBA),
                  pl.BlockSpec((None, seq, LANES), lambda h, b: (b, 0, 0)),
                  pl.BlockSpec((None, W_OUT_CAST_ROWS, d), lambda h, b: (layer, slab(h, b), 0))],
        out_specs=[out, out, pl.BlockSpec((W_OUT_CAST_ROWS, d), lambda h, b: (slab(h, b), 0))],
        out_shape=[out_sds, out_sds, jax.ShapeDtypeStruct((dw, d), BF16)],
        scratch_shapes=_moba_scratch(seq) + _fox_scratch(seq),
        compiler_params=_cparams(2),
        name="causal_attention",
    )(rel_bias, p, p, kc, w_out3)


def _mem_kernel(b_ref, q_ref, k_ref, v_ref, g_ref, f_ref, o_ref, kc_ref, vaug_ref, *, layer, n_fox_heads):
    n_heads, seq, _ = q_ref.shape
    tiles_per_head = seq // MEM_Q_TILE

    _forget_bias_columns(f_ref, b_ref, layer, kc_ref, n_fox_heads)
    vaug_ref[:, :, LANES:] = jnp.ones(v_ref.shape, BF16)
    vaug_ref[:, :, :LANES] = v_ref[...]

    def tile(t):
        h, i = divmod(t, tiles_per_head)
        return h, slice(i * MEM_Q_TILE, (i + 1) * MEM_Q_TILE)

    def score_matmul(t):
        h, rows = tile(t)
        return lax.dot_general(q_ref[h, rows, :], k_ref[h], _NT, preferred_element_type=F32)

    def finish(t, a):
        h, rows = tile(t)
        o_ref[h, rows, :] = _softmax_pv_folded(a, vaug_ref[h], g_ref[h, rows, :])

    _run_interleaved([_stream_events(lambda: None, n_heads * tiles_per_head, score_matmul, finish)])


def _mem_attention(p, mkv, fb, b_forget, layer, batch, seq, n_mem, q0, g0):
    nh = N_HEADS_MEM
    assert q0 % nh == 0 and g0 % nh == 0 and N_HEADS_FOX <= SUBLANES
    heads = lambda off: pl.BlockSpec((nh, seq, LANES), lambda b: (off // nh, b, 0))
    kv = lambda blk: pl.BlockSpec((nh, n_mem, LANES), lambda b: (blk, b, 0))
    return pl.pallas_call(
        functools.partial(_mem_kernel, layer=layer, n_fox_heads=N_HEADS_FOX),
        grid=(batch,),
        in_specs=[pl.BlockSpec(memory_space=pltpu.SMEM),
                  heads(q0), kv(0), kv(1), heads(g0),
                  pl.BlockSpec((seq, LANES), lambda b: (b, 0))],
        out_specs=[pl.BlockSpec((nh, seq, LANES), lambda b: (0, b, 0)),
                   pl.BlockSpec((None, seq, LANES), lambda b: (b, 0, 0))],
        out_shape=[jax.ShapeDtypeStruct((nh, batch * seq, LANES), BF16),
                   jax.ShapeDtypeStruct((batch, seq, LANES), BF16)],
        scratch_shapes=[pltpu.VMEM((nh, n_mem, 2 * LANES), BF16)],
        compiler_params=_cparams(1),
        name="mem_attention",
    )(b_forget, p, mkv, mkv, p, fb)


def _out_proj_kernel(ya_ref, yb_ref, ym_ref, w_ref, x_ref, g_ref, o_ref):
    y = jnp.concatenate([src[c] for src in (ya_ref, yb_ref, ym_ref) for c in range(src.shape[0])], axis=1)
    o = jnp.dot(y, w_ref[...], preferred_element_type=F32)
    ms = jnp.mean(o * o, axis=-1, keepdims=True)
    o_ref[...] = x_ref[...] + o * lax.rsqrt(ms + EPS) * g_ref[...]


def _out_proj(ya, yb, ym, w, x2, g, *, tm):
    m, d = x2.shape
    heads = lambda y: pl.BlockSpec((y.shape[0], tm, LANES), lambda i: (0, i, 0))
    return pl.pallas_call(
        _out_proj_kernel,
        grid=(m // tm,),
        in_specs=[heads(ya), heads(yb), heads(ym),
                  pl.BlockSpec(w.shape, lambda i: (0, 0), pipeline_mode=pl.Buffered(1)),
                  pl.BlockSpec((tm, d), lambda i: (i, 0)),
                  pl.BlockSpec((1, d), lambda i: (0, 0))],
        out_specs=pl.BlockSpec((tm, d), lambda i: (i, 0)),
        out_shape=jax.ShapeDtypeStruct((m, d), F32),
        compiler_params=_cparams(1),
        name="out_proj",
    )(ya, yb, ym, w, x2, g)


def kernel(x, mem, w_in, b_forget, w_mem_kv, w_out, g_pre, g_post, g_mem, rel_bias):
    batch, seq, d = x.shape
    n_mem = mem.shape[1]
    depth = w_in.shape[0]
    assert N_HEADS_MOBA == N_HEADS_FOX
    n_causal = N_HEADS_MOBA + N_HEADS_FOX
    wm = N_HEADS_MEM * HEAD_DIM
    n_in = w_in.shape[2]
    assert n_in == n_causal * HEAD_GROUP * HEAD_DIM + N_HEADS_FOX + 2 * wm
    qm0 = n_causal * HEAD_GROUP
    gm0 = qm0 + N_HEADS_MEM

    x2 = x.reshape(batch * seq, d)
    mem2 = mem.reshape(batch * n_mem, d)
    for layer in range(depth):
        w_t3 = jnp.transpose(w_in, (0, 2, 1)).reshape(depth * n_in, d // LANES, LANES)
        w_main_t, w_gate_t, mkv = _w_repack_mem_proj(w_t3, layer, n_in, N_HEADS_MOBA, N_HEADS_FOX, mem2,
                                                     g_mem[layer].reshape(1, d), w_mem_kv)
        p, fb = _norm_proj(x2, g_pre[layer].reshape(1, d), w_main_t, w_gate_t,
                           tm=IN_PROJ_TILE[0], tn=IN_PROJ_TILE[1])
        ym, kc = _mem_attention(p, mkv, fb, b_forget, layer, batch, seq, n_mem, qm0, gm0)
        ya, yb, w_out_bf = _causal_attention(p, rel_bias, kc, w_out, layer, batch, seq)
        x2 = _out_proj(ya, yb, ym, w_out_bf, x2, g_post[layer].reshape(1, d), tm=OUT_PROJ_ROWS)
    return x2.reshape(batch, seq, d)
```

```python
import functools
import math

import jax
import jax.numpy as jnp
from jax import lax
from jax.experimental import pallas as pl
from jax.experimental.pallas import tpu as pltpu

HEAD_DIM = 128
N_HEADS_MOBA = 6
N_HEADS_FOX = 6
N_HEADS_MEM = 4
MOBA_BLOCK = 256
MOBA_TOPK = 3
T5_BUCKETS = 32
T5_MAX_DIST = 128
EPS = 1e-6
NEG = -1e30

LANES = 128
SUBLANES = 8
VMEM_LIMIT_BYTES = 56 * 1024 * 1024

IN_PROJ_TILE = (1024, 1792)
NORM_ROW_CHUNK = 256
OUT_PROJ_ROWS = 512
HEAD_GROUP = 4
W_OUT_CAST_ROWS = 128
Q_TILE = MOBA_BLOCK
MEM_Q_TILE = 1024

F32 = jnp.float32
BF16 = jnp.bfloat16
_NT = (((1,), (1,)), ((), ()))


def _cparams(n_axes):
    return pltpu.CompilerParams(dimension_semantics=("arbitrary",) * n_axes,
                                vmem_limit_bytes=VMEM_LIMIT_BYTES)


def _norm_proj_kernel(x_ref, g_ref, w_ref, wf_ref, p_ref, f_ref, h_ref, *, n_col_blocks, row_chunk):
    tm = x_ref.shape[0]
    j = pl.program_id(1)

    def project(rows):
        acc = lax.dot_general(h_ref[rows, :], w_ref[...], _NT, preferred_element_type=F32)
        for c in range(n_col_blocks):
            p_ref[c, rows, :] = acc[:, c * LANES:(c + 1) * LANES].astype(BF16)

    @pl.when(j == 0)
    def _normalize_and_project():
        for c in range(tm // row_chunk):
            rows = slice(c * row_chunk, (c + 1) * row_chunk)
            xf = x_ref[rows, :]
            ms = jnp.mean(xf * xf, axis=-1, keepdims=True)
            h = (xf * lax.rsqrt(ms + EPS) * g_ref[...]).astype(BF16)
            h_ref[rows, :] = h
            f_ref[rows, :] = lax.dot_general(h, wf_ref[...], _NT, preferred_element_type=F32)
            project(rows)

    @pl.when(j > 0)
    def _project():
        project(slice(0, tm))


def _norm_proj(x2, g, w_t, wf_t, *, tm, tn):
    m, d = x2.shape
    n = w_t.shape[0]
    ncb = tn // LANES
    assert n % tn == 0 and m % tm == 0
    return pl.pallas_call(
        functools.partial(_norm_proj_kernel, n_col_blocks=ncb, row_chunk=NORM_ROW_CHUNK),
        grid=(m // tm, n // tn),
        in_specs=[pl.BlockSpec((tm, d), lambda i, j: (i, 0)),
                  pl.BlockSpec((1, d), lambda i, j: (0, 0)),
                  pl.BlockSpec((tn, d), lambda i, j: (j, 0)),
                  pl.BlockSpec((LANES, d), lambda i, j: (0, 0))],
        out_specs=[pl.BlockSpec((ncb, tm, LANES), lambda i, j: (j, i, 0)),
                   pl.BlockSpec((tm, LANES), lambda i, j: (i, 0))],
        out_shape=[jax.ShapeDtypeStruct((n // LANES, m, LANES), BF16),
                   jax.ShapeDtypeStruct((m, LANES), F32)],
        scratch_shapes=[pltpu.VMEM((tm, d), BF16)],
        compiler_params=_cparams(2),
        name="norm_proj_gate",
    )(x2, g, w_t, wf_t)


def _w_repack_kernel(src_ref, mem_ref, gm_ref, wm_ref, main_ref, gate_ref, mkv_ref,
                     buf_ref, gbuf_ref, hm_ref, sem_ref, gsem_ref, *,
                     base_row, n_heads, n_aligned, n_gate, n_mem_tiles, row_chunk):
    t = pl.program_id(0)
    n_steps = pl.num_programs(0)
    n_slices = src_ref.shape[1]

    def source_row(step, g):
        head_blk = (step // n_heads) * (HEAD_GROUP * n_heads) + g * n_heads + step % n_heads
        blk = jnp.where(step < 2 * n_heads, head_blk, HEAD_GROUP * step + g)
        return base_row + blk * LANES + jnp.where(blk * LANES >= n_aligned, n_gate, 0)

    def block_copies(step, slot):
        return [pltpu.make_async_copy(src_ref.at[pl.ds(source_row(step, g), LANES), s, :],
                                      buf_ref.at[slot, pl.ds(g * LANES, LANES), pl.ds(s * LANES, LANES)],
                                      sem_ref.at[slot])
                for g in range(HEAD_GROUP) for s in range(n_slices)]

    def gate_copies():
        gate_row0 = base_row + n_aligned
        return [pltpu.make_async_copy(src_ref.at[pl.ds(gate_row0, SUBLANES), s, :],
                                      gbuf_ref.at[:, pl.ds(s * LANES, LANES)],
                                      gsem_ref.at[0]) for s in range(n_slices)]

    def start_all(copies):
        for n, cp in enumerate(copies):
            cp.start(priority=n % 2)

    @pl.when(t == 0)
    def _first():
        start_all(gate_copies() + block_copies(0, 0))

    @pl.when(t + 1 < n_steps)
    def _prefetch():
        start_all(block_copies(t + 1, (t + 1) % 2))

    @pl.when(t == 0)
    def _gate_rows_and_mem_norm():
        for cp in gate_copies():
            cp.wait()
        row = lax.broadcasted_iota(jnp.int32, gbuf_ref.shape, 0)
        gate = jnp.where(row < n_gate, gbuf_ref[...], 0.0)
        gate_ref[...] = jnp.concatenate(
            [gate, jnp.zeros((gate_ref.shape[0] - SUBLANES, gate.shape[1]), F32)], axis=0).astype(BF16)
        for c in range(mem_ref.shape[0] // row_chunk):
            r = slice(c * row_chunk, (c + 1) * row_chunk)
            xf = mem_ref[r, :]
            ms = jnp.mean(xf * xf, axis=-1, keepdims=True)
            hm_ref[r, :] = (xf * lax.rsqrt(ms + EPS) * gm_ref[...]).astype(BF16)

    @pl.when(t < n_mem_tiles)
    def _mem_projection():
        mkv_ref[...] = jnp.dot(hm_ref[...], wm_ref[...].astype(BF16),
                               preferred_element_type=F32).astype(BF16)

    for cp in block_copies(t, t % 2):
        cp.wait()
    main_ref[...] = buf_ref[t % 2].astype(BF16)


def _w_repack_mem_proj(w_t3, layer, n_total, n_heads, n_gate, mem2, g_mem, w_mem3):
    n_slices = w_t3.shape[1]
    d = n_slices * LANES
    rows = HEAD_GROUP * LANES
    n_aligned = 2 * n_heads * rows
    n_main = n_total - n_gate
    m_mem = mem2.shape[0]
    n_mem_tiles = w_mem3.shape[2] // LANES
    n_steps = n_main // rows
    assert n_main % rows == 0 and n_aligned <= n_main and n_gate <= SUBLANES
    assert w_mem3.shape[2] % LANES == 0 and n_mem_tiles <= n_steps
    mem_tile = lambda t: jnp.minimum(t, n_mem_tiles - 1)
    return pl.pallas_call(
        functools.partial(_w_repack_kernel, base_row=layer * n_total, n_heads=n_heads,
                          n_aligned=n_aligned, n_gate=n_gate,
                          n_mem_tiles=n_mem_tiles, row_chunk=NORM_ROW_CHUNK),
        grid=(n_steps,),
        in_specs=[pl.BlockSpec(memory_space=pl.ANY),
                  pl.BlockSpec((m_mem, d), lambda t: (0, 0), pipeline_mode=pl.Buffered(1)),
                  pl.BlockSpec((1, d), lambda t: (0, 0)),
                  pl.BlockSpec((None, d, LANES), lambda t: (layer, 0, mem_tile(t)))],
        out_specs=[pl.BlockSpec((rows, d), lambda t: (t, 0)),
                   pl.BlockSpec((LANES, d), lambda t: (0, 0)),
                   pl.BlockSpec((None, m_mem, LANES), lambda t: (mem_tile(t), 0, 0))],
        out_shape=[jax.ShapeDtypeStruct((n_main, d), BF16),
                   jax.ShapeDtypeStruct((LANES, d), BF16),
                   jax.ShapeDtypeStruct((n_mem_tiles, m_mem, LANES), BF16)],
        scratch_shapes=[pltpu.VMEM((2, rows, d), F32), pltpu.VMEM((SUBLANES, d), F32),
                        pltpu.VMEM((m_mem, d), BF16),
                        pltpu.SemaphoreType.DMA((2,)), pltpu.SemaphoreType.DMA((1,))],
        compiler_params=_cparams(1),
        name="w_repack_mem_proj",
    )(w_t3, mem2, g_mem, w_mem3)


N_SPLIT = 3


def _split3(v):
    hi = v.astype(BF16).astype(F32)
    mid = (v - hi).astype(BF16).astype(F32)
    lo = (v - hi - mid).astype(BF16).astype(F32)
    return hi, mid, lo


def _forget_bias_columns(f_ref, b_ref, layer, kc_ref, n_heads):
    head_lane = lax.broadcasted_iota(jnp.int32, (1, LANES), 1)
    bias = jnp.zeros((1, LANES), F32)
    for hh in range(n_heads):
        bias = jnp.where(head_lane == hh, b_ref[layer, hh], bias)
    z = f_ref[...] + bias
    logf = jnp.minimum(z, 0.0) - jnp.log1p(jnp.exp(-jnp.abs(z)))
    x = logf.T[0:SUBLANES, :]
    s_len = x.shape[1]
    lane = lax.broadcasted_iota(jnp.int32, x.shape, 1)
    shift = 1
    while shift < s_len:
        x = x + jnp.where(lane >= shift, pltpu.roll(x, shift, 1), 0.0)
        shift *= 2
    row = lax.broadcasted_iota(jnp.int32, x.shape, 0)
    x = jnp.where(row < n_heads, x, 0.0)
    terms = _split3(x * (-1.0 / HEAD_DIM ** -0.5))
    pad = jnp.zeros((LANES - SUBLANES * N_SPLIT, s_len), F32)
    kc_ref[...] = jnp.concatenate(list(terms) + [pad], axis=0).T.astype(BF16)


LOG2E = math.log2(math.e)


def _softmax_pv_folded(a, v_aug, g):
    m = jnp.max(a, axis=1, keepdims=True)
    p = jnp.exp2((a - m) * (HEAD_DIM ** -0.5 * LOG2E)).astype(BF16)
    acc = jnp.dot(p, v_aug, preferred_element_type=F32)
    gf = g.astype(F32)
    silu = gf * (1.0 / (1.0 + jnp.exp(-gf)))
    return (acc[:, :LANES] / acc[:, LANES:] * silu).astype(BF16)


SCORE_AHEAD = 1


def _stream_events(setup, n_tiles, score_matmul, finish):
    scores = {}

    def score(i):
        return lambda: scores.__setitem__(i, score_matmul(i))

    def fin(i):
        return lambda: finish(i, scores.pop(i))

    events = [setup] + [score(i) for i in range(min(SCORE_AHEAD, n_tiles))]
    for i in range(n_tiles):
        if i + SCORE_AHEAD < n_tiles:
            events.append(score(i + SCORE_AHEAD))
        events.append(fin(i))
    return events


def _run_interleaved(streams, lead=0):
    queues = [list(ev) for ev in streams]
    for _ in range(min(lead, len(queues[0]))):
        queues[0].pop(0)()
    while any(queues):
        for q in queues:
            if q:
                q.pop(0)()


def _t5_bias(rb_ref, h, dist):
    max_exact = T5_BUCKETS // 2
    n = jnp.maximum(dist, 0)
    nf = jnp.maximum(n, 1).astype(F32)
    val = jnp.log(nf / max_exact) / math.log(T5_MAX_DIST / max_exact) * (T5_BUCKETS - max_exact)
    bias = jnp.zeros(dist.shape, F32)
    for kk in range(T5_BUCKETS):
        if kk < max_exact:
            hit = n == kk
        elif kk < T5_BUCKETS - 1:
            hit = (n >= max_exact) & (val >= kk - max_exact) & (val < kk - max_exact + 1)
        else:
            hit = (n >= max_exact) & (val >= kk - max_exact)
        bias = jnp.where(hit, rb_ref[h, kk], bias)
    return bias


def _moba_head_constants(rb_ref, h, kaug_ref, bown_ref, bprev_ref):
    inv_scale = 1.0 / HEAD_DIM ** -0.5
    blk = MOBA_BLOCK
    seq = kaug_ref.shape[0]
    far_bias = rb_ref[h, T5_BUCKETS - 1]
    srow = lax.broadcasted_iota(jnp.int32, (seq, LANES), 0)
    col = lax.broadcasted_iota(jnp.int32, (seq, LANES), 1)
    kaug_ref[:, LANES:] = jnp.where((col & (SUBLANES - 1)) * blk == srow - (srow & (blk - 1)),
                                    1.0, 0.0).astype(BF16)
    t = T5_MAX_DIST
    assert blk == 2 * t
    dist = lax.broadcasted_iota(jnp.int32, (t, t), 0) - lax.broadcasted_iota(jnp.int32, (t, t), 1)
    near = jnp.where(dist >= 0, (_t5_bias(rb_ref, h, dist) - far_bias) * inv_scale, NEG)
    mid = (_t5_bias(rb_ref, h, dist + t) - far_bias) * inv_scale
    bown_ref[0:t, 0:t] = near
    bown_ref[0:t, t:blk] = jnp.full((t, t), NEG, F32)
    bown_ref[t:blk, 0:t] = mid
    bown_ref[t:blk, t:blk] = near
    bprev_ref[...] = jnp.zeros((blk, blk), F32)
    bprev_ref[0:t, t:blk] = mid


def _moba_stream(q_ref, k_ref, v_ref, g_ref, o_ref,
                 kaug_ref, qaug_ref, vaug_ref, kmean_ref, bown_ref, bprev_ref):
    blk = MOBA_BLOCK
    seq = q_ref.shape[0]
    n_blocks = seq // blk
    n_chunks = seq // LANES

    def setup():
        _moba_setup(q_ref, k_ref, v_ref, kaug_ref, qaug_ref, vaug_ref, kmean_ref, n_blocks, n_chunks)

    def score_matmul(i):
        return lax.dot_general(qaug_ref[i * blk:(i + 1) * blk, :], kaug_ref[0:(i + 1) * blk, :], _NT,
                               preferred_element_type=F32)

    def finish(i, a):
        rows = slice(i * blk, (i + 1) * blk)
        n = (i + 1) * blk
        parts = [a[:, n - blk:] + bown_ref[...]]
        if i >= 1:
            parts.insert(0, a[:, n - 2 * blk:n - blk] + bprev_ref[...])
        if i >= 2:
            parts.insert(0, a[:, :n - 2 * blk])
        a = parts[0] if len(parts) == 1 else jnp.concatenate(parts, axis=1)
        o_ref[rows, :] = _softmax_pv_folded(a, vaug_ref[0:n, :], g_ref[rows, :])

    return setup, n_blocks, score_matmul, finish


def _moba_setup(q_ref, k_ref, v_ref, kaug_ref, qaug_ref, vaug_ref, kmean_ref, n_blocks, n_chunks):
    blk = MOBA_BLOCK
    seq = q_ref.shape[0]
    kaug_ref[:, :LANES] = k_ref[...]
    vaug_ref[:, :LANES] = v_ref[...]
    vaug_ref[:, LANES:] = jnp.ones((seq, LANES), BF16)
    blk_row = lax.broadcasted_iota(jnp.int32, (SUBLANES, LANES), 0)
    km = jnp.zeros((SUBLANES, LANES), F32)
    for j in range(n_blocks):
        mean_j = jnp.mean(k_ref[j * blk:(j + 1) * blk, :].astype(F32), axis=0, keepdims=True)
        km = jnp.where(blk_row == j, mean_j, km)
    kmean_ref[...] = jnp.concatenate([km, jnp.zeros((LANES - SUBLANES, LANES), F32)], axis=0).astype(BF16)

    gate_t = lax.dot_general(kmean_ref[...], q_ref[...], _NT, preferred_element_type=F32)[0:SUBLANES, :]
    jrow = lax.broadcasted_iota(jnp.int32, gate_t.shape, 0)
    qpos = lax.broadcasted_iota(jnp.int32, gate_t.shape, 1)
    qblk = jnp.right_shift(qpos, int(math.log2(blk)))
    gm = jnp.where(jrow < qblk, gate_t, NEG)
    sel_bias = jnp.zeros(gate_t.shape, F32)
    for j in range(n_blocks):
        gj = gm[j:j + 1, :]
        beats = (gm > gj) | ((gm == gj) & (jrow < j))
        rank = jnp.sum(beats.astype(F32), axis=0, keepdims=True)
        allowed = ((rank < MOBA_TOPK) & (qblk[0:1, :] > j)) | (qblk[0:1, :] == j)
        sel_bias = jnp.where(jrow == j, jnp.where(allowed, 0.0, NEG), sel_bias)
    assert SUBLANES * n_chunks == LANES
    packed_t = jnp.concatenate([sel_bias[:, c * LANES:(c + 1) * LANES] for c in range(n_chunks)], axis=0).T
    lane_chunk = jnp.right_shift(lax.broadcasted_iota(jnp.int32, (LANES, LANES), 1),
                                 int(math.log2(SUBLANES)))
    qaug_ref[:, :LANES] = q_ref[...]
    for c in range(n_chunks):
        qaug_ref[c * LANES:(c + 1) * LANES, LANES:] = jnp.where(lane_chunk == c, packed_t, 0.0).astype(BF16)


def _moba_scratch(seq):
    blk = MOBA_BLOCK
    return [pltpu.VMEM((seq, 2 * LANES), BF16),
            pltpu.VMEM((seq, 2 * LANES), BF16),
            pltpu.VMEM((seq, 2 * LANES), BF16),
            pltpu.VMEM((LANES, LANES), BF16),
            pltpu.VMEM((blk, blk), F32),
            pltpu.VMEM((blk, blk), F32)]


def _fox_stream(h, q_ref, k_ref, v_ref, g_ref, kc_ref, o_ref, qaug_ref, kaug_ref, vaug_ref, causal_ref):
    seq = q_ref.shape[0]

    def setup():
        col = lax.broadcasted_iota(jnp.int32, (seq, LANES), 1)
        qaug_ref[:, :LANES] = q_ref[...]
        qaug_ref[:, LANES:] = jnp.where(((col & (SUBLANES - 1)) == h) & (col < SUBLANES * N_SPLIT),
                                        1.0, 0.0).astype(BF16)
        kaug_ref[:, :LANES] = k_ref[...]
        kaug_ref[:, LANES:] = kc_ref[...]
        vaug_ref[:, :LANES] = v_ref[...]
        vaug_ref[:, LANES:] = jnp.ones((seq, LANES), BF16)
        r = lax.broadcasted_iota(jnp.int32, (Q_TILE, Q_TILE), 0)
        c = lax.broadcasted_iota(jnp.int32, (Q_TILE, Q_TILE), 1)
        causal_ref[...] = jnp.where(c <= r, 0.0, NEG)

    def score_matmul(i):
        return lax.dot_general(qaug_ref[i * Q_TILE:(i + 1) * Q_TILE, :], kaug_ref[0:(i + 1) * Q_TILE, :],
                               _NT, preferred_element_type=F32)

    def finish(i, a):
        rows = slice(i * Q_TILE, (i + 1) * Q_TILE)
        n = (i + 1) * Q_TILE
        own = a[:, n - Q_TILE:] + causal_ref[...]
        a = own if i == 0 else jnp.concatenate([a[:, :n - Q_TILE], own], axis=1)
        o_ref[rows, :] = _softmax_pv_folded(a, vaug_ref[0:n, :], g_ref[rows, :])

    return setup, seq // Q_TILE, score_matmul, finish


def _fox_scratch(seq):
    return [pltpu.VMEM((seq, 2 * LANES), BF16), pltpu.VMEM((seq, 2 * LANES), BF16),
            pltpu.VMEM((seq, 2 * LANES), BF16), pltpu.VMEM((Q_TILE, Q_TILE), F32)]


FOX_LEAD_EVENTS = 11


def _causal_attention_kernel(*refs):
    rb_ref, moba_qkvg, fox_qkvg, kc, wo_ref, oa, ob, wob_ref = refs[:8]
    qa, ka, va, ga = (moba_qkvg.at[i] for i in range(HEAD_GROUP))
    qb, kb, vb, gb = (fox_qkvg.at[i] for i in range(HEAD_GROUP))
    n_moba = len(_moba_scratch(qa.shape[0]))
    moba_scratch, fox_scratch = refs[8:8 + n_moba], refs[8 + n_moba:]
    kaug_ref, _, _, _, bown_ref, bprev_ref = moba_scratch
    h = pl.program_id(0)

    @pl.when(pl.program_id(1) == 0)
    def _():
        _moba_head_constants(rb_ref, h, kaug_ref, bown_ref, bprev_ref)

    wob_ref[...] = wo_ref[...].astype(BF16)
    _run_interleaved([_stream_events(*_fox_stream(h, qb, kb, vb, gb, kc, ob, *fox_scratch)),
                      _stream_events(*_moba_stream(qa, ka, va, ga, oa, *moba_scratch))],
                     lead=FOX_LEAD_EVENTS)


def _causal_attention(p, rel_bias, kc, w_out3, layer, batch, seq):
    nb = seq // MOBA_BLOCK
    assert seq % MOBA_BLOCK == 0 and MOBA_TOPK < SUBLANES and nb <= SUBLANES and MOBA_BLOCK > T5_MAX_DIST
    assert N_HEADS_MOBA == N_HEADS_FOX <= SUBLANES
    dw, d = w_out3.shape[1:]
    n_slabs = dw // W_OUT_CAST_ROWS
    assert dw % W_OUT_CAST_ROWS == 0 and n_slabs <= N_HEADS_MOBA * batch
    slab = lambda h, b: jnp.minimum(h * batch + b, n_slabs - 1)
    head = lambda first: pl.BlockSpec((HEAD_GROUP, seq, LANES), lambda h, b: (first + h, b, 0))
    out = pl.BlockSpec((None, seq, LANES), lambda h, b: (h, b, 0))
    out_sds = jax.ShapeDtypeStruct((N_HEADS_MOBA, batch * seq, LANES), BF16)
    return pl.pallas_call(
        _causal_attention_kernel,
        grid=(N_HEADS_MOBA, batch),
        in_specs=[pl.BlockSpec(memory_space=pltpu.SMEM), head(0), head(N_HEADS_MOBA),
                  pl.BlockSpec((None, seq, LANES), lambda h, b: (b, 0, 0)),
                  pl.BlockSpec((None, W_OUT_CAST_ROWS, d), lambda h, b: (layer, slab(h, b), 0))],
        out_specs=[out, out, pl.BlockSpec((W_OUT_CAST_ROWS, d), lambda h, b: (slab(h, b), 0))],
        out_shape=[out_sds, out_sds, jax.ShapeDtypeStruct((dw, d), BF16)],
        scratch_shapes=_moba_scratch(seq) + _fox_scratch(seq),
        compiler_params=_cparams(2),
        name="causal_attention",
    )(rel_bias, p, p, kc, w_out3)


def _mem_kernel(b_ref, q_ref, k_ref, v_ref, g_ref, f_ref, o_ref, kc_ref, vaug_ref, *, layer, n_fox_heads):
    n_heads, seq, _ = q_ref.shape
    tiles_per_head = seq // MEM_Q_TILE

    _forget_bias_columns(f_ref, b_ref, layer, kc_ref, n_fox_heads)
    vaug_ref[:, :, LANES:] = jnp.ones(v_ref.shape, BF16)
    vaug_ref[:, :, :LANES] = v_ref[...]

    def tile(t):
        h, i = divmod(t, tiles_per_head)
        return h, slice(i * MEM_Q_TILE, (i + 1) * MEM_Q_TILE)

    def score_matmul(t):
        h, rows = tile(t)
        return lax.dot_general(q_ref[h, rows, :], k_ref[h], _NT, preferred_element_type=F32)

    def finish(t, a):
        h, rows = tile(t)
        o_ref[h, rows, :] = _softmax_pv_folded(a, vaug_ref[h], g_ref[h, rows, :])

    _run_interleaved([_stream_events(lambda: None, n_heads * tiles_per_head, score_matmul, finish)])


def _mem_attention(p, mkv, fb, b_forget, layer, batch, seq, n_mem, q0, g0):
    nh = N_HEADS_MEM
    assert q0 % nh == 0 and g0 % nh == 0 and N_HEADS_FOX <= SUBLANES
    heads = lambda off: pl.BlockSpec((nh, seq, LANES), lambda b: (off // nh, b, 0))
    kv = lambda blk: pl.BlockSpec((nh, n_mem, LANES), lambda b: (blk, b, 0))
    return pl.pallas_call(
        functools.partial(_mem_kernel, layer=layer, n_fox_heads=N_HEADS_FOX),
        grid=(batch,),
        in_specs=[pl.BlockSpec(memory_space=pltpu.SMEM),
                  heads(q0), kv(0), kv(1), heads(g0),
                  pl.BlockSpec((seq, LANES), lambda b: (b, 0))],
        out_specs=[pl.BlockSpec((nh, seq, LANES), lambda b: (0, b, 0)),
                   pl.BlockSpec((None, seq, LANES), lambda b: (b, 0, 0))],
        out_shape=[jax.ShapeDtypeStruct((nh, batch * seq, LANES), BF16),
                   jax.ShapeDtypeStruct((batch, seq, LANES), BF16)],
        scratch_shapes=[pltpu.VMEM((nh, n_mem, 2 * LANES), BF16)],
        compiler_params=_cparams(1),
        name="mem_attention",
    )(b_forget, p, mkv, mkv, p, fb)


def _out_proj_kernel(ya_ref, yb_ref, ym_ref, w_ref, x_ref, g_ref, o_ref):
    y = jnp.concatenate([src[c] for src in (ya_ref, yb_ref, ym_ref) for c in range(src.shape[0])], axis=1)
    o = jnp.dot(y, w_ref[...], preferred_element_type=F32)
    ms = jnp.mean(o * o, axis=-1, keepdims=True)
    o_ref[...] = x_ref[...] + o * lax.rsqrt(ms + EPS) * g_ref[...]


def _out_proj(ya, yb, ym, w, x2, g, *, tm):
    m, d = x2.shape
    heads = lambda y: pl.BlockSpec((y.shape[0], tm, LANES), lambda i: (0, i, 0))
    return pl.pallas_call(
        _out_proj_kernel,
        grid=(m // tm,),
        in_specs=[heads(ya), heads(yb), heads(ym),
                  pl.BlockSpec(w.shape, lambda i: (0, 0), pipeline_mode=pl.Buffered(1)),
                  pl.BlockSpec((tm, d), lambda i: (i, 0)),
                  pl.BlockSpec((1, d), lambda i: (0, 0))],
        out_specs=pl.BlockSpec((tm, d), lambda i: (i, 0)),
        out_shape=jax.ShapeDtypeStruct((m, d), F32),
        compiler_params=_cparams(1),
        name="out_proj",
    )(ya, yb, ym, w, x2, g)


def kernel(x, mem, w_in, b_forget, w_mem_kv, w_out, g_pre, g_post, g_mem, rel_bias):
    batch, seq, d = x.shape
    n_mem = mem.shape[1]
    depth = w_in.shape[0]
    assert N_HEADS_MOBA == N_HEADS_FOX
    n_causal = N_HEADS_MOBA + N_HEADS_FOX
    wm = N_HEADS_MEM * HEAD_DIM
    n_in = w_in.shape[2]
    assert n_in == n_causal * HEAD_GROUP * HEAD_DIM + N_HEADS_FOX + 2 * wm
    qm0 = n_causal * HEAD_GROUP
    gm0 = qm0 + N_HEADS_MEM

    x2 = x.reshape(batch * seq, d)
    mem2 = mem.reshape(batch * n_mem, d)
    for layer in range(depth):
        w_t3 = jnp.transpose(w_in, (0, 2, 1)).reshape(depth * n_in, d // LANES, LANES)
        w_main_t, w_gate_t, mkv = _w_repack_mem_proj(w_t3, layer, n_in, N_HEADS_MOBA, N_HEADS_FOX, mem2,
                                                     g_mem[layer].reshape(1, d), w_mem_kv)
        p, fb = _norm_proj(x2, g_pre[layer].reshape(1, d), w_main_t, w_gate_t,
                           tm=IN_PROJ_TILE[0], tn=IN_PROJ_TILE[1])
        ym, kc = _mem_attention(p, mkv, fb, b_forget, layer, batch, seq, n_mem, qm0, gm0)
        ya, yb, w_out_bf = _causal_attention(p, rel_bias, kc, w_out, layer, batch, seq)
        x2 = _out_proj(ya, yb, ym, w_out_bf, x2, g_post[layer].reshape(1, d), tm=OUT_PROJ_ROWS)
    return x2.reshape(batch, seq, d)
```

```python
import functools
import math

import jax
import jax.numpy as jnp
from jax import lax
from jax.experimental import pallas as pl
from jax.experimental.pallas import tpu as pltpu

HEAD_DIM = 128
N_HEADS_MOBA = 6
N_HEADS_FOX = 6
N_HEADS_MEM = 4
MOBA_BLOCK = 256
MOBA_TOPK = 3
T5_BUCKETS = 32
T5_MAX_DIST = 128
EPS = 1e-6
NEG = -1e30

LANES = 128
SUBLANES = 8
VMEM_LIMIT_BYTES = 56 * 1024 * 1024

IN_PROJ_TILE = (1024, 1792)
NORM_ROW_CHUNK = 256
OUT_PROJ_ROWS = 512
HEAD_GROUP = 4
W_OUT_CAST_ROWS = 128
Q_TILE = MOBA_BLOCK
MEM_Q_TILE = 1024

F32 = jnp.float32
BF16 = jnp.bfloat16
_NT = (((1,), (1,)), ((), ()))


def _cparams(n_axes):
    return pltpu.CompilerParams(dimension_semantics=("arbitrary",) * n_axes,
                                vmem_limit_bytes=VMEM_LIMIT_BYTES)


def _norm_proj_kernel(x_ref, g_ref, w_hbm, wf_ref, p_hbm, f_ref, h_ref, wbuf, obuf, wsem, osem, *,
                      n_col_tiles, row_chunk):
    i = pl.program_id(0)
    n_steps = pl.num_programs(0)
    tm = x_ref.shape[0]
    tn = wbuf.shape[1]
    ncb = tn // LANES
    nt = n_col_tiles
    row0 = pl.multiple_of(i * tm, tm)

    def w_copy(tile, slot):
        return pltpu.make_async_copy(w_hbm.at[pl.ds(tile * tn, tn), :], wbuf.at[slot], wsem.at[slot])

    def out_copy(tile, slot):
        return pltpu.make_async_copy(obuf.at[slot],
                                     p_hbm.at[pl.ds(tile * ncb, ncb), pl.ds(row0, tm), :], osem.at[slot])

    @pl.when(i == 0)
    def _prime():
        obuf[...] = jnp.zeros(obuf.shape, BF16)
        out_copy(0, 0).start()
        out_copy(1, 1).start()
        w_copy(0, 0).start()

    def project(tile, rows):
        slot = tile % 2
        acc = lax.dot_general(h_ref[rows, :], wbuf[slot], _NT, preferred_element_type=F32)
        for c in range(ncb):
            obuf[slot, c, rows, :] = acc[:, c * LANES:(c + 1) * LANES].astype(BF16)

    w_copy(1, 1).start()
    w_copy(0, 0).wait()
    out_copy(0, 0).wait()
    for c in range(tm // row_chunk):
        rows = slice(c * row_chunk, (c + 1) * row_chunk)
        xf = x_ref[rows, :]
        ms = jnp.mean(xf * xf, axis=-1, keepdims=True)
        h = (xf * lax.rsqrt(ms + EPS) * g_ref[...]).astype(BF16)
        h_ref[rows, :] = h
        f_ref[rows, :] = lax.dot_general(h, wf_ref[...], _NT, preferred_element_type=F32)
        project(0, rows)

    for tile in range(1, nt):
        slot = tile % 2
        out_copy(tile - 1, 1 - slot).start()
        w_copy((tile + 1) % nt, 1 - slot).start()
        w_copy(tile, slot).wait()
        out_copy(tile, slot).wait()
        for c in range(tm // row_chunk):
            project(tile, slice(c * row_chunk, (c + 1) * row_chunk))
    out_copy(nt - 1, (nt - 1) % 2).start()

    @pl.when(i == n_steps - 1)
    def _drain():
        w_copy(0, 0).wait()
        out_copy(nt - 2, (nt - 2) % 2).wait()
        out_copy(nt - 1, (nt - 1) % 2).wait()


def _norm_proj(x2, g, w_t, wf_t, *, tm, tn):
    m, d = x2.shape
    n = w_t.shape[0]
    ncb = tn // LANES
    nt = n // tn
    assert n % tn == 0 and m % tm == 0 and nt >= 2 and nt % 2 == 0
    return pl.pallas_call(
        functools.partial(_norm_proj_kernel, n_col_tiles=nt, row_chunk=NORM_ROW_CHUNK),
        grid=(m // tm,),
        in_specs=[pl.BlockSpec((tm, d), lambda i: (i, 0)),
                  pl.BlockSpec((1, d), lambda i: (0, 0)),
                  pl.BlockSpec(memory_space=pl.ANY),
                  pl.BlockSpec((LANES, d), lambda i: (0, 0))],
        out_specs=[pl.BlockSpec(memory_space=pl.ANY),
                   pl.BlockSpec((tm, LANES), lambda i: (i, 0))],
        out_shape=[jax.ShapeDtypeStruct((n // LANES, m, LANES), BF16),
                   jax.ShapeDtypeStruct((m, LANES), F32)],
        scratch_shapes=[pltpu.VMEM((tm, d), BF16), pltpu.VMEM((2, tn, d), BF16),
                        pltpu.VMEM((2, ncb, tm, LANES), BF16),
                        pltpu.SemaphoreType.DMA((2,)), pltpu.SemaphoreType.DMA((2,))],
        compiler_params=_cparams(1),
        name="norm_proj_gate",
    )(x2, g, w_t, wf_t)


def _w_repack_kernel(src_ref, mem_ref, gm_ref, wm_ref, main_ref, gate_ref, mkv_ref,
                     buf_ref, gbuf_ref, hm_ref, sem_ref, gsem_ref, *,
                     base_row, n_heads, n_aligned, n_gate, n_mem_tiles, row_chunk):
    t = pl.program_id(0)
    n_steps = pl.num_programs(0)
    n_slices = src_ref.shape[1]

    def source_row(step, g):
        head_blk = (step // n_heads) * (HEAD_GROUP * n_heads) + g * n_heads + step % n_heads
        blk = jnp.where(step < 2 * n_heads, head_blk, HEAD_GROUP * step + g)
        return base_row + blk * LANES + jnp.where(blk * LANES >= n_aligned, n_gate, 0)

    def block_copies(step, slot):
        return [pltpu.make_async_copy(src_ref.at[pl.ds(source_row(step, g), LANES), s, :],
                                      buf_ref.at[slot, pl.ds(g * LANES, LANES), pl.ds(s * LANES, LANES)],
                                      sem_ref.at[slot])
                for g in range(HEAD_GROUP) for s in range(n_slices)]

    def gate_copies():
        gate_row0 = base_row + n_aligned
        return [pltpu.make_async_copy(src_ref.at[pl.ds(gate_row0, SUBLANES), s, :],
                                      gbuf_ref.at[:, pl.ds(s * LANES, LANES)],
                                      gsem_ref.at[0]) for s in range(n_slices)]

    @pl.when(t == 0)
    def _first():
        for cp in gate_copies() + block_copies(0, 0):
            cp.start()

    @pl.when(t + 1 < n_steps)
    def _prefetch():
        for cp in block_copies(t + 1, (t + 1) % 2):
            cp.start()

    @pl.when(t == 0)
    def _gate_rows_and_mem_norm():
        for cp in gate_copies():
            cp.wait()
        row = lax.broadcasted_iota(jnp.int32, gbuf_ref.shape, 0)
        gate = jnp.where(row < n_gate, gbuf_ref[...], 0.0)
        gate_ref[...] = jnp.concatenate(
            [gate, jnp.zeros((gate_ref.shape[0] - SUBLANES, gate.shape[1]), F32)], axis=0).astype(BF16)
        for c in range(mem_ref.shape[0] // row_chunk):
            r = slice(c * row_chunk, (c + 1) * row_chunk)
            xf = mem_ref[r, :]
            ms = jnp.mean(xf * xf, axis=-1, keepdims=True)
            hm_ref[r, :] = (xf * lax.rsqrt(ms + EPS) * gm_ref[...]).astype(BF16)

    @pl.when(t < n_mem_tiles)
    def _mem_projection():
        mkv_ref[...] = jnp.dot(hm_ref[...], wm_ref[...].astype(BF16),
                               preferred_element_type=F32).astype(BF16)

    for cp in block_copies(t, t % 2):
        cp.wait()
    main_ref[...] = buf_ref[t % 2].astype(BF16)


def _w_repack_mem_proj(w_t3, layer, n_total, n_heads, n_gate, mem2, g_mem, w_mem3):
    n_slices = w_t3.shape[1]
    d = n_slices * LANES
    rows = HEAD_GROUP * LANES
    n_aligned = 2 * n_heads * rows
    n_main = n_total - n_gate
    m_mem = mem2.shape[0]
    n_mem_tiles = w_mem3.shape[2] // LANES
    n_steps = n_main // rows
    assert n_main % rows == 0 and n_aligned <= n_main and n_gate <= SUBLANES
    assert w_mem3.shape[2] % LANES == 0 and n_mem_tiles <= n_steps
    mem_tile = lambda t: jnp.minimum(t, n_mem_tiles - 1)
    return pl.pallas_call(
        functools.partial(_w_repack_kernel, base_row=layer * n_total, n_heads=n_heads,
                          n_aligned=n_aligned, n_gate=n_gate,
                          n_mem_tiles=n_mem_tiles, row_chunk=NORM_ROW_CHUNK),
        grid=(n_steps,),
        in_specs=[pl.BlockSpec(memory_space=pl.ANY),
                  pl.BlockSpec((m_mem, d), lambda t: (0, 0), pipeline_mode=pl.Buffered(1)),
                  pl.BlockSpec((1, d), lambda t: (0, 0)),
                  pl.BlockSpec((None, d, LANES), lambda t: (layer, 0, mem_tile(t)))],
        out_specs=[pl.BlockSpec((rows, d), lambda t: (t, 0)),
                   pl.BlockSpec((LANES, d), lambda t: (0, 0)),
                   pl.BlockSpec((None, m_mem, LANES), lambda t: (mem_tile(t), 0, 0))],
        out_shape=[jax.ShapeDtypeStruct((n_main, d), BF16),
                   jax.ShapeDtypeStruct((LANES, d), BF16),
                   jax.ShapeDtypeStruct((n_mem_tiles, m_mem, LANES), BF16)],
        scratch_shapes=[pltpu.VMEM((2, rows, d), F32), pltpu.VMEM((SUBLANES, d), F32),
                        pltpu.VMEM((m_mem, d), BF16),
                        pltpu.SemaphoreType.DMA((2,)), pltpu.SemaphoreType.DMA((1,))],
        compiler_params=_cparams(1),
        name="w_repack_mem_proj",
    )(w_t3, mem2, g_mem, w_mem3)


N_SPLIT = 3


def _split3(v):
    hi = v.astype(BF16).astype(F32)
    mid = (v - hi).astype(BF16).astype(F32)
    lo = (v - hi - mid).astype(BF16).astype(F32)
    return hi, mid, lo


def _forget_bias_columns(f_ref, b_ref, layer, kc_ref, n_heads):
    head_lane = lax.broadcasted_iota(jnp.int32, (1, LANES), 1)
    bias = jnp.zeros((1, LANES), F32)
    for hh in range(n_heads):
        bias = jnp.where(head_lane == hh, b_ref[layer, hh], bias)
    z = f_ref[...] + bias
    logf = jnp.minimum(z, 0.0) - jnp.log1p(jnp.exp(-jnp.abs(z)))
    x = logf.T[0:SUBLANES, :]
    s_len = x.shape[1]
    lane = lax.broadcasted_iota(jnp.int32, x.shape, 1)
    shift = 1
    while shift < s_len:
        x = x + jnp.where(lane >= shift, pltpu.roll(x, shift, 1), 0.0)
        shift *= 2
    row = lax.broadcasted_iota(jnp.int32, x.shape, 0)
    x = jnp.where(row < n_heads, x, 0.0)
    terms = _split3(x * (-1.0 / HEAD_DIM ** -0.5))
    pad = jnp.zeros((LANES - SUBLANES * N_SPLIT, s_len), F32)
    kc_ref[...] = jnp.concatenate(list(terms) + [pad], axis=0).T.astype(BF16)


LOG2E = math.log2(math.e)


def _softmax_pv_folded(a, v_aug, g):
    m = jnp.max(a, axis=1, keepdims=True)
    p = jnp.exp2((a - m) * (HEAD_DIM ** -0.5 * LOG2E)).astype(BF16)
    acc = jnp.dot(p, v_aug, preferred_element_type=F32)
    gf = g.astype(F32)
    silu = gf * (1.0 / (1.0 + jnp.exp(-gf)))
    return (acc[:, :LANES] / acc[:, LANES:] * silu).astype(BF16)


SCORE_AHEAD = 1


def _stream_events(setup, n_tiles, score_matmul, finish):
    scores = {}

    def score(i):
        return lambda: scores.__setitem__(i, score_matmul(i))

    def fin(i):
        return lambda: finish(i, scores.pop(i))

    events = [setup] + [score(i) for i in range(min(SCORE_AHEAD, n_tiles))]
    for i in range(n_tiles):
        if i + SCORE_AHEAD < n_tiles:
            events.append(score(i + SCORE_AHEAD))
        events.append(fin(i))
    return events


def _run_interleaved(streams, lead=0):
    queues = [list(ev) for ev in streams]
    for _ in range(min(lead, len(queues[0]))):
        queues[0].pop(0)()
    while any(queues):
        for q in queues:
            if q:
                q.pop(0)()


def _t5_bias(rb_ref, h, dist):
    max_exact = T5_BUCKETS // 2
    n = jnp.maximum(dist, 0)
    nf = jnp.maximum(n, 1).astype(F32)
    val = jnp.log(nf / max_exact) / math.log(T5_MAX_DIST / max_exact) * (T5_BUCKETS - max_exact)
    bias = jnp.zeros(dist.shape, F32)
    for kk in range(T5_BUCKETS):
        if kk < max_exact:
            hit = n == kk
        elif kk < T5_BUCKETS - 1:
            hit = (n >= max_exact) & (val >= kk - max_exact) & (val < kk - max_exact + 1)
        else:
            hit = (n >= max_exact) & (val >= kk - max_exact)
        bias = jnp.where(hit, rb_ref[h, kk], bias)
    return bias


def _moba_head_constants(rb_ref, h, kaug_ref, bown_ref, bprev_ref):
    inv_scale = 1.0 / HEAD_DIM ** -0.5
    blk = MOBA_BLOCK
    seq = kaug_ref.shape[0]
    far_bias = rb_ref[h, T5_BUCKETS - 1]
    srow = lax.broadcasted_iota(jnp.int32, (seq, LANES), 0)
    col = lax.broadcasted_iota(jnp.int32, (seq, LANES), 1)
    kaug_ref[:, LANES:] = jnp.where((col & (SUBLANES - 1)) * blk == srow - (srow & (blk - 1)),
                                    1.0, 0.0).astype(BF16)
    t = T5_MAX_DIST
    assert blk == 2 * t
    dist = lax.broadcasted_iota(jnp.int32, (t, t), 0) - lax.broadcasted_iota(jnp.int32, (t, t), 1)
    near = jnp.where(dist >= 0, (_t5_bias(rb_ref, h, dist) - far_bias) * inv_scale, NEG)
    mid = (_t5_bias(rb_ref, h, dist + t) - far_bias) * inv_scale
    bown_ref[0:t, 0:t] = near
    bown_ref[0:t, t:blk] = jnp.full((t, t), NEG, F32)
    bown_ref[t:blk, 0:t] = mid
    bown_ref[t:blk, t:blk] = near
    bprev_ref[...] = jnp.zeros((blk, blk), F32)
    bprev_ref[0:t, t:blk] = mid


def _moba_stream(q_ref, k_ref, v_ref, g_ref, o_ref,
                 kaug_ref, qaug_ref, vaug_ref, kmean_ref, bown_ref, bprev_ref):
    blk = MOBA_BLOCK
    seq = q_ref.shape[0]
    n_blocks = seq // blk
    n_chunks = seq // LANES

    def setup():
        _moba_setup(q_ref, k_ref, v_ref, kaug_ref, qaug_ref, vaug_ref, kmean_ref, n_blocks, n_chunks)

    def score_matmul(i):
        return lax.dot_general(qaug_ref[i * blk:(i + 1) * blk, :], kaug_ref[0:(i + 1) * blk, :], _NT,
                               preferred_element_type=F32)

    def finish(i, a):
        rows = slice(i * blk, (i + 1) * blk)
        n = (i + 1) * blk
        parts = [a[:, n - blk:] + bown_ref[...]]
        if i >= 1:
            parts.insert(0, a[:, n - 2 * blk:n - blk] + bprev_ref[...])
        if i >= 2:
            parts.insert(0, a[:, :n - 2 * blk])
        a = parts[0] if len(parts) == 1 else jnp.concatenate(parts, axis=1)
        o_ref[rows, :] = _softmax_pv_folded(a, vaug_ref[0:n, :], g_ref[rows, :])

    return setup, n_blocks, score_matmul, finish


def _moba_setup(q_ref, k_ref, v_ref, kaug_ref, qaug_ref, vaug_ref, kmean_ref, n_blocks, n_chunks):
    blk = MOBA_BLOCK
    seq = q_ref.shape[0]
    kaug_ref[:, :LANES] = k_ref[...]
    vaug_ref[:, :LANES] = v_ref[...]
    vaug_ref[:, LANES:] = jnp.ones((seq, LANES), BF16)
    blk_row = lax.broadcasted_iota(jnp.int32, (SUBLANES, LANES), 0)
    km = jnp.zeros((SUBLANES, LANES), F32)
    for j in range(n_blocks):
        mean_j = jnp.mean(k_ref[j * blk:(j + 1) * blk, :].astype(F32), axis=0, keepdims=True)
        km = jnp.where(blk_row == j, mean_j, km)
    kmean_ref[...] = jnp.concatenate([km, jnp.zeros((LANES - SUBLANES, LANES), F32)], axis=0).astype(BF16)

    gate_t = lax.dot_general(kmean_ref[...], q_ref[...], _NT, preferred_element_type=F32)[0:SUBLANES, :]
    jrow = lax.broadcasted_iota(jnp.int32, gate_t.shape, 0)
    qpos = lax.broadcasted_iota(jnp.int32, gate_t.shape, 1)
    qblk = jnp.right_shift(qpos, int(math.log2(blk)))
    gm = jnp.where(jrow < qblk, gate_t, NEG)
    sel_bias = jnp.zeros(gate_t.shape, F32)
    for j in range(n_blocks):
        gj = gm[j:j + 1, :]
        beats = (gm > gj) | ((gm == gj) & (jrow < j))
        rank = jnp.sum(beats.astype(F32), axis=0, keepdims=True)
        allowed = ((rank < MOBA_TOPK) & (qblk[0:1, :] > j)) | (qblk[0:1, :] == j)
        sel_bias = jnp.where(jrow == j, jnp.where(allowed, 0.0, NEG), sel_bias)
    assert SUBLANES * n_chunks == LANES
    packed_t = jnp.concatenate([sel_bias[:, c * LANES:(c + 1) * LANES] for c in range(n_chunks)], axis=0).T
    lane_chunk = jnp.right_shift(lax.broadcasted_iota(jnp.int32, (LANES, LANES), 1),
                                 int(math.log2(SUBLANES)))
    qaug_ref[:, :LANES] = q_ref[...]
    for c in range(n_chunks):
        qaug_ref[c * LANES:(c + 1) * LANES, LANES:] = jnp.where(lane_chunk == c, packed_t, 0.0).astype(BF16)


def _moba_scratch(seq):
    blk = MOBA_BLOCK
    return [pltpu.VMEM((seq, 2 * LANES), BF16),
            pltpu.VMEM((seq, 2 * LANES), BF16),
            pltpu.VMEM((seq, 2 * LANES), BF16),
            pltpu.VMEM((LANES, LANES), BF16),
            pltpu.VMEM((blk, blk), F32),
            pltpu.VMEM((blk, blk), F32)]


def _fox_stream(h, q_ref, k_ref, v_ref, g_ref, kc_ref, o_ref, qaug_ref, kaug_ref, vaug_ref, causal_ref):
    seq = q_ref.shape[0]

    def setup():
        col = lax.broadcasted_iota(jnp.int32, (seq, LANES), 1)
        qaug_ref[:, :LANES] = q_ref[...]
        qaug_ref[:, LANES:] = jnp.where(((col & (SUBLANES - 1)) == h) & (col < SUBLANES * N_SPLIT),
                                        1.0, 0.0).astype(BF16)
        kaug_ref[:, :LANES] = k_ref[...]
        kaug_ref[:, LANES:] = kc_ref[...]
        vaug_ref[:, :LANES] = v_ref[...]
        vaug_ref[:, LANES:] = jnp.ones((seq, LANES), BF16)
        r = lax.broadcasted_iota(jnp.int32, (Q_TILE, Q_TILE), 0)
        c = lax.broadcasted_iota(jnp.int32, (Q_TILE, Q_TILE), 1)
        causal_ref[...] = jnp.where(c <= r, 0.0, NEG)

    def score_matmul(i):
        return lax.dot_general(qaug_ref[i * Q_TILE:(i + 1) * Q_TILE, :], kaug_ref[0:(i + 1) * Q_TILE, :],
                               _NT, preferred_element_type=F32)

    def finish(i, a):
        rows = slice(i * Q_TILE, (i + 1) * Q_TILE)
        n = (i + 1) * Q_TILE
        own = a[:, n - Q_TILE:] + causal_ref[...]
        a = own if i == 0 else jnp.concatenate([a[:, :n - Q_TILE], own], axis=1)
        o_ref[rows, :] = _softmax_pv_folded(a, vaug_ref[0:n, :], g_ref[rows, :])

    return setup, seq // Q_TILE, score_matmul, finish


def _fox_scratch(seq):
    return [pltpu.VMEM((seq, 2 * LANES), BF16), pltpu.VMEM((seq, 2 * LANES), BF16),
            pltpu.VMEM((seq, 2 * LANES), BF16), pltpu.VMEM((Q_TILE, Q_TILE), F32)]


FOX_LEAD_EVENTS = 11


def _causal_attention_kernel(*refs):
    rb_ref, moba_qkvg, fox_qkvg, kc, wo_ref, oa, ob, wob_ref = refs[:8]
    qa, ka, va, ga = (moba_qkvg.at[i] for i in range(HEAD_GROUP))
    qb, kb, vb, gb = (fox_qkvg.at[i] for i in range(HEAD_GROUP))
    n_moba = len(_moba_scratch(qa.shape[0]))
    moba_scratch, fox_scratch = refs[8:8 + n_moba], refs[8 + n_moba:]
    kaug_ref, _, _, _, bown_ref, bprev_ref = moba_scratch
    h = pl.program_id(0)

    @pl.when(pl.program_id(1) == 0)
    def _():
        _moba_head_constants(rb_ref, h, kaug_ref, bown_ref, bprev_ref)

    wob_ref[...] = wo_ref[...].astype(BF16)
    _run_interleaved([_stream_events(*_fox_stream(h, qb, kb, vb, gb, kc, ob, *fox_scratch)),
                      _stream_events(*_moba_stream(qa, ka, va, ga, oa, *moba_scratch))],
                     lead=FOX_LEAD_EVENTS)


def _causal_attention(p, rel_bias, kc, w_out3, layer, batch, seq):
    nb = seq // MOBA_BLOCK
    assert seq % MOBA_BLOCK == 0 and MOBA_TOPK < SUBLANES and nb <= SUBLANES and MOBA_BLOCK > T5_MAX_DIST
    assert N_HEADS_MOBA == N_HEADS_FOX <= SUBLANES
    dw, d = w_out3.shape[1:]
    n_slabs = dw // W_OUT_CAST_ROWS
    assert dw % W_OUT_CAST_ROWS == 0 and n_slabs <= N_HEADS_MOBA * batch
    slab = lambda h, b: jnp.minimum(h * batch + b, n_slabs - 1)
    head = lambda first: pl.BlockSpec((HEAD_GROUP, seq, LANES), lambda h, b: (first + h, b, 0))
    out = pl.BlockSpec((None, seq, LANES), lambda h, b: (h, b, 0))
    out_sds = jax.ShapeDtypeStruct((N_HEADS_MOBA, batch * seq, LANES), BF16)
    return pl.pallas_call(
        _causal_attention_kernel,
        grid=(N_HEADS_MOBA, batch),
        in_specs=[pl.BlockSpec(memory_space=pltpu.SMEM), head(0), head(N_HEADS_MOBA),
                  pl.BlockSpec((None, seq, LANES), lambda h, b: (b, 0, 0)),
                  pl.BlockSpec((None, W_OUT_CAST_ROWS, d), lambda h, b: (layer, slab(h, b), 0))],
        out_specs=[out, out, pl.BlockSpec((W_OUT_CAST_ROWS, d), lambda h, b: (slab(h, b), 0))],
        out_shape=[out_sds, out_sds, jax.ShapeDtypeStruct((dw, d), BF16)],
        scratch_shapes=_moba_scratch(seq) + _fox_scratch(seq),
        compiler_params=_cparams(2),
        name="causal_attention",
    )(rel_bias, p, p, kc, w_out3)


def _mem_kernel(b_ref, q_ref, k_ref, v_ref, g_ref, f_ref, o_ref, kc_ref, vaug_ref, *, layer, n_fox_heads):
    n_heads, seq, _ = q_ref.shape
    tiles_per_head = seq // MEM_Q_TILE

    _forget_bias_columns(f_ref, b_ref, layer, kc_ref, n_fox_heads)
    vaug_ref[:, :, LANES:] = jnp.ones(v_ref.shape, BF16)
    vaug_ref[:, :, :LANES] = v_ref[...]

    def tile(t):
        h, i = divmod(t, tiles_per_head)
        return h, slice(i * MEM_Q_TILE, (i + 1) * MEM_Q_TILE)

    def score_matmul(t):
        h, rows = tile(t)
        return lax.dot_general(q_ref[h, rows, :], k_ref[h], _NT, preferred_element_type=F32)

    def finish(t, a):
        h, rows = tile(t)
        o_ref[h, rows, :] = _softmax_pv_folded(a, vaug_ref[h], g_ref[h, rows, :])

    _run_interleaved([_stream_events(lambda: None, n_heads * tiles_per_head, score_matmul, finish)])


def _mem_attention(p, mkv, fb, b_forget, layer, batch, seq, n_mem, q0, g0):
    nh = N_HEADS_MEM
    assert q0 % nh == 0 and g0 % nh == 0 and N_HEADS_FOX <= SUBLANES
    heads = lambda off: pl.BlockSpec((nh, seq, LANES), lambda b: (off // nh, b, 0))
    kv = lambda blk: pl.BlockSpec((nh, n_mem, LANES), lambda b: (blk, b, 0))
    return pl.pallas_call(
        functools.partial(_mem_kernel, layer=layer, n_fox_heads=N_HEADS_FOX),
        grid=(batch,),
        in_specs=[pl.BlockSpec(memory_space=pltpu.SMEM),
                  heads(q0), kv(0), kv(1), heads(g0),
                  pl.BlockSpec((seq, LANES), lambda b: (b, 0))],
        out_specs=[pl.BlockSpec((nh, seq, LANES), lambda b: (0, b, 0)),
                   pl.BlockSpec((None, seq, LANES), lambda b: (b, 0, 0))],
        out_shape=[jax.ShapeDtypeStruct((nh, batch * seq, LANES), BF16),
                   jax.ShapeDtypeStruct((batch, seq, LANES), BF16)],
        scratch_shapes=[pltpu.VMEM((nh, n_mem, 2 * LANES), BF16)],
        compiler_params=_cparams(1),
        name="mem_attention",
    )(b_forget, p, mkv, mkv, p, fb)


def _out_proj_kernel(ya_ref, yb_ref, ym_ref, w_ref, x_ref, g_ref, o_ref):
    y = jnp.concatenate([src[c] for src in (ya_ref, yb_ref, ym_ref) for c in range(src.shape[0])], axis=1)
    o = jnp.dot(y, w_ref[...], preferred_element_type=F32)
    ms = jnp.mean(o * o, axis=-1, keepdims=True)
    o_ref[...] = x_ref[...] + o * lax.rsqrt(ms + EPS) * g_ref[...]


def _out_proj(ya, yb, ym, w, x2, g, *, tm):
    m, d = x2.shape
    heads = lambda y: pl.BlockSpec((y.shape[0], tm, LANES), lambda i: (0, i, 0))
    return pl.pallas_call(
        _out_proj_kernel,
        grid=(m // tm,),
        in_specs=[heads(ya), heads(yb), heads(ym),
                  pl.BlockSpec(w.shape, lambda i: (0, 0), pipeline_mode=pl.Buffered(1)),
                  pl.BlockSpec((tm, d), lambda i: (i, 0)),
                  pl.BlockSpec((1, d), lambda i: (0, 0))],
        out_specs=pl.BlockSpec((tm, d), lambda i: (i, 0)),
        out_shape=jax.ShapeDtypeStruct((m, d), F32),
        compiler_params=_cparams(1),
        name="out_proj",
    )(ya, yb, ym, w, x2, g)


def kernel(x, mem, w_in, b_forget, w_mem_kv, w_out, g_pre, g_post, g_mem, rel_bias):
    batch, seq, d = x.shape
    n_mem = mem.shape[1]
    depth = w_in.shape[0]
    assert N_HEADS_MOBA == N_HEADS_FOX
    n_causal = N_HEADS_MOBA + N_HEADS_FOX
    wm = N_HEADS_MEM * HEAD_DIM
    n_in = w_in.shape[2]
    assert n_in == n_causal * HEAD_GROUP * HEAD_DIM + N_HEADS_FOX + 2 * wm
    qm0 = n_causal * HEAD_GROUP
    gm0 = qm0 + N_HEADS_MEM

    x2 = x.reshape(batch * seq, d)
    mem2 = mem.reshape(batch * n_mem, d)
    for layer in range(depth):
        w_t3 = jnp.transpose(w_in, (0, 2, 1)).reshape(depth * n_in, d // LANES, LANES)
        w_main_t, w_gate_t, mkv = _w_repack_mem_proj(w_t3, layer, n_in, N_HEADS_MOBA, N_HEADS_FOX, mem2,
                                                     g_mem[layer].reshape(1, d), w_mem_kv)
        p, fb = _norm_proj(x2, g_pre[layer].reshape(1, d), w_main_t, w_gate_t,
                           tm=IN_PROJ_TILE[0], tn=IN_PROJ_TILE[1])
        ym, kc = _mem_attention(p, mkv, fb, b_forget, layer, batch, seq, n_mem, qm0, gm0)
        ya, yb, w_out_bf = _causal_attention(p, rel_bias, kc, w_out, layer, batch, seq)
        x2 = _out_proj(ya, yb, ym, w_out_bf, x2, g_post[layer].reshape(1, d), tm=OUT_PROJ_ROWS)
    return x2.reshape(batch, seq, d)
```

```python
import functools
import math

import jax
import jax.numpy as jnp
from jax import lax
from jax.experimental import pallas as pl
from jax.experimental.pallas import tpu as pltpu

HEAD_DIM = 128
N_HEADS_MOBA = 6
N_HEADS_FOX = 6
N_HEADS_MEM = 4
MOBA_BLOCK = 256
MOBA_TOPK = 3
T5_BUCKETS = 32
T5_MAX_DIST = 128
EPS = 1e-6
NEG = -1e30

LANES = 128
SUBLANES = 8
VMEM_LIMIT_BYTES = 56 * 1024 * 1024

IN_PROJ_TILE = (1024, 1792)
NORM_ROW_CHUNK = 256
OUT_PROJ_ROWS = 512
HEAD_GROUP = 4
W_OUT_CAST_ROWS = 128
Q_TILE = MOBA_BLOCK
MEM_Q_TILE = 1024

F32 = jnp.float32
BF16 = jnp.bfloat16
_NT = (((1,), (1,)), ((), ()))


def _cparams(n_axes):
    return pltpu.CompilerParams(dimension_semantics=("arbitrary",) * n_axes,
                                vmem_limit_bytes=VMEM_LIMIT_BYTES)


def _norm_proj_kernel(x_ref, g_ref, w_ref, wf_ref, p_ref, f_ref, h_ref, *, n_col_blocks, row_chunk):
    tm = x_ref.shape[0]
    j = pl.program_id(1)

    def project(rows):
        acc = lax.dot_general(h_ref[rows, :], w_ref[...], _NT, preferred_element_type=F32)
        for c in range(n_col_blocks):
            p_ref[c, rows, :] = acc[:, c * LANES:(c + 1) * LANES].astype(BF16)

    @pl.when(j == 0)
    def _normalize_and_project():
        for c in range(tm // row_chunk):
            rows = slice(c * row_chunk, (c + 1) * row_chunk)
            xf = x_ref[rows, :]
            ms = jnp.mean(xf * xf, axis=-1, keepdims=True)
            h = (xf * lax.rsqrt(ms + EPS) * g_ref[...]).astype(BF16)
            h_ref[rows, :] = h
            f_ref[rows, :] = lax.dot_general(h, wf_ref[...], _NT, preferred_element_type=F32)
            project(rows)

    @pl.when(j > 0)
    def _project():
        project(slice(0, tm))


def _norm_proj(x2, g, w_t, wf_t, *, tm, tn):
    m, d = x2.shape
    n = w_t.shape[0]
    ncb = tn // LANES
    assert n % tn == 0 and m % tm == 0
    body = functools.partial(_norm_proj_kernel, n_col_blocks=ncb, row_chunk=NORM_ROW_CHUNK)
    in_specs = [pl.BlockSpec((tm, d), lambda i, j: (i, 0)),
                pl.BlockSpec((1, d), lambda i, j: (0, 0)),
                pl.BlockSpec((tn, d), lambda i, j: (j, 0)),
                pl.BlockSpec((LANES, d), lambda i, j: (0, 0))]
    out_specs = [pl.BlockSpec((ncb, tm, LANES), lambda i, j: (j, i, 0)),
                 pl.BlockSpec((tm, LANES), lambda i, j: (i, 0))]

    def pipelined(x_hbm, g_hbm, w_hbm, wf_hbm, p_hbm, f_hbm, h_ref):
        pltpu.emit_pipeline(body, grid=(m // tm, n // tn), in_specs=in_specs, out_specs=out_specs)(
            x_hbm, g_hbm, w_hbm, wf_hbm, p_hbm, f_hbm, scratches=(h_ref,))

    return pl.pallas_call(
        pipelined,
        in_specs=[pl.BlockSpec(memory_space=pl.ANY)] * 4,
        out_specs=[pl.BlockSpec(memory_space=pl.ANY)] * 2,
        out_shape=[jax.ShapeDtypeStruct((n // LANES, m, LANES), BF16),
                   jax.ShapeDtypeStruct((m, LANES), F32)],
        scratch_shapes=[pltpu.VMEM((tm, d), BF16)],
        compiler_params=pltpu.CompilerParams(vmem_limit_bytes=VMEM_LIMIT_BYTES),
        name="norm_proj_gate",
    )(x2, g, w_t, wf_t)


def _w_repack_kernel(src_ref, mem_ref, gm_ref, wm_ref, main_ref, gate_ref, mkv_ref,
                     buf_ref, gbuf_ref, hm_ref, sem_ref, gsem_ref, *,
                     base_row, n_heads, n_aligned, n_gate, n_mem_tiles, row_chunk):
    t = pl.program_id(0)
    n_steps = pl.num_programs(0)
    n_slices = src_ref.shape[1]

    def source_row(step, g):
        head_blk = (step // n_heads) * (HEAD_GROUP * n_heads) + g * n_heads + step % n_heads
        blk = jnp.where(step < 2 * n_heads, head_blk, HEAD_GROUP * step + g)
        return base_row + blk * LANES + jnp.where(blk * LANES >= n_aligned, n_gate, 0)

    def block_copies(step, slot):
        return [pltpu.make_async_copy(src_ref.at[pl.ds(source_row(step, g), LANES), s, :],
                                      buf_ref.at[slot, pl.ds(g * LANES, LANES), pl.ds(s * LANES, LANES)],
                                      sem_ref.at[slot])
                for g in range(HEAD_GROUP) for s in range(n_slices)]

    def gate_copies():
        gate_row0 = base_row + n_aligned
        return [pltpu.make_async_copy(src_ref.at[pl.ds(gate_row0, SUBLANES), s, :],
                                      gbuf_ref.at[:, pl.ds(s * LANES, LANES)],
                                      gsem_ref.at[0]) for s in range(n_slices)]

    @pl.when(t == 0)
    def _first():
        for cp in gate_copies() + block_copies(0, 0):
            cp.start()

    @pl.when(t + 1 < n_steps)
    def _prefetch():
        for cp in block_copies(t + 1, (t + 1) % 2):
            cp.start()

    @pl.when(t == 0)
    def _gate_rows_and_mem_norm():
        for cp in gate_copies():
            cp.wait()
        row = lax.broadcasted_iota(jnp.int32, gbuf_ref.shape, 0)
        gate = jnp.where(row < n_gate, gbuf_ref[...], 0.0)
        gate_ref[...] = jnp.concatenate(
            [gate, jnp.zeros((gate_ref.shape[0] - SUBLANES, gate.shape[1]), F32)], axis=0).astype(BF16)
        for c in range(mem_ref.shape[0] // row_chunk):
            r = slice(c * row_chunk, (c + 1) * row_chunk)
            xf = mem_ref[r, :]
            ms = jnp.mean(xf * xf, axis=-1, keepdims=True)
            hm_ref[r, :] = (xf * lax.rsqrt(ms + EPS) * gm_ref[...]).astype(BF16)

    @pl.when(t < n_mem_tiles)
    def _mem_projection():
        mkv_ref[...] = jnp.dot(hm_ref[...], wm_ref[...].astype(BF16),
                               preferred_element_type=F32).astype(BF16)

    for cp in block_copies(t, t % 2):
        cp.wait()
    main_ref[...] = buf_ref[t % 2].astype(BF16)


def _w_repack_mem_proj(w_t3, layer, n_total, n_heads, n_gate, mem2, g_mem, w_mem3):
    n_slices = w_t3.shape[1]
    d = n_slices * LANES
    rows = HEAD_GROUP * LANES
    n_aligned = 2 * n_heads * rows
    n_main = n_total - n_gate
    m_mem = mem2.shape[0]
    n_mem_tiles = w_mem3.shape[2] // LANES
    n_steps = n_main // rows
    assert n_main % rows == 0 and n_aligned <= n_main and n_gate <= SUBLANES
    assert w_mem3.shape[2] % LANES == 0 and n_mem_tiles <= n_steps
    mem_tile = lambda t: jnp.minimum(t, n_mem_tiles - 1)
    return pl.pallas_call(
        functools.partial(_w_repack_kernel, base_row=layer * n_total, n_heads=n_heads,
                          n_aligned=n_aligned, n_gate=n_gate,
                          n_mem_tiles=n_mem_tiles, row_chunk=NORM_ROW_CHUNK),
        grid=(n_steps,),
        in_specs=[pl.BlockSpec(memory_space=pl.ANY),
                  pl.BlockSpec((m_mem, d), lambda t: (0, 0), pipeline_mode=pl.Buffered(1)),
                  pl.BlockSpec((1, d), lambda t: (0, 0)),
                  pl.BlockSpec((None, d, LANES), lambda t: (layer, 0, mem_tile(t)))],
        out_specs=[pl.BlockSpec((rows, d), lambda t: (t, 0)),
                   pl.BlockSpec((LANES, d), lambda t: (0, 0)),
                   pl.BlockSpec((None, m_mem, LANES), lambda t: (mem_tile(t), 0, 0))],
        out_shape=[jax.ShapeDtypeStruct((n_main, d), BF16),
                   jax.ShapeDtypeStruct((LANES, d), BF16),
                   jax.ShapeDtypeStruct((n_mem_tiles, m_mem, LANES), BF16)],
        scratch_shapes=[pltpu.VMEM((2, rows, d), F32), pltpu.VMEM((SUBLANES, d), F32),
                        pltpu.VMEM((m_mem, d), BF16),
                        pltpu.SemaphoreType.DMA((2,)), pltpu.SemaphoreType.DMA((1,))],
        compiler_params=_cparams(1),
        name="w_repack_mem_proj",
    )(w_t3, mem2, g_mem, w_mem3)


N_SPLIT = 3


def _split3(v):
    hi = v.astype(BF16).astype(F32)
    mid = (v - hi).astype(BF16).astype(F32)
    lo = (v - hi - mid).astype(BF16).astype(F32)
    return hi, mid, lo


def _forget_bias_columns(f_ref, b_ref, layer, kc_ref, n_heads):
    head_lane = lax.broadcasted_iota(jnp.int32, (1, LANES), 1)
    bias = jnp.zeros((1, LANES), F32)
    for hh in range(n_heads):
        bias = jnp.where(head_lane == hh, b_ref[layer, hh], bias)
    z = f_ref[...] + bias
    logf = jnp.minimum(z, 0.0) - jnp.log1p(jnp.exp(-jnp.abs(z)))
    x = logf.T[0:SUBLANES, :]
    s_len = x.shape[1]
    lane = lax.broadcasted_iota(jnp.int32, x.shape, 1)
    shift = 1
    while shift < s_len:
        x = x + jnp.where(lane >= shift, pltpu.roll(x, shift, 1), 0.0)
        shift *= 2
    row = lax.broadcasted_iota(jnp.int32, x.shape, 0)
    x = jnp.where(row < n_heads, x, 0.0)
    terms = _split3(x * (-1.0 / HEAD_DIM ** -0.5))
    pad = jnp.zeros((LANES - SUBLANES * N_SPLIT, s_len), F32)
    kc_ref[...] = jnp.concatenate(list(terms) + [pad], axis=0).T.astype(BF16)


LOG2E = math.log2(math.e)


def _softmax_pv_folded(a, v_aug, g):
    m = jnp.max(a, axis=1, keepdims=True)
    p = jnp.exp2((a - m) * (HEAD_DIM ** -0.5 * LOG2E)).astype(BF16)
    acc = jnp.dot(p, v_aug, preferred_element_type=F32)
    gf = g.astype(F32)
    silu = gf * (1.0 / (1.0 + jnp.exp(-gf)))
    return (acc[:, :LANES] / acc[:, LANES:] * silu).astype(BF16)


SCORE_AHEAD = 1


def _stream_events(setup, n_tiles, score_matmul, finish):
    scores = {}

    def score(i):
        return lambda: scores.__setitem__(i, score_matmul(i))

    def fin(i):
        return lambda: finish(i, scores.pop(i))

    events = [setup] + [score(i) for i in range(min(SCORE_AHEAD, n_tiles))]
    for i in range(n_tiles):
        if i + SCORE_AHEAD < n_tiles:
            events.append(score(i + SCORE_AHEAD))
        events.append(fin(i))
    return events


def _run_interleaved(streams, lead=0):
    queues = [list(ev) for ev in streams]
    for _ in range(min(lead, len(queues[0]))):
        queues[0].pop(0)()
    while any(queues):
        for q in queues:
            if q:
                q.pop(0)()


def _t5_bias(rb_ref, h, dist):
    max_exact = T5_BUCKETS // 2
    n = jnp.maximum(dist, 0)
    nf = jnp.maximum(n, 1).astype(F32)
    val = jnp.log(nf / max_exact) / math.log(T5_MAX_DIST / max_exact) * (T5_BUCKETS - max_exact)
    bias = jnp.zeros(dist.shape, F32)
    for kk in range(T5_BUCKETS):
        if kk < max_exact:
            hit = n == kk
        elif kk < T5_BUCKETS - 1:
            hit = (n >= max_exact) & (val >= kk - max_exact) & (val < kk - max_exact + 1)
        else:
            hit = (n >= max_exact) & (val >= kk - max_exact)
        bias = jnp.where(hit, rb_ref[h, kk], bias)
    return bias


def _moba_head_constants(rb_ref, h, kaug_ref, bown_ref, bprev_ref):
    inv_scale = 1.0 / HEAD_DIM ** -0.5
    blk = MOBA_BLOCK
    seq = kaug_ref.shape[0]
    far_bias = rb_ref[h, T5_BUCKETS - 1]
    srow = lax.broadcasted_iota(jnp.int32, (seq, LANES), 0)
    col = lax.broadcasted_iota(jnp.int32, (seq, LANES), 1)
    kaug_ref[:, LANES:] = jnp.where((col & (SUBLANES - 1)) * blk == srow - (srow & (blk - 1)),
                                    1.0, 0.0).astype(BF16)
    t = T5_MAX_DIST
    assert blk == 2 * t
    dist = lax.broadcasted_iota(jnp.int32, (t, t), 0) - lax.broadcasted_iota(jnp.int32, (t, t), 1)
    near = jnp.where(dist >= 0, (_t5_bias(rb_ref, h, dist) - far_bias) * inv_scale, NEG)
    mid = (_t5_bias(rb_ref, h, dist + t) - far_bias) * inv_scale
    bown_ref[0:t, 0:t] = near
    bown_ref[0:t, t:blk] = jnp.full((t, t), NEG, F32)
    bown_ref[t:blk, 0:t] = mid
    bown_ref[t:blk, t:blk] = near
    bprev_ref[...] = jnp.zeros((blk, blk), F32)
    bprev_ref[0:t, t:blk] = mid


def _moba_stream(q_ref, k_ref, v_ref, g_ref, o_ref,
                 kaug_ref, qaug_ref, vaug_ref, kmean_ref, bown_ref, bprev_ref):
    blk = MOBA_BLOCK
    seq = q_ref.shape[0]
    n_blocks = seq // blk
    n_chunks = seq // LANES

    def setup():
        _moba_setup(q_ref, k_ref, v_ref, kaug_ref, qaug_ref, vaug_ref, kmean_ref, n_blocks, n_chunks)

    def score_matmul(i):
        return lax.dot_general(qaug_ref[i * blk:(i + 1) * blk, :], kaug_ref[0:(i + 1) * blk, :], _NT,
                               preferred_element_type=F32)

    def finish(i, a):
        rows = slice(i * blk, (i + 1) * blk)
        n = (i + 1) * blk
        parts = [a[:, n - blk:] + bown_ref[...]]
        if i >= 1:
            parts.insert(0, a[:, n - 2 * blk:n - blk] + bprev_ref[...])
        if i >= 2:
            parts.insert(0, a[:, :n - 2 * blk])
        a = parts[0] if len(parts) == 1 else jnp.concatenate(parts, axis=1)
        o_ref[rows, :] = _softmax_pv_folded(a, vaug_ref[0:n, :], g_ref[rows, :])

    return setup, n_blocks, score_matmul, finish


def _moba_setup(q_ref, k_ref, v_ref, kaug_ref, qaug_ref, vaug_ref, kmean_ref, n_blocks, n_chunks):
    blk = MOBA_BLOCK
    seq = q_ref.shape[0]
    kaug_ref[:, :LANES] = k_ref[...]
    vaug_ref[:, :LANES] = v_ref[...]
    vaug_ref[:, LANES:] = jnp.ones((seq, LANES), BF16)
    blk_row = lax.broadcasted_iota(jnp.int32, (SUBLANES, LANES), 0)
    km = jnp.zeros((SUBLANES, LANES), F32)
    for j in range(n_blocks):
        mean_j = jnp.mean(k_ref[j * blk:(j + 1) * blk, :].astype(F32), axis=0, keepdims=True)
        km = jnp.where(blk_row == j, mean_j, km)
    kmean_ref[...] = jnp.concatenate([km, jnp.zeros((LANES - SUBLANES, LANES), F32)], axis=0).astype(BF16)

    gate_t = lax.dot_general(kmean_ref[...], q_ref[...], _NT, preferred_element_type=F32)[0:SUBLANES, :]
    jrow = lax.broadcasted_iota(jnp.int32, gate_t.shape, 0)
    qpos = lax.broadcasted_iota(jnp.int32, gate_t.shape, 1)
    qblk = jnp.right_shift(qpos, int(math.log2(blk)))
    gm = jnp.where(jrow < qblk, gate_t, NEG)
    sel_bias = jnp.zeros(gate_t.shape, F32)
    for j in range(n_blocks):
        gj = gm[j:j + 1, :]
        beats = (gm > gj) | ((gm == gj) & (jrow < j))
        rank = jnp.sum(beats.astype(F32), axis=0, keepdims=True)
        allowed = ((rank < MOBA_TOPK) & (qblk[0:1, :] > j)) | (qblk[0:1, :] == j)
        sel_bias = jnp.where(jrow == j, jnp.where(allowed, 0.0, NEG), sel_bias)
    assert SUBLANES * n_chunks == LANES
    packed_t = jnp.concatenate([sel_bias[:, c * LANES:(c + 1) * LANES] for c in range(n_chunks)], axis=0).T
    lane_chunk = jnp.right_shift(lax.broadcasted_iota(jnp.int32, (LANES, LANES), 1),
                                 int(math.log2(SUBLANES)))
    qaug_ref[:, :LANES] = q_ref[...]
    for c in range(n_chunks):
        qaug_ref[c * LANES:(c + 1) * LANES, LANES:] = jnp.where(lane_chunk == c, packed_t, 0.0).astype(BF16)


def _moba_scratch(seq):
    blk = MOBA_BLOCK
    return [pltpu.VMEM((seq, 2 * LANES), BF16),
            pltpu.VMEM((seq, 2 * LANES), BF16),
            pltpu.VMEM((seq, 2 * LANES), BF16),
            pltpu.VMEM((LANES, LANES), BF16),
            pltpu.VMEM((blk, blk), F32),
            pltpu.VMEM((blk, blk), F32)]


def _fox_stream(h, q_ref, k_ref, v_ref, g_ref, kc_ref, o_ref, qaug_ref, kaug_ref, vaug_ref, causal_ref):
    seq = q_ref.shape[0]

    def setup():
        col = lax.broadcasted_iota(jnp.int32, (seq, LANES), 1)
        qaug_ref[:, :LANES] = q_ref[...]
        qaug_ref[:, LANES:] = jnp.where(((col & (SUBLANES - 1)) == h) & (col < SUBLANES * N_SPLIT),
                                        1.0, 0.0).astype(BF16)
        kaug_ref[:, :LANES] = k_ref[...]
        kaug_ref[:, LANES:] = kc_ref[...]
        vaug_ref[:, :LANES] = v_ref[...]
        vaug_ref[:, LANES:] = jnp.ones((seq, LANES), BF16)
        r = lax.broadcasted_iota(jnp.int32, (Q_TILE, Q_TILE), 0)
        c = lax.broadcasted_iota(jnp.int32, (Q_TILE, Q_TILE), 1)
        causal_ref[...] = jnp.where(c <= r, 0.0, NEG)

    def score_matmul(i):
        return lax.dot_general(qaug_ref[i * Q_TILE:(i + 1) * Q_TILE, :], kaug_ref[0:(i + 1) * Q_TILE, :],
                               _NT, preferred_element_type=F32)

    def finish(i, a):
        rows = slice(i * Q_TILE, (i + 1) * Q_TILE)
        n = (i + 1) * Q_TILE
        own = a[:, n - Q_TILE:] + causal_ref[...]
        a = own if i == 0 else jnp.concatenate([a[:, :n - Q_TILE], own], axis=1)
        o_ref[rows, :] = _softmax_pv_folded(a, vaug_ref[0:n, :], g_ref[rows, :])

    return setup, seq // Q_TILE, score_matmul, finish


def _fox_scratch(seq):
    return [pltpu.VMEM((seq, 2 * LANES), BF16), pltpu.VMEM((seq, 2 * LANES), BF16),
            pltpu.VMEM((seq, 2 * LANES), BF16), pltpu.VMEM((Q_TILE, Q_TILE), F32)]


FOX_LEAD_EVENTS = 11


def _causal_attention_kernel(*refs):
    rb_ref, moba_qkvg, fox_qkvg, kc, wo_ref, oa, ob, wob_ref = refs[:8]
    qa, ka, va, ga = (moba_qkvg.at[i] for i in range(HEAD_GROUP))
    qb, kb, vb, gb = (fox_qkvg.at[i] for i in range(HEAD_GROUP))
    n_moba = len(_moba_scratch(qa.shape[0]))
    moba_scratch, fox_scratch = refs[8:8 + n_moba], refs[8 + n_moba:]
    kaug_ref, _, _, _, bown_ref, bprev_ref = moba_scratch
    h = pl.program_id(0)

    @pl.when(pl.program_id(1) == 0)
    def _():
        _moba_head_constants(rb_ref, h, kaug_ref, bown_ref, bprev_ref)

    wob_ref[...] = wo_ref[...].astype(BF16)
    _run_interleaved([_stream_events(*_fox_stream(h, qb, kb, vb, gb, kc, ob, *fox_scratch)),
                      _stream_events(*_moba_stream(qa, ka, va, ga, oa, *moba_scratch))],
                     lead=FOX_LEAD_EVENTS)


def _causal_attention(p, rel_bias, kc, w_out3, layer, batch, seq):
    nb = seq // MOBA_BLOCK
    assert seq % MOBA_BLOCK == 0 and MOBA_TOPK < SUBLANES and nb <= SUBLANES and MOBA_BLOCK > T5_MAX_DIST
    assert N_HEADS_MOBA == N_HEADS_FOX <= SUBLANES
    dw, d = w_out3.shape[1:]
    n_slabs = dw // W_OUT_CAST_ROWS
    assert dw % W_OUT_CAST_ROWS == 0 and n_slabs <= N_HEADS_MOBA * batch
    slab = lambda h, b: jnp.minimum(h * batch + b, n_slabs - 1)
    head = lambda first: pl.BlockSpec((HEAD_GROUP, seq, LANES), lambda h, b: (first + h, b, 0))
    out = pl.BlockSpec((None, seq, LANES), lambda h, b: (h, b, 0))
    out_sds = jax.ShapeDtypeStruct((N_HEADS_MOBA, batch * seq, LANES), BF16)
    return pl.pallas_call(
        _causal_attention_kernel,
        grid=(N_HEADS_MOBA, batch),
        in_specs=[pl.BlockSpec(memory_space=pltpu.SMEM), head(0), head(N_HEADS_MOBA),
                  pl.BlockSpec((None, seq, LANES), lambda h, b: (b, 0, 0)),
                  pl.BlockSpec((None, W_OUT_CAST_ROWS, d), lambda h, b: (layer, slab(h, b), 0))],
        out_specs=[out, out, pl.BlockSpec((W_OUT_CAST_ROWS, d), lambda h, b: (slab(h, b), 0))],
        out_shape=[out_sds, out_sds, jax.ShapeDtypeStruct((dw, d), BF16)],
        scratch_shapes=_moba_scratch(seq) + _fox_scratch(seq),
        compiler_params=_cparams(2),
        name="causal_attention",
    )(rel_bias, p, p, kc, w_out3)


def _mem_kernel(b_ref, q_ref, k_ref, v_ref, g_ref, f_ref, o_ref, kc_ref, vaug_ref, *, layer, n_fox_heads):
    n_heads, seq, _ = q_ref.shape
    tiles_per_head = seq // MEM_Q_TILE

    _forget_bias_columns(f_ref, b_ref, layer, kc_ref, n_fox_heads)
    vaug_ref[:, :, LANES:] = jnp.ones(v_ref.shape, BF16)
    vaug_ref[:, :, :LANES] = v_ref[...]

    def tile(t):
        h, i = divmod(t, tiles_per_head)
        return h, slice(i * MEM_Q_TILE, (i + 1) * MEM_Q_TILE)

    def score_matmul(t):
        h, rows = tile(t)
        return lax.dot_general(q_ref[h, rows, :], k_ref[h], _NT, preferred_element_type=F32)

    def finish(t, a):
        h, rows = tile(t)
        o_ref[h, rows, :] = _softmax_pv_folded(a, vaug_ref[h], g_ref[h, rows, :])

    _run_interleaved([_stream_events(lambda: None, n_heads * tiles_per_head, score_matmul, finish)])


def _mem_attention(p, mkv, fb, b_forget, layer, batch, seq, n_mem, q0, g0):
    nh = N_HEADS_MEM
    assert q0 % nh == 0 and g0 % nh == 0 and N_HEADS_FOX <= SUBLANES
    heads = lambda off: pl.BlockSpec((nh, seq, LANES), lambda b: (off // nh, b, 0))
    kv = lambda blk: pl.BlockSpec((nh, n_mem, LANES), lambda b: (blk, b, 0))
    return pl.pallas_call(
        functools.partial(_mem_kernel, layer=layer, n_fox_heads=N_HEADS_FOX),
        grid=(batch,),
        in_specs=[pl.BlockSpec(memory_space=pltpu.SMEM),
                  heads(q0), kv(0), kv(1), heads(g0),
                  pl.BlockSpec((seq, LANES), lambda b: (b, 0))],
        out_specs=[pl.BlockSpec((nh, seq, LANES), lambda b: (0, b, 0)),
                   pl.BlockSpec((None, seq, LANES), lambda b: (b, 0, 0))],
        out_shape=[jax.ShapeDtypeStruct((nh, batch * seq, LANES), BF16),
                   jax.ShapeDtypeStruct((batch, seq, LANES), BF16)],
        scratch_shapes=[pltpu.VMEM((nh, n_mem, 2 * LANES), BF16)],
        compiler_params=_cparams(1),
        name="mem_attention",
    )(b_forget, p, mkv, mkv, p, fb)


def _out_proj_kernel(ya_ref, yb_ref, ym_ref, w_ref, x_ref, g_ref, o_ref):
    y = jnp.concatenate([src[c] for src in (ya_ref, yb_ref, ym_ref) for c in range(src.shape[0])], axis=1)
    o = jnp.dot(y, w_ref[...], preferred_element_type=F32)
    ms = jnp.mean(o * o, axis=-1, keepdims=True)
    o_ref[...] = x_ref[...] + o * lax.rsqrt(ms + EPS) * g_ref[...]


def _out_proj(ya, yb, ym, w, x2, g, *, tm):
    m, d = x2.shape
    heads = lambda y: pl.BlockSpec((y.shape[0], tm, LANES), lambda i: (0, i, 0))
    return pl.pallas_call(
        _out_proj_kernel,
        grid=(m // tm,),
        in_specs=[heads(ya), heads(yb), heads(ym),
                  pl.BlockSpec(w.shape, lambda i: (0, 0), pipeline_mode=pl.Buffered(1)),
                  pl.BlockSpec((tm, d), lambda i: (i, 0)),
                  pl.BlockSpec((1, d), lambda i: (0, 0))],
        out_specs=pl.BlockSpec((tm, d), lambda i: (i, 0)),
        out_shape=jax.ShapeDtypeStruct((m, d), F32),
        compiler_params=_cparams(1),
        name="out_proj",
    )(ya, yb, ym, w, x2, g)


def kernel(x, mem, w_in, b_forget, w_mem_kv, w_out, g_pre, g_post, g_mem, rel_bias):
    batch, seq, d = x.shape
    n_mem = mem.shape[1]
    depth = w_in.shape[0]
    assert N_HEADS_MOBA == N_HEADS_FOX
    n_causal = N_HEADS_MOBA + N_HEADS_FOX
    wm = N_HEADS_MEM * HEAD_DIM
    n_in = w_in.shape[2]
    assert n_in == n_causal * HEAD_GROUP * HEAD_DIM + N_HEADS_FOX + 2 * wm
    qm0 = n_causal * HEAD_GROUP
    gm0 = qm0 + N_HEADS_MEM

    x2 = x.reshape(batch * seq, d)
    mem2 = mem.reshape(batch * n_mem, d)
    for layer in range(depth):
        w_t3 = jnp.transpose(w_in, (0, 2, 1)).reshape(depth * n_in, d // LANES, LANES)
        w_main_t, w_gate_t, mkv = _w_repack_mem_proj(w_t3, layer, n_in, N_HEADS_MOBA, N_HEADS_FOX, mem2,
                                                     g_mem[layer].reshape(1, d), w_mem_kv)
        p, fb = _norm_proj(x2, g_pre[layer].reshape(1, d), w_main_t, w_gate_t,
                           tm=IN_PROJ_TILE[0], tn=IN_PROJ_TILE[1])
        ym, kc = _mem_attention(p, mkv, fb, b_forget, layer, batch, seq, n_mem, qm0, gm0)
        ya, yb, w_out_bf = _causal_attention(p, rel_bias, kc, w_out, layer, batch, seq)
        x2 = _out_proj(ya, yb, ym, w_out_bf, x2, g_post[layer].reshape(1, d), tm=OUT_PROJ_ROWS)
    return x2.reshape(batch, seq, d)
```

```python
import functools
import math

import jax
import jax.numpy as jnp
from jax import lax
from jax.experimental import pallas as pl
from jax.experimental.pallas import tpu as pltpu

HEAD_DIM = 128
N_HEADS_MOBA = 6
N_HEADS_FOX = 6
N_HEADS_MEM = 4
MOBA_BLOCK = 256
MOBA_TOPK = 3
T5_BUCKETS = 32
T5_MAX_DIST = 128
EPS = 1e-6
NEG = -1e30

LANES = 128
SUBLANES = 8
VMEM_LIMIT_BYTES = 56 * 1024 * 1024

IN_PROJ_TILE = (1024, 1792)
NORM_ROW_CHUNK = 256
OUT_PROJ_ROWS = 512
HEAD_GROUP = 4
W_OUT_CAST_ROWS = 128
Q_TILE = MOBA_BLOCK
MEM_Q_TILE = 1024

F32 = jnp.float32
BF16 = jnp.bfloat16
_NT = (((1,), (1,)), ((), ()))


def _cparams(n_axes):
    return pltpu.CompilerParams(dimension_semantics=("arbitrary",) * n_axes,
                                vmem_limit_bytes=VMEM_LIMIT_BYTES)


def _norm_proj_kernel(x_ref, g_ref, w_ref, wf_ref, p_ref, f_ref, h_ref, *, n_col_blocks, row_chunk):
    tm = x_ref.shape[0]
    j = pl.program_id(1)

    def project(rows):
        acc = lax.dot_general(h_ref[rows, :], w_ref[...], _NT, preferred_element_type=F32)
        for c in range(n_col_blocks):
            p_ref[c, rows, :] = acc[:, c * LANES:(c + 1) * LANES].astype(BF16)

    @pl.when(j == 0)
    def _normalize_and_project():
        for c in range(tm // row_chunk):
            rows = slice(c * row_chunk, (c + 1) * row_chunk)
            xf = x_ref[rows, :]
            ms = jnp.mean(xf * xf, axis=-1, keepdims=True)
            h = (xf * lax.rsqrt(ms + EPS) * g_ref[...]).astype(BF16)
            h_ref[rows, :] = h
            f_ref[rows, :] = lax.dot_general(h, wf_ref[...], _NT, preferred_element_type=F32)
            project(rows)

    @pl.when(j > 0)
    def _project():
        project(slice(0, tm))


def _norm_proj(x2, g, w_t, wf_t, *, tm, tn):
    m, d = x2.shape
    n = w_t.shape[0]
    ncb = tn // LANES
    assert n % tn == 0 and m % tm == 0
    body = functools.partial(_norm_proj_kernel, n_col_blocks=ncb, row_chunk=NORM_ROW_CHUNK)
    in_specs = [pl.BlockSpec((tm, d), lambda i, j: (i, 0), pipeline_mode=pl.Buffered(2, use_lookahead=True)),
                pl.BlockSpec((1, d), lambda i, j: (0, 0)),
                pl.BlockSpec((tn, d), lambda i, j: (j, 0)),
                pl.BlockSpec((LANES, d), lambda i, j: (0, 0))]
    out_specs = [pl.BlockSpec((ncb, tm, LANES), lambda i, j: (j, i, 0)),
                 pl.BlockSpec((tm, LANES), lambda i, j: (i, 0))]

    def pipelined(x_hbm, g_hbm, w_hbm, wf_hbm, p_hbm, f_hbm, h_ref):
        pltpu.emit_pipeline(body, grid=(m // tm, n // tn), in_specs=in_specs, out_specs=out_specs)(
            x_hbm, g_hbm, w_hbm, wf_hbm, p_hbm, f_hbm, scratches=(h_ref,))

    return pl.pallas_call(
        pipelined,
        in_specs=[pl.BlockSpec(memory_space=pl.ANY)] * 4,
        out_specs=[pl.BlockSpec(memory_space=pl.ANY)] * 2,
        out_shape=[jax.ShapeDtypeStruct((n // LANES, m, LANES), BF16),
                   jax.ShapeDtypeStruct((m, LANES), F32)],
        scratch_shapes=[pltpu.VMEM((tm, d), BF16)],
        compiler_params=pltpu.CompilerParams(vmem_limit_bytes=VMEM_LIMIT_BYTES),
        name="norm_proj_gate",
    )(x2, g, w_t, wf_t)


def _w_repack_kernel(src_ref, mem_ref, gm_ref, wm_ref, main_ref, gate_ref, mkv_ref,
                     buf_ref, gbuf_ref, hm_ref, sem_ref, gsem_ref, *,
                     base_row, n_heads, n_aligned, n_gate, n_mem_tiles, row_chunk):
    t = pl.program_id(0)
    n_steps = pl.num_programs(0)
    n_slices = src_ref.shape[1]

    def source_row(step, g):
        head_blk = (step // n_heads) * (HEAD_GROUP * n_heads) + g * n_heads + step % n_heads
        blk = jnp.where(step < 2 * n_heads, head_blk, HEAD_GROUP * step + g)
        return base_row + blk * LANES + jnp.where(blk * LANES >= n_aligned, n_gate, 0)

    def block_copies(step, slot):
        return [pltpu.make_async_copy(src_ref.at[pl.ds(source_row(step, g), LANES), s, :],
                                      buf_ref.at[slot, pl.ds(g * LANES, LANES), pl.ds(s * LANES, LANES)],
                                      sem_ref.at[slot])
                for g in range(HEAD_GROUP) for s in range(n_slices)]

    def gate_copies():
        gate_row0 = base_row + n_aligned
        return [pltpu.make_async_copy(src_ref.at[pl.ds(gate_row0, SUBLANES), s, :],
                                      gbuf_ref.at[:, pl.ds(s * LANES, LANES)],
                                      gsem_ref.at[0]) for s in range(n_slices)]

    @pl.when(t == 0)
    def _first():
        for cp in gate_copies() + block_copies(0, 0):
            cp.start()

    @pl.when(t + 1 < n_steps)
    def _prefetch():
        for cp in block_copies(t + 1, (t + 1) % 2):
            cp.start()

    @pl.when(t == 0)
    def _gate_rows_and_mem_norm():
        for cp in gate_copies():
            cp.wait()
        row = lax.broadcasted_iota(jnp.int32, gbuf_ref.shape, 0)
        gate = jnp.where(row < n_gate, gbuf_ref[...], 0.0)
        gate_ref[...] = jnp.concatenate(
            [gate, jnp.zeros((gate_ref.shape[0] - SUBLANES, gate.shape[1]), F32)], axis=0).astype(BF16)
        for c in range(mem_ref.shape[0] // row_chunk):
            r = slice(c * row_chunk, (c + 1) * row_chunk)
            xf = mem_ref[r, :]
            ms = jnp.mean(xf * xf, axis=-1, keepdims=True)
            hm_ref[r, :] = (xf * lax.rsqrt(ms + EPS) * gm_ref[...]).astype(BF16)

    @pl.when(t < n_mem_tiles)
    def _mem_projection():
        mkv_ref[...] = jnp.dot(hm_ref[...], wm_ref[...].astype(BF16),
                               preferred_element_type=F32).astype(BF16)

    for cp in block_copies(t, t % 2):
        cp.wait()
    main_ref[...] = buf_ref[t % 2].astype(BF16)


def _w_repack_mem_proj(w_t3, layer, n_total, n_heads, n_gate, mem2, g_mem, w_mem3):
    n_slices = w_t3.shape[1]
    d = n_slices * LANES
    rows = HEAD_GROUP * LANES
    n_aligned = 2 * n_heads * rows
    n_main = n_total - n_gate
    m_mem = mem2.shape[0]
    n_mem_tiles = w_mem3.shape[2] // LANES
    n_steps = n_main // rows
    assert n_main % rows == 0 and n_aligned <= n_main and n_gate <= SUBLANES
    assert w_mem3.shape[2] % LANES == 0 and n_mem_tiles <= n_steps
    mem_tile = lambda t: jnp.minimum(t, n_mem_tiles - 1)
    return pl.pallas_call(
        functools.partial(_w_repack_kernel, base_row=layer * n_total, n_heads=n_heads,
                          n_aligned=n_aligned, n_gate=n_gate,
                          n_mem_tiles=n_mem_tiles, row_chunk=NORM_ROW_CHUNK),
        grid=(n_steps,),
        in_specs=[pl.BlockSpec(memory_space=pl.ANY),
                  pl.BlockSpec((m_mem, d), lambda t: (0, 0), pipeline_mode=pl.Buffered(1)),
                  pl.BlockSpec((1, d), lambda t: (0, 0)),
                  pl.BlockSpec((None, d, LANES), lambda t: (layer, 0, mem_tile(t)))],
        out_specs=[pl.BlockSpec((rows, d), lambda t: (t, 0)),
                   pl.BlockSpec((LANES, d), lambda t: (0, 0)),
                   pl.BlockSpec((None, m_mem, LANES), lambda t: (mem_tile(t), 0, 0))],
        out_shape=[jax.ShapeDtypeStruct((n_main, d), BF16),
                   jax.ShapeDtypeStruct((LANES, d), BF16),
                   jax.ShapeDtypeStruct((n_mem_tiles, m_mem, LANES), BF16)],
        scratch_shapes=[pltpu.VMEM((2, rows, d), F32), pltpu.VMEM((SUBLANES, d), F32),
                        pltpu.VMEM((m_mem, d), BF16),
                        pltpu.SemaphoreType.DMA((2,)), pltpu.SemaphoreType.DMA((1,))],
        compiler_params=_cparams(1),
        name="w_repack_mem_proj",
    )(w_t3, mem2, g_mem, w_mem3)


N_SPLIT = 3


def _split3(v):
    hi = v.astype(BF16).astype(F32)
    mid = (v - hi).astype(BF16).astype(F32)
    lo = (v - hi - mid).astype(BF16).astype(F32)
    return hi, mid, lo


def _forget_bias_columns(f_ref, b_ref, layer, kc_ref, n_heads):
    head_lane = lax.broadcasted_iota(jnp.int32, (1, LANES), 1)
    bias = jnp.zeros((1, LANES), F32)
    for hh in range(n_heads):
        bias = jnp.where(head_lane == hh, b_ref[layer, hh], bias)
    z = f_ref[...] + bias
    logf = jnp.minimum(z, 0.0) - jnp.log1p(jnp.exp(-jnp.abs(z)))
    x = logf.T[0:SUBLANES, :]
    s_len = x.shape[1]
    lane = lax.broadcasted_iota(jnp.int32, x.shape, 1)
    shift = 1
    while shift < s_len:
        x = x + jnp.where(lane >= shift, pltpu.roll(x, shift, 1), 0.0)
        shift *= 2
    row = lax.broadcasted_iota(jnp.int32, x.shape, 0)
    x = jnp.where(row < n_heads, x, 0.0)
    terms = _split3(x * (-1.0 / HEAD_DIM ** -0.5))
    pad = jnp.zeros((LANES - SUBLANES * N_SPLIT, s_len), F32)
    kc_ref[...] = jnp.concatenate(list(terms) + [pad], axis=0).T.astype(BF16)


LOG2E = math.log2(math.e)


def _softmax_pv_folded(a, v_aug, g):
    m = jnp.max(a, axis=1, keepdims=True)
    p = jnp.exp2((a - m) * (HEAD_DIM ** -0.5 * LOG2E)).astype(BF16)
    acc = jnp.dot(p, v_aug, preferred_element_type=F32)
    gf = g.astype(F32)
    silu = gf * (1.0 / (1.0 + jnp.exp(-gf)))
    return (acc[:, :LANES] / acc[:, LANES:] * silu).astype(BF16)


SCORE_AHEAD = 1


def _stream_events(setup, n_tiles, score_matmul, finish):
    scores = {}

    def score(i):
        return lambda: scores.__setitem__(i, score_matmul(i))

    def fin(i):
        return lambda: finish(i, scores.pop(i))

    events = [setup] + [score(i) for i in range(min(SCORE_AHEAD, n_tiles))]
    for i in range(n_tiles):
        if i + SCORE_AHEAD < n_tiles:
            events.append(score(i + SCORE_AHEAD))
        events.append(fin(i))
    return events


def _run_interleaved(streams, lead=0):
    queues = [list(ev) for ev in streams]
    for _ in range(min(lead, len(queues[0]))):
        queues[0].pop(0)()
    while any(queues):
        for q in queues:
            if q:
                q.pop(0)()


def _t5_bias(rb_ref, h, dist):
    max_exact = T5_BUCKETS // 2
    n = jnp.maximum(dist, 0)
    nf = jnp.maximum(n, 1).astype(F32)
    val = jnp.log(nf / max_exact) / math.log(T5_MAX_DIST / max_exact) * (T5_BUCKETS - max_exact)
    bias = jnp.zeros(dist.shape, F32)
    for kk in range(T5_BUCKETS):
        if kk < max_exact:
            hit = n == kk
        elif kk < T5_BUCKETS - 1:
            hit = (n >= max_exact) & (val >= kk - max_exact) & (val < kk - max_exact + 1)
        else:
            hit = (n >= max_exact) & (val >= kk - max_exact)
        bias = jnp.where(hit, rb_ref[h, kk], bias)
    return bias


def _moba_head_constants(rb_ref, h, kaug_ref, bown_ref, bprev_ref):
    inv_scale = 1.0 / HEAD_DIM ** -0.5
    blk = MOBA_BLOCK
    seq = kaug_ref.shape[0]
    far_bias = rb_ref[h, T5_BUCKETS - 1]
    srow = lax.broadcasted_iota(jnp.int32, (seq, LANES), 0)
    col = lax.broadcasted_iota(jnp.int32, (seq, LANES), 1)
    kaug_ref[:, LANES:] = jnp.where((col & (SUBLANES - 1)) * blk == srow - (srow & (blk - 1)),
                                    1.0, 0.0).astype(BF16)
    t = T5_MAX_DIST
    assert blk == 2 * t
    dist = lax.broadcasted_iota(jnp.int32, (t, t), 0) - lax.broadcasted_iota(jnp.int32, (t, t), 1)
    near = jnp.where(dist >= 0, (_t5_bias(rb_ref, h, dist) - far_bias) * inv_scale, NEG)
    mid = (_t5_bias(rb_ref, h, dist + t) - far_bias) * inv_scale
    bown_ref[0:t, 0:t] = near
    bown_ref[0:t, t:blk] = jnp.full((t, t), NEG, F32)
    bown_ref[t:blk, 0:t] = mid
    bown_ref[t:blk, t:blk] = near
    bprev_ref[...] = jnp.zeros((blk, blk), F32)
    bprev_ref[0:t, t:blk] = mid


def _moba_stream(q_ref, k_ref, v_ref, g_ref, o_ref,
                 kaug_ref, qaug_ref, vaug_ref, kmean_ref, bown_ref, bprev_ref):
    blk = MOBA_BLOCK
    seq = q_ref.shape[0]
    n_blocks = seq // blk
    n_chunks = seq // LANES

    def setup():
        _moba_setup(q_ref, k_ref, v_ref, kaug_ref, qaug_ref, vaug_ref, kmean_ref, n_blocks, n_chunks)

    def score_matmul(i):
        return lax.dot_general(qaug_ref[i * blk:(i + 1) * blk, :], kaug_ref[0:(i + 1) * blk, :], _NT,
                               preferred_element_type=F32)

    def finish(i, a):
        rows = slice(i * blk, (i + 1) * blk)
        n = (i + 1) * blk
        parts = [a[:, n - blk:] + bown_ref[...]]
        if i >= 1:
            parts.insert(0, a[:, n - 2 * blk:n - blk] + bprev_ref[...])
        if i >= 2:
            parts.insert(0, a[:, :n - 2 * blk])
        a = parts[0] if len(parts) == 1 else jnp.concatenate(parts, axis=1)
        o_ref[rows, :] = _softmax_pv_folded(a, vaug_ref[0:n, :], g_ref[rows, :])

    return setup, n_blocks, score_matmul, finish


def _moba_setup(q_ref, k_ref, v_ref, kaug_ref, qaug_ref, vaug_ref, kmean_ref, n_blocks, n_chunks):
    blk = MOBA_BLOCK
    seq = q_ref.shape[0]
    kaug_ref[:, :LANES] = k_ref[...]
    vaug_ref[:, :LANES] = v_ref[...]
    vaug_ref[:, LANES:] = jnp.ones((seq, LANES), BF16)
    blk_row = lax.broadcasted_iota(jnp.int32, (SUBLANES, LANES), 0)
    km = jnp.zeros((SUBLANES, LANES), F32)
    for j in range(n_blocks):
        mean_j = jnp.mean(k_ref[j * blk:(j + 1) * blk, :].astype(F32), axis=0, keepdims=True)
        km = jnp.where(blk_row == j, mean_j, km)
    kmean_ref[...] = jnp.concatenate([km, jnp.zeros((LANES - SUBLANES, LANES), F32)], axis=0).astype(BF16)

    gate_t = lax.dot_general(kmean_ref[...], q_ref[...], _NT, preferred_element_type=F32)[0:SUBLANES, :]
    jrow = lax.broadcasted_iota(jnp.int32, gate_t.shape, 0)
    qpos = lax.broadcasted_iota(jnp.int32, gate_t.shape, 1)
    qblk = jnp.right_shift(qpos, int(math.log2(blk)))
    gm = jnp.where(jrow < qblk, gate_t, NEG)
    sel_bias = jnp.zeros(gate_t.shape, F32)
    for j in range(n_blocks):
        gj = gm[j:j + 1, :]
        beats = (gm > gj) | ((gm == gj) & (jrow < j))
        rank = jnp.sum(beats.astype(F32), axis=0, keepdims=True)
        allowed = ((rank < MOBA_TOPK) & (qblk[0:1, :] > j)) | (qblk[0:1, :] == j)
        sel_bias = jnp.where(jrow == j, jnp.where(allowed, 0.0, NEG), sel_bias)
    assert SUBLANES * n_chunks == LANES
    packed_t = jnp.concatenate([sel_bias[:, c * LANES:(c + 1) * LANES] for c in range(n_chunks)], axis=0).T
    lane_chunk = jnp.right_shift(lax.broadcasted_iota(jnp.int32, (LANES, LANES), 1),
                                 int(math.log2(SUBLANES)))
    qaug_ref[:, :LANES] = q_ref[...]
    for c in range(n_chunks):
        qaug_ref[c * LANES:(c + 1) * LANES, LANES:] = jnp.where(lane_chunk == c, packed_t, 0.0).astype(BF16)


def _moba_scratch(seq):
    blk = MOBA_BLOCK
    return [pltpu.VMEM((seq, 2 * LANES), BF16),
            pltpu.VMEM((seq, 2 * LANES), BF16),
            pltpu.VMEM((seq, 2 * LANES), BF16),
            pltpu.VMEM((LANES, LANES), BF16),
            pltpu.VMEM((blk, blk), F32),
            pltpu.VMEM((blk, blk), F32)]


def _fox_stream(h, q_ref, k_ref, v_ref, g_ref, kc_ref, o_ref, qaug_ref, kaug_ref, vaug_ref, causal_ref):
    seq = q_ref.shape[0]

    def setup():
        col = lax.broadcasted_iota(jnp.int32, (seq, LANES), 1)
        qaug_ref[:, :LANES] = q_ref[...]
        qaug_ref[:, LANES:] = jnp.where(((col & (SUBLANES - 1)) == h) & (col < SUBLANES * N_SPLIT),
                                        1.0, 0.0).astype(BF16)
        kaug_ref[:, :LANES] = k_ref[...]
        kaug_ref[:, LANES:] = kc_ref[...]
        vaug_ref[:, :LANES] = v_ref[...]
        vaug_ref[:, LANES:] = jnp.ones((seq, LANES), BF16)
        r = lax.broadcasted_iota(jnp.int32, (Q_TILE, Q_TILE), 0)
        c = lax.broadcasted_iota(jnp.int32, (Q_TILE, Q_TILE), 1)
        causal_ref[...] = jnp.where(c <= r, 0.0, NEG)

    def score_matmul(i):
        return lax.dot_general(qaug_ref[i * Q_TILE:(i + 1) * Q_TILE, :], kaug_ref[0:(i + 1) * Q_TILE, :],
                               _NT, preferred_element_type=F32)

    def finish(i, a):
        rows = slice(i * Q_TILE, (i + 1) * Q_TILE)
        n = (i + 1) * Q_TILE
        own = a[:, n - Q_TILE:] + causal_ref[...]
        a = own if i == 0 else jnp.concatenate([a[:, :n - Q_TILE], own], axis=1)
        o_ref[rows, :] = _softmax_pv_folded(a, vaug_ref[0:n, :], g_ref[rows, :])

    return setup, seq // Q_TILE, score_matmul, finish


def _fox_scratch(seq):
    return [pltpu.VMEM((seq, 2 * LANES), BF16), pltpu.VMEM((seq, 2 * LANES), BF16),
            pltpu.VMEM((seq, 2 * LANES), BF16), pltpu.VMEM((Q_TILE, Q_TILE), F32)]


FOX_LEAD_EVENTS = 11


def _causal_attention_kernel(*refs):
    rb_ref, moba_qkvg, fox_qkvg, kc, wo_ref, oa, ob, wob_ref = refs[:8]
    qa, ka, va, ga = (moba_qkvg.at[i] for i in range(HEAD_GROUP))
    qb, kb, vb, gb = (fox_qkvg.at[i] for i in range(HEAD_GROUP))
    n_moba = len(_moba_scratch(qa.shape[0]))
    moba_scratch, fox_scratch = refs[8:8 + n_moba], refs[8 + n_moba:]
    kaug_ref, _, _, _, bown_ref, bprev_ref = moba_scratch
    h = pl.program_id(0)

    @pl.when(pl.program_id(1) == 0)
    def _():
        _moba_head_constants(rb_ref, h, kaug_ref, bown_ref, bprev_ref)

    wob_ref[...] = wo_ref[...].astype(BF16)
    _run_interleaved([_stream_events(*_fox_stream(h, qb, kb, vb, gb, kc, ob, *fox_scratch)),
                      _stream_events(*_moba_stream(qa, ka, va, ga, oa, *moba_scratch))],
                     lead=FOX_LEAD_EVENTS)


def _causal_attention(p, rel_bias, kc, w_out3, layer, batch, seq):
    nb = seq // MOBA_BLOCK
    assert seq % MOBA_BLOCK == 0 and MOBA_TOPK < SUBLANES and nb <= SUBLANES and MOBA_BLOCK > T5_MAX_DIST
    assert N_HEADS_MOBA == N_HEADS_FOX <= SUBLANES
    dw, d = w_out3.shape[1:]
    n_slabs = dw // W_OUT_CAST_ROWS
    assert dw % W_OUT_CAST_ROWS == 0 and n_slabs <= N_HEADS_MOBA * batch
    slab = lambda h, b: jnp.minimum(h * batch + b, n_slabs - 1)
    head = lambda first: pl.BlockSpec((HEAD_GROUP, seq, LANES), lambda h, b: (first + h, b, 0))
    out = pl.BlockSpec((None, seq, LANES), lambda h, b: (h, b, 0))
    out_sds = jax.ShapeDtypeStruct((N_HEADS_MOBA, batch * seq, LANES), BF16)
    return pl.pallas_call(
        _causal_attention_kernel,
        grid=(N_HEADS_MOBA, batch),
        in_specs=[pl.BlockSpec(memory_space=pltpu.SMEM), head(0), head(N_HEADS_MOBA),
                  pl.BlockSpec((None, seq, LANES), lambda h, b: (b, 0, 0)),
                  pl.BlockSpec((None, W_OUT_CAST_ROWS, d), lambda h, b: (layer, slab(h, b), 0))],
        out_specs=[out, out, pl.BlockSpec((W_OUT_CAST_ROWS, d), lambda h, b: (slab(h, b), 0))],
        out_shape=[out_sds, out_sds, jax.ShapeDtypeStruct((dw, d), BF16)],
        scratch_shapes=_moba_scratch(seq) + _fox_scratch(seq),
        compiler_params=_cparams(2),
        name="causal_attention",
    )(rel_bias, p, p, kc, w_out3)


def _mem_kernel(b_ref, q_ref, k_ref, v_ref, g_ref, f_ref, o_ref, kc_ref, vaug_ref, *, layer, n_fox_heads):
    n_heads, seq, _ = q_ref.shape
    tiles_per_head = seq // MEM_Q_TILE

    _forget_bias_columns(f_ref, b_ref, layer, kc_ref, n_fox_heads)
    vaug_ref[:, :, LANES:] = jnp.ones(v_ref.shape, BF16)
    vaug_ref[:, :, :LANES] = v_ref[...]

    def tile(t):
        h, i = divmod(t, tiles_per_head)
        return h, slice(i * MEM_Q_TILE, (i + 1) * MEM_Q_TILE)

    def score_matmul(t):
        h, rows = tile(t)
        return lax.dot_general(q_ref[h, rows, :], k_ref[h], _NT, preferred_element_type=F32)

    def finish(t, a):
        h, rows = tile(t)
        o_ref[h, rows, :] = _softmax_pv_folded(a, vaug_ref[h], g_ref[h, rows, :])

    _run_interleaved([_stream_events(lambda: None, n_heads * tiles_per_head, score_matmul, finish)])


def _mem_attention(p, mkv, fb, b_forget, layer, batch, seq, n_mem, q0, g0):
    nh = N_HEADS_MEM
    assert q0 % nh == 0 and g0 % nh == 0 and N_HEADS_FOX <= SUBLANES
    heads = lambda off: pl.BlockSpec((nh, seq, LANES), lambda b: (off // nh, b, 0))
    kv = lambda blk: pl.BlockSpec((nh, n_mem, LANES), lambda b: (blk, b, 0))
    return pl.pallas_call(
        functools.partial(_mem_kernel, layer=layer, n_fox_heads=N_HEADS_FOX),
        grid=(batch,),
        in_specs=[pl.BlockSpec(memory_space=pltpu.SMEM),
                  heads(q0), kv(0), kv(1), heads(g0),
                  pl.BlockSpec((seq, LANES), lambda b: (b, 0))],
        out_specs=[pl.BlockSpec((nh, seq, LANES), lambda b: (0, b, 0)),
                   pl.BlockSpec((None, seq, LANES), lambda b: (b, 0, 0))],
        out_shape=[jax.ShapeDtypeStruct((nh, batch * seq, LANES), BF16),
                   jax.ShapeDtypeStruct((batch, seq, LANES), BF16)],
        scratch_shapes=[pltpu.VMEM((nh, n_mem, 2 * LANES), BF16)],
        compiler_params=_cparams(1),
        name="mem_attention",
    )(b_forget, p, mkv, mkv, p, fb)


def _out_proj_kernel(ya_ref, yb_ref, ym_ref, w_ref, x_ref, g_ref, o_ref):
    y = jnp.concatenate([src[c] for src in (ya_ref, yb_ref, ym_ref) for c in range(src.shape[0])], axis=1)
    o = jnp.dot(y, w_ref[...], preferred_element_type=F32)
    ms = jnp.mean(o * o, axis=-1, keepdims=True)
    o_ref[...] = x_ref[...] + o * lax.rsqrt(ms + EPS) * g_ref[...]


def _out_proj(ya, yb, ym, w, x2, g, *, tm):
    m, d = x2.shape
    heads = lambda y: pl.BlockSpec((y.shape[0], tm, LANES), lambda i: (0, i, 0))
    return pl.pallas_call(
        _out_proj_kernel,
        grid=(m // tm,),
        in_specs=[heads(ya), heads(yb), heads(ym),
                  pl.BlockSpec(w.shape, lambda i: (0, 0), pipeline_mode=pl.Buffered(1)),
                  pl.BlockSpec((tm, d), lambda i: (i, 0)),
                  pl.BlockSpec((1, d), lambda i: (0, 0))],
        out_specs=pl.BlockSpec((tm, d), lambda i: (i, 0)),
        out_shape=jax.ShapeDtypeStruct((m, d), F32),
        compiler_params=_cparams(1),
        name="out_proj",
    )(ya, yb, ym, w, x2, g)


def kernel(x, mem, w_in, b_forget, w_mem_kv, w_out, g_pre, g_post, g_mem, rel_bias):
    batch, seq, d = x.shape
    n_mem = mem.shape[1]
    depth = w_in.shape[0]
    assert N_HEADS_MOBA == N_HEADS_FOX
    n_causal = N_HEADS_MOBA + N_HEADS_FOX
    wm = N_HEADS_MEM * HEAD_DIM
    n_in = w_in.shape[2]
    assert n_in == n_causal * HEAD_GROUP * HEAD_DIM + N_HEADS_FOX + 2 * wm
    qm0 = n_causal * HEAD_GROUP
    gm0 = qm0 + N_HEADS_MEM

    x2 = x.reshape(batch * seq, d)
    mem2 = mem.reshape(batch * n_mem, d)
    for layer in range(depth):
        w_t3 = jnp.transpose(w_in, (0, 2, 1)).reshape(depth * n_in, d // LANES, LANES)
        w_main_t, w_gate_t, mkv = _w_repack_mem_proj(w_t3, layer, n_in, N_HEADS_MOBA, N_HEADS_FOX, mem2,
                                                     g_mem[layer].reshape(1, d), w_mem_kv)
        p, fb = _norm_proj(x2, g_pre[layer].reshape(1, d), w_main_t, w_gate_t,
                           tm=IN_PROJ_TILE[0], tn=IN_PROJ_TILE[1])
        ym, kc = _mem_attention(p, mkv, fb, b_forget, layer, batch, seq, n_mem, qm0, gm0)
        ya, yb, w_out_bf = _causal_attention(p, rel_bias, kc, w_out, layer, batch, seq)
        x2 = _out_proj(ya, yb, ym, w_out_bf, x2, g_post[layer].reshape(1, d), tm=OUT_PROJ_ROWS)
    return x2.reshape(batch, seq, d)
```
